```python
import math
import jax, jax.numpy as jnp
from jax import lax
import numpy as np

D_MODEL = 1024
BATCH = 32
SEQ = 256
DEPTH = 2
DEC_BATCH = 2
DEC_SEQ = 2048
PAST_LEN = 512

GRID_W = 64
Q_BLOCK = 128
MLA_HEADS = 8
MLA_NOPE = 128
MLA_ROPE = 64
MLA_V = 128
MLA_Q_LORA = 512
MLA_KV_LORA = 256
MLA_SCALE = 1.0 / math.sqrt(MLA_NOPE + MLA_ROPE)
ROPE_THETA = 10000.0
NA_HEADS = 16
NA_HD = D_MODEL // NA_HEADS
NA_WIN_ROWS = 8
NA_WIN_COLS = 16
NA_SCALE = 1.0 / math.sqrt(NA_HD)
FFN_DENSE = 2816
N_EXPERTS = 8
TOP_K = 2
FFN_EXPERT = 3584
ALPHA = (2 * DEPTH) ** 0.25
BETA = (8 * DEPTH) ** -0.25
LN_EPS = 1e-5
RMS_EPS = 1e-6

kernel_name = "hybrid_mla_natten_prefix_diffusion_step"


def layer_norm(x, g, b):
    xf = x.astype(jnp.float32)
    mu = jnp.mean(xf, axis=-1, keepdims=True)
    var = jnp.mean(jnp.square(xf - mu), axis=-1, keepdims=True)
    y = (xf - mu) * lax.rsqrt(var + LN_EPS) * g.astype(jnp.float32) + b.astype(jnp.float32)
    return y.astype(x.dtype)


def rms_norm(x, g):
    xf = x.astype(jnp.float32)
    y = xf * lax.rsqrt(jnp.mean(jnp.square(xf), axis=-1, keepdims=True) + RMS_EPS) * g.astype(jnp.float32)
    return y.astype(x.dtype)


def ada_mod(cvec, w, b):
    m = jax.nn.silu(cvec) @ w + b
    if m.ndim == 2:
        m = m[:, None, :]
    return jnp.split(m, 6, axis=-1)


def axial_angles(T):
    half = MLA_ROPE // 2
    inv_freq = 1.0 / (ROPE_THETA ** (jnp.arange(0, half, 2, dtype=jnp.float32) / half))
    t = jnp.arange(T)
    row = (t // GRID_W).astype(jnp.float32)
    col = (t % GRID_W).astype(jnp.float32)
    ang_r = row[:, None] * inv_freq[None, :]
    ang_c = col[:, None] * inv_freq[None, :]
    return ang_r[:, None, :], ang_c[:, None, :]


def rope_rotate(x, ang):
    n = x.shape[-1] // 2
    cos = jnp.cos(ang).astype(x.dtype)
    sin = jnp.sin(ang).astype(x.dtype)
    x1, x2 = x[..., :n], x[..., n:]
    return jnp.concatenate([x1 * cos - x2 * sin, x1 * sin + x2 * cos], axis=-1)


def axial_rope(x, ang_r, ang_c):
    half = x.shape[-1] // 2
    return jnp.concatenate([rope_rotate(x[..., :half], ang_r), rope_rotate(x[..., half:], ang_c)], axis=-1)


def block_attention(q, k, v, scale):
    B, T, H, d = q.shape
    qb = min(Q_BLOCK, T)
    nb = T // qb
    qs = q.reshape(B, nb, qb, H, d).swapaxes(0, 1)

    def one(qi):
        s = jnp.einsum('bqhd,bshd->bhqs', qi, k).astype(jnp.float32) * scale
        p = jax.nn.softmax(s, axis=-1).astype(v.dtype)
        return jnp.einsum('bhqs,bshd->bqhd', p, v)

    o = lax.map(one, qs)
    return o.swapaxes(0, 1).reshape(B, T, H, v.shape[-1])


def mla_queries(h, p):
    B, T, _ = h.shape
    cq = rms_norm(h @ p['w_dq'], p['g_q'])
    q = (cq @ p['w_uq']).reshape(B, T, MLA_HEADS, MLA_NOPE + MLA_ROPE)
    return q[..., :MLA_NOPE], q[..., MLA_NOPE:]


def mla_compress(h, p):
    kv = h @ p['w_dkv']
    ckv = rms_norm(kv[..., :MLA_KV_LORA], p['g_kv'])
    return ckv, kv[..., MLA_KV_LORA:]


def mla_expand(ckv, p):
    B, S, _ = ckv.shape
    kv = (ckv @ p['w_ukv']).reshape(B, S, MLA_HEADS, MLA_NOPE + MLA_V)
    return kv[..., :MLA_NOPE], kv[..., MLA_NOPE:]


def mla_attention(q_nope, q_rope, k_nope, k_rope, v):
    B, T, H, _ = q_nope.shape
    qb = min(Q_BLOCK, T)
    nb = T // qb
    qn = q_nope.reshape(B, nb, qb, H, MLA_NOPE).swapaxes(0, 1)
    qr = q_rope.reshape(B, nb, qb, H, MLA_ROPE).swapaxes(0, 1)

    def one(args):
        a, r = args
        s = (jnp.einsum('bqhd,bshd->bhqs', a, k_nope)
             + jnp.einsum('bqhr,bsr->bhqs', r, k_rope)).astype(jnp.float32) * MLA_SCALE
        pr = jax.nn.softmax(s, axis=-1).astype(v.dtype)
        return jnp.einsum('bhqs,bshd->bqhd', pr, v)

    o = lax.map(one, (qn, qr))
    return o.swapaxes(0, 1).reshape(B, T, H, MLA_V)


def mla_context(h, p):
    B, T, _ = h.shape
    qn, qr = mla_queries(h, p)
    ckv, kr = mla_compress(h, p)
    kn, v = mla_expand(ckv, p)
    o = mla_attention(qn, qr, kn, kr, v)
    return o.reshape(B, T, MLA_HEADS * MLA_V) @ p['w_o'], (ckv, kr)


def mla_latent(h, cache_ckv, cache_krope, p):
    B, T, _ = h.shape
    ang_r, ang_c = axial_angles(T)
    qn, qr = mla_queries(h, p)
    qr = axial_rope(qr, ang_r, ang_c)
    ckv, kr = mla_compress(h, p)
    kr = axial_rope(kr[:, :, None, :], ang_r, ang_c)[:, :, 0, :]
    kn_lat, v_lat = mla_expand(ckv, p)
    kn_ctx, v_ctx = mla_expand(cache_ckv, p)
    k_nope = jnp.concatenate([kn_lat, kn_ctx], axis=1)
    k_rope = jnp.concatenate([kr, cache_krope], axis=1)
    v = jnp.concatenate([v_lat, v_ctx], axis=1)
    o = mla_attention(qn, qr, k_nope, k_rope, v)
    return o.reshape(B, T, MLA_HEADS * MLA_V) @ p['w_o']


def na_qkv(h, p):
    B, T, _ = h.shape
    qkv = (h @ p['w_qkv']).reshape(B, T, 3, NA_HEADS, NA_HD)
    return qkv[:, :, 0], qkv[:, :, 1], qkv[:, :, 2]


def na_context(h, p):
    B, T, _ = h.shape
    q, k, v = na_qkv(h, p)
    o = block_attention(q, k, v, NA_SCALE)
    return o.reshape(B, T, D_MODEL) @ p['w_o'], (k, v)


def neighbourhood_attention(q, k, v, k_ctx, v_ctx, rpb):
    B, T, H, d = q.shape
    rows = T // GRID_W
    wr = min(NA_WIN_ROWS, rows)
    wc = NA_WIN_COLS
    qg = q.reshape(B, rows, GRID_W, H, d).swapaxes(0, 1)
    kg = k.reshape(B, rows, GRID_W, H, d)
    vg = v.reshape(B, rows, GRID_W, H, d)
    col = jnp.arange(GRID_W)
    col_start = jnp.clip(col - wc // 2, 0, GRID_W - wc)
    col_idx = col_start[:, None] + jnp.arange(wc)[None, :]
    col_off = col_idx - col[:, None]
    rpb_cols = rpb[:, :, col_off + NA_WIN_COLS - 1]

    def one_row(args):
        r, q_r = args
        rs = jnp.clip(r - wr // 2, 0, rows - wr)
        k_rows = lax.dynamic_slice_in_dim(kg, rs, wr, axis=1)
        v_rows = lax.dynamic_slice_in_dim(vg, rs, wr, axis=1)
        k_nb = k_rows[:, :, col_idx]
        v_nb = v_rows[:, :, col_idx]
        row_off = rs + jnp.arange(wr) - r
        bias = rpb_cols[:, row_off + NA_WIN_ROWS - 1].transpose(0, 2, 1, 3)
        s_nb = jnp.einsum('bqhd,biqjhd->bhqij', q_r, k_nb).astype(jnp.float32) * NA_SCALE
        s_nb = (s_nb + bias[None].astype(jnp.float32)).reshape(B, H, GRID_W, wr * wc)
        s_ctx = jnp.einsum('bqhd,bshd->bhqs', q_r, k_ctx).astype(jnp.float32) * NA_SCALE
        pr = jax.nn.softmax(jnp.concatenate([s_nb, s_ctx], axis=-1), axis=-1).astype(v.dtype)
        p_nb = pr[..., :wr * wc].reshape(B, H, GRID_W, wr, wc)
        p_ctx = pr[..., wr * wc:]
        return (jnp.einsum('bhqij,biqjhd->bqhd', p_nb, v_nb)
                + jnp.einsum('bhqs,bshd->bqhd', p_ctx, v_ctx))

    o = lax.map(one_row, (jnp.arange(rows), qg))
    return o.swapaxes(0, 1).reshape(B, T, H, d)


def na_latent(h, cache_k, cache_v, p):
    B, T, _ = h.shape
    q, k, v = na_qkv(h, p)
    o = neighbourhood_attention(q, k, v, cache_k, cache_v, p['rpb'])
    return o.reshape(B, T, D_MODEL) @ p['w_o']


def swiglu(x, w_in, w_out):
    gu = x @ w_in
    g, u = jnp.split(gu, 2, axis=-1)
    return (jax.nn.silu(g) * u) @ w_out


def moe_ffn(h, p):
    B, T, D = h.shape
    xt = h.reshape(B * T, D)
    logits = (xt @ p['w_router']).astype(jnp.float32)
    top_v, top_i = lax.top_k(logits, TOP_K)
    w = jax.nn.softmax(top_v, axis=-1)
    gates = jnp.sum(jax.nn.one_hot(top_i, N_EXPERTS, dtype=jnp.float32) * w[..., None], axis=1)
    gates = gates.astype(h.dtype)
    out = jnp.zeros_like(xt)
    for e in range(N_EXPERTS):
        out = out + gates[:, e:e + 1] * swiglu(xt, p['w_in'][e], p['w_out'][e])
    return out.reshape(B, T, D)


def channel_mixer(l, h, p):
    if l % 2 == 0:
        return swiglu(h, p['w_in'], p['w_out'])
    return moe_ffn(h, p)


def context_layer(l, x, c_ctx, p):
    sh1, sc1, g1, sh2, sc2, g2 = ada_mod(c_ctx, p['w_ada'], p['b_ada'])
    h = x * (1 + sc1) + sh1
    if l % 2 == 0:
        out, st = mla_context(h, p['mix'])
    else:
        out, st = na_context(h, p['mix'])
    x = layer_norm(ALPHA * x + g1 * out, p['ln1_g'], p['ln1_b'])
    h = x * (1 + sc2) + sh2
    x = layer_norm(ALPHA * x + g2 * channel_mixer(l, h, p['ffn']), p['ln2_g'], p['ln2_b'])
    return x, st


def latent_layer(l, x, c, cache, p):
    sh1, sc1, g1, sh2, sc2, g2 = ada_mod(c, p['w_ada'], p['b_ada'])
    h = x * (1 + sc1) + sh1
    if l % 2 == 0:
        out = mla_latent(h, cache[0], cache[1], p['mix'])
    else:
        out = na_latent(h, cache[0], cache[1], p['mix'])
    x = layer_norm(ALPHA * x + g1 * out, p['ln1_g'], p['ln1_b'])
    h = x * (1 + sc2) + sh2
    x = layer_norm(ALPHA * x + g2 * channel_mixer(l, h, p['ffn']), p['ln2_g'], p['ln2_b'])
    return x


def setup_inputs(seed: int = 0) -> dict:
    key = jax.random.key(seed)
    ks = iter(jax.random.split(key, 48))

    def nrm(shape, scale):
        return jax.random.normal(next(ks), shape, jnp.float32) * scale

    D = D_MODEL
    return {
        'x_prompt': nrm((BATCH, SEQ, D), 1.0),
        'x_sample': nrm((DEC_BATCH, DEC_SEQ, D), 1.0),
        'cache_ckv_l0': nrm((DEC_BATCH, PAST_LEN, MLA_KV_LORA), 1.0),
        'cache_krope_l0': nrm((DEC_BATCH, PAST_LEN, MLA_ROPE), 1.0),
        'cache_k_l1': nrm((DEC_BATCH, PAST_LEN, NA_HEADS, NA_HD), 1.0),
        'cache_v_l1': nrm((DEC_BATCH, PAST_LEN, NA_HEADS, NA_HD), 1.0),
        'c': nrm((DEC_BATCH, D), 1.0),
        'c_ctx': nrm((D,), 1.0),
        'w_ada_l0': nrm((D, 6 * D), 0.5 * D ** -0.5),
        'b_ada_l0': nrm((6 * D,), 0.02),
        'mla_w_dq': nrm((D, MLA_Q_LORA), D ** -0.5),
        'mla_g_q': 1.0 + nrm((MLA_Q_LORA,), 0.02),
        'mla_w_uq': nrm((MLA_Q_LORA, MLA_HEADS * (MLA_NOPE + MLA_ROPE)), MLA_Q_LORA ** -0.5),
        'mla_w_dkv': nrm((D, MLA_KV_LORA + MLA_ROPE), D ** -0.5),
        'mla_g_kv': 1.0 + nrm((MLA_KV_LORA,), 0.02),
        'mla_w_ukv': nrm((MLA_KV_LORA, MLA_HEADS * (MLA_NOPE + MLA_V)), MLA_KV_LORA ** -0.5),
        'mla_w_o': nrm((MLA_HEADS * MLA_V, D), BETA * (MLA_HEADS * MLA_V) ** -0.5),
        'ln1_g_l0': 1.0 + nrm((D,), 0.02),
        'ln1_b_l0': nrm((D,), 0.02),
        'ffn_w_in': nrm((D, 2 * FFN_DENSE), D ** -0.5),
        'ffn_w_out': nrm((FFN_DENSE, D), BETA * FFN_DENSE ** -0.5),
        'ln2_g_l0': 1.0 + nrm((D,), 0.02),
        'ln2_b_l0': nrm((D,), 0.02),
        'w_ada_l1': nrm((D, 6 * D), 0.5 * D ** -0.5),
        'b_ada_l1': nrm((6 * D,), 0.02),
        'na_w_qkv': nrm((D, 3 * D), D ** -0.5),
        'na_rpb': nrm((NA_HEADS, 2 * NA_WIN_ROWS - 1, 2 * NA_WIN_COLS - 1), 0.1),
        'na_w_o': nrm((D, D), BETA * D ** -0.5),
        'ln1_g_l1': 1.0 + nrm((D,), 0.02),
        'ln1_b_l1': nrm((D,), 0.02),
        'moe_w_router': nrm((D, N_EXPERTS), D ** -0.5),
        'moe_w_in': nrm((N_EXPERTS, D, 2 * FFN_EXPERT), D ** -0.5),
        'moe_w_out': nrm((N_EXPERTS, FFN_EXPERT, D), BETA * FFN_EXPERT ** -0.5),
        'ln2_g_l1': 1.0 + nrm((D,), 0.02),
        'ln2_b_l1': nrm((D,), 0.02),
    }


def reference(x_prompt, x_sample, cache_ckv_l0, cache_krope_l0, cache_k_l1, cache_v_l1, c, c_ctx,
              w_ada_l0, b_ada_l0, mla_w_dq, mla_g_q, mla_w_uq, mla_w_dkv, mla_g_kv, mla_w_ukv, mla_w_o,
              ln1_g_l0, ln1_b_l0, ffn_w_in, ffn_w_out, ln2_g_l0, ln2_b_l0,
              w_ada_l1, b_ada_l1, na_w_qkv, na_rpb, na_w_o, ln1_g_l1, ln1_b_l1,
              moe_w_router, moe_w_in, moe_w_out, ln2_g_l1, ln2_b_l1):
    params = [
        dict(w_ada=w_ada_l0, b_ada=b_ada_l0,
             mix=dict(w_dq=mla_w_dq, g_q=mla_g_q, w_uq=mla_w_uq, w_dkv=mla_w_dkv,
                      g_kv=mla_g_kv, w_ukv=mla_w_ukv, w_o=mla_w_o),
             ln1_g=ln1_g_l0, ln1_b=ln1_b_l0,
             ffn=dict(w_in=ffn_w_in, w_out=ffn_w_out),
             ln2_g=ln2_g_l0, ln2_b=ln2_b_l0),
        dict(w_ada=w_ada_l1, b_ada=b_ada_l1,
             mix=dict(w_qkv=na_w_qkv, rpb=na_rpb, w_o=na_w_o),
             ln1_g=ln1_g_l1, ln1_b=ln1_b_l1,
             ffn=dict(w_router=moe_w_router, w_in=moe_w_in, w_out=moe_w_out),
             ln2_g=ln2_g_l1, ln2_b=ln2_b_l1),
    ]
    caches = [(cache_ckv_l0, cache_krope_l0), (cache_k_l1, cache_v_l1)]

    xp = x_prompt
    states = []
    for l in range(DEPTH):
        xp, st = context_layer(l, xp, c_ctx, params[l])
        states.append(st)
    new_ckv_l0, new_krope_l0 = states[0]
    new_k_l1, new_v_l1 = states[1]

    xs = x_sample
    for l in range(DEPTH):
        xs = latent_layer(l, xs, c, caches[l], params[l])

    return (xp, xs, new_ckv_l0, new_krope_l0, new_k_l1, new_v_l1)
```

```python
import functools
import math

import numpy as np
import jax
import jax.numpy as jnp
from jax import lax
from jax.experimental import pallas as pl
from jax.experimental.pallas import tpu as pltpu

F32 = jnp.float32
BF16 = jnp.bfloat16

D_MODEL = 1024
BATCH = 32
SEQ = 256
DEPTH = 2
DEC_BATCH = 2
DEC_SEQ = 2048
PAST_LEN = 512
GRID_W = 64
MLA_HEADS = 8
MLA_NOPE = 128
MLA_ROPE = 64
MLA_V = 128
MLA_Q_LORA = 512
MLA_KV_LORA = 256
MLA_SCALE = 1.0 / math.sqrt(MLA_NOPE + MLA_ROPE)
ROPE_THETA = 10000.0
NA_HEADS = 16
NA_HD = D_MODEL // NA_HEADS
NA_WIN_ROWS = 8
NA_WIN_COLS = 16
NA_SCALE = 1.0 / math.sqrt(NA_HD)
FFN_DENSE = 2816
N_EXPERTS = 8
TOP_K = 2
FFN_EXPERT = 3584
ALPHA = (2 * DEPTH) ** 0.25
LN_EPS = 1e-5
RMS_EPS = 1e-6

N_CTX = BATCH * SEQ
N_LAT = DEC_BATCH * DEC_SEQ
N_TOK = N_CTX + N_LAT
GRID_ROWS = DEC_SEQ // GRID_W
Q_HEAD_PAD = 256
LANES = 128
NEG_BIG = -1e30

VMEM_LIMIT = 56 * 1024 * 1024

TM_PROJ = 512
TM_FFN = 512
TF_DENSE = 1408
TM_MOE = 256
TF_MOE = 896
N_PAIRS = N_TOK * TOP_K
NT_MOE = (N_PAIRS + N_EXPERTS * TM_MOE) // TM_MOE
ROWS_SORTED = NT_MOE * TM_MOE
TM_ROUTE = 256


def _cparams(sem, vmem=VMEM_LIMIT):
    return pltpu.CompilerParams(dimension_semantics=sem, vmem_limit_bytes=vmem)


def _silu(x):
    return x * jax.nn.sigmoid(x)


def _layer_norm(y, g, b):
    mu = jnp.mean(y, axis=-1, keepdims=True)
    d = y - mu
    var = jnp.mean(d * d, axis=-1, keepdims=True)
    return d * lax.rsqrt(var + LN_EPS) * g + b


def _rms_norm(y, g):
    return y * lax.rsqrt(jnp.mean(y * y, axis=-1, keepdims=True) + RMS_EPS) * g


def _dot(a, b):
    return jnp.dot(a, b, preferred_element_type=F32)


def _dot_nt(a, b):
    return lax.dot_general(a, b, (((1,), (1,)), ((), ())), preferred_element_type=F32)


def _ada_kernel(c_ref, w_ref, b_ref, o_ref):
    s = _silu(c_ref[...]).astype(BF16)
    o_ref[...] = _dot(s, w_ref[...].astype(BF16)) + b_ref[...]


def _ada_mod(cvecs, w, b):
    tn = 1536
    m = pl.pallas_call(
        _ada_kernel,
        grid=(6 * D_MODEL // tn,),
        in_specs=[pl.BlockSpec((8, D_MODEL), lambda j: (0, 0)),
                  pl.BlockSpec((D_MODEL, tn), lambda j: (0, j)),
                  pl.BlockSpec((1, tn), lambda j: (0, j))],
        out_specs=pl.BlockSpec((8, tn), lambda j: (0, j)),
        out_shape=jax.ShapeDtypeStruct((8, 6 * D_MODEL), F32),
        compiler_params=_cparams(("arbitrary",)),
        name="ada_mod",
    )(cvecs, w, b.reshape(1, -1))
    return m[:3].reshape(3 * 6, 1, D_MODEL)


def _mod_spec(j, row_fn):
    return pl.BlockSpec((None, 1, D_MODEL), lambda i, *_: (row_fn(i) * 6 + j, 0, 0))


def _row_fn(latent, tm):
    if not latent:
        return lambda i: 0
    per = DEC_SEQ // tm
    return lambda i: 1 + i // per


def _const_spec(shape):
    nd = len(shape)
    return pl.BlockSpec(shape, lambda *_: (0,) * nd)


def _rope_kernel(invf_ref, o_ref):
    i = pl.program_id(0)
    tm = o_ref.shape[1]
    t = i * tm + lax.broadcasted_iota(jnp.int32, (tm, LANES), 0)
    lane = lax.broadcasted_iota(jnp.int32, (tm, LANES), 1)
    row = t >> int(math.log2(GRID_W))
    col = t & (GRID_W - 1)
    pos = jnp.where(lane < MLA_ROPE // 2, row, col).astype(F32)
    ang = pos * invf_ref[...]
    cos = jnp.cos(ang)
    sin = jnp.sin(ang)
    unit = lane >> int(math.log2(MLA_ROPE // 4))
    first = (unit == 0) | (unit == 2)
    second = (unit == 1) | (unit == 3)
    o_ref[0] = jnp.where(lane < MLA_ROPE, cos, 0.0)
    o_ref[1] = jnp.where(first, -sin, 0.0)
    o_ref[2] = jnp.where(second, sin, 0.0)


def _rope_tables():
    half = MLA_ROPE // 2
    inv_freq = (1.0 / (ROPE_THETA ** (np.arange(0, half, 2, dtype=np.float32) / half))).astype(np.float32)
    lane_f = np.zeros((1, LANES), np.float32)
    lane_f[0, :MLA_ROPE] = np.tile(inv_freq, 4)
    tm = 256
    return pl.pallas_call(
        _rope_kernel,
        grid=(DEC_SEQ // tm,),
        in_specs=[_const_spec((1, LANES))],
        out_specs=pl.BlockSpec((3, tm, LANES), lambda i: (0, i, 0)),
        out_shape=jax.ShapeDtypeStruct((3, DEC_SEQ, LANES), F32),
        compiler_params=_cparams(("arbitrary",)),
        name="rope_tables",
    )(jnp.asarray(lane_f))


def _rotate(v, tab_ref):
    return (v * tab_ref[0] + pltpu.roll(v, LANES - MLA_ROPE // 4, 1) * tab_ref[1]
            + pltpu.roll(v, MLA_ROPE // 4, 1) * tab_ref[2])


def _mla_proj_kernel(*refs, rope):
    if rope:
        (x_ref, sc_ref, sh_ref, wa_ref, gq_ref, gkv_ref, wuq_ref, wukv_ref, tab_ref,
         q_ref, kvx_ref, ckv_ref, kr_ref, krp_ref) = refs
    else:
        (x_ref, sc_ref, sh_ref, wa_ref, gq_ref, gkv_ref, wuq_ref, wukv_ref,
         q_ref, kvx_ref, ckv_ref, kr_ref, krp_ref) = refs
    h = (x_ref[...] * (1.0 + sc_ref[...]) + sh_ref[...]).astype(BF16)
    t = _dot(h, wa_ref[...])
    cq = _rms_norm(t[:, :MLA_Q_LORA], gq_ref[...])
    ckv = _rms_norm(t[:, MLA_Q_LORA:MLA_Q_LORA + MLA_KV_LORA], gkv_ref[...])
    kr = t[:, MLA_Q_LORA + MLA_KV_LORA:]
    ckv_ref[...] = ckv
    kr_ref[...] = kr
    kvx_ref[...] = _dot(ckv.astype(BF16), wukv_ref[...]).astype(BF16)
    q = _dot(cq.astype(BF16), wuq_ref[...])
    krp = jnp.concatenate([kr, jnp.zeros_like(kr)], axis=-1)
    if rope:
        krp_ref[...] = _rotate(krp, tab_ref).astype(BF16)
        for hd in range(MLA_HEADS):
            lo = hd * Q_HEAD_PAD
            q_ref[:, lo:lo + MLA_NOPE] = q[:, lo:lo + MLA_NOPE].astype(BF16)
            q_ref[:, lo + MLA_NOPE:lo + Q_HEAD_PAD] = _rotate(q[:, lo + MLA_NOPE:lo + Q_HEAD_PAD], tab_ref).astype(BF16)
    else:
        krp_ref[...] = krp.astype(BF16)
        q_ref[...] = q.astype(BF16)


def _mla_proj(x, mod, wa, gq, gkv, wuq, wukv, tab, latent):
    n = x.shape[0]
    tm = TM_PROJ
    rf = _row_fn(latent, tm)
    tok = lambda w: pl.BlockSpec((tm, w), lambda i: (i, 0))
    in_specs = [tok(D_MODEL), _mod_spec(1, rf), _mod_spec(0, rf),
                _const_spec(wa.shape), _const_spec(gq.shape), _const_spec(gkv.shape),
                _const_spec(wuq.shape), _const_spec(wukv.shape)]
    args = [x, mod, mod, wa, gq, gkv, wuq, wukv]
    if latent:
        per = DEC_SEQ // tm
        in_specs.append(pl.BlockSpec((3, tm, LANES), lambda i: (0, i % per, 0)))
        args.append(tab)
    return pl.pallas_call(
        functools.partial(_mla_proj_kernel, rope=latent),
        grid=(n // tm,),
        in_specs=in_specs,
        out_specs=[tok(MLA_HEADS * Q_HEAD_PAD), tok(MLA_HEADS * (MLA_NOPE + MLA_V)),
                   tok(MLA_KV_LORA), tok(MLA_ROPE), tok(LANES)],
        out_shape=[jax.ShapeDtypeStruct((n, MLA_HEADS * Q_HEAD_PAD), BF16),
                   jax.ShapeDtypeStruct((n, MLA_HEADS * (MLA_NOPE + MLA_V)), BF16),
                   jax.ShapeDtypeStruct((n, MLA_KV_LORA), F32),
                   jax.ShapeDtypeStruct((n, MLA_ROPE), F32),
                   jax.ShapeDtypeStruct((n, LANES), BF16)],
        compiler_params=_cparams(("arbitrary",)),
        name="mla_proj_lat" if latent else "mla_proj_ctx",
    )(*args)


def _matmul_kernel(a_ref, b_ref, o_ref):
    o_ref[...] = _dot(a_ref[...].astype(BF16), b_ref[...]).astype(o_ref.dtype)


def _matmul(a, b, out_dtype, tm, name):
    m, k = a.shape
    n = b.shape[1]
    return pl.pallas_call(
        _matmul_kernel,
        grid=(m // tm,),
        in_specs=[pl.BlockSpec((tm, k), lambda i: (i, 0)), _const_spec(b.shape)],
        out_specs=pl.BlockSpec((tm, n), lambda i: (i, 0)),
        out_shape=jax.ShapeDtypeStruct((m, n), out_dtype),
        compiler_params=_cparams(("arbitrary",)),
        name=name,
    )(a, b)


def _mla_head(qh, kparts, vparts):
    ss = [_dot_nt(qh, k) * MLA_SCALE for k in kparts]
    m = functools.reduce(jnp.maximum, [jnp.max(s, axis=-1, keepdims=True) for s in ss])
    ps = [jnp.exp(s - m) for s in ss]
    l = functools.reduce(jnp.add, [jnp.sum(p, axis=-1, keepdims=True) for p in ps])
    o = functools.reduce(jnp.add, [_dot(p.astype(BF16), v) for p, v in zip(ps, vparts)])
    return o / l


def _mla_attn_ctx_kernel(q_ref, kvx_ref, krp_ref, o_ref):
    krp = krp_ref[...]
    for hd in range(MLA_HEADS):
        lo = hd * (MLA_NOPE + MLA_V)
        qh = q_ref[:, hd * Q_HEAD_PAD:(hd + 1) * Q_HEAD_PAD]
        kh = jnp.concatenate([kvx_ref[:, lo:lo + MLA_NOPE], krp], axis=-1)
        vh = kvx_ref[:, lo + MLA_NOPE:lo + MLA_NOPE + MLA_V]
        o_ref[:, hd * MLA_V:(hd + 1) * MLA_V] = _mla_head(qh, [kh], [vh]).astype(BF16)


def _mla_attn_ctx(q, kvx, krp):
    tok = lambda w: pl.BlockSpec((SEQ, w), lambda b: (b, 0))
    return pl.pallas_call(
        _mla_attn_ctx_kernel,
        grid=(BATCH,),
        in_specs=[tok(q.shape[1]), tok(kvx.shape[1]), tok(LANES)],
        out_specs=tok(MLA_HEADS * MLA_V),
        out_shape=jax.ShapeDtypeStruct((N_CTX, MLA_HEADS * MLA_V), BF16),
        compiler_params=_cparams(("arbitrary",)),
        name="mla_attn_ctx",
    )(q, kvx, krp)


def _mla_attn_lat_kernel(q_ref, kvl_ref, krl_ref, kvc_ref, krc_ref, o_ref):
    krl = krl_ref[...]
    krc = krc_ref[...]
    for hd in range(MLA_HEADS):
        lo = hd * (MLA_NOPE + MLA_V)
        qh = q_ref[:, hd * Q_HEAD_PAD:(hd + 1) * Q_HEAD_PAD]
        kl = jnp.concatenate([kvl_ref[:, lo:lo + MLA_NOPE], krl], axis=-1)
        kc = jnp.concatenate([kvc_ref[:, lo:lo + MLA_NOPE], krc], axis=-1)
        vl = kvl_ref[:, lo + MLA_NOPE:lo + MLA_NOPE + MLA_V]
        vc = kvc_ref[:, lo + MLA_NOPE:lo + MLA_NOPE + MLA_V]
        o_ref[:, hd * MLA_V:(hd + 1) * MLA_V] = _mla_head(qh, [kl, kc], [vl, vc]).astype(BF16)


def _mla_attn_lat(q, kvx_lat, krp_lat, kvx_ctx, krp_ctx):
    tq = 256
    per = DEC_SEQ // tq
    wkv = kvx_lat.shape[-1]
    return pl.pallas_call(
        _mla_attn_lat_kernel,
        grid=(DEC_BATCH, per),
        in_specs=[pl.BlockSpec((tq, q.shape[1]), lambda b, i: (b * per + i, 0)),
                  pl.BlockSpec((None, DEC_SEQ, wkv), lambda b, i: (b, 0, 0)),
                  pl.BlockSpec((None, DEC_SEQ, LANES), lambda b, i: (b, 0, 0)),
                  pl.BlockSpec((None, PAST_LEN, wkv), lambda b, i: (b, 0, 0)),
                  pl.BlockSpec((None, PAST_LEN, LANES), lambda b, i: (b, 0, 0))],
        out_specs=pl.BlockSpec((tq, MLA_HEADS * MLA_V), lambda b, i: (b * per + i, 0)),
        out_shape=jax.ShapeDtypeStruct((N_LAT, MLA_HEADS * MLA_V), BF16),
        compiler_params=_cparams(("arbitrary", "arbitrary")),
        name="mla_attn_lat",
    )(q, kvx_lat.reshape(DEC_BATCH, DEC_SEQ, wkv), krp_lat.reshape(DEC_BATCH, DEC_SEQ, LANES), kvx_ctx, krp_ctx)


def _proj_ln_kernel(o_ref, x_ref, wo_ref, g_ref, lg_ref, lb_ref, x1_ref):
    out = _dot(o_ref[...], wo_ref[...])
    x1_ref[...] = _layer_norm(ALPHA * x_ref[...] + g_ref[...] * out, lg_ref[...], lb_ref[...])


def _top2(logits):
    lane = lax.broadcasted_iota(jnp.int32, logits.shape, 1).astype(F32)
    m1 = jnp.max(logits, axis=-1, keepdims=True)
    i1 = jnp.min(jnp.where(logits == m1, lane, float(LANES)), axis=-1, keepdims=True)
    rest = jnp.where(lane == i1, -jnp.inf, logits)
    m2 = jnp.max(rest, axis=-1, keepdims=True)
    i2 = jnp.min(jnp.where(rest == m2, lane, float(LANES)), axis=-1, keepdims=True)
    e = jnp.exp(m2 - m1)
    w1 = 1.0 / (1.0 + e)
    w2 = e / (1.0 + e)
    return lane, i1, i2, w1, w2


def _proj_ln_route_kernel(oc_ref, xc_ref, ol_ref, xl_ref, wo_ref, g_ref, lg_ref, lb_ref, sc_ref, sh_ref, wr_ref,
                          x1_ref, hm_ref, rt_ref, *, n_ctx_tiles):
    def body(o_ref, x_ref):
        out = _dot(o_ref[...], wo_ref[...])
        x1 = _layer_norm(ALPHA * x_ref[...] + g_ref[...] * out, lg_ref[...], lb_ref[...])
        x1_ref[...] = x1
        hm = x1 * (1.0 + sc_ref[...]) + sh_ref[...]
        hm_ref[...] = hm
        logits = _dot(hm.astype(BF16), wr_ref[...])
        lane = lax.broadcasted_iota(jnp.int32, logits.shape, 1)
        logits = jnp.where(lane < N_EXPERTS, logits, -jnp.inf)
        _, i1, i2, w1, w2 = _top2(logits)
        rt_ref[...] = jnp.where(lane == 0, i1, jnp.where(lane == 1, i2, jnp.where(lane == 2, w1, jnp.where(lane == 3, w2, 0.0))))

    is_ctx = pl.program_id(0) < n_ctx_tiles
    pl.when(is_ctx)(lambda: body(oc_ref, xc_ref))
    pl.when(jnp.logical_not(is_ctx))(lambda: body(ol_ref, xl_ref))


def _proj_ln(o, x, wo, mod, lg, lb, latent):
    n = x.shape[0]
    tm = TM_PROJ
    rf = _row_fn(latent, tm)
    tok = lambda w: pl.BlockSpec((tm, w), lambda i: (i, 0))
    return pl.pallas_call(
        _proj_ln_kernel,
        grid=(n // tm,),
        in_specs=[tok(o.shape[1]), tok(D_MODEL), _const_spec(wo.shape), _mod_spec(2, rf),
                  _const_spec(lg.shape), _const_spec(lb.shape)],
        out_specs=tok(D_MODEL),
        out_shape=jax.ShapeDtypeStruct((n, D_MODEL), F32),
        compiler_params=_cparams(("arbitrary",)),
        name="proj_ln_lat" if latent else "proj_ln_ctx",
    )(o, x, wo, mod, lg, lb)


def _all_row_fn(tm):
    nc = N_CTX // tm
    per = DEC_SEQ // tm
    return lambda i: jnp.where(i < nc, 0, 1 + (i - nc) // per)


def _proj_ln_route(o_ctx, x_ctx, o_lat, x_lat, wo, mod, lg, lb, wr):
    tm = TM_PROJ
    nc = N_CTX // tm
    rf = _all_row_fn(tm)
    ctx = lambda w: pl.BlockSpec((tm, w), lambda i: (jnp.minimum(i, nc - 1), 0))
    lat = lambda w: pl.BlockSpec((tm, w), lambda i: (jnp.maximum(i - nc, 0), 0))
    tok = lambda w: pl.BlockSpec((tm, w), lambda i: (i, 0))
    return pl.pallas_call(
        functools.partial(_proj_ln_route_kernel, n_ctx_tiles=nc),
        grid=(N_TOK // tm,),
        in_specs=[ctx(D_MODEL), ctx(D_MODEL), lat(D_MODEL), lat(D_MODEL), _const_spec(wo.shape), _mod_spec(2, rf),
                  _const_spec(lg.shape), _const_spec(lb.shape), _mod_spec(4, rf), _mod_spec(3, rf),
                  _const_spec(wr.shape)],
        out_specs=[tok(D_MODEL), tok(D_MODEL), tok(LANES)],
        out_shape=[jax.ShapeDtypeStruct((N_TOK, D_MODEL), F32),
                   jax.ShapeDtypeStruct((N_TOK, D_MODEL), F32),
                   jax.ShapeDtypeStruct((N_TOK, LANES), F32)],
        compiler_params=_cparams(("arbitrary",)),
        name="proj_ln_route",
    )(o_ctx, x_ctx, o_lat, x_lat, wo, mod, lg, lb, mod, mod, wr)


def _ffn_dense_kernel(x_ref, sc_ref, sh_ref, g_ref, wg_ref, wu_ref, wd_ref, lg_ref, lb_ref,
                      o_ref, h_scr, acc_scr):
    f = pl.program_id(1)

    @pl.when(f == 0)
    def _():
        h_scr[...] = (x_ref[...] * (1.0 + sc_ref[...]) + sh_ref[...]).astype(BF16)
        acc_scr[...] = jnp.zeros_like(acc_scr)

    h = h_scr[...]
    a = (_silu(_dot(h, wg_ref[...])) * _dot(h, wu_ref[...])).astype(BF16)
    acc_scr[...] += _dot(a, wd_ref[...])

    @pl.when(f == pl.num_programs(1) - 1)
    def _():
        o_ref[...] = _layer_norm(ALPHA * x_ref[...] + g_ref[...] * acc_scr[...], lg_ref[...], lb_ref[...])


def _ffn_dense(x, mod, w_in, w_out, lg, lb, latent):
    n = x.shape[0]
    tm, tf = TM_FFN, TF_DENSE
    nf = FFN_DENSE // tf
    rf = _row_fn(latent, tm)
    mod_spec = lambda j: pl.BlockSpec((None, 1, D_MODEL), lambda i, f: (rf(i) * 6 + j, 0, 0))
    tok = pl.BlockSpec((tm, D_MODEL), lambda i, f: (i, 0))
    return pl.pallas_call(
        _ffn_dense_kernel,
        grid=(n // tm, nf),
        in_specs=[tok, mod_spec(4), mod_spec(3), mod_spec(5),
                  pl.BlockSpec((D_MODEL, tf), lambda i, f: (0, f)),
                  pl.BlockSpec((D_MODEL, tf), lambda i, f: (0, nf + f)),
                  pl.BlockSpec((tf, D_MODEL), lambda i, f: (f, 0)),
                  pl.BlockSpec((1, D_MODEL), lambda i, f: (0, 0)),
                  pl.BlockSpec((1, D_MODEL), lambda i, f: (0, 0))],
        out_specs=tok,
        out_shape=jax.ShapeDtypeStruct((n, D_MODEL), F32),
        scratch_shapes=[pltpu.VMEM((tm, D_MODEL), BF16), pltpu.VMEM((tm, D_MODEL), F32)],
        compiler_params=_cparams(("arbitrary", "arbitrary")),
        name="ffn_dense_lat" if latent else "ffn_dense_ctx",
    )(x, mod, mod, mod, w_in, w_in, w_out, lg, lb)


def _na_qkv_kernel(x_ref, sc_ref, sh_ref, w_ref, q_ref, k_ref, v_ref):
    h = (x_ref[...] * (1.0 + sc_ref[...]) + sh_ref[...]).astype(BF16)
    qkv = _dot(h, w_ref[...])
    q_ref[...] = qkv[:, :D_MODEL].astype(q_ref.dtype)
    k_ref[...] = qkv[:, D_MODEL:2 * D_MODEL].astype(k_ref.dtype)
    v_ref[...] = qkv[:, 2 * D_MODEL:].astype(v_ref.dtype)


def _na_qkv(x, mod, w, kv_dtype, latent):
    n = x.shape[0]
    tm = TM_PROJ
    rf = _row_fn(latent, tm)
    tok = pl.BlockSpec((tm, D_MODEL), lambda i: (i, 0))
    return pl.pallas_call(
        _na_qkv_kernel,
        grid=(n // tm,),
        in_specs=[tok, _mod_spec(1, rf), _mod_spec(0, rf), _const_spec(w.shape)],
        out_specs=[tok, tok, tok],
        out_shape=[jax.ShapeDtypeStruct((n, D_MODEL), BF16),
                   jax.ShapeDtypeStruct((n, D_MODEL), kv_dtype),
                   jax.ShapeDtypeStruct((n, D_MODEL), kv_dtype)],
        compiler_params=_cparams(("arbitrary",)),
        name="na_qkv_lat" if latent else "na_qkv_ctx",
    )(x, mod, mod, w)


def _na_attn_ctx_kernel(q_ref, k_ref, v_ref, o_ref):
    for hd in range(NA_HEADS):
        sl = slice(hd * NA_HD, (hd + 1) * NA_HD)
        s = _dot_nt(q_ref[:, sl], k_ref[:, sl].astype(BF16)) * NA_SCALE
        m = jnp.max(s, axis=-1, keepdims=True)
        p = jnp.exp(s - m)
        l = jnp.sum(p, axis=-1, keepdims=True)
        o = _dot(p.astype(BF16), v_ref[:, sl].astype(BF16)) / l
        o_ref[:, sl] = o.astype(BF16)


def _na_attn_ctx(q, k, v):
    tok = pl.BlockSpec((SEQ, D_MODEL), lambda b: (b, 0))
    return pl.pallas_call(
        _na_attn_ctx_kernel,
        grid=(BATCH,),
        in_specs=[tok, tok, tok],
        out_specs=tok,
        out_shape=jax.ShapeDtypeStruct((N_CTX, D_MODEL), BF16),
        compiler_params=_cparams(("arbitrary",)),
        name="na_attn_ctx",
    )(q, k, v)


def _na_row_start(r):
    return jnp.clip(r - NA_WIN_ROWS // 2, 0, GRID_ROWS - NA_WIN_ROWS)


def _na_attn_lat_kernel(q_ref, k_ref, v_ref, kc_ref, vc_ref, bias_ref, o_ref):
    r = pl.program_id(1)
    start = pl.multiple_of(_na_row_start(r) * GRID_W, GRID_W)
    win = pl.ds(start, NA_WIN_ROWS * GRID_W)
    for hd in range(NA_HEADS):
        sl = slice(hd * NA_HD, (hd + 1) * NA_HD)
        qh = q_ref[:, sl]
        s_nb = _dot_nt(qh, k_ref[win, sl]) * NA_SCALE + bias_ref[hd]
        s_cx = _dot_nt(qh, kc_ref[:, sl]) * NA_SCALE
        m = jnp.maximum(jnp.max(s_nb, axis=-1, keepdims=True), jnp.max(s_cx, axis=-1, keepdims=True))
        p_nb = jnp.exp(s_nb - m)
        p_cx = jnp.exp(s_cx - m)
        l = jnp.sum(p_nb, axis=-1, keepdims=True) + jnp.sum(p_cx, axis=-1, keepdims=True)
        o = (_dot(p_nb.astype(BF16), v_ref[win, sl]) + _dot(p_cx.astype(BF16), vc_ref[:, sl])) / l
        o_ref[:, sl] = o.astype(BF16)


def _na_bias_table(rpb):
    s = np.arange(NA_WIN_ROWS)[:, None]
    j = np.arange(NA_WIN_ROWS)[None, :]
    ridx = j - s + NA_WIN_ROWS - 1
    c = np.arange(GRID_W)[:, None]
    kc = np.arange(GRID_W)[None, :]
    cs = np.clip(c - NA_WIN_COLS // 2, 0, GRID_W - NA_WIN_COLS)
    valid = (kc >= cs) & (kc < cs + NA_WIN_COLS)
    cidx = np.clip(kc - c + NA_WIN_COLS - 1, 0, 2 * NA_WIN_COLS - 2)
    tab = rpb[:, ridx[:, :, None, None], cidx[None, None, :, :]]
    tab = jnp.where(jnp.asarray(valid)[None, None, None], tab, NEG_BIG)
    return tab.transpose(0, 1, 3, 2, 4).reshape(NA_HEADS, NA_WIN_ROWS, GRID_W, NA_WIN_ROWS * GRID_W)


def _na_attn_lat(q, k, v, kc, vc, bias_tab):
    row = pl.BlockSpec((GRID_W, D_MODEL), lambda b, r: (b * GRID_ROWS + r, 0))
    full = lambda s: pl.BlockSpec((None, s, D_MODEL), lambda b, r: (b, 0, 0))
    bias_spec = pl.BlockSpec((NA_HEADS, None, GRID_W, NA_WIN_ROWS * GRID_W),
                             lambda b, r: (0, r - _na_row_start(r), 0, 0))
    return pl.pallas_call(
        _na_attn_lat_kernel,
        grid=(DEC_BATCH, GRID_ROWS),
        in_specs=[row, full(DEC_SEQ), full(DEC_SEQ), full(PAST_LEN), full(PAST_LEN), bias_spec],
        out_specs=row,
        out_shape=jax.ShapeDtypeStruct((N_LAT, D_MODEL), BF16),
        compiler_params=_cparams(("arbitrary", "arbitrary")),
        name="na_attn_lat",
    )(q, k.reshape(DEC_BATCH, DEC_SEQ, D_MODEL), v.reshape(DEC_BATCH, DEC_SEQ, D_MODEL), kc, vc, bias_tab)


def _route_rank_kernel(rt_ref, rank_ref, cnt_ref, carry):
    i = pl.program_id(0)

    @pl.when(i == 0)
    def _():
        carry[...] = jnp.zeros_like(carry)

    rt = rt_ref[...]
    tm = rt.shape[0]
    lane = lax.broadcasted_iota(jnp.int32, (tm, LANES), 1).astype(F32)
    oh1 = (lane == rt[:, 0:1]).astype(F32)
    oh2 = (lane == rt[:, 1:2]).astype(F32)
    rr = lax.broadcasted_iota(jnp.int32, (tm, tm), 0)
    cc = lax.broadcasted_iota(jnp.int32, (tm, tm), 1)
    below = (cc < rr).astype(BF16)
    tot1 = jnp.sum(oh1, axis=0, keepdims=True)
    tot2 = jnp.sum(oh2, axis=0, keepdims=True)
    base = carry[...]
    cum1 = _dot(below, oh1.astype(BF16)) + base
    cum2 = _dot(below, oh2.astype(BF16)) + base + tot1
    rank1 = jnp.sum(oh1 * cum1, axis=-1, keepdims=True)
    rank2 = jnp.sum(oh2 * cum2, axis=-1, keepdims=True)
    lane_i = lax.broadcasted_iota(jnp.int32, (tm, LANES), 1)
    rank_ref[...] = jnp.where(lane_i == 0, rank1, jnp.where(lane_i == 1, rank2, 0.0))
    carry[...] = base + tot1 + tot2
    cnt_ref[...] = jnp.broadcast_to(carry[...], cnt_ref.shape)


def _route_rank(rt_all):
    tm = TM_ROUTE
    return pl.pallas_call(
        _route_rank_kernel,
        grid=(N_TOK // tm,),
        in_specs=[pl.BlockSpec((tm, LANES), lambda i: (i, 0))],
        out_specs=[pl.BlockSpec((tm, LANES), lambda i: (i, 0)), pl.BlockSpec((8, LANES), lambda i: (0, 0))],
        out_shape=[jax.ShapeDtypeStruct((N_TOK, LANES), F32), jax.ShapeDtypeStruct((8, LANES), F32)],
        scratch_shapes=[pltpu.VMEM((1, LANES), F32)],
        compiler_params=_cparams(("arbitrary",)),
        name="route_rank",
    )(rt_all)


def _moe_scatter_kernel(pos_ref, h_ref, xs_in, xs_ref, sem):
    del xs_in
    i = pl.program_id(0)
    tm = h_ref.shape[0]

    def copies(t):
        src = h_ref.at[pl.ds(t, 1)]
        p0 = pos_ref[i * tm + t]
        p1 = pos_ref[N_TOK + i * tm + t]
        return (pltpu.make_async_copy(src, xs_ref.at[pl.ds(p0, 1)], sem),
                pltpu.make_async_copy(src, xs_ref.at[pl.ds(p1, 1)], sem))

    def start(t, c):
        for cp in copies(t):
            cp.start()
        return c

    def wait(t, c):
        for cp in copies(t):
            cp.wait()
        return c

    lax.fori_loop(0, tm, start, 0)
    lax.fori_loop(0, tm, wait, 0)


def _moe_scatter(pos, hm_all):
    tm = TM_ROUTE
    zeros = jnp.zeros((ROWS_SORTED, D_MODEL), F32)
    return pl.pallas_call(
        _moe_scatter_kernel,
        grid_spec=pltpu.PrefetchScalarGridSpec(
            num_scalar_prefetch=1,
            grid=(N_TOK // tm,),
            in_specs=[pl.BlockSpec((tm, D_MODEL), lambda i, pos: (i, 0)),
                      pl.BlockSpec(memory_space=pl.ANY)],
            out_specs=pl.BlockSpec(memory_space=pl.ANY),
            scratch_shapes=[pltpu.SemaphoreType.DMA(())],
        ),
        out_shape=jax.ShapeDtypeStruct((ROWS_SORTED, D_MODEL), F32),
        input_output_aliases={2: 0},
        compiler_params=_cparams(("arbitrary",)),
        name="moe_scatter",
    )(pos, hm_all, zeros)


def _tile_changed(te_ref, j):
    prev = te_ref[jnp.maximum(j - 1, 0)]
    return (j == 0) | (te_ref[j] != prev)


def _moe_up_kernel(te_ref, na_ref, x_ref, wg_ref, wu_ref, o_ref, wg_scr, wu_scr):
    j = pl.program_id(1)

    @pl.when(j < na_ref[0])
    def _():
        @pl.when(_tile_changed(te_ref, j))
        def _():
            wg_scr[...] = wg_ref[...].astype(BF16)
            wu_scr[...] = wu_ref[...].astype(BF16)

        x = x_ref[...].astype(BF16)
        o_ref[...] = (_silu(_dot(x, wg_scr[...])) * _dot(x, wu_scr[...])).astype(BF16)

    @pl.when(j >= na_ref[0])
    def _():
        o_ref[...] = jnp.zeros_like(o_ref)


def _moe_up(te, na, xs, w_in):
    tm, tf = TM_MOE, TF_MOE
    nf = FFN_EXPERT // tf
    row = lambda j, na: jnp.minimum(j, na[0] - 1)
    return pl.pallas_call(
        _moe_up_kernel,
        grid_spec=pltpu.PrefetchScalarGridSpec(
            num_scalar_prefetch=2,
            grid=(nf, NT_MOE),
            in_specs=[pl.BlockSpec((tm, D_MODEL), lambda f, j, te, na: (row(j, na), 0)),
                      pl.BlockSpec((None, D_MODEL, tf), lambda f, j, te, na: (te[j], 0, f)),
                      pl.BlockSpec((None, D_MODEL, tf), lambda f, j, te, na: (te[j], 0, nf + f))],
            out_specs=pl.BlockSpec((tm, tf), lambda f, j, te, na: (j, f)),
            scratch_shapes=[pltpu.VMEM((D_MODEL, tf), BF16), pltpu.VMEM((D_MODEL, tf), BF16)],
        ),
        out_shape=jax.ShapeDtypeStruct((ROWS_SORTED, FFN_EXPERT), BF16),
        compiler_params=_cparams(("arbitrary", "arbitrary")),
        name="moe_up",
    )(te, na, xs, w_in, w_in)


def _moe_down_kernel(te_ref, na_ref, h_ref, w_ref, o_ref, w_scr):
    j = pl.program_id(1)

    @pl.when(j < na_ref[0])
    def _():
        @pl.when(_tile_changed(te_ref, j))
        def _():
            w_scr[...] = w_ref[...].astype(BF16)

        o_ref[...] = _dot(h_ref[...], w_scr[...])

    @pl.when(j >= na_ref[0])
    def _():
        o_ref[...] = jnp.zeros_like(o_ref)


def _moe_down(te, na, hmid, w_out):
    tm = TM_MOE
    tn = D_MODEL // 2
    row = lambda j, na: jnp.minimum(j, na[0] - 1)
    return pl.pallas_call(
        _moe_down_kernel,
        grid_spec=pltpu.PrefetchScalarGridSpec(
            num_scalar_prefetch=2,
            grid=(D_MODEL // tn, NT_MOE),
            in_specs=[pl.BlockSpec((tm, FFN_EXPERT), lambda n, j, te, na: (row(j, na), 0)),
                      pl.BlockSpec((None, FFN_EXPERT, tn), lambda n, j, te, na: (te[j], 0, n))],
            out_specs=pl.BlockSpec((tm, tn), lambda n, j, te, na: (j, n)),
            scratch_shapes=[pltpu.VMEM((FFN_EXPERT, tn), BF16)],
        ),
        out_shape=jax.ShapeDtypeStruct((ROWS_SORTED, D_MODEL), F32),
        compiler_params=_cparams(("arbitrary", "arbitrary")),
        name="moe_down",
    )(te, na, hmid, w_out)


def _moe_combine_kernel(pos_ref, y_ref, rt_ref, x_ref, g_ref, lg_ref, lb_ref, o_ref, ybuf, sem, *, tok_off):
    i = pl.program_id(0)
    tm = x_ref.shape[0]

    def copies(t):
        p0 = pos_ref[tok_off + i * tm + t]
        p1 = pos_ref[N_TOK + tok_off + i * tm + t]
        return (pltpu.make_async_copy(y_ref.at[pl.ds(p0, 1)], ybuf.at[0, pl.ds(t, 1)], sem),
                pltpu.make_async_copy(y_ref.at[pl.ds(p1, 1)], ybuf.at[1, pl.ds(t, 1)], sem))

    def start(t, c):
        for cp in copies(t):
            cp.start()
        return c

    def wait(t, c):
        for cp in copies(t):
            cp.wait()
        return c

    lax.fori_loop(0, tm, start, 0)
    lax.fori_loop(0, tm, wait, 0)
    rt = rt_ref[...]
    moe = rt[:, 2:3] * ybuf[0] + rt[:, 3:4] * ybuf[1]
    o_ref[...] = _layer_norm(ALPHA * x_ref[...] + g_ref[...] * moe, lg_ref[...], lb_ref[...])


def _moe_combine(pos, y, rt_all, x_all, mod, lg, lb, latent):
    n = N_LAT if latent else N_CTX
    tm = TM_ROUTE
    rf = _row_fn(latent, tm)
    tok_off = N_CTX if latent else 0
    off = tok_off // tm
    return pl.pallas_call(
        functools.partial(_moe_combine_kernel, tok_off=tok_off),
        grid_spec=pltpu.PrefetchScalarGridSpec(
            num_scalar_prefetch=1,
            grid=(n // tm,),
            in_specs=[pl.BlockSpec(memory_space=pl.ANY),
                      pl.BlockSpec((tm, LANES), lambda i, pos: (i + off, 0)),
                      pl.BlockSpec((tm, D_MODEL), lambda i, pos: (i + off, 0)),
                      _mod_spec(5, rf),
                      pl.BlockSpec((1, D_MODEL), lambda i, pos: (0, 0)),
                      pl.BlockSpec((1, D_MODEL), lambda i, pos: (0, 0))],
            out_specs=pl.BlockSpec((tm, D_MODEL), lambda i, pos: (i, 0)),
            scratch_shapes=[pltpu.VMEM((2, tm, D_MODEL), F32), pltpu.SemaphoreType.DMA(())],
        ),
        out_shape=jax.ShapeDtypeStruct((n, D_MODEL), F32),
        compiler_params=_cparams(("arbitrary",)),
        name="moe_combine_lat" if latent else "moe_combine_ctx",
    )(pos, y, rt_all, x_all, mod, lg, lb)


def _routing_positions(rt_all, ranks, counts):
    tm = TM_MOE
    cnt = counts[0, :N_EXPERTS].astype(jnp.int32)
    padded = ((cnt + tm - 1) // tm) * tm
    gend = jnp.cumsum(padded)
    gstart = gend - padded
    i1 = rt_all[:, 0].astype(jnp.int32)
    i2 = rt_all[:, 1].astype(jnp.int32)
    pos = jnp.concatenate([gstart[i1] + ranks[:, 0].astype(jnp.int32),
                           gstart[i2] + ranks[:, 1].astype(jnp.int32)])
    tile_start = jnp.arange(NT_MOE, dtype=jnp.int32) * tm
    te = jnp.sum((tile_start[:, None] >= gend[None, :]).astype(jnp.int32), axis=1)
    n_active = (gend[-1] // tm).astype(jnp.int32)
    last_e = jnp.take(te, jnp.maximum(n_active - 1, 0))
    te = jnp.where(tile_start < gend[-1], te, last_e).astype(jnp.int32)
    return pos, te, n_active.reshape(1)


def kernel(x_prompt, x_sample, cache_ckv_l0, cache_krope_l0, cache_k_l1, cache_v_l1, c, c_ctx, w_ada_l0, b_ada_l0, mla_w_dq, mla_g_q, mla_w_uq, mla_w_dkv, mla_g_kv, mla_w_ukv, mla_w_o, ln1_g_l0, ln1_b_l0, ffn_w_in, ffn_w_out, ln2_g_l0, ln2_b_l0, w_ada_l1, b_ada_l1, na_w_qkv, na_rpb, na_w_o, ln1_g_l1, ln1_b_l1, moe_w_router, moe_w_in, moe_w_out, ln2_g_l1, ln2_b_l1):
    row = lambda v: v.reshape(1, -1)
    xp = x_prompt.reshape(N_CTX, D_MODEL)
    xs = x_sample.reshape(N_LAT, D_MODEL)
    groups = ((xp, False), (xs, True))

    cvecs = jnp.concatenate([c_ctx[None], c, jnp.zeros((8 - 1 - DEC_BATCH, D_MODEL), F32)], axis=0)
    mod0 = _ada_mod(cvecs, w_ada_l0, b_ada_l0)
    mod1 = _ada_mod(cvecs, w_ada_l1, b_ada_l1)

    wa = jnp.concatenate([mla_w_dq, mla_w_dkv], axis=1).astype(BF16)
    wuq = mla_w_uq.reshape(MLA_Q_LORA, MLA_HEADS, MLA_NOPE + MLA_ROPE)
    wuq = jnp.pad(wuq, ((0, 0), (0, 0), (0, Q_HEAD_PAD - MLA_NOPE - MLA_ROPE)))
    wuq = wuq.reshape(MLA_Q_LORA, MLA_HEADS * Q_HEAD_PAD).astype(BF16)
    wukv = mla_w_ukv.astype(BF16)
    wo0 = mla_w_o.astype(BF16)
    w_in0 = ffn_w_in.astype(BF16)
    w_out0 = ffn_w_out.astype(BF16)
    wqkv = na_w_qkv.astype(BF16)
    wo1 = na_w_o.astype(BF16)
    wr = jnp.pad(moe_w_router, ((0, 0), (0, LANES - N_EXPERTS))).astype(BF16)
    bias_tab = _na_bias_table(na_rpb)
    tab = _rope_tables()

    kvx_cache = _matmul(cache_ckv_l0.reshape(DEC_BATCH * PAST_LEN, MLA_KV_LORA), wukv, BF16, 512, "mla_expand_cache")
    kvx_cache = kvx_cache.reshape(DEC_BATCH, PAST_LEN, -1)
    krp_cache = jnp.pad(cache_krope_l0, ((0, 0), (0, 0), (0, LANES - MLA_ROPE))).astype(BF16)
    x1 = []
    new_ckv = new_kr = None
    for x, latent in groups:
        q, kvx, ckv, kr, krp = _mla_proj(x, mod0, wa, row(mla_g_q), row(mla_g_kv), wuq, wukv, tab, latent)
        if latent:
            o = _mla_attn_lat(q, kvx, krp, kvx_cache, krp_cache)
        else:
            new_ckv, new_kr = ckv, kr
            o = _mla_attn_ctx(q, kvx, krp)
        xa = _proj_ln(o, x, wo0, mod0, row(ln1_g_l0), row(ln1_b_l0), latent)
        x1.append(_ffn_dense(xa, mod0, w_in0, w_out0, row(ln2_g_l0), row(ln2_b_l0), latent))

    kc = cache_k_l1.reshape(DEC_BATCH, PAST_LEN, D_MODEL).astype(BF16)
    vc = cache_v_l1.reshape(DEC_BATCH, PAST_LEN, D_MODEL).astype(BF16)
    q, new_k, new_v = _na_qkv(x1[0], mod1, wqkv, F32, False)
    o_ctx = _na_attn_ctx(q, new_k, new_v)
    q, k, v = _na_qkv(x1[1], mod1, wqkv, BF16, True)
    o_lat = _na_attn_lat(q, k, v, kc, vc, bias_tab)
    x2_all, hm_all, rt_all = _proj_ln_route(o_ctx, x1[0], o_lat, x1[1], wo1, mod1, row(ln1_g_l1), row(ln1_b_l1), wr)

    ranks, counts = _route_rank(rt_all)
    pos, te, n_active = _routing_positions(rt_all, ranks, counts)
    x_sorted = _moe_scatter(pos, hm_all)
    hmid = _moe_up(te, n_active, x_sorted, moe_w_in)
    y = _moe_down(te, n_active, hmid, moe_w_out)
    outs = [_moe_combine(pos, y, rt_all, x2_all, mod1, row(ln2_g_l1), row(ln2_b_l1), latent)
            for latent in (False, True)]

    return (outs[0].reshape(BATCH, SEQ, D_MODEL),
            outs[1].reshape(DEC_BATCH, DEC_SEQ, D_MODEL),
            new_ckv.reshape(BATCH, SEQ, MLA_KV_LORA),
            new_kr.reshape(BATCH, SEQ, MLA_ROPE),
            new_k.reshape(BATCH, SEQ, NA_HEADS, NA_HD),
            new_v.reshape(BATCH, SEQ, NA_HEADS, NA_HD))
```

```python
import functools
import math

import numpy as np
import jax
import jax.numpy as jnp
from jax import lax
from jax.experimental import pallas as pl
from jax.experimental.pallas import tpu as pltpu

F32 = jnp.float32
BF16 = jnp.bfloat16

D_MODEL = 1024
BATCH = 32
SEQ = 256
DEPTH = 2
DEC_BATCH = 2
DEC_SEQ = 2048
PAST_LEN = 512
GRID_W = 64
MLA_HEADS = 8
MLA_NOPE = 128
MLA_ROPE = 64
MLA_V = 128
MLA_Q_LORA = 512
MLA_KV_LORA = 256
MLA_SCALE = 1.0 / math.sqrt(MLA_NOPE + MLA_ROPE)
ROPE_THETA = 10000.0
NA_HEADS = 16
NA_HD = D_MODEL // NA_HEADS
NA_WIN_ROWS = 8
NA_WIN_COLS = 16
NA_SCALE = 1.0 / math.sqrt(NA_HD)
FFN_DENSE = 2816
N_EXPERTS = 8
TOP_K = 2
FFN_EXPERT = 3584
ALPHA = (2 * DEPTH) ** 0.25
LN_EPS = 1e-5
RMS_EPS = 1e-6

N_CTX = BATCH * SEQ
N_LAT = DEC_BATCH * DEC_SEQ
N_TOK = N_CTX + N_LAT
GRID_ROWS = DEC_SEQ // GRID_W
Q_HEAD_PAD = 256
LANES = 128
NEG_BIG = -1e30

VMEM_LIMIT = 56 * 1024 * 1024

TM_PROJ = 512
TM_FFN = 512
TF_DENSE = 1408
TM_MOE = 512
TF_MOE = 896
N_PAIRS = N_TOK * TOP_K
NT_MOE = (N_PAIRS + N_EXPERTS * TM_MOE) // TM_MOE
ROWS_SORTED = NT_MOE * TM_MOE
TM_ROUTE = 256


def _cparams(sem, vmem=VMEM_LIMIT):
    return pltpu.CompilerParams(dimension_semantics=sem, vmem_limit_bytes=vmem)


def _silu(x):
    return x * jax.nn.sigmoid(x)


def _layer_norm(y, g, b):
    mu = jnp.mean(y, axis=-1, keepdims=True)
    d = y - mu
    var = jnp.mean(d * d, axis=-1, keepdims=True)
    return d * lax.rsqrt(var + LN_EPS) * g + b


def _rms_norm(y, g):
    return y * lax.rsqrt(jnp.mean(y * y, axis=-1, keepdims=True) + RMS_EPS) * g


def _dot(a, b):
    return jnp.dot(a, b, preferred_element_type=F32)


def _dot_nt(a, b):
    return lax.dot_general(a, b, (((1,), (1,)), ((), ())), preferred_element_type=F32)


def _ada_kernel(c_ref, w_ref, b_ref, o_ref):
    s = _silu(c_ref[...]).astype(BF16)
    o_ref[...] = _dot(s, w_ref[...].astype(BF16)) + b_ref[...]


def _ada_mod(cvecs, w, b):
    tn = 1536
    m = pl.pallas_call(
        _ada_kernel,
        grid=(6 * D_MODEL // tn,),
        in_specs=[pl.BlockSpec((8, D_MODEL), lambda j: (0, 0)),
                  pl.BlockSpec((D_MODEL, tn), lambda j: (0, j)),
                  pl.BlockSpec((1, tn), lambda j: (0, j))],
        out_specs=pl.BlockSpec((8, tn), lambda j: (0, j)),
        out_shape=jax.ShapeDtypeStruct((8, 6 * D_MODEL), F32),
        compiler_params=_cparams(("arbitrary",)),
        name="ada_mod",
    )(cvecs, w, b.reshape(1, -1))
    return m[:3].reshape(3 * 6, 1, D_MODEL)


def _mod_spec(j, row_fn):
    return pl.BlockSpec((None, 1, D_MODEL), lambda i, *_: (row_fn(i) * 6 + j, 0, 0))


def _row_fn(latent, tm):
    if not latent:
        return lambda i: 0
    per = DEC_SEQ // tm
    return lambda i: 1 + i // per


def _const_spec(shape):
    nd = len(shape)
    return pl.BlockSpec(shape, lambda *_: (0,) * nd)


def _rope_kernel(invf_ref, o_ref):
    i = pl.program_id(0)
    tm = o_ref.shape[1]
    t = i * tm + lax.broadcasted_iota(jnp.int32, (tm, LANES), 0)
    lane = lax.broadcasted_iota(jnp.int32, (tm, LANES), 1)
    row = t >> int(math.log2(GRID_W))
    col = t & (GRID_W - 1)
    pos = jnp.where(lane < MLA_ROPE // 2, row, col).astype(F32)
    ang = pos * invf_ref[...]
    cos = jnp.cos(ang)
    sin = jnp.sin(ang)
    unit = lane >> int(math.log2(MLA_ROPE // 4))
    first = (unit == 0) | (unit == 2)
    second = (unit == 1) | (unit == 3)
    o_ref[0] = jnp.where(lane < MLA_ROPE, cos, 0.0)
    o_ref[1] = jnp.where(first, -sin, 0.0)
    o_ref[2] = jnp.where(second, sin, 0.0)


def _rope_tables():
    half = MLA_ROPE // 2
    inv_freq = (1.0 / (ROPE_THETA ** (np.arange(0, half, 2, dtype=np.float32) / half))).astype(np.float32)
    lane_f = np.zeros((1, LANES), np.float32)
    lane_f[0, :MLA_ROPE] = np.tile(inv_freq, 4)
    tm = 256
    return pl.pallas_call(
        _rope_kernel,
        grid=(DEC_SEQ // tm,),
        in_specs=[_const_spec((1, LANES))],
        out_specs=pl.BlockSpec((3, tm, LANES), lambda i: (0, i, 0)),
        out_shape=jax.ShapeDtypeStruct((3, DEC_SEQ, LANES), F32),
        compiler_params=_cparams(("arbitrary",)),
        name="rope_tables",
    )(jnp.asarray(lane_f))


def _rotate(v, tab_ref):
    return (v * tab_ref[0] + pltpu.roll(v, LANES - MLA_ROPE // 4, 1) * tab_ref[1]
            + pltpu.roll(v, MLA_ROPE // 4, 1) * tab_ref[2])


def _mla_proj_kernel(*refs, rope):
    if rope:
        (x_ref, sc_ref, sh_ref, wa_ref, gq_ref, gkv_ref, wuq_ref, wukv_ref, tab_ref,
         q_ref, kvx_ref, ckv_ref, kr_ref, krp_ref) = refs
    else:
        (x_ref, sc_ref, sh_ref, wa_ref, gq_ref, gkv_ref, wuq_ref, wukv_ref,
         q_ref, kvx_ref, ckv_ref, kr_ref, krp_ref) = refs
    h = (x_ref[...] * (1.0 + sc_ref[...]) + sh_ref[...]).astype(BF16)
    t = _dot(h, wa_ref[...])
    cq = _rms_norm(t[:, :MLA_Q_LORA], gq_ref[...])
    ckv = _rms_norm(t[:, MLA_Q_LORA:MLA_Q_LORA + MLA_KV_LORA], gkv_ref[...])
    kr = t[:, MLA_Q_LORA + MLA_KV_LORA:]
    ckv_ref[...] = ckv
    kr_ref[...] = kr
    kvx_ref[...] = _dot(ckv.astype(BF16), wukv_ref[...]).astype(BF16)
    q = _dot(cq.astype(BF16), wuq_ref[...])
    krp = jnp.concatenate([kr, jnp.zeros_like(kr)], axis=-1)
    if rope:
        krp_ref[...] = _rotate(krp, tab_ref).astype(BF16)
        for hd in range(MLA_HEADS):
            lo = hd * Q_HEAD_PAD
            q_ref[:, lo:lo + MLA_NOPE] = q[:, lo:lo + MLA_NOPE].astype(BF16)
            q_ref[:, lo + MLA_NOPE:lo + Q_HEAD_PAD] = _rotate(q[:, lo + MLA_NOPE:lo + Q_HEAD_PAD], tab_ref).astype(BF16)
    else:
        krp_ref[...] = krp.astype(BF16)
        q_ref[...] = q.astype(BF16)


def _mla_proj(x, mod, wa, gq, gkv, wuq, wukv, tab, latent):
    n = x.shape[0]
    tm = TM_PROJ
    rf = _row_fn(latent, tm)
    tok = lambda w: pl.BlockSpec((tm, w), lambda i: (i, 0))
    in_specs = [tok(D_MODEL), _mod_spec(1, rf), _mod_spec(0, rf),
                _const_spec(wa.shape), _const_spec(gq.shape), _const_spec(gkv.shape),
                _const_spec(wuq.shape), _const_spec(wukv.shape)]
    args = [x, mod, mod, wa, gq, gkv, wuq, wukv]
    if latent:
        per = DEC_SEQ // tm
        in_specs.append(pl.BlockSpec((3, tm, LANES), lambda i: (0, i % per, 0)))
        args.append(tab)
    return pl.pallas_call(
        functools.partial(_mla_proj_kernel, rope=latent),
        grid=(n // tm,),
        in_specs=in_specs,
        out_specs=[tok(MLA_HEADS * Q_HEAD_PAD), tok(MLA_HEADS * (MLA_NOPE + MLA_V)),
                   tok(MLA_KV_LORA), tok(MLA_ROPE), tok(LANES)],
        out_shape=[jax.ShapeDtypeStruct((n, MLA_HEADS * Q_HEAD_PAD), BF16),
                   jax.ShapeDtypeStruct((n, MLA_HEADS * (MLA_NOPE + MLA_V)), BF16),
                   jax.ShapeDtypeStruct((n, MLA_KV_LORA), F32),
                   jax.ShapeDtypeStruct((n, MLA_ROPE), F32),
                   jax.ShapeDtypeStruct((n, LANES), BF16)],
        compiler_params=_cparams(("arbitrary",)),
        name="mla_proj_lat" if latent else "mla_proj_ctx",
    )(*args)


def _matmul_kernel(a_ref, b_ref, o_ref):
    o_ref[...] = _dot(a_ref[...].astype(BF16), b_ref[...]).astype(o_ref.dtype)


def _matmul(a, b, out_dtype, tm, name):
    m, k = a.shape
    n = b.shape[1]
    return pl.pallas_call(
        _matmul_kernel,
        grid=(m // tm,),
        in_specs=[pl.BlockSpec((tm, k), lambda i: (i, 0)), _const_spec(b.shape)],
        out_specs=pl.BlockSpec((tm, n), lambda i: (i, 0)),
        out_shape=jax.ShapeDtypeStruct((m, n), out_dtype),
        compiler_params=_cparams(("arbitrary",)),
        name=name,
    )(a, b)


def _mla_head(qh, kparts, vparts):
    ss = [_dot_nt(qh, k) * MLA_SCALE for k in kparts]
    m = functools.reduce(jnp.maximum, [jnp.max(s, axis=-1, keepdims=True) for s in ss])
    ps = [jnp.exp(s - m) for s in ss]
    l = functools.reduce(jnp.add, [jnp.sum(p, axis=-1, keepdims=True) for p in ps])
    o = functools.reduce(jnp.add, [_dot(p.astype(BF16), v) for p, v in zip(ps, vparts)])
    return o / l


def _mla_attn_ctx_kernel(q_ref, kvx_ref, krp_ref, o_ref):
    krp = krp_ref[...]
    for hd in range(MLA_HEADS):
        lo = hd * (MLA_NOPE + MLA_V)
        qh = q_ref[:, hd * Q_HEAD_PAD:(hd + 1) * Q_HEAD_PAD]
        kh = jnp.concatenate([kvx_ref[:, lo:lo + MLA_NOPE], krp], axis=-1)
        vh = kvx_ref[:, lo + MLA_NOPE:lo + MLA_NOPE + MLA_V]
        o_ref[:, hd * MLA_V:(hd + 1) * MLA_V] = _mla_head(qh, [kh], [vh]).astype(BF16)


def _mla_attn_ctx(q, kvx, krp):
    tok = lambda w: pl.BlockSpec((SEQ, w), lambda b: (b, 0))
    return pl.pallas_call(
        _mla_attn_ctx_kernel,
        grid=(BATCH,),
        in_specs=[tok(q.shape[1]), tok(kvx.shape[1]), tok(LANES)],
        out_specs=tok(MLA_HEADS * MLA_V),
        out_shape=jax.ShapeDtypeStruct((N_CTX, MLA_HEADS * MLA_V), BF16),
        compiler_params=_cparams(("arbitrary",)),
        name="mla_attn_ctx",
    )(q, kvx, krp)


def _mla_attn_lat_kernel(q_ref, kvl_ref, krl_ref, kvc_ref, krc_ref, o_ref):
    krl = krl_ref[...]
    krc = krc_ref[...]
    for hd in range(MLA_HEADS):
        lo = hd * (MLA_NOPE + MLA_V)
        qh = q_ref[:, hd * Q_HEAD_PAD:(hd + 1) * Q_HEAD_PAD]
        kl = jnp.concatenate([kvl_ref[:, lo:lo + MLA_NOPE], krl], axis=-1)
        kc = jnp.concatenate([kvc_ref[:, lo:lo + MLA_NOPE], krc], axis=-1)
        vl = kvl_ref[:, lo + MLA_NOPE:lo + MLA_NOPE + MLA_V]
        vc = kvc_ref[:, lo + MLA_NOPE:lo + MLA_NOPE + MLA_V]
        o_ref[:, hd * MLA_V:(hd + 1) * MLA_V] = _mla_head(qh, [kl, kc], [vl, vc]).astype(BF16)


def _mla_attn_lat(q, kvx_lat, krp_lat, kvx_ctx, krp_ctx):
    tq = 256
    per = DEC_SEQ // tq
    wkv = kvx_lat.shape[-1]
    return pl.pallas_call(
        _mla_attn_lat_kernel,
        grid=(DEC_BATCH, per),
        in_specs=[pl.BlockSpec((tq, q.shape[1]), lambda b, i: (b * per + i, 0)),
                  pl.BlockSpec((None, DEC_SEQ, wkv), lambda b, i: (b, 0, 0)),
                  pl.BlockSpec((None, DEC_SEQ, LANES), lambda b, i: (b, 0, 0)),
                  pl.BlockSpec((None, PAST_LEN, wkv), lambda b, i: (b, 0, 0)),
                  pl.BlockSpec((None, PAST_LEN, LANES), lambda b, i: (b, 0, 0))],
        out_specs=pl.BlockSpec((tq, MLA_HEADS * MLA_V), lambda b, i: (b * per + i, 0)),
        out_shape=jax.ShapeDtypeStruct((N_LAT, MLA_HEADS * MLA_V), BF16),
        compiler_params=_cparams(("arbitrary", "arbitrary")),
        name="mla_attn_lat",
    )(q, kvx_lat.reshape(DEC_BATCH, DEC_SEQ, wkv), krp_lat.reshape(DEC_BATCH, DEC_SEQ, LANES), kvx_ctx, krp_ctx)


def _proj_ln_kernel(o_ref, x_ref, wo_ref, g_ref, lg_ref, lb_ref, x1_ref):
    out = _dot(o_ref[...], wo_ref[...])
    x1_ref[...] = _layer_norm(ALPHA * x_ref[...] + g_ref[...] * out, lg_ref[...], lb_ref[...])


def _top2(logits):
    lane = lax.broadcasted_iota(jnp.int32, logits.shape, 1).astype(F32)
    m1 = jnp.max(logits, axis=-1, keepdims=True)
    i1 = jnp.min(jnp.where(logits == m1, lane, float(LANES)), axis=-1, keepdims=True)
    rest = jnp.where(lane == i1, -jnp.inf, logits)
    m2 = jnp.max(rest, axis=-1, keepdims=True)
    i2 = jnp.min(jnp.where(rest == m2, lane, float(LANES)), axis=-1, keepdims=True)
    e = jnp.exp(m2 - m1)
    w1 = 1.0 / (1.0 + e)
    w2 = e / (1.0 + e)
    return lane, i1, i2, w1, w2


def _proj_ln_route_kernel(oc_ref, xc_ref, ol_ref, xl_ref, wo_ref, g_ref, lg_ref, lb_ref, sc_ref, sh_ref, wr_ref,
                          x1_ref, hm_ref, rt_ref, *, n_ctx_tiles):
    def body(o_ref, x_ref):
        out = _dot(o_ref[...], wo_ref[...])
        x1 = _layer_norm(ALPHA * x_ref[...] + g_ref[...] * out, lg_ref[...], lb_ref[...])
        x1_ref[...] = x1
        hm = x1 * (1.0 + sc_ref[...]) + sh_ref[...]
        hm_ref[...] = hm
        logits = _dot(hm.astype(BF16), wr_ref[...])
        lane = lax.broadcasted_iota(jnp.int32, logits.shape, 1)
        logits = jnp.where(lane < N_EXPERTS, logits, -jnp.inf)
        _, i1, i2, w1, w2 = _top2(logits)
        rt_ref[...] = jnp.where(lane == 0, i1, jnp.where(lane == 1, i2, jnp.where(lane == 2, w1, jnp.where(lane == 3, w2, 0.0))))

    is_ctx = pl.program_id(0) < n_ctx_tiles
    pl.when(is_ctx)(lambda: body(oc_ref, xc_ref))
    pl.when(jnp.logical_not(is_ctx))(lambda: body(ol_ref, xl_ref))


def _proj_ln(o, x, wo, mod, lg, lb, latent):
    n = x.shape[0]
    tm = TM_PROJ
    rf = _row_fn(latent, tm)
    tok = lambda w: pl.BlockSpec((tm, w), lambda i: (i, 0))
    return pl.pallas_call(
        _proj_ln_kernel,
        grid=(n // tm,),
        in_specs=[tok(o.shape[1]), tok(D_MODEL), _const_spec(wo.shape), _mod_spec(2, rf),
                  _const_spec(lg.shape), _const_spec(lb.shape)],
        out_specs=tok(D_MODEL),
        out_shape=jax.ShapeDtypeStruct((n, D_MODEL), F32),
        compiler_params=_cparams(("arbitrary",)),
        name="proj_ln_lat" if latent else "proj_ln_ctx",
    )(o, x, wo, mod, lg, lb)


def _all_row_fn(tm):
    nc = N_CTX // tm
    per = DEC_SEQ // tm
    return lambda i: jnp.where(i < nc, 0, 1 + (i - nc) // per)


def _proj_ln_route(o_ctx, x_ctx, o_lat, x_lat, wo, mod, lg, lb, wr):
    tm = TM_PROJ
    nc = N_CTX // tm
    rf = _all_row_fn(tm)
    ctx = lambda w: pl.BlockSpec((tm, w), lambda i: (jnp.minimum(i, nc - 1), 0))
    lat = lambda w: pl.BlockSpec((tm, w), lambda i: (jnp.maximum(i - nc, 0), 0))
    tok = lambda w: pl.BlockSpec((tm, w), lambda i: (i, 0))
    return pl.pallas_call(
        functools.partial(_proj_ln_route_kernel, n_ctx_tiles=nc),
        grid=(N_TOK // tm,),
        in_specs=[ctx(D_MODEL), ctx(D_MODEL), lat(D_MODEL), lat(D_MODEL), _const_spec(wo.shape), _mod_spec(2, rf),
                  _const_spec(lg.shape), _const_spec(lb.shape), _mod_spec(4, rf), _mod_spec(3, rf),
                  _const_spec(wr.shape)],
        out_specs=[tok(D_MODEL), tok(D_MODEL), tok(LANES)],
        out_shape=[jax.ShapeDtypeStruct((N_TOK, D_MODEL), F32),
                   jax.ShapeDtypeStruct((N_TOK, D_MODEL), F32),
                   jax.ShapeDtypeStruct((N_TOK, LANES), F32)],
        compiler_params=_cparams(("arbitrary",)),
        name="proj_ln_route",
    )(o_ctx, x_ctx, o_lat, x_lat, wo, mod, lg, lb, mod, mod, wr)


def _ffn_dense_kernel(x_ref, sc_ref, sh_ref, g_ref, wg_ref, wu_ref, wd_ref, lg_ref, lb_ref,
                      o_ref, h_scr, acc_scr):
    f = pl.program_id(1)

    @pl.when(f == 0)
    def _():
        h_scr[...] = (x_ref[...] * (1.0 + sc_ref[...]) + sh_ref[...]).astype(BF16)
        acc_scr[...] = jnp.zeros_like(acc_scr)

    h = h_scr[...]
    a = (_silu(_dot(h, wg_ref[...])) * _dot(h, wu_ref[...])).astype(BF16)
    acc_scr[...] += _dot(a, wd_ref[...])

    @pl.when(f == pl.num_programs(1) - 1)
    def _():
        o_ref[...] = _layer_norm(ALPHA * x_ref[...] + g_ref[...] * acc_scr[...], lg_ref[...], lb_ref[...])


def _ffn_dense(x, mod, w_in, w_out, lg, lb, latent):
    n = x.shape[0]
    tm, tf = TM_FFN, TF_DENSE
    nf = FFN_DENSE // tf
    rf = _row_fn(latent, tm)
    mod_spec = lambda j: pl.BlockSpec((None, 1, D_MODEL), lambda i, f: (rf(i) * 6 + j, 0, 0))
    tok = pl.BlockSpec((tm, D_MODEL), lambda i, f: (i, 0))
    return pl.pallas_call(
        _ffn_dense_kernel,
        grid=(n // tm, nf),
        in_specs=[tok, mod_spec(4), mod_spec(3), mod_spec(5),
                  pl.BlockSpec((D_MODEL, tf), lambda i, f: (0, f)),
                  pl.BlockSpec((D_MODEL, tf), lambda i, f: (0, nf + f)),
                  pl.BlockSpec((tf, D_MODEL), lambda i, f: (f, 0)),
                  pl.BlockSpec((1, D_MODEL), lambda i, f: (0, 0)),
                  pl.BlockSpec((1, D_MODEL), lambda i, f: (0, 0))],
        out_specs=tok,
        out_shape=jax.ShapeDtypeStruct((n, D_MODEL), F32),
        scratch_shapes=[pltpu.VMEM((tm, D_MODEL), BF16), pltpu.VMEM((tm, D_MODEL), F32)],
        compiler_params=_cparams(("arbitrary", "arbitrary")),
        name="ffn_dense_lat" if latent else "ffn_dense_ctx",
    )(x, mod, mod, mod, w_in, w_in, w_out, lg, lb)


def _na_qkv_kernel(x_ref, sc_ref, sh_ref, w_ref, q_ref, k_ref, v_ref):
    h = (x_ref[...] * (1.0 + sc_ref[...]) + sh_ref[...]).astype(BF16)
    qkv = _dot(h, w_ref[...])
    q_ref[...] = qkv[:, :D_MODEL].astype(q_ref.dtype)
    k_ref[...] = qkv[:, D_MODEL:2 * D_MODEL].astype(k_ref.dtype)
    v_ref[...] = qkv[:, 2 * D_MODEL:].astype(v_ref.dtype)


def _na_qkv(x, mod, w, kv_dtype, latent):
    n = x.shape[0]
    tm = TM_PROJ
    rf = _row_fn(latent, tm)
    tok = pl.BlockSpec((tm, D_MODEL), lambda i: (i, 0))
    return pl.pallas_call(
        _na_qkv_kernel,
        grid=(n // tm,),
        in_specs=[tok, _mod_spec(1, rf), _mod_spec(0, rf), _const_spec(w.shape)],
        out_specs=[tok, tok, tok],
        out_shape=[jax.ShapeDtypeStruct((n, D_MODEL), BF16),
                   jax.ShapeDtypeStruct((n, D_MODEL), kv_dtype),
                   jax.ShapeDtypeStruct((n, D_MODEL), kv_dtype)],
        compiler_params=_cparams(("arbitrary",)),
        name="na_qkv_lat" if latent else "na_qkv_ctx",
    )(x, mod, mod, w)


def _na_attn_ctx_kernel(q_ref, k_ref, v_ref, o_ref):
    for hd in range(NA_HEADS):
        sl = slice(hd * NA_HD, (hd + 1) * NA_HD)
        s = _dot_nt(q_ref[:, sl], k_ref[:, sl].astype(BF16)) * NA_SCALE
        m = jnp.max(s, axis=-1, keepdims=True)
        p = jnp.exp(s - m)
        l = jnp.sum(p, axis=-1, keepdims=True)
        o = _dot(p.astype(BF16), v_ref[:, sl].astype(BF16)) / l
        o_ref[:, sl] = o.astype(BF16)


def _na_attn_ctx(q, k, v):
    tok = pl.BlockSpec((SEQ, D_MODEL), lambda b: (b, 0))
    return pl.pallas_call(
        _na_attn_ctx_kernel,
        grid=(BATCH,),
        in_specs=[tok, tok, tok],
        out_specs=tok,
        out_shape=jax.ShapeDtypeStruct((N_CTX, D_MODEL), BF16),
        compiler_params=_cparams(("arbitrary",)),
        name="na_attn_ctx",
    )(q, k, v)


def _na_row_start(r):
    return jnp.clip(r - NA_WIN_ROWS // 2, 0, GRID_ROWS - NA_WIN_ROWS)


def _na_attn_lat_kernel(q_ref, k_ref, v_ref, kc_ref, vc_ref, bias_ref, o_ref):
    r = pl.program_id(1)
    start = pl.multiple_of(_na_row_start(r) * GRID_W, GRID_W)
    win = pl.ds(start, NA_WIN_ROWS * GRID_W)
    for hd in range(NA_HEADS):
        sl = slice(hd * NA_HD, (hd + 1) * NA_HD)
        qh = q_ref[:, sl]
        s_nb = _dot_nt(qh, k_ref[win, sl]) * NA_SCALE + bias_ref[hd]
        s_cx = _dot_nt(qh, kc_ref[:, sl]) * NA_SCALE
        m = jnp.maximum(jnp.max(s_nb, axis=-1, keepdims=True), jnp.max(s_cx, axis=-1, keepdims=True))
        p_nb = jnp.exp(s_nb - m)
        p_cx = jnp.exp(s_cx - m)
        l = jnp.sum(p_nb, axis=-1, keepdims=True) + jnp.sum(p_cx, axis=-1, keepdims=True)
        o = (_dot(p_nb.astype(BF16), v_ref[win, sl]) + _dot(p_cx.astype(BF16), vc_ref[:, sl])) / l
        o_ref[:, sl] = o.astype(BF16)


def _na_bias_kernel(e_ref, e64_ref, o_ref):
    n_dr = 2 * NA_WIN_ROWS - 1
    shape = (GRID_W, LANES)
    lane = lax.broadcasted_iota(jnp.int32, shape, 1)
    c = lax.broadcasted_iota(jnp.int32, shape, 0)
    kc = lane & (GRID_W - 1)
    cs = jnp.clip(c - NA_WIN_COLS // 2, 0, GRID_W - NA_WIN_COLS)
    valid = (kc >= cs) & (kc < cs + NA_WIN_COLS)
    toeplitz = lambda ref, a: pltpu.roll(jnp.broadcast_to(ref[a:a + 1, :], shape), 0, 1, stride=1, stride_axis=0)
    lo = [toeplitz(e_ref, a) for a in range(n_dr)]
    hi = [toeplitz(e64_ref, a) for a in range(n_dr)]
    pairs = [jnp.where(valid, jnp.where(lane < GRID_W, lo[a], hi[a + 1]), NEG_BIG) for a in range(n_dr - 1)]
    for s in range(NA_WIN_ROWS):
        for p in range(NA_WIN_ROWS // 2):
            o_ref[s, :, p * LANES:(p + 1) * LANES] = pairs[2 * p - s + NA_WIN_ROWS - 1]


def _na_bias_table(rpb):
    n_dr = 2 * NA_WIN_ROWS - 1
    e = jnp.zeros((NA_HEADS, n_dr, LANES), F32)
    e = e.at[:, :, :NA_WIN_COLS].set(rpb[:, :, NA_WIN_COLS - 1:])
    e = e.at[:, :, LANES - (NA_WIN_COLS - 1):].set(rpb[:, :, :NA_WIN_COLS - 1])
    e64 = jnp.roll(e, GRID_W, axis=-1)
    spec = pl.BlockSpec((None, n_dr, LANES), lambda h: (h, 0, 0))
    return pl.pallas_call(
        _na_bias_kernel,
        grid=(NA_HEADS,),
        in_specs=[spec, spec],
        out_specs=pl.BlockSpec((None, NA_WIN_ROWS, GRID_W, NA_WIN_ROWS * GRID_W), lambda h: (h, 0, 0, 0)),
        out_shape=jax.ShapeDtypeStruct((NA_HEADS, NA_WIN_ROWS, GRID_W, NA_WIN_ROWS * GRID_W), F32),
        compiler_params=_cparams(("arbitrary",)),
        name="na_bias_table",
    )(e, e64)


def _na_attn_lat(q, k, v, kc, vc, bias_tab):
    row = pl.BlockSpec((GRID_W, D_MODEL), lambda b, r: (b * GRID_ROWS + r, 0))
    full = lambda s: pl.BlockSpec((None, s, D_MODEL), lambda b, r: (b, 0, 0))
    bias_spec = pl.BlockSpec((NA_HEADS, None, GRID_W, NA_WIN_ROWS * GRID_W),
                             lambda b, r: (0, r - _na_row_start(r), 0, 0))
    return pl.pallas_call(
        _na_attn_lat_kernel,
        grid=(DEC_BATCH, GRID_ROWS),
        in_specs=[row, full(DEC_SEQ), full(DEC_SEQ), full(PAST_LEN), full(PAST_LEN), bias_spec],
        out_specs=row,
        out_shape=jax.ShapeDtypeStruct((N_LAT, D_MODEL), BF16),
        compiler_params=_cparams(("arbitrary", "arbitrary")),
        name="na_attn_lat",
    )(q, k.reshape(DEC_BATCH, DEC_SEQ, D_MODEL), v.reshape(DEC_BATCH, DEC_SEQ, D_MODEL), kc, vc, bias_tab)


def _route_rank_kernel(rt_ref, rank_ref, cnt_ref, carry):
    i = pl.program_id(0)

    @pl.when(i == 0)
    def _():
        carry[...] = jnp.zeros_like(carry)

    rt = rt_ref[...]
    tm = rt.shape[0]
    lane = lax.broadcasted_iota(jnp.int32, (tm, LANES), 1).astype(F32)
    oh1 = (lane == rt[:, 0:1]).astype(F32)
    oh2 = (lane == rt[:, 1:2]).astype(F32)
    rr = lax.broadcasted_iota(jnp.int32, (tm, tm), 0)
    cc = lax.broadcasted_iota(jnp.int32, (tm, tm), 1)
    below = (cc < rr).astype(BF16)
    tot1 = jnp.sum(oh1, axis=0, keepdims=True)
    tot2 = jnp.sum(oh2, axis=0, keepdims=True)
    base = carry[...]
    cum1 = _dot(below, oh1.astype(BF16)) + base
    cum2 = _dot(below, oh2.astype(BF16)) + base + tot1
    rank1 = jnp.sum(oh1 * cum1, axis=-1, keepdims=True)
    rank2 = jnp.sum(oh2 * cum2, axis=-1, keepdims=True)
    lane_i = lax.broadcasted_iota(jnp.int32, (tm, LANES), 1)
    rank_ref[...] = jnp.where(lane_i == 0, rank1, jnp.where(lane_i == 1, rank2, 0.0))
    carry[...] = base + tot1 + tot2
    cnt_ref[...] = jnp.broadcast_to(carry[...], cnt_ref.shape)


def _route_rank(rt_all):
    tm = TM_ROUTE
    return pl.pallas_call(
        _route_rank_kernel,
        grid=(N_TOK // tm,),
        in_specs=[pl.BlockSpec((tm, LANES), lambda i: (i, 0))],
        out_specs=[pl.BlockSpec((tm, LANES), lambda i: (i, 0)), pl.BlockSpec((8, LANES), lambda i: (0, 0))],
        out_shape=[jax.ShapeDtypeStruct((N_TOK, LANES), F32), jax.ShapeDtypeStruct((8, LANES), F32)],
        scratch_shapes=[pltpu.VMEM((1, LANES), F32)],
        compiler_params=_cparams(("arbitrary",)),
        name="route_rank",
    )(rt_all)


def _moe_scatter_kernel(pos_ref, h_ref, xs_in, xs_ref, sem):
    del xs_in
    i = pl.program_id(0)
    tm = h_ref.shape[0]

    def copies(t):
        src = h_ref.at[pl.ds(t, 1)]
        p0 = pos_ref[i * tm + t]
        p1 = pos_ref[N_TOK + i * tm + t]
        return (pltpu.make_async_copy(src, xs_ref.at[pl.ds(p0, 1)], sem),
                pltpu.make_async_copy(src, xs_ref.at[pl.ds(p1, 1)], sem))

    def start(t, c):
        for cp in copies(t):
            cp.start()
        return c

    lax.fori_loop(0, tm, start, 0, unroll=8)
    rows = xs_ref.at[pl.ds(0, 2 * tm)]
    pltpu.make_async_copy(rows, rows, sem).wait()


def _moe_scatter(pos, hm_all):
    tm = TM_ROUTE
    zeros = jnp.zeros((ROWS_SORTED, D_MODEL), F32)
    return pl.pallas_call(
        _moe_scatter_kernel,
        grid_spec=pltpu.PrefetchScalarGridSpec(
            num_scalar_prefetch=1,
            grid=(N_TOK // tm,),
            in_specs=[pl.BlockSpec((tm, D_MODEL), lambda i, pos: (i, 0)),
                      pl.BlockSpec(memory_space=pl.ANY)],
            out_specs=pl.BlockSpec(memory_space=pl.ANY),
            scratch_shapes=[pltpu.SemaphoreType.DMA(())],
        ),
        out_shape=jax.ShapeDtypeStruct((ROWS_SORTED, D_MODEL), F32),
        input_output_aliases={2: 0},
        compiler_params=_cparams(("arbitrary",)),
        name="moe_scatter",
    )(pos, hm_all, zeros)


def _tile_changed(te_ref, j):
    prev = te_ref[jnp.maximum(j - 1, 0)]
    return (j == 0) | (te_ref[j] != prev)


def _moe_up_kernel(te_ref, na_ref, x_ref, wg_ref, wu_ref, o_ref, wg_scr, wu_scr):
    j = pl.program_id(1)

    @pl.when(j < na_ref[0])
    def _():
        @pl.when(_tile_changed(te_ref, j))
        def _():
            wg_scr[...] = wg_ref[...].astype(BF16)
            wu_scr[...] = wu_ref[...].astype(BF16)

        x = x_ref[...].astype(BF16)
        o_ref[...] = (_silu(_dot(x, wg_scr[...])) * _dot(x, wu_scr[...])).astype(BF16)

    @pl.when(j >= na_ref[0])
    def _():
        o_ref[...] = jnp.zeros_like(o_ref)


def _moe_up(te, na, xs, w_in):
    tm, tf = TM_MOE, TF_MOE
    nf = FFN_EXPERT // tf
    row = lambda j, na: jnp.minimum(j, na[0] - 1)
    return pl.pallas_call(
        _moe_up_kernel,
        grid_spec=pltpu.PrefetchScalarGridSpec(
            num_scalar_prefetch=2,
            grid=(nf, NT_MOE),
            in_specs=[pl.BlockSpec((tm, D_MODEL), lambda f, j, te, na: (row(j, na), 0)),
                      pl.BlockSpec((None, D_MODEL, tf), lambda f, j, te, na: (te[j], 0, f)),
                      pl.BlockSpec((None, D_MODEL, tf), lambda f, j, te, na: (te[j], 0, nf + f))],
            out_specs=pl.BlockSpec((tm, tf), lambda f, j, te, na: (j, f)),
            scratch_shapes=[pltpu.VMEM((D_MODEL, tf), BF16), pltpu.VMEM((D_MODEL, tf), BF16)],
        ),
        out_shape=jax.ShapeDtypeStruct((ROWS_SORTED, FFN_EXPERT), BF16),
        compiler_params=_cparams(("arbitrary", "arbitrary")),
        name="moe_up",
    )(te, na, xs, w_in, w_in)


def _moe_down_kernel(te_ref, na_ref, h_ref, w_ref, o_ref, w_scr):
    j = pl.program_id(1)

    @pl.when(j < na_ref[0])
    def _():
        @pl.when(_tile_changed(te_ref, j))
        def _():
            w_scr[...] = w_ref[...].astype(BF16)

        o_ref[...] = _dot(h_ref[...], w_scr[...])

    @pl.when(j >= na_ref[0])
    def _():
        o_ref[...] = jnp.zeros_like(o_ref)


def _moe_down(te, na, hmid, w_out):
    tm = TM_MOE
    tn = D_MODEL // 2
    row = lambda j, na: jnp.minimum(j, na[0] - 1)
    return pl.pallas_call(
        _moe_down_kernel,
        grid_spec=pltpu.PrefetchScalarGridSpec(
            num_scalar_prefetch=2,
            grid=(D_MODEL // tn, NT_MOE),
            in_specs=[pl.BlockSpec((tm, FFN_EXPERT), lambda n, j, te, na: (row(j, na), 0)),
                      pl.BlockSpec((None, FFN_EXPERT, tn), lambda n, j, te, na: (te[j], 0, n))],
            out_specs=pl.BlockSpec((tm, tn), lambda n, j, te, na: (j, n)),
            scratch_shapes=[pltpu.VMEM((FFN_EXPERT, tn), BF16)],
        ),
        out_shape=jax.ShapeDtypeStruct((ROWS_SORTED, D_MODEL), F32),
        compiler_params=_cparams(("arbitrary", "arbitrary")),
        name="moe_down",
    )(te, na, hmid, w_out)


def _moe_combine_kernel(pos_ref, y_ref, rt_ref, x_ref, g_ref, lg_ref, lb_ref, o_ref, ybuf, sem, *, tok_off):
    i = pl.program_id(0)
    tm = x_ref.shape[0]

    def copies(t):
        p0 = pos_ref[tok_off + i * tm + t]
        p1 = pos_ref[N_TOK + tok_off + i * tm + t]
        return (pltpu.make_async_copy(y_ref.at[pl.ds(p0, 1)], ybuf.at[0, pl.ds(t, 1)], sem),
                pltpu.make_async_copy(y_ref.at[pl.ds(p1, 1)], ybuf.at[1, pl.ds(t, 1)], sem))

    def start(t, c):
        for cp in copies(t):
            cp.start()
        return c

    lax.fori_loop(0, tm, start, 0, unroll=8)
    pltpu.make_async_copy(ybuf, ybuf, sem).wait()
    rt = rt_ref[...]
    moe = rt[:, 2:3] * ybuf[0] + rt[:, 3:4] * ybuf[1]
    o_ref[...] = _layer_norm(ALPHA * x_ref[...] + g_ref[...] * moe, lg_ref[...], lb_ref[...])


def _moe_combine(pos, y, rt_all, x_all, mod, lg, lb, latent):
    n = N_LAT if latent else N_CTX
    tm = TM_ROUTE
    rf = _row_fn(latent, tm)
    tok_off = N_CTX if latent else 0
    off = tok_off // tm
    return pl.pallas_call(
        functools.partial(_moe_combine_kernel, tok_off=tok_off),
        grid_spec=pltpu.PrefetchScalarGridSpec(
            num_scalar_prefetch=1,
            grid=(n // tm,),
            in_specs=[pl.BlockSpec(memory_space=pl.ANY),
                      pl.BlockSpec((tm, LANES), lambda i, pos: (i + off, 0)),
                      pl.BlockSpec((tm, D_MODEL), lambda i, pos: (i + off, 0)),
                      _mod_spec(5, rf),
                      pl.BlockSpec((1, D_MODEL), lambda i, pos: (0, 0)),
                      pl.BlockSpec((1, D_MODEL), lambda i, pos: (0, 0))],
            out_specs=pl.BlockSpec((tm, D_MODEL), lambda i, pos: (i, 0)),
            scratch_shapes=[pltpu.VMEM((2, tm, D_MODEL), F32), pltpu.SemaphoreType.DMA(())],
        ),
        out_shape=jax.ShapeDtypeStruct((n, D_MODEL), F32),
        compiler_params=_cparams(("arbitrary",)),
        name="moe_combine_lat" if latent else "moe_combine_ctx",
    )(pos, y, rt_all, x_all, mod, lg, lb)


def _routing_positions(rt_all, ranks, counts):
    tm = TM_MOE
    cnt = counts[0, :N_EXPERTS].astype(jnp.int32)
    padded = ((cnt + tm - 1) // tm) * tm
    gend = jnp.cumsum(padded)
    gstart = gend - padded
    i1 = rt_all[:, 0].astype(jnp.int32)
    i2 = rt_all[:, 1].astype(jnp.int32)
    pos = jnp.concatenate([gstart[i1] + ranks[:, 0].astype(jnp.int32),
                           gstart[i2] + ranks[:, 1].astype(jnp.int32)])
    tile_start = jnp.arange(NT_MOE, dtype=jnp.int32) * tm
    te = jnp.sum((tile_start[:, None] >= gend[None, :]).astype(jnp.int32), axis=1)
    n_active = (gend[-1] // tm).astype(jnp.int32)
    last_e = jnp.take(te, jnp.maximum(n_active - 1, 0))
    te = jnp.where(tile_start < gend[-1], te, last_e).astype(jnp.int32)
    return pos, te, n_active.reshape(1)


def kernel(x_prompt, x_sample, cache_ckv_l0, cache_krope_l0, cache_k_l1, cache_v_l1, c, c_ctx, w_ada_l0, b_ada_l0, mla_w_dq, mla_g_q, mla_w_uq, mla_w_dkv, mla_g_kv, mla_w_ukv, mla_w_o, ln1_g_l0, ln1_b_l0, ffn_w_in, ffn_w_out, ln2_g_l0, ln2_b_l0, w_ada_l1, b_ada_l1, na_w_qkv, na_rpb, na_w_o, ln1_g_l1, ln1_b_l1, moe_w_router, moe_w_in, moe_w_out, ln2_g_l1, ln2_b_l1):
    row = lambda v: v.reshape(1, -1)
    xp = x_prompt.reshape(N_CTX, D_MODEL)
    xs = x_sample.reshape(N_LAT, D_MODEL)
    groups = ((xp, False), (xs, True))

    cvecs = jnp.concatenate([c_ctx[None], c, jnp.zeros((8 - 1 - DEC_BATCH, D_MODEL), F32)], axis=0)
    mod0 = _ada_mod(cvecs, w_ada_l0, b_ada_l0)
    mod1 = _ada_mod(cvecs, w_ada_l1, b_ada_l1)

    wa = jnp.concatenate([mla_w_dq, mla_w_dkv], axis=1).astype(BF16)
    wuq = mla_w_uq.reshape(MLA_Q_LORA, MLA_HEADS, MLA_NOPE + MLA_ROPE)
    wuq = jnp.pad(wuq, ((0, 0), (0, 0), (0, Q_HEAD_PAD - MLA_NOPE - MLA_ROPE)))
    wuq = wuq.reshape(MLA_Q_LORA, MLA_HEADS * Q_HEAD_PAD).astype(BF16)
    wukv = mla_w_ukv.astype(BF16)
    wo0 = mla_w_o.astype(BF16)
    w_in0 = ffn_w_in.astype(BF16)
    w_out0 = ffn_w_out.astype(BF16)
    wqkv = na_w_qkv.astype(BF16)
    wo1 = na_w_o.astype(BF16)
    wr = jnp.pad(moe_w_router, ((0, 0), (0, LANES - N_EXPERTS))).astype(BF16)
    bias_tab = _na_bias_table(na_rpb)
    tab = _rope_tables()

    kvx_cache = _matmul(cache_ckv_l0.reshape(DEC_BATCH * PAST_LEN, MLA_KV_LORA), wukv, BF16, 512, "mla_expand_cache")
    kvx_cache = kvx_cache.reshape(DEC_BATCH, PAST_LEN, -1)
    krp_cache = jnp.pad(cache_krope_l0, ((0, 0), (0, 0), (0, LANES - MLA_ROPE))).astype(BF16)
    x1 = []
    new_ckv = new_kr = None
    for x, latent in groups:
        q, kvx, ckv, kr, krp = _mla_proj(x, mod0, wa, row(mla_g_q), row(mla_g_kv), wuq, wukv, tab, latent)
        if latent:
            o = _mla_attn_lat(q, kvx, krp, kvx_cache, krp_cache)
        else:
            new_ckv, new_kr = ckv, kr
            o = _mla_attn_ctx(q, kvx, krp)
        xa = _proj_ln(o, x, wo0, mod0, row(ln1_g_l0), row(ln1_b_l0), latent)
        x1.append(_ffn_dense(xa, mod0, w_in0, w_out0, row(ln2_g_l0), row(ln2_b_l0), latent))

    kc = cache_k_l1.reshape(DEC_BATCH, PAST_LEN, D_MODEL).astype(BF16)
    vc = cache_v_l1.reshape(DEC_BATCH, PAST_LEN, D_MODEL).astype(BF16)
    q, new_k, new_v = _na_qkv(x1[0], mod1, wqkv, F32, False)
    o_ctx = _na_attn_ctx(q, new_k, new_v)
    q, k, v = _na_qkv(x1[1], mod1, wqkv, BF16, True)
    o_lat = _na_attn_lat(q, k, v, kc, vc, bias_tab)
    x2_all, hm_all, rt_all = _proj_ln_route(o_ctx, x1[0], o_lat, x1[1], wo1, mod1, row(ln1_g_l1), row(ln1_b_l1), wr)

    ranks, counts = _route_rank(rt_all)
    pos, te, n_active = _routing_positions(rt_all, ranks, counts)
    x_sorted = _moe_scatter(pos, hm_all)
    hmid = _moe_up(te, n_active, x_sorted, moe_w_in)
    y = _moe_down(te, n_active, hmid, moe_w_out)
    outs = [_moe_combine(pos, y, rt_all, x2_all, mod1, row(ln2_g_l1), row(ln2_b_l1), latent)
            for latent in (False, True)]

    return (outs[0].reshape(BATCH, SEQ, D_MODEL),
            outs[1].reshape(DEC_BATCH, DEC_SEQ, D_MODEL),
            new_ckv.reshape(BATCH, SEQ, MLA_KV_LORA),
            new_kr.reshape(BATCH, SEQ, MLA_ROPE),
            new_k.reshape(BATCH, SEQ, NA_HEADS, NA_HD),
            new_v.reshape(BATCH, SEQ, NA_HEADS, NA_HD))
```

```python
import functools
import math

import numpy as np
import jax
import jax.numpy as jnp
from jax import lax
from jax.experimental import pallas as pl
from jax.experimental.pallas import tpu as pltpu

F32 = jnp.float32
BF16 = jnp.bfloat16

D_MODEL = 1024
BATCH = 32
SEQ = 256
DEPTH = 2
DEC_BATCH = 2
DEC_SEQ = 2048
PAST_LEN = 512
GRID_W = 64
MLA_HEADS = 8
MLA_NOPE = 128
MLA_ROPE = 64
MLA_V = 128
MLA_Q_LORA = 512
MLA_KV_LORA = 256
MLA_SCALE = 1.0 / math.sqrt(MLA_NOPE + MLA_ROPE)
ROPE_THETA = 10000.0
NA_HEADS = 16
NA_HD = D_MODEL // NA_HEADS
NA_WIN_ROWS = 8
NA_WIN_COLS = 16
NA_SCALE = 1.0 / math.sqrt(NA_HD)
FFN_DENSE = 2816
N_EXPERTS = 8
TOP_K = 2
FFN_EXPERT = 3584
ALPHA = (2 * DEPTH) ** 0.25
LN_EPS = 1e-5
RMS_EPS = 1e-6

N_CTX = BATCH * SEQ
N_LAT = DEC_BATCH * DEC_SEQ
N_TOK = N_CTX + N_LAT
GRID_ROWS = DEC_SEQ // GRID_W
Q_HEAD_PAD = 256
LANES = 128
NEG_BIG = -1e30
NA_PAIR = 2 * NA_HD
NA_BLK_ROWS = 2
NA_WIN_PAIRS = 5
NA_N_VARIANTS = 5

VMEM_LIMIT = 56 * 1024 * 1024

TM_PROJ = 512
TM_FFN = 512
TF_DENSE = 1408
TM_MOE = 512
TF_MOE = 896
N_PAIRS = N_TOK * TOP_K
NT_MOE = (N_PAIRS + N_EXPERTS * TM_MOE) // TM_MOE
ROWS_SORTED = NT_MOE * TM_MOE
TM_ROUTE = 256


def _cparams(sem, vmem=VMEM_LIMIT):
    return pltpu.CompilerParams(dimension_semantics=sem, vmem_limit_bytes=vmem)


def _silu(x):
    return x * jax.nn.sigmoid(x)


def _layer_norm(y, g, b):
    mu = jnp.mean(y, axis=-1, keepdims=True)
    d = y - mu
    var = jnp.mean(d * d, axis=-1, keepdims=True)
    return d * lax.rsqrt(var + LN_EPS) * g + b


def _rms_norm(y, g):
    return y * lax.rsqrt(jnp.mean(y * y, axis=-1, keepdims=True) + RMS_EPS) * g


def _dot(a, b):
    return jnp.dot(a, b, preferred_element_type=F32)


def _dot_nt(a, b):
    return lax.dot_general(a, b, (((1,), (1,)), ((), ())), preferred_element_type=F32)


def _ada_kernel(c_ref, w_ref, b_ref, o_ref):
    s = _silu(c_ref[...]).astype(BF16)
    o_ref[...] = _dot(s, w_ref[...].astype(BF16)) + b_ref[...]


def _ada_mod(cvecs, w, b):
    tn = 1536
    m = pl.pallas_call(
        _ada_kernel,
        grid=(6 * D_MODEL // tn,),
        in_specs=[pl.BlockSpec((8, D_MODEL), lambda j: (0, 0)),
                  pl.BlockSpec((D_MODEL, tn), lambda j: (0, j)),
                  pl.BlockSpec((1, tn), lambda j: (0, j))],
        out_specs=pl.BlockSpec((8, tn), lambda j: (0, j)),
        out_shape=jax.ShapeDtypeStruct((8, 6 * D_MODEL), F32),
        compiler_params=_cparams(("arbitrary",)),
        name="ada_mod",
    )(cvecs, w, b.reshape(1, -1))
    return m[:3].reshape(3 * 6, 1, D_MODEL)


def _mod_spec(j, row_fn):
    return pl.BlockSpec((None, 1, D_MODEL), lambda i, *_: (row_fn(i) * 6 + j, 0, 0))


def _row_fn(latent, tm):
    if not latent:
        return lambda i: 0
    per = DEC_SEQ // tm
    return lambda i: 1 + i // per


def _const_spec(shape):
    nd = len(shape)
    return pl.BlockSpec(shape, lambda *_: (0,) * nd)


def _rope_kernel(invf_ref, o_ref):
    i = pl.program_id(0)
    tm = o_ref.shape[1]
    t = i * tm + lax.broadcasted_iota(jnp.int32, (tm, LANES), 0)
    lane = lax.broadcasted_iota(jnp.int32, (tm, LANES), 1)
    row = t >> int(math.log2(GRID_W))
    col = t & (GRID_W - 1)
    pos = jnp.where(lane < MLA_ROPE // 2, row, col).astype(F32)
    ang = pos * invf_ref[...]
    cos = jnp.cos(ang)
    sin = jnp.sin(ang)
    unit = lane >> int(math.log2(MLA_ROPE // 4))
    first = (unit == 0) | (unit == 2)
    second = (unit == 1) | (unit == 3)
    o_ref[0] = jnp.where(lane < MLA_ROPE, cos, 0.0)
    o_ref[1] = jnp.where(first, -sin, 0.0)
    o_ref[2] = jnp.where(second, sin, 0.0)


def _rope_tables():
    half = MLA_ROPE // 2
    inv_freq = (1.0 / (ROPE_THETA ** (np.arange(0, half, 2, dtype=np.float32) / half))).astype(np.float32)
    lane_f = np.zeros((1, LANES), np.float32)
    lane_f[0, :MLA_ROPE] = np.tile(inv_freq, 4)
    tm = 256
    return pl.pallas_call(
        _rope_kernel,
        grid=(DEC_SEQ // tm,),
        in_specs=[_const_spec((1, LANES))],
        out_specs=pl.BlockSpec((3, tm, LANES), lambda i: (0, i, 0)),
        out_shape=jax.ShapeDtypeStruct((3, DEC_SEQ, LANES), F32),
        compiler_params=_cparams(("arbitrary",)),
        name="rope_tables",
    )(jnp.asarray(lane_f))


def _rotate(v, tab_ref):
    return (v * tab_ref[0] + pltpu.roll(v, LANES - MLA_ROPE // 4, 1) * tab_ref[1]
            + pltpu.roll(v, MLA_ROPE // 4, 1) * tab_ref[2])


def _mla_proj_kernel(*refs, rope):
    if rope:
        (x_ref, sc_ref, sh_ref, wa_ref, gq_ref, gkv_ref, wuq_ref, wukv_ref, tab_ref,
         q_ref, kvx_ref, ckv_ref, kr_ref, krp_ref) = refs
    else:
        (x_ref, sc_ref, sh_ref, wa_ref, gq_ref, gkv_ref, wuq_ref, wukv_ref,
         q_ref, kvx_ref, ckv_ref, kr_ref, krp_ref) = refs
    h = (x_ref[...] * (1.0 + sc_ref[...]) + sh_ref[...]).astype(BF16)
    t = _dot(h, wa_ref[...])
    cq = _rms_norm(t[:, :MLA_Q_LORA], gq_ref[...])
    ckv = _rms_norm(t[:, MLA_Q_LORA:MLA_Q_LORA + MLA_KV_LORA], gkv_ref[...])
    kr = t[:, MLA_Q_LORA + MLA_KV_LORA:]
    ckv_ref[...] = ckv
    kr_ref[...] = kr
    kvx_ref[...] = _dot(ckv.astype(BF16), wukv_ref[...]).astype(BF16)
    q = _dot(cq.astype(BF16), wuq_ref[...])
    krp = jnp.concatenate([kr, jnp.zeros_like(kr)], axis=-1)
    if rope:
        krp_ref[...] = _rotate(krp, tab_ref).astype(BF16)
        for hd in range(MLA_HEADS):
            lo = hd * Q_HEAD_PAD
            q_ref[:, lo:lo + MLA_NOPE] = q[:, lo:lo + MLA_NOPE].astype(BF16)
            q_ref[:, lo + MLA_NOPE:lo + Q_HEAD_PAD] = _rotate(q[:, lo + MLA_NOPE:lo + Q_HEAD_PAD], tab_ref).astype(BF16)
    else:
        krp_ref[...] = krp.astype(BF16)
        q_ref[...] = q.astype(BF16)


def _mla_proj(x, mod, wa, gq, gkv, wuq, wukv, tab, latent):
    n = x.shape[0]
    tm = TM_PROJ
    rf = _row_fn(latent, tm)
    tok = lambda w: pl.BlockSpec((tm, w), lambda i: (i, 0))
    in_specs = [tok(D_MODEL), _mod_spec(1, rf), _mod_spec(0, rf),
                _const_spec(wa.shape), _const_spec(gq.shape), _const_spec(gkv.shape),
                _const_spec(wuq.shape), _const_spec(wukv.shape)]
    args = [x, mod, mod, wa, gq, gkv, wuq, wukv]
    if latent:
        per = DEC_SEQ // tm
        in_specs.append(pl.BlockSpec((3, tm, LANES), lambda i: (0, i % per, 0)))
        args.append(tab)
    return pl.pallas_call(
        functools.partial(_mla_proj_kernel, rope=latent),
        grid=(n // tm,),
        in_specs=in_specs,
        out_specs=[tok(MLA_HEADS * Q_HEAD_PAD), tok(MLA_HEADS * (MLA_NOPE + MLA_V)),
                   tok(MLA_KV_LORA), tok(MLA_ROPE), tok(LANES)],
        out_shape=[jax.ShapeDtypeStruct((n, MLA_HEADS * Q_HEAD_PAD), BF16),
                   jax.ShapeDtypeStruct((n, MLA_HEADS * (MLA_NOPE + MLA_V)), BF16),
                   jax.ShapeDtypeStruct((n, MLA_KV_LORA), F32),
                   jax.ShapeDtypeStruct((n, MLA_ROPE), F32),
                   jax.ShapeDtypeStruct((n, LANES), BF16)],
        compiler_params=_cparams(("arbitrary",)),
        name="mla_proj_lat" if latent else "mla_proj_ctx",
    )(*args)


def _matmul_kernel(a_ref, b_ref, o_ref):
    o_ref[...] = _dot(a_ref[...].astype(BF16), b_ref[...]).astype(o_ref.dtype)


def _matmul(a, b, out_dtype, tm, name):
    m, k = a.shape
    n = b.shape[1]
    return pl.pallas_call(
        _matmul_kernel,
        grid=(m // tm,),
        in_specs=[pl.BlockSpec((tm, k), lambda i: (i, 0)), _const_spec(b.shape)],
        out_specs=pl.BlockSpec((tm, n), lambda i: (i, 0)),
        out_shape=jax.ShapeDtypeStruct((m, n), out_dtype),
        compiler_params=_cparams(("arbitrary",)),
        name=name,
    )(a, b)


def _mla_head(qh, kparts, vparts):
    ss = [_dot_nt(qh, k) * MLA_SCALE for k in kparts]
    m = functools.reduce(jnp.maximum, [jnp.max(s, axis=-1, keepdims=True) for s in ss])
    ps = [jnp.exp(s - m) for s in ss]
    l = functools.reduce(jnp.add, [jnp.sum(p, axis=-1, keepdims=True) for p in ps])
    o = functools.reduce(jnp.add, [_dot(p.astype(BF16), v) for p, v in zip(ps, vparts)])
    return o / l


def _mla_attn_ctx_kernel(q_ref, kvx_ref, krp_ref, o_ref):
    krp = krp_ref[...]
    for hd in range(MLA_HEADS):
        lo = hd * (MLA_NOPE + MLA_V)
        qh = q_ref[:, hd * Q_HEAD_PAD:(hd + 1) * Q_HEAD_PAD]
        kh = jnp.concatenate([kvx_ref[:, lo:lo + MLA_NOPE], krp], axis=-1)
        vh = kvx_ref[:, lo + MLA_NOPE:lo + MLA_NOPE + MLA_V]
        o_ref[:, hd * MLA_V:(hd + 1) * MLA_V] = _mla_head(qh, [kh], [vh]).astype(BF16)


def _mla_attn_ctx(q, kvx, krp):
    tok = lambda w: pl.BlockSpec((SEQ, w), lambda b: (b, 0))
    return pl.pallas_call(
        _mla_attn_ctx_kernel,
        grid=(BATCH,),
        in_specs=[tok(q.shape[1]), tok(kvx.shape[1]), tok(LANES)],
        out_specs=tok(MLA_HEADS * MLA_V),
        out_shape=jax.ShapeDtypeStruct((N_CTX, MLA_HEADS * MLA_V), BF16),
        compiler_params=_cparams(("arbitrary",)),
        name="mla_attn_ctx",
    )(q, kvx, krp)


def _mla_attn_lat_kernel(q_ref, kvl_ref, krl_ref, kvc_ref, krc_ref, o_ref):
    krl = krl_ref[...]
    krc = krc_ref[...]
    for hd in range(MLA_HEADS):
        lo = hd * (MLA_NOPE + MLA_V)
        qh = q_ref[:, hd * Q_HEAD_PAD:(hd + 1) * Q_HEAD_PAD]
        kl = jnp.concatenate([kvl_ref[:, lo:lo + MLA_NOPE], krl], axis=-1)
        kc = jnp.concatenate([kvc_ref[:, lo:lo + MLA_NOPE], krc], axis=-1)
        vl = kvl_ref[:, lo + MLA_NOPE:lo + MLA_NOPE + MLA_V]
        vc = kvc_ref[:, lo + MLA_NOPE:lo + MLA_NOPE + MLA_V]
        o_ref[:, hd * MLA_V:(hd + 1) * MLA_V] = _mla_head(qh, [kl, kc], [vl, vc]).astype(BF16)


def _mla_attn_lat(q, kvx_lat, krp_lat, kvx_ctx, krp_ctx):
    tq = 256
    per = DEC_SEQ // tq
    wkv = kvx_lat.shape[-1]
    return pl.pallas_call(
        _mla_attn_lat_kernel,
        grid=(DEC_BATCH, per),
        in_specs=[pl.BlockSpec((tq, q.shape[1]), lambda b, i: (b * per + i, 0)),
                  pl.BlockSpec((None, DEC_SEQ, wkv), lambda b, i: (b, 0, 0)),
                  pl.BlockSpec((None, DEC_SEQ, LANES), lambda b, i: (b, 0, 0)),
                  pl.BlockSpec((None, PAST_LEN, wkv), lambda b, i: (b, 0, 0)),
                  pl.BlockSpec((None, PAST_LEN, LANES), lambda b, i: (b, 0, 0))],
        out_specs=pl.BlockSpec((tq, MLA_HEADS * MLA_V), lambda b, i: (b * per + i, 0)),
        out_shape=jax.ShapeDtypeStruct((N_LAT, MLA_HEADS * MLA_V), BF16),
        compiler_params=_cparams(("arbitrary", "arbitrary")),
        name="mla_attn_lat",
    )(q, kvx_lat.reshape(DEC_BATCH, DEC_SEQ, wkv), krp_lat.reshape(DEC_BATCH, DEC_SEQ, LANES), kvx_ctx, krp_ctx)


def _proj_ln_kernel(o_ref, x_ref, wo_ref, g_ref, lg_ref, lb_ref, x1_ref):
    out = _dot(o_ref[...], wo_ref[...])
    x1_ref[...] = _layer_norm(ALPHA * x_ref[...] + g_ref[...] * out, lg_ref[...], lb_ref[...])


def _top2(logits):
    lane = lax.broadcasted_iota(jnp.int32, logits.shape, 1).astype(F32)
    m1 = jnp.max(logits, axis=-1, keepdims=True)
    i1 = jnp.min(jnp.where(logits == m1, lane, float(LANES)), axis=-1, keepdims=True)
    rest = jnp.where(lane == i1, -jnp.inf, logits)
    m2 = jnp.max(rest, axis=-1, keepdims=True)
    i2 = jnp.min(jnp.where(rest == m2, lane, float(LANES)), axis=-1, keepdims=True)
    e = jnp.exp(m2 - m1)
    w1 = 1.0 / (1.0 + e)
    w2 = e / (1.0 + e)
    return lane, i1, i2, w1, w2


def _proj_ln_route_kernel(oc_ref, xc_ref, ol_ref, xl_ref, wo_ref, g_ref, lg_ref, lb_ref, sc_ref, sh_ref, wr_ref,
                          x1_ref, hm_ref, rt_ref, *, n_ctx_tiles):
    def body(o_ref, x_ref):
        out = _dot(o_ref[...], wo_ref[...])
        x1 = _layer_norm(ALPHA * x_ref[...] + g_ref[...] * out, lg_ref[...], lb_ref[...])
        x1_ref[...] = x1
        hm = x1 * (1.0 + sc_ref[...]) + sh_ref[...]
        hm_ref[...] = hm
        logits = _dot(hm.astype(BF16), wr_ref[...])
        lane = lax.broadcasted_iota(jnp.int32, logits.shape, 1)
        logits = jnp.where(lane < N_EXPERTS, logits, -jnp.inf)
        _, i1, i2, w1, w2 = _top2(logits)
        rt_ref[...] = jnp.where(lane == 0, i1, jnp.where(lane == 1, i2, jnp.where(lane == 2, w1, jnp.where(lane == 3, w2, 0.0))))

    is_ctx = pl.program_id(0) < n_ctx_tiles
    pl.when(is_ctx)(lambda: body(oc_ref, xc_ref))
    pl.when(jnp.logical_not(is_ctx))(lambda: body(ol_ref, xl_ref))


def _proj_ln(o, x, wo, mod, lg, lb, latent):
    n = x.shape[0]
    tm = TM_PROJ
    rf = _row_fn(latent, tm)
    tok = lambda w: pl.BlockSpec((tm, w), lambda i: (i, 0))
    return pl.pallas_call(
        _proj_ln_kernel,
        grid=(n // tm,),
        in_specs=[tok(o.shape[1]), tok(D_MODEL), _const_spec(wo.shape), _mod_spec(2, rf),
                  _const_spec(lg.shape), _const_spec(lb.shape)],
        out_specs=tok(D_MODEL),
        out_shape=jax.ShapeDtypeStruct((n, D_MODEL), F32),
        compiler_params=_cparams(("arbitrary",)),
        name="proj_ln_lat" if latent else "proj_ln_ctx",
    )(o, x, wo, mod, lg, lb)


def _all_row_fn(tm):
    nc = N_CTX // tm
    per = DEC_SEQ // tm
    return lambda i: jnp.where(i < nc, 0, 1 + (i - nc) // per)


def _proj_ln_route(o_ctx, x_ctx, o_lat, x_lat, wo, mod, lg, lb, wr):
    tm = TM_PROJ
    nc = N_CTX // tm
    rf = _all_row_fn(tm)
    ctx = lambda w: pl.BlockSpec((tm, w), lambda i: (jnp.minimum(i, nc - 1), 0))
    lat = lambda w: pl.BlockSpec((tm, w), lambda i: (jnp.maximum(i - nc, 0), 0))
    tok = lambda w: pl.BlockSpec((tm, w), lambda i: (i, 0))
    return pl.pallas_call(
        functools.partial(_proj_ln_route_kernel, n_ctx_tiles=nc),
        grid=(N_TOK // tm,),
        in_specs=[ctx(D_MODEL), ctx(D_MODEL), lat(D_MODEL), lat(D_MODEL), _const_spec(wo.shape), _mod_spec(2, rf),
                  _const_spec(lg.shape), _const_spec(lb.shape), _mod_spec(4, rf), _mod_spec(3, rf),
                  _const_spec(wr.shape)],
        out_specs=[tok(D_MODEL), tok(D_MODEL), tok(LANES)],
        out_shape=[jax.ShapeDtypeStruct((N_TOK, D_MODEL), F32),
                   jax.ShapeDtypeStruct((N_TOK, D_MODEL), F32),
                   jax.ShapeDtypeStruct((N_TOK, LANES), F32)],
        compiler_params=_cparams(("arbitrary",)),
        name="proj_ln_route",
    )(o_ctx, x_ctx, o_lat, x_lat, wo, mod, lg, lb, mod, mod, wr)


def _ffn_dense_kernel(x_ref, sc_ref, sh_ref, g_ref, wgu_ref, wd_ref, lg_ref, lb_ref,
                      o_ref, h_scr, acc_scr):
    f = pl.program_id(1)
    tf = wd_ref.shape[0]

    @pl.when(f == 0)
    def _():
        h_scr[...] = (x_ref[...] * (1.0 + sc_ref[...]) + sh_ref[...]).astype(BF16)
        acc_scr[...] = jnp.zeros_like(acc_scr)

    gu = _dot(h_scr[...], wgu_ref[...])
    a = (_silu(gu[:, :tf]) * gu[:, tf:]).astype(BF16)
    acc_scr[...] += _dot(a, wd_ref[...])

    @pl.when(f == pl.num_programs(1) - 1)
    def _():
        o_ref[...] = _layer_norm(ALPHA * x_ref[...] + g_ref[...] * acc_scr[...], lg_ref[...], lb_ref[...])


def _ffn_dense(x, mod, w_in, w_out, lg, lb, latent):
    n = x.shape[0]
    tm, tf = TM_FFN, TF_DENSE
    nf = FFN_DENSE // tf
    rf = _row_fn(latent, tm)
    mod_spec = lambda j: pl.BlockSpec((None, 1, D_MODEL), lambda i, f: (rf(i) * 6 + j, 0, 0))
    tok = pl.BlockSpec((tm, D_MODEL), lambda i, f: (i, 0))
    return pl.pallas_call(
        _ffn_dense_kernel,
        grid=(n // tm, nf),
        in_specs=[tok, mod_spec(4), mod_spec(3), mod_spec(5),
                  pl.BlockSpec((D_MODEL, 2 * tf), lambda i, f: (0, f)),
                  pl.BlockSpec((tf, D_MODEL), lambda i, f: (f, 0)),
                  pl.BlockSpec((1, D_MODEL), lambda i, f: (0, 0)),
                  pl.BlockSpec((1, D_MODEL), lambda i, f: (0, 0))],
        out_specs=tok,
        out_shape=jax.ShapeDtypeStruct((n, D_MODEL), F32),
        scratch_shapes=[pltpu.VMEM((tm, D_MODEL), BF16), pltpu.VMEM((tm, D_MODEL), F32)],
        compiler_params=_cparams(("arbitrary", "arbitrary")),
        name="ffn_dense_lat" if latent else "ffn_dense_ctx",
    )(x, mod, mod, mod, w_in, w_out, lg, lb)


def _na_qkv_kernel(x_ref, sc_ref, sh_ref, w_ref, q_ref, k_ref, v_ref, kt_ref):
    h = (x_ref[...] * (1.0 + sc_ref[...]) + sh_ref[...]).astype(BF16)
    qkv = _dot(h, w_ref[...])
    q_ref[...] = (qkv[:, :D_MODEL] * NA_SCALE).astype(BF16)
    k = qkv[:, D_MODEL:2 * D_MODEL]
    k_ref[...] = k.astype(k_ref.dtype)
    v_ref[...] = qkv[:, 2 * D_MODEL:].astype(v_ref.dtype)
    kt = k.T
    nblk, _, w = kt_ref.shape
    for t in range(nblk):
        kt_ref[t] = kt[:, t * w:(t + 1) * w].astype(BF16)


def _na_qkv(x, mod, w, kv_dtype, latent):
    n = x.shape[0]
    tm = TM_PROJ
    kt_w = NA_BLK_ROWS * GRID_W if latent else SEQ
    rf = _row_fn(latent, tm)
    tok = pl.BlockSpec((tm, D_MODEL), lambda i: (i, 0))
    return pl.pallas_call(
        _na_qkv_kernel,
        grid=(n // tm,),
        in_specs=[tok, _mod_spec(1, rf), _mod_spec(0, rf), _const_spec(w.shape)],
        out_specs=[tok, tok, tok, pl.BlockSpec((tm // kt_w, D_MODEL, kt_w), lambda i: (i, 0, 0))],
        out_shape=[jax.ShapeDtypeStruct((n, D_MODEL), BF16),
                   jax.ShapeDtypeStruct((n, D_MODEL), kv_dtype),
                   jax.ShapeDtypeStruct((n, D_MODEL), kv_dtype),
                   jax.ShapeDtypeStruct((n // kt_w, D_MODEL, kt_w), BF16)],
        compiler_params=_cparams(("arbitrary",)),
        name="na_qkv_lat" if latent else "na_qkv_ctx",
    )(x, mod, mod, w)


def _softmax_pv(scores, values):
    m = functools.reduce(jnp.maximum, [jnp.max(s, axis=-1, keepdims=True) for s in scores])
    ps = [jnp.exp(s - m) for s in scores]
    l = functools.reduce(jnp.add, [jnp.sum(p, axis=-1, keepdims=True) for p in ps])
    o = functools.reduce(jnp.add, [_dot(p.astype(BF16), v) for p, v in zip(ps, values)])
    return o / l


def _head_of_pair(x, half):
    lane = lax.broadcasted_iota(jnp.int32, x.shape, 1)
    keep = (lane < NA_HD) if half == 0 else (lane >= NA_HD)
    return jnp.where(keep, x, jnp.zeros_like(x))


def _merge_pair(o0, o1):
    lane = lax.broadcasted_iota(jnp.int32, o0.shape, 1)
    return jnp.where(lane < NA_HD, o0, o1)


def _na_attn_ctx_kernel(q_ref, kt_ref, v_ref, o_ref):
    for p in range(NA_HEADS // 2):
        cols = slice(p * NA_PAIR, (p + 1) * NA_PAIR)
        qp = q_ref[:, cols]
        ktp = kt_ref[cols, :]
        vp = v_ref[:, cols].astype(BF16)
        outs = [_softmax_pv([_dot(_head_of_pair(qp, half), ktp)], [vp]) for half in range(2)]
        o_ref[:, cols] = _merge_pair(*outs).astype(BF16)


def _na_attn_ctx(q, kt, v):
    tok = pl.BlockSpec((SEQ, D_MODEL), lambda b: (b, 0))
    return pl.pallas_call(
        _na_attn_ctx_kernel,
        grid=(BATCH,),
        in_specs=[tok, pl.BlockSpec((None, D_MODEL, SEQ), lambda b: (b, 0, 0)), tok],
        out_specs=tok,
        out_shape=jax.ShapeDtypeStruct((N_CTX, D_MODEL), BF16),
        compiler_params=_cparams(("arbitrary",)),
        name="na_attn_ctx",
    )(q, kt, v)


def _na_win_start(m):
    return jnp.clip(m - NA_WIN_ROWS // 4, 0, GRID_ROWS // 2 - NA_WIN_PAIRS)


def _na_attn_lat_kernel(q_ref, kt_ref, v_ref, kct_ref, vc_ref, bias_ref, o_ref):
    rp0 = _na_win_start(pl.program_id(1))
    blk = NA_BLK_ROWS * GRID_W
    win = pl.ds(pl.multiple_of(rp0 * blk, blk), NA_WIN_PAIRS * blk)
    for p in range(NA_HEADS // 2):
        cols = slice(p * NA_PAIR, (p + 1) * NA_PAIR)
        qp = q_ref[:, cols]
        kw = jnp.concatenate([kt_ref[rp0 + t, cols, :] for t in range(NA_WIN_PAIRS)], axis=1)
        kc = kct_ref[cols, :]
        vw = v_ref[win, cols]
        vc = vc_ref[:, cols]
        outs = []
        for half in range(2):
            qh = _head_of_pair(qp, half)
            s_nb = _dot(qh, kw) + bias_ref[2 * p + half]
            outs.append(_softmax_pv([s_nb, _dot(qh, kc)], [vw, vc]))
        o_ref[:, cols] = _merge_pair(*outs).astype(BF16)


def _na_bias_kernel(e_ref, e64_ref, o_ref):
    n_dr = 2 * NA_WIN_ROWS - 1
    shape = (GRID_W, LANES)
    lane = lax.broadcasted_iota(jnp.int32, shape, 1)
    c = lax.broadcasted_iota(jnp.int32, shape, 0)
    kc = lane & (GRID_W - 1)
    cs = jnp.clip(c - NA_WIN_COLS // 2, 0, GRID_W - NA_WIN_COLS)
    valid = (kc >= cs) & (kc < cs + NA_WIN_COLS)
    toeplitz = lambda ref, a: pltpu.roll(jnp.broadcast_to(ref[a:a + 1, :], shape), 0, 1, stride=1, stride_axis=0)
    rows = ([toeplitz(e_ref, a) for a in range(n_dr)], [toeplitz(e64_ref, a) for a in range(n_dr)])
    masked = jnp.full(shape, NEG_BIG, F32)
    half_blocks = GRID_ROWS // NA_BLK_ROWS
    for v, m in enumerate((0, 1, 2, half_blocks - 2, half_blocks - 1)):
        rp0 = min(max(m - NA_WIN_ROWS // 4, 0), half_blocks - NA_WIN_PAIRS)
        for i in range(NA_BLK_ROWS):
            r = NA_BLK_ROWS * m + i
            rs = min(max(r - NA_WIN_ROWS // 2, 0), GRID_ROWS - NA_WIN_ROWS)
            for t in range(NA_WIN_PAIRS):
                halves = []
                for u in range(2):
                    kr = 2 * (rp0 + t) + u
                    halves.append(rows[u][kr - r + NA_WIN_ROWS - 1] if rs <= kr < rs + NA_WIN_ROWS else masked)
                o_ref[v, i * GRID_W:(i + 1) * GRID_W, t * LANES:(t + 1) * LANES] = jnp.where(
                    valid, jnp.where(lane < GRID_W, halves[0], halves[1]), NEG_BIG)


def _na_bias_table(rpb):
    n_dr = 2 * NA_WIN_ROWS - 1
    blk = NA_BLK_ROWS * GRID_W
    e = jnp.zeros((NA_HEADS, n_dr, LANES), F32)
    e = e.at[:, :, :NA_WIN_COLS].set(rpb[:, :, NA_WIN_COLS - 1:])
    e = e.at[:, :, LANES - (NA_WIN_COLS - 1):].set(rpb[:, :, :NA_WIN_COLS - 1])
    e64 = jnp.roll(e, GRID_W, axis=-1)
    spec = pl.BlockSpec((None, n_dr, LANES), lambda h: (h, 0, 0))
    return pl.pallas_call(
        _na_bias_kernel,
        grid=(NA_HEADS,),
        in_specs=[spec, spec],
        out_specs=pl.BlockSpec((None, NA_N_VARIANTS, blk, NA_WIN_PAIRS * blk), lambda h: (h, 0, 0, 0)),
        out_shape=jax.ShapeDtypeStruct((NA_HEADS, NA_N_VARIANTS, blk, NA_WIN_PAIRS * blk), F32),
        compiler_params=_cparams(("arbitrary",)),
        name="na_bias_table",
    )(e, e64)


def _na_attn_lat(q, kt, v, kct, vc, bias_tab):
    blk = NA_BLK_ROWS * GRID_W
    nblk = DEC_SEQ // blk
    row = pl.BlockSpec((blk, D_MODEL), lambda b, m: (b * nblk + m, 0))
    bias_spec = pl.BlockSpec((NA_HEADS, None, blk, NA_WIN_PAIRS * blk), lambda b, m: (0, m - _na_win_start(m), 0, 0))
    return pl.pallas_call(
        _na_attn_lat_kernel,
        grid=(DEC_BATCH, nblk),
        in_specs=[row,
                  pl.BlockSpec((None, nblk, D_MODEL, blk), lambda b, m: (b, 0, 0, 0)),
                  pl.BlockSpec((None, DEC_SEQ, D_MODEL), lambda b, m: (b, 0, 0)),
                  pl.BlockSpec((None, D_MODEL, PAST_LEN), lambda b, m: (b, 0, 0)),
                  pl.BlockSpec((None, PAST_LEN, D_MODEL), lambda b, m: (b, 0, 0)),
                  bias_spec],
        out_specs=row,
        out_shape=jax.ShapeDtypeStruct((N_LAT, D_MODEL), BF16),
        compiler_params=_cparams(("arbitrary", "arbitrary")),
        name="na_attn_lat",
    )(q, kt.reshape(DEC_BATCH, nblk, D_MODEL, blk), v.reshape(DEC_BATCH, DEC_SEQ, D_MODEL), kct, vc, bias_tab)


def _route_rank_kernel(rt_ref, rank_ref, cnt_ref, carry):
    i = pl.program_id(0)

    @pl.when(i == 0)
    def _():
        carry[...] = jnp.zeros_like(carry)

    rt = rt_ref[...]
    tm = rt.shape[0]
    lane = lax.broadcasted_iota(jnp.int32, (tm, LANES), 1).astype(F32)
    oh1 = (lane == rt[:, 0:1]).astype(F32)
    oh2 = (lane == rt[:, 1:2]).astype(F32)
    rr = lax.broadcasted_iota(jnp.int32, (tm, tm), 0)
    cc = lax.broadcasted_iota(jnp.int32, (tm, tm), 1)
    below = (cc < rr).astype(BF16)
    tot1 = jnp.sum(oh1, axis=0, keepdims=True)
    tot2 = jnp.sum(oh2, axis=0, keepdims=True)
    base = carry[...]
    cum1 = _dot(below, oh1.astype(BF16)) + base
    cum2 = _dot(below, oh2.astype(BF16)) + base + tot1
    rank1 = jnp.sum(oh1 * cum1, axis=-1, keepdims=True)
    rank2 = jnp.sum(oh2 * cum2, axis=-1, keepdims=True)
    lane_i = lax.broadcasted_iota(jnp.int32, (tm, LANES), 1)
    rank_ref[...] = jnp.where(lane_i == 0, rank1, jnp.where(lane_i == 1, rank2, 0.0))
    carry[...] = base + tot1 + tot2
    cnt_ref[...] = jnp.broadcast_to(carry[...], cnt_ref.shape)


def _route_rank(rt_all):
    tm = TM_ROUTE
    return pl.pallas_call(
        _route_rank_kernel,
        grid=(N_TOK // tm,),
        in_specs=[pl.BlockSpec((tm, LANES), lambda i: (i, 0))],
        out_specs=[pl.BlockSpec((tm, LANES), lambda i: (i, 0)), pl.BlockSpec((8, LANES), lambda i: (0, 0))],
        out_shape=[jax.ShapeDtypeStruct((N_TOK, LANES), F32), jax.ShapeDtypeStruct((8, LANES), F32)],
        scratch_shapes=[pltpu.VMEM((1, LANES), F32)],
        compiler_params=_cparams(("arbitrary",)),
        name="route_rank",
    )(rt_all)


def _moe_scatter_kernel(pos_ref, h_ref, xs_in, xs_ref, sem):
    del xs_in
    i = pl.program_id(0)
    tm = h_ref.shape[0]

    def copies(t):
        src = h_ref.at[pl.ds(t, 1)]
        p0 = pos_ref[i * tm + t]
        p1 = pos_ref[N_TOK + i * tm + t]
        return (pltpu.make_async_copy(src, xs_ref.at[pl.ds(p0, 1)], sem),
                pltpu.make_async_copy(src, xs_ref.at[pl.ds(p1, 1)], sem))

    def start(t, c):
        for cp in copies(t):
            cp.start()
        return c

    lax.fori_loop(0, tm, start, 0, unroll=8)
    rows = xs_ref.at[pl.ds(0, 2 * tm)]
    pltpu.make_async_copy(rows, rows, sem).wait()


def _moe_scatter(pos, hm_all):
    tm = TM_ROUTE
    zeros = jnp.zeros((ROWS_SORTED, D_MODEL), F32)
    return pl.pallas_call(
        _moe_scatter_kernel,
        grid_spec=pltpu.PrefetchScalarGridSpec(
            num_scalar_prefetch=1,
            grid=(N_TOK // tm,),
            in_specs=[pl.BlockSpec((tm, D_MODEL), lambda i, pos: (i, 0)),
                      pl.BlockSpec(memory_space=pl.ANY)],
            out_specs=pl.BlockSpec(memory_space=pl.ANY),
            scratch_shapes=[pltpu.SemaphoreType.DMA(())],
        ),
        out_shape=jax.ShapeDtypeStruct((ROWS_SORTED, D_MODEL), F32),
        input_output_aliases={2: 0},
        compiler_params=_cparams(("arbitrary",)),
        name="moe_scatter",
    )(pos, hm_all, zeros)


def _tile_changed(te_ref, j):
    prev = te_ref[jnp.maximum(j - 1, 0)]
    return (j == 0) | (te_ref[j] != prev)


def _moe_up_kernel(te_ref, na_ref, x_ref, wg_ref, wu_ref, o_ref, w_scr):
    j = pl.program_id(1)
    tf = wg_ref.shape[1]

    @pl.when(j < na_ref[0])
    def _():
        @pl.when(_tile_changed(te_ref, j))
        def _():
            w_scr[:, :tf] = wg_ref[...].astype(BF16)
            w_scr[:, tf:] = wu_ref[...].astype(BF16)

        gu = _dot(x_ref[...].astype(BF16), w_scr[...])
        o_ref[...] = (_silu(gu[:, :tf]) * gu[:, tf:]).astype(BF16)

    @pl.when(j >= na_ref[0])
    def _():
        o_ref[...] = jnp.zeros_like(o_ref)


def _moe_up(te, na, xs, w_in):
    tm, tf = TM_MOE, TF_MOE
    nf = FFN_EXPERT // tf
    row = lambda j, na: jnp.minimum(j, na[0] - 1)
    return pl.pallas_call(
        _moe_up_kernel,
        grid_spec=pltpu.PrefetchScalarGridSpec(
            num_scalar_prefetch=2,
            grid=(nf, NT_MOE),
            in_specs=[pl.BlockSpec((tm, D_MODEL), lambda f, j, te, na: (row(j, na), 0)),
                      pl.BlockSpec((None, D_MODEL, tf), lambda f, j, te, na: (te[j], 0, f)),
                      pl.BlockSpec((None, D_MODEL, tf), lambda f, j, te, na: (te[j], 0, nf + f))],
            out_specs=pl.BlockSpec((tm, tf), lambda f, j, te, na: (j, f)),
            scratch_shapes=[pltpu.VMEM((D_MODEL, 2 * tf), BF16)],
        ),
        out_shape=jax.ShapeDtypeStruct((ROWS_SORTED, FFN_EXPERT), BF16),
        compiler_params=_cparams(("arbitrary", "arbitrary")),
        name="moe_up",
    )(te, na, xs, w_in, w_in)


def _moe_down_kernel(te_ref, na_ref, h_ref, w_ref, o_ref, w_scr):
    j = pl.program_id(1)

    @pl.when(j < na_ref[0])
    def _():
        @pl.when(_tile_changed(te_ref, j))
        def _():
            w_scr[...] = w_ref[...].astype(BF16)

        o_ref[...] = _dot(h_ref[...], w_scr[...])

    @pl.when(j >= na_ref[0])
    def _():
        o_ref[...] = jnp.zeros_like(o_ref)


def _moe_down(te, na, hmid, w_out):
    tm = TM_MOE
    tn = D_MODEL // 2
    row = lambda j, na: jnp.minimum(j, na[0] - 1)
    return pl.pallas_call(
        _moe_down_kernel,
        grid_spec=pltpu.PrefetchScalarGridSpec(
            num_scalar_prefetch=2,
            grid=(D_MODEL // tn, NT_MOE),
            in_specs=[pl.BlockSpec((tm, FFN_EXPERT), lambda n, j, te, na: (row(j, na), 0)),
                      pl.BlockSpec((None, FFN_EXPERT, tn), lambda n, j, te, na: (te[j], 0, n))],
            out_specs=pl.BlockSpec((tm, tn), lambda n, j, te, na: (j, n)),
            scratch_shapes=[pltpu.VMEM((FFN_EXPERT, tn), BF16)],
        ),
        out_shape=jax.ShapeDtypeStruct((ROWS_SORTED, D_MODEL), F32),
        compiler_params=_cparams(("arbitrary", "arbitrary")),
        name="moe_down",
    )(te, na, hmid, w_out)


def _moe_combine_kernel(pos_ref, y_ref, rt_ref, x_ref, g_ref, lg_ref, lb_ref, o_ref, ybuf, sem, *, tok_off):
    i = pl.program_id(0)
    tm = x_ref.shape[0]

    def copies(t):
        p0 = pos_ref[tok_off + i * tm + t]
        p1 = pos_ref[N_TOK + tok_off + i * tm + t]
        return (pltpu.make_async_copy(y_ref.at[pl.ds(p0, 1)], ybuf.at[0, pl.ds(t, 1)], sem),
                pltpu.make_async_copy(y_ref.at[pl.ds(p1, 1)], ybuf.at[1, pl.ds(t, 1)], sem))

    def start(t, c):
        for cp in copies(t):
            cp.start()
        return c

    lax.fori_loop(0, tm, start, 0, unroll=8)
    pltpu.make_async_copy(ybuf, ybuf, sem).wait()
    rt = rt_ref[...]
    moe = rt[:, 2:3] * ybuf[0] + rt[:, 3:4] * ybuf[1]
    o_ref[...] = _layer_norm(ALPHA * x_ref[...] + g_ref[...] * moe, lg_ref[...], lb_ref[...])


def _moe_combine(pos, y, rt_all, x_all, mod, lg, lb, latent):
    n = N_LAT if latent else N_CTX
    tm = TM_ROUTE
    rf = _row_fn(latent, tm)
    tok_off = N_CTX if latent else 0
    off = tok_off // tm
    return pl.pallas_call(
        functools.partial(_moe_combine_kernel, tok_off=tok_off),
        grid_spec=pltpu.PrefetchScalarGridSpec(
            num_scalar_prefetch=1,
            grid=(n // tm,),
            in_specs=[pl.BlockSpec(memory_space=pl.ANY),
                      pl.BlockSpec((tm, LANES), lambda i, pos: (i + off, 0)),
                      pl.BlockSpec((tm, D_MODEL), lambda i, pos: (i + off, 0)),
                      _mod_spec(5, rf),
                      pl.BlockSpec((1, D_MODEL), lambda i, pos: (0, 0)),
                      pl.BlockSpec((1, D_MODEL), lambda i, pos: (0, 0))],
            out_specs=pl.BlockSpec((tm, D_MODEL), lambda i, pos: (i, 0)),
            scratch_shapes=[pltpu.VMEM((2, tm, D_MODEL), F32), pltpu.SemaphoreType.DMA(())],
        ),
        out_shape=jax.ShapeDtypeStruct((n, D_MODEL), F32),
        compiler_params=_cparams(("arbitrary",)),
        name="moe_combine_lat" if latent else "moe_combine_ctx",
    )(pos, y, rt_all, x_all, mod, lg, lb)


def _routing_positions(rt_all, ranks, counts):
    tm = TM_MOE
    cnt = counts[0, :N_EXPERTS].astype(jnp.int32)
    padded = ((cnt + tm - 1) // tm) * tm
    gend = jnp.cumsum(padded)
    gstart = gend - padded
    i1 = rt_all[:, 0].astype(jnp.int32)
    i2 = rt_all[:, 1].astype(jnp.int32)
    pos = jnp.concatenate([gstart[i1] + ranks[:, 0].astype(jnp.int32),
                           gstart[i2] + ranks[:, 1].astype(jnp.int32)])
    tile_start = jnp.arange(NT_MOE, dtype=jnp.int32) * tm
    te = jnp.sum((tile_start[:, None] >= gend[None, :]).astype(jnp.int32), axis=1)
    n_active = (gend[-1] // tm).astype(jnp.int32)
    last_e = jnp.take(te, jnp.maximum(n_active - 1, 0))
    te = jnp.where(tile_start < gend[-1], te, last_e).astype(jnp.int32)
    return pos, te, n_active.reshape(1)


def kernel(x_prompt, x_sample, cache_ckv_l0, cache_krope_l0, cache_k_l1, cache_v_l1, c, c_ctx, w_ada_l0, b_ada_l0, mla_w_dq, mla_g_q, mla_w_uq, mla_w_dkv, mla_g_kv, mla_w_ukv, mla_w_o, ln1_g_l0, ln1_b_l0, ffn_w_in, ffn_w_out, ln2_g_l0, ln2_b_l0, w_ada_l1, b_ada_l1, na_w_qkv, na_rpb, na_w_o, ln1_g_l1, ln1_b_l1, moe_w_router, moe_w_in, moe_w_out, ln2_g_l1, ln2_b_l1):
    row = lambda v: v.reshape(1, -1)
    xp = x_prompt.reshape(N_CTX, D_MODEL)
    xs = x_sample.reshape(N_LAT, D_MODEL)
    groups = ((xp, False), (xs, True))

    cvecs = jnp.concatenate([c_ctx[None], c, jnp.zeros((8 - 1 - DEC_BATCH, D_MODEL), F32)], axis=0)
    mod0 = _ada_mod(cvecs, w_ada_l0, b_ada_l0)
    mod1 = _ada_mod(cvecs, w_ada_l1, b_ada_l1)

    wa = jnp.concatenate([mla_w_dq, mla_w_dkv], axis=1).astype(BF16)
    wuq = mla_w_uq.reshape(MLA_Q_LORA, MLA_HEADS, MLA_NOPE + MLA_ROPE)
    wuq = jnp.pad(wuq, ((0, 0), (0, 0), (0, Q_HEAD_PAD - MLA_NOPE - MLA_ROPE)))
    wuq = wuq.reshape(MLA_Q_LORA, MLA_HEADS * Q_HEAD_PAD).astype(BF16)
    wukv = mla_w_ukv.astype(BF16)
    wo0 = mla_w_o.astype(BF16)
    nf0 = FFN_DENSE // TF_DENSE
    w_in0 = ffn_w_in.reshape(D_MODEL, 2, nf0, TF_DENSE).transpose(0, 2, 1, 3).reshape(D_MODEL, 2 * FFN_DENSE).astype(BF16)
    w_out0 = ffn_w_out.astype(BF16)
    wqkv = na_w_qkv.astype(BF16)
    wo1 = na_w_o.astype(BF16)
    wr = jnp.pad(moe_w_router, ((0, 0), (0, LANES - N_EXPERTS))).astype(BF16)
    bias_tab = _na_bias_table(na_rpb)
    tab = _rope_tables()

    kvx_cache = _matmul(cache_ckv_l0.reshape(DEC_BATCH * PAST_LEN, MLA_KV_LORA), wukv, BF16, 512, "mla_expand_cache")
    kvx_cache = kvx_cache.reshape(DEC_BATCH, PAST_LEN, -1)
    krp_cache = jnp.pad(cache_krope_l0, ((0, 0), (0, 0), (0, LANES - MLA_ROPE))).astype(BF16)
    x1 = []
    new_ckv = new_kr = None
    for x, latent in groups:
        q, kvx, ckv, kr, krp = _mla_proj(x, mod0, wa, row(mla_g_q), row(mla_g_kv), wuq, wukv, tab, latent)
        if latent:
            o = _mla_attn_lat(q, kvx, krp, kvx_cache, krp_cache)
        else:
            new_ckv, new_kr = ckv, kr
            o = _mla_attn_ctx(q, kvx, krp)
        xa = _proj_ln(o, x, wo0, mod0, row(ln1_g_l0), row(ln1_b_l0), latent)
        x1.append(_ffn_dense(xa, mod0, w_in0, w_out0, row(ln2_g_l0), row(ln2_b_l0), latent))

    kct = cache_k_l1.reshape(DEC_BATCH, PAST_LEN, D_MODEL).transpose(0, 2, 1).astype(BF16)
    vc = cache_v_l1.reshape(DEC_BATCH, PAST_LEN, D_MODEL).astype(BF16)
    q, new_k, new_v, kt = _na_qkv(x1[0], mod1, wqkv, F32, False)
    o_ctx = _na_attn_ctx(q, kt, new_v)
    q, _, v, kt = _na_qkv(x1[1], mod1, wqkv, BF16, True)
    o_lat = _na_attn_lat(q, kt, v, kct, vc, bias_tab)
    x2_all, hm_all, rt_all = _proj_ln_route(o_ctx, x1[0], o_lat, x1[1], wo1, mod1, row(ln1_g_l1), row(ln1_b_l1), wr)

    ranks, counts = _route_rank(rt_all)
    pos, te, n_active = _routing_positions(rt_all, ranks, counts)
    x_sorted = _moe_scatter(pos, hm_all)
    hmid = _moe_up(te, n_active, x_sorted, moe_w_in)
    y = _moe_down(te, n_active, hmid, moe_w_out)
    outs = [_moe_combine(pos, y, rt_all, x2_all, mod1, row(ln2_g_l1), row(ln2_b_l1), latent)
            for latent in (False, True)]

    return (outs[0].reshape(BATCH, SEQ, D_MODEL),
            outs[1].reshape(DEC_BATCH, DEC_SEQ, D_MODEL),
            new_ckv.reshape(BATCH, SEQ, MLA_KV_LORA),
            new_kr.reshape(BATCH, SEQ, MLA_ROPE),
            new_k.reshape(BATCH, SEQ, NA_HEADS, NA_HD),
            new_v.reshape(BATCH, SEQ, NA_HEADS, NA_HD))
```

```python
import functools
import math

import numpy as np
import jax
import jax.numpy as jnp
from jax import lax
from jax.experimental import pallas as pl
from jax.experimental.pallas import tpu as pltpu

F32 = jnp.float32
BF16 = jnp.bfloat16

D_MODEL = 1024
BATCH = 32
SEQ = 256
DEPTH = 2
DEC_BATCH = 2
DEC_SEQ = 2048
PAST_LEN = 512
GRID_W = 64
MLA_HEADS = 8
MLA_NOPE = 128
MLA_ROPE = 64
MLA_V = 128
MLA_Q_LORA = 512
MLA_KV_LORA = 256
MLA_SCALE = 1.0 / math.sqrt(MLA_NOPE + MLA_ROPE)
ROPE_THETA = 10000.0
NA_HEADS = 16
NA_HD = D_MODEL // NA_HEADS
NA_WIN_ROWS = 8
NA_WIN_COLS = 16
NA_SCALE = 1.0 / math.sqrt(NA_HD)
FFN_DENSE = 2816
N_EXPERTS = 8
TOP_K = 2
FFN_EXPERT = 3584
ALPHA = (2 * DEPTH) ** 0.25
LN_EPS = 1e-5
RMS_EPS = 1e-6

N_CTX = BATCH * SEQ
N_LAT = DEC_BATCH * DEC_SEQ
N_TOK = N_CTX + N_LAT
GRID_ROWS = DEC_SEQ // GRID_W
Q_HEAD_PAD = 256
LANES = 128
SUBLANES = 8
NEG_BIG = -1e30
NA_PAIR = 2 * NA_HD
NA_BLK_ROWS = 2
NA_WIN_PAIRS = 5
NA_N_VARIANTS = 5

VMEM_LIMIT = 56 * 1024 * 1024

TM_PROJ = 512
TM_FFN = 512
MXU_TILE = 256
FFN_CHUNKS = ((0, 6 * MXU_TILE), (6 * MXU_TILE, FFN_DENSE))
TM_MOE = 512
TF_MOE = 1792
N_PAIRS = N_TOK * TOP_K
NT_MOE = (N_PAIRS + N_EXPERTS * TM_MOE) // TM_MOE
ROWS_SORTED = NT_MOE * TM_MOE
TM_ROUTE = 256


def _cparams(sem, vmem=VMEM_LIMIT):
    return pltpu.CompilerParams(dimension_semantics=sem, vmem_limit_bytes=vmem)


def _silu(x):
    return x * jax.nn.sigmoid(x)


def _layer_norm(y, g, b):
    mu = jnp.mean(y, axis=-1, keepdims=True)
    d = y - mu
    var = jnp.mean(d * d, axis=-1, keepdims=True)
    return d * lax.rsqrt(var + LN_EPS) * g + b


def _rms_norm(y, g):
    return y * lax.rsqrt(jnp.mean(y * y, axis=-1, keepdims=True) + RMS_EPS) * g


def _dot(a, b):
    return jnp.dot(a, b, preferred_element_type=F32)


def _dot_nt(a, b):
    return lax.dot_general(a, b, (((1,), (1,)), ((), ())), preferred_element_type=F32)


def _ada_kernel(c_ref, w_ref, b_ref, o_ref):
    s = _silu(c_ref[...]).astype(BF16)
    o_ref[...] = _dot(s, w_ref[...].astype(BF16)) + b_ref[...]


def _ada_mod(cvecs, w, b):
    tn = 1536
    m = pl.pallas_call(
        _ada_kernel,
        grid=(6 * D_MODEL // tn,),
        in_specs=[pl.BlockSpec((8, D_MODEL), lambda j: (0, 0)),
                  pl.BlockSpec((D_MODEL, tn), lambda j: (0, j)),
                  pl.BlockSpec((1, tn), lambda j: (0, j))],
        out_specs=pl.BlockSpec((8, tn), lambda j: (0, j)),
        out_shape=jax.ShapeDtypeStruct((8, 6 * D_MODEL), F32),
        compiler_params=_cparams(("arbitrary",)),
        name="ada_mod",
    )(cvecs, w, b.reshape(1, -1))
    return m[:3].reshape(3 * 6, 1, D_MODEL)


def _mod_spec(j, row_fn):
    return pl.BlockSpec((None, 1, D_MODEL), lambda i, *_: (row_fn(i) * 6 + j, 0, 0))


def _row_fn(latent, tm):
    if not latent:
        return lambda i: 0
    per = DEC_SEQ // tm
    return lambda i: 1 + i // per


def _const_spec(shape):
    nd = len(shape)
    return pl.BlockSpec(shape, lambda *_: (0,) * nd, pipeline_mode=pl.Buffered(1))


def _rope_kernel(invf_ref, o_ref):
    i = pl.program_id(0)
    tm = o_ref.shape[1]
    t = i * tm + lax.broadcasted_iota(jnp.int32, (tm, LANES), 0)
    lane = lax.broadcasted_iota(jnp.int32, (tm, LANES), 1)
    row = t >> int(math.log2(GRID_W))
    col = t & (GRID_W - 1)
    pos = jnp.where(lane < MLA_ROPE // 2, row, col).astype(F32)
    ang = pos * invf_ref[...]
    cos = jnp.cos(ang)
    sin = jnp.sin(ang)
    unit = lane >> int(math.log2(MLA_ROPE // 4))
    first = (unit == 0) | (unit == 2)
    second = (unit == 1) | (unit == 3)
    o_ref[0] = jnp.where(lane < MLA_ROPE, cos, 0.0)
    o_ref[1] = jnp.where(first, -sin, 0.0)
    o_ref[2] = jnp.where(second, sin, 0.0)


def _rope_tables():
    half = MLA_ROPE // 2
    inv_freq = (1.0 / (ROPE_THETA ** (np.arange(0, half, 2, dtype=np.float32) / half))).astype(np.float32)
    lane_f = np.zeros((1, LANES), np.float32)
    lane_f[0, :MLA_ROPE] = np.tile(inv_freq, 4)
    tm = 256
    return pl.pallas_call(
        _rope_kernel,
        grid=(DEC_SEQ // tm,),
        in_specs=[_const_spec((1, LANES))],
        out_specs=pl.BlockSpec((3, tm, LANES), lambda i: (0, i, 0)),
        out_shape=jax.ShapeDtypeStruct((3, DEC_SEQ, LANES), F32),
        compiler_params=_cparams(("arbitrary",)),
        name="rope_tables",
    )(jnp.asarray(lane_f))


def _rotate(v, tab_ref):
    return (v * tab_ref[0] + pltpu.roll(v, LANES - MLA_ROPE // 4, 1) * tab_ref[1]
            + pltpu.roll(v, MLA_ROPE // 4, 1) * tab_ref[2])


def _mla_proj_kernel(*refs, rope):
    if rope:
        (x_ref, sc_ref, sh_ref, wa_ref, gq_ref, gkv_ref, wuq_ref, wukv_ref, tab_ref,
         q_ref, kvx_ref, ckv_ref, kr_ref, krp_ref) = refs
    else:
        (x_ref, sc_ref, sh_ref, wa_ref, gq_ref, gkv_ref, wuq_ref, wukv_ref,
         q_ref, kvx_ref, ckv_ref, kr_ref, krp_ref) = refs
    h = (x_ref[...] * (1.0 + sc_ref[...]) + sh_ref[...]).astype(BF16)
    t = _dot(h, wa_ref[...])
    cq = _rms_norm(t[:, :MLA_Q_LORA], gq_ref[...])
    ckv = _rms_norm(t[:, MLA_Q_LORA:MLA_Q_LORA + MLA_KV_LORA], gkv_ref[...])
    kr = t[:, MLA_Q_LORA + MLA_KV_LORA:]
    ckv_ref[...] = ckv
    kr_ref[...] = kr
    kvx_ref[...] = _dot(ckv.astype(BF16), wukv_ref[...]).astype(BF16)
    q = _dot(cq.astype(BF16), wuq_ref[...])
    krp = jnp.concatenate([kr, jnp.zeros_like(kr)], axis=-1)
    if rope:
        krp_ref[...] = _rotate(krp, tab_ref).astype(BF16)
        for hd in range(MLA_HEADS):
            lo = hd * Q_HEAD_PAD
            q_ref[:, lo:lo + MLA_NOPE] = q[:, lo:lo + MLA_NOPE].astype(BF16)
            q_ref[:, lo + MLA_NOPE:lo + Q_HEAD_PAD] = _rotate(q[:, lo + MLA_NOPE:lo + Q_HEAD_PAD], tab_ref).astype(BF16)
    else:
        krp_ref[...] = krp.astype(BF16)
        q_ref[...] = q.astype(BF16)


def _mla_proj(x, mod, wa, gq, gkv, wuq, wukv, tab, latent):
    n = x.shape[0]
    tm = TM_PROJ
    rf = _row_fn(latent, tm)
    tok = lambda w: pl.BlockSpec((tm, w), lambda i: (i, 0))
    in_specs = [tok(D_MODEL), _mod_spec(1, rf), _mod_spec(0, rf),
                _const_spec(wa.shape), _const_spec(gq.shape), _const_spec(gkv.shape),
                _const_spec(wuq.shape), _const_spec(wukv.shape)]
    args = [x, mod, mod, wa, gq, gkv, wuq, wukv]
    if latent:
        per = DEC_SEQ // tm
        in_specs.append(pl.BlockSpec((3, tm, LANES), lambda i: (0, i % per, 0)))
        args.append(tab)
    return pl.pallas_call(
        functools.partial(_mla_proj_kernel, rope=latent),
        grid=(n // tm,),
        in_specs=in_specs,
        out_specs=[tok(MLA_HEADS * Q_HEAD_PAD), tok(MLA_HEADS * (MLA_NOPE + MLA_V)),
                   tok(MLA_KV_LORA), tok(MLA_ROPE), tok(LANES)],
        out_shape=[jax.ShapeDtypeStruct((n, MLA_HEADS * Q_HEAD_PAD), BF16),
                   jax.ShapeDtypeStruct((n, MLA_HEADS * (MLA_NOPE + MLA_V)), BF16),
                   jax.ShapeDtypeStruct((n, MLA_KV_LORA), F32),
                   jax.ShapeDtypeStruct((n, MLA_ROPE), F32),
                   jax.ShapeDtypeStruct((n, LANES), BF16)],
        compiler_params=_cparams(("arbitrary",)),
        name="mla_proj_lat" if latent else "mla_proj_ctx",
    )(*args)


def _matmul_kernel(a_ref, b_ref, o_ref):
    o_ref[...] = _dot(a_ref[...].astype(BF16), b_ref[...]).astype(o_ref.dtype)


def _matmul(a, b, out_dtype, tm, name):
    m, k = a.shape
    n = b.shape[1]
    return pl.pallas_call(
        _matmul_kernel,
        grid=(m // tm,),
        in_specs=[pl.BlockSpec((tm, k), lambda i: (i, 0)), _const_spec(b.shape)],
        out_specs=pl.BlockSpec((tm, n), lambda i: (i, 0)),
        out_shape=jax.ShapeDtypeStruct((m, n), out_dtype),
        compiler_params=_cparams(("arbitrary",)),
        name=name,
    )(a, b)


def _mla_head(qh, kparts, vparts):
    ss = [_dot_nt(qh, k) * MLA_SCALE for k in kparts]
    m = functools.reduce(jnp.maximum, [jnp.max(s, axis=-1, keepdims=True) for s in ss])
    ps = [jnp.exp(s - m) for s in ss]
    l = functools.reduce(jnp.add, [jnp.sum(p, axis=-1, keepdims=True) for p in ps])
    o = functools.reduce(jnp.add, [_dot(p.astype(BF16), v) for p, v in zip(ps, vparts)])
    return o / l


def _mla_attn_ctx_kernel(q_ref, kvx_ref, krp_ref, o_ref):
    krp = krp_ref[...]
    for hd in range(MLA_HEADS):
        lo = hd * (MLA_NOPE + MLA_V)
        qh = q_ref[:, hd * Q_HEAD_PAD:(hd + 1) * Q_HEAD_PAD]
        kh = jnp.concatenate([kvx_ref[:, lo:lo + MLA_NOPE], krp], axis=-1)
        vh = kvx_ref[:, lo + MLA_NOPE:lo + MLA_NOPE + MLA_V]
        o_ref[:, hd * MLA_V:(hd + 1) * MLA_V] = _mla_head(qh, [kh], [vh]).astype(BF16)


def _mla_attn_ctx(q, kvx, krp):
    tok = lambda w: pl.BlockSpec((SEQ, w), lambda b: (b, 0))
    return pl.pallas_call(
        _mla_attn_ctx_kernel,
        grid=(BATCH,),
        in_specs=[tok(q.shape[1]), tok(kvx.shape[1]), tok(LANES)],
        out_specs=tok(MLA_HEADS * MLA_V),
        out_shape=jax.ShapeDtypeStruct((N_CTX, MLA_HEADS * MLA_V), BF16),
        compiler_params=_cparams(("arbitrary",)),
        name="mla_attn_ctx",
    )(q, kvx, krp)


def _mla_attn_lat_kernel(q_ref, kvl_ref, krl_ref, kvc_ref, krc_ref, o_ref):
    krl = krl_ref[...]
    krc = krc_ref[...]
    for hd in range(MLA_HEADS):
        lo = hd * (MLA_NOPE + MLA_V)
        qh = q_ref[:, hd * Q_HEAD_PAD:(hd + 1) * Q_HEAD_PAD]
        kl = jnp.concatenate([kvl_ref[:, lo:lo + MLA_NOPE], krl], axis=-1)
        kc = jnp.concatenate([kvc_ref[:, lo:lo + MLA_NOPE], krc], axis=-1)
        vl = kvl_ref[:, lo + MLA_NOPE:lo + MLA_NOPE + MLA_V]
        vc = kvc_ref[:, lo + MLA_NOPE:lo + MLA_NOPE + MLA_V]
        o_ref[:, hd * MLA_V:(hd + 1) * MLA_V] = _mla_head(qh, [kl, kc], [vl, vc]).astype(BF16)


def _mla_attn_lat(q, kvx_lat, krp_lat, kvx_ctx, krp_ctx):
    tq = 256
    per = DEC_SEQ // tq
    wkv = kvx_lat.shape[-1]
    return pl.pallas_call(
        _mla_attn_lat_kernel,
        grid=(DEC_BATCH, per),
        in_specs=[pl.BlockSpec((tq, q.shape[1]), lambda b, i: (b * per + i, 0)),
                  pl.BlockSpec((None, DEC_SEQ, wkv), lambda b, i: (b, 0, 0)),
                  pl.BlockSpec((None, DEC_SEQ, LANES), lambda b, i: (b, 0, 0)),
                  pl.BlockSpec((None, PAST_LEN, wkv), lambda b, i: (b, 0, 0)),
                  pl.BlockSpec((None, PAST_LEN, LANES), lambda b, i: (b, 0, 0))],
        out_specs=pl.BlockSpec((tq, MLA_HEADS * MLA_V), lambda b, i: (b * per + i, 0)),
        out_shape=jax.ShapeDtypeStruct((N_LAT, MLA_HEADS * MLA_V), BF16),
        compiler_params=_cparams(("arbitrary", "arbitrary")),
        name="mla_attn_lat",
    )(q, kvx_lat.reshape(DEC_BATCH, DEC_SEQ, wkv), krp_lat.reshape(DEC_BATCH, DEC_SEQ, LANES), kvx_ctx, krp_ctx)


def _top2(logits):
    lane = lax.broadcasted_iota(jnp.int32, logits.shape, 1).astype(F32)
    m1 = jnp.max(logits, axis=-1, keepdims=True)
    i1 = jnp.min(jnp.where(logits == m1, lane, float(LANES)), axis=-1, keepdims=True)
    rest = jnp.where(lane == i1, -jnp.inf, logits)
    m2 = jnp.max(rest, axis=-1, keepdims=True)
    i2 = jnp.min(jnp.where(rest == m2, lane, float(LANES)), axis=-1, keepdims=True)
    e = jnp.exp(m2 - m1)
    w1 = 1.0 / (1.0 + e)
    w2 = e / (1.0 + e)
    return lane, i1, i2, w1, w2


def _proj_ln_route_kernel(oc_ref, xc_ref, ol_ref, xl_ref, wo_ref, g_ref, lg_ref, lb_ref, sc_ref, sh_ref, wr_ref,
                          x1_ref, rt_ref, cnt_ref, carry, *, n_ctx_tiles):
    i = pl.program_id(0)

    @pl.when(i == 0)
    def _():
        carry[...] = jnp.zeros_like(carry)

    def body(o_ref, x_ref):
        out = _dot(o_ref[...], wo_ref[...])
        x1 = _layer_norm(ALPHA * x_ref[...] + g_ref[...] * out, lg_ref[...], lb_ref[...])
        x1_ref[...] = x1
        hm = x1 * (1.0 + sc_ref[...]) + sh_ref[...]
        logits = _dot(hm.astype(BF16), wr_ref[...])
        lane = lax.broadcasted_iota(jnp.int32, logits.shape, 1)
        logits = jnp.where(lane < N_EXPERTS, logits, -jnp.inf)
        lane_f, i1, i2, w1, w2 = _top2(logits)
        tm = logits.shape[0]
        oh1 = (lane_f == i1).astype(F32)
        oh2 = (lane_f == i2).astype(F32)
        rr = lax.broadcasted_iota(jnp.int32, (tm, tm), 0)
        cc = lax.broadcasted_iota(jnp.int32, (tm, tm), 1)
        below = (cc < rr).astype(BF16)
        tot1 = jnp.sum(oh1, axis=0, keepdims=True)
        tot2 = jnp.sum(oh2, axis=0, keepdims=True)
        base = carry[...]
        cum1 = _dot(below, oh1.astype(BF16)) + base
        cum2 = _dot(below, oh2.astype(BF16)) + (base + tot1)
        rank1 = jnp.sum(oh1 * cum1, axis=-1, keepdims=True)
        rank2 = jnp.sum(oh2 * cum2, axis=-1, keepdims=True)
        vals = (i1, i2, w1, w2, rank1, rank2)
        rt = jnp.zeros_like(logits)
        for k, val in enumerate(vals):
            rt = jnp.where(lane == k, val, rt)
        rt_ref[...] = rt
        carry[...] = base + tot1 + tot2
        cnt_ref[...] = jnp.broadcast_to(carry[...], cnt_ref.shape)

    is_ctx = i < n_ctx_tiles
    pl.when(is_ctx)(lambda: body(oc_ref, xc_ref))
    pl.when(jnp.logical_not(is_ctx))(lambda: body(ol_ref, xl_ref))


def _all_row_fn(tm):
    nc = N_CTX // tm
    per = DEC_SEQ // tm
    return lambda i: jnp.where(i < nc, 0, 1 + (i - nc) // per)


def _proj_ln_route(o_ctx, x_ctx, o_lat, x_lat, wo, mod, lg, lb, wr):
    tm = TM_PROJ
    nc = N_CTX // tm
    rf = _all_row_fn(tm)
    ctx = lambda w: pl.BlockSpec((tm, w), lambda i: (jnp.minimum(i, nc - 1), 0))
    lat = lambda w: pl.BlockSpec((tm, w), lambda i: (jnp.maximum(i - nc, 0), 0))
    tok = lambda w: pl.BlockSpec((tm, w), lambda i: (i, 0))
    return pl.pallas_call(
        functools.partial(_proj_ln_route_kernel, n_ctx_tiles=nc),
        grid=(N_TOK // tm,),
        in_specs=[ctx(D_MODEL), ctx(D_MODEL), lat(D_MODEL), lat(D_MODEL), _const_spec(wo.shape), _mod_spec(2, rf),
                  _const_spec(lg.shape), _const_spec(lb.shape), _mod_spec(4, rf), _mod_spec(3, rf),
                  _const_spec(wr.shape)],
        out_specs=[tok(D_MODEL), tok(LANES), pl.BlockSpec((8, LANES), lambda i: (0, 0))],
        out_shape=[jax.ShapeDtypeStruct((N_TOK, D_MODEL), F32),
                   jax.ShapeDtypeStruct((N_TOK, LANES), F32),
                   jax.ShapeDtypeStruct((8, LANES), F32)],
        scratch_shapes=[pltpu.VMEM((1, LANES), F32)],
        compiler_params=_cparams(("arbitrary",)),
        name="proj_ln_route",
    )(o_ctx, x_ctx, o_lat, x_lat, wo, mod, lg, lb, mod, mod, wr)


def _attn_out_ffn_kernel(o_ref, x_ref, wo_ref, g1_ref, lg1_ref, lb1_ref, sc_ref, sh_ref, g2_ref,
                         wi_ref, wd_ref, lg2_ref, lb2_ref, y_ref):
    x1 = _layer_norm(ALPHA * x_ref[...] + g1_ref[...] * _dot(o_ref[...], wo_ref[...]), lg1_ref[...], lb1_ref[...])
    h = (x1 * (1.0 + sc_ref[...]) + sh_ref[...]).astype(BF16)
    acc = None
    for lo, hi in FFN_CHUNKS:
        gate = _dot(h, wi_ref[:, lo:hi])
        up = _dot(h, wi_ref[:, FFN_DENSE + lo:FFN_DENSE + hi])
        part = _dot((_silu(gate) * up).astype(BF16), wd_ref[lo:hi, :])
        acc = part if acc is None else acc + part
    y_ref[...] = _layer_norm(ALPHA * x1 + g2_ref[...] * acc, lg2_ref[...], lb2_ref[...])


def _attn_out_ffn(o, x, wo, mod, lg1, lb1, w_in, w_out, lg2, lb2, latent):
    n = x.shape[0]
    tm = TM_FFN
    rf = _row_fn(latent, tm)
    tok = lambda w: pl.BlockSpec((tm, w), lambda i: (i, 0))
    vec = _const_spec((1, D_MODEL))
    return pl.pallas_call(
        _attn_out_ffn_kernel,
        grid=(n // tm,),
        in_specs=[tok(o.shape[1]), tok(D_MODEL), _const_spec(wo.shape), _mod_spec(2, rf), vec, vec,
                  _mod_spec(4, rf), _mod_spec(3, rf), _mod_spec(5, rf),
                  _const_spec(w_in.shape), _const_spec(w_out.shape), vec, vec],
        out_specs=tok(D_MODEL),
        out_shape=jax.ShapeDtypeStruct((n, D_MODEL), F32),
        compiler_params=_cparams(("arbitrary",)),
        name="attn_out_ffn_lat" if latent else "attn_out_ffn_ctx",
    )(o, x, wo, mod, lg1, lb1, mod, mod, mod, w_in, w_out, lg2, lb2)


def _na_qkv_kernel(x_ref, sc_ref, sh_ref, w_ref, q_ref, k_ref, v_ref, kt_ref):
    h = (x_ref[...] * (1.0 + sc_ref[...]) + sh_ref[...]).astype(BF16)
    qkv = _dot(h, w_ref[...])
    q_ref[...] = (qkv[:, :D_MODEL] * NA_SCALE).astype(BF16)
    k = qkv[:, D_MODEL:2 * D_MODEL]
    k_ref[...] = k.astype(k_ref.dtype)
    v_ref[...] = qkv[:, 2 * D_MODEL:].astype(v_ref.dtype)
    kt = k.T
    nblk, _, w = kt_ref.shape
    for t in range(nblk):
        kt_ref[t] = kt[:, t * w:(t + 1) * w].astype(BF16)


def _na_qkv(x, mod, w, kv_dtype, latent):
    n = x.shape[0]
    tm = TM_PROJ
    kt_w = NA_BLK_ROWS * GRID_W if latent else SEQ
    rf = _row_fn(latent, tm)
    tok = pl.BlockSpec((tm, D_MODEL), lambda i: (i, 0))
    return pl.pallas_call(
        _na_qkv_kernel,
        grid=(n // tm,),
        in_specs=[tok, _mod_spec(1, rf), _mod_spec(0, rf), _const_spec(w.shape)],
        out_specs=[tok, tok, tok, pl.BlockSpec((tm // kt_w, D_MODEL, kt_w), lambda i: (i, 0, 0))],
        out_shape=[jax.ShapeDtypeStruct((n, D_MODEL), BF16),
                   jax.ShapeDtypeStruct((n, D_MODEL), kv_dtype),
                   jax.ShapeDtypeStruct((n, D_MODEL), kv_dtype),
                   jax.ShapeDtypeStruct((n // kt_w, D_MODEL, kt_w), BF16)],
        compiler_params=_cparams(("arbitrary",)),
        name="na_qkv_lat" if latent else "na_qkv_ctx",
    )(x, mod, mod, w)


def _softmax_pv(scores, values):
    m = functools.reduce(jnp.maximum, [jnp.max(s, axis=-1, keepdims=True) for s in scores])
    ps = [jnp.exp(s - m) for s in scores]
    l = functools.reduce(jnp.add, [jnp.sum(p, axis=-1, keepdims=True) for p in ps])
    o = functools.reduce(jnp.add, [_dot(p.astype(BF16), v) for p, v in zip(ps, values)])
    return o / l


def _head_of_pair(x, half):
    lane = lax.broadcasted_iota(jnp.int32, x.shape, 1)
    keep = (lane < NA_HD) if half == 0 else (lane >= NA_HD)
    return jnp.where(keep, x, jnp.zeros_like(x))


def _merge_pair(o0, o1):
    lane = lax.broadcasted_iota(jnp.int32, o0.shape, 1)
    return jnp.where(lane < NA_HD, o0, o1)


def _na_attn_ctx_kernel(q_ref, kt_ref, v_ref, o_ref):
    for p in range(NA_HEADS // 2):
        cols = slice(p * NA_PAIR, (p + 1) * NA_PAIR)
        qp = q_ref[:, cols]
        ktp = kt_ref[cols, :]
        vp = v_ref[:, cols].astype(BF16)
        outs = [_softmax_pv([_dot(_head_of_pair(qp, half), ktp)], [vp]) for half in range(2)]
        o_ref[:, cols] = _merge_pair(*outs).astype(BF16)


def _na_attn_ctx(q, kt, v):
    tok = pl.BlockSpec((SEQ, D_MODEL), lambda b: (b, 0))
    return pl.pallas_call(
        _na_attn_ctx_kernel,
        grid=(BATCH,),
        in_specs=[tok, pl.BlockSpec((None, D_MODEL, SEQ), lambda b: (b, 0, 0)), tok],
        out_specs=tok,
        out_shape=jax.ShapeDtypeStruct((N_CTX, D_MODEL), BF16),
        compiler_params=_cparams(("arbitrary",)),
        name="na_attn_ctx",
    )(q, kt, v)


def _na_win_start(m):
    return jnp.clip(m - NA_WIN_ROWS // 4, 0, GRID_ROWS // 2 - NA_WIN_PAIRS)


def _na_attn_lat_kernel(q_ref, kt_ref, v_ref, kct_ref, vc_ref, bias_ref, o_ref):
    rp0 = _na_win_start(pl.program_id(1))
    blk = NA_BLK_ROWS * GRID_W
    win = pl.ds(pl.multiple_of(rp0 * blk, blk), NA_WIN_PAIRS * blk)
    for p in range(NA_HEADS // 2):
        cols = slice(p * NA_PAIR, (p + 1) * NA_PAIR)
        qp = q_ref[:, cols]
        kw = jnp.concatenate([kt_ref[rp0 + t, cols, :] for t in range(NA_WIN_PAIRS)], axis=1)
        kc = kct_ref[cols, :]
        vw = v_ref[win, cols]
        vc = vc_ref[:, cols]
        outs = []
        for half in range(2):
            qh = _head_of_pair(qp, half)
            s_nb = _dot(qh, kw) + bias_ref[2 * p + half]
            outs.append(_softmax_pv([s_nb, _dot(qh, kc)], [vw, vc]))
        o_ref[:, cols] = _merge_pair(*outs).astype(BF16)


def _na_bias_kernel(e_ref, e64_ref, o_ref):
    n_dr = 2 * NA_WIN_ROWS - 1
    shape = (GRID_W, LANES)
    lane = lax.broadcasted_iota(jnp.int32, shape, 1)
    c = lax.broadcasted_iota(jnp.int32, shape, 0)
    kc = lane & (GRID_W - 1)
    cs = jnp.clip(c - NA_WIN_COLS // 2, 0, GRID_W - NA_WIN_COLS)
    valid = (kc >= cs) & (kc < cs + NA_WIN_COLS)
    toeplitz = lambda ref, a: pltpu.roll(jnp.broadcast_to(ref[a:a + 1, :], shape), 0, 1, stride=1, stride_axis=0)
    rows = ([toeplitz(e_ref, a) for a in range(n_dr)], [toeplitz(e64_ref, a) for a in range(n_dr)])
    masked = jnp.full(shape, NEG_BIG, F32)
    half_blocks = GRID_ROWS // NA_BLK_ROWS
    for v, m in enumerate((0, 1, 2, half_blocks - 2, half_blocks - 1)):
        rp0 = min(max(m - NA_WIN_ROWS // 4, 0), half_blocks - NA_WIN_PAIRS)
        for i in range(NA_BLK_ROWS):
            r = NA_BLK_ROWS * m + i
            rs = min(max(r - NA_WIN_ROWS // 2, 0), GRID_ROWS - NA_WIN_ROWS)
            for t in range(NA_WIN_PAIRS):
                halves = []
                for u in range(2):
                    kr = 2 * (rp0 + t) + u
                    halves.append(rows[u][kr - r + NA_WIN_ROWS - 1] if rs <= kr < rs + NA_WIN_ROWS else masked)
                o_ref[v, i * GRID_W:(i + 1) * GRID_W, t * LANES:(t + 1) * LANES] = jnp.where(
                    valid, jnp.where(lane < GRID_W, halves[0], halves[1]), NEG_BIG)


def _na_bias_table(rpb):
    n_dr = 2 * NA_WIN_ROWS - 1
    blk = NA_BLK_ROWS * GRID_W
    e = jnp.zeros((NA_HEADS, n_dr, LANES), F32)
    e = e.at[:, :, :NA_WIN_COLS].set(rpb[:, :, NA_WIN_COLS - 1:])
    e = e.at[:, :, LANES - (NA_WIN_COLS - 1):].set(rpb[:, :, :NA_WIN_COLS - 1])
    e64 = jnp.roll(e, GRID_W, axis=-1)
    spec = pl.BlockSpec((None, n_dr, LANES), lambda h: (h, 0, 0))
    return pl.pallas_call(
        _na_bias_kernel,
        grid=(NA_HEADS,),
        in_specs=[spec, spec],
        out_specs=pl.BlockSpec((None, NA_N_VARIANTS, blk, NA_WIN_PAIRS * blk), lambda h: (h, 0, 0, 0)),
        out_shape=jax.ShapeDtypeStruct((NA_HEADS, NA_N_VARIANTS, blk, NA_WIN_PAIRS * blk), F32),
        compiler_params=_cparams(("arbitrary",)),
        name="na_bias_table",
    )(e, e64)


def _na_attn_lat(q, kt, v, kct, vc, bias_tab):
    blk = NA_BLK_ROWS * GRID_W
    nblk = DEC_SEQ // blk
    row = pl.BlockSpec((blk, D_MODEL), lambda b, m: (b * nblk + m, 0))
    bias_spec = pl.BlockSpec((NA_HEADS, None, blk, NA_WIN_PAIRS * blk), lambda b, m: (0, m - _na_win_start(m), 0, 0))
    return pl.pallas_call(
        _na_attn_lat_kernel,
        grid=(DEC_BATCH, nblk),
        in_specs=[row,
                  pl.BlockSpec((None, nblk, D_MODEL, blk), lambda b, m: (b, 0, 0, 0)),
                  pl.BlockSpec((None, DEC_SEQ, D_MODEL), lambda b, m: (b, 0, 0)),
                  pl.BlockSpec((None, D_MODEL, PAST_LEN), lambda b, m: (b, 0, 0)),
                  pl.BlockSpec((None, PAST_LEN, D_MODEL), lambda b, m: (b, 0, 0)),
                  bias_spec],
        out_specs=row,
        out_shape=jax.ShapeDtypeStruct((N_LAT, D_MODEL), BF16),
        compiler_params=_cparams(("arbitrary", "arbitrary")),
        name="na_attn_lat",
    )(q, kt.reshape(DEC_BATCH, nblk, D_MODEL, blk), v.reshape(DEC_BATCH, DEC_SEQ, D_MODEL), kct, vc, bias_tab)


def _moe_scatter_kernel(pos_ref, pad_ref, na_ref, x_ref, sc_ref, sh_ref, xs_ref, hm_scr, zero_scr, sem, zsem):
    i = pl.program_id(0)
    tm = x_ref.shape[0]
    hm_scr[...] = x_ref[...] * (1.0 + sc_ref[...]) + sh_ref[...]

    def copies(t):
        src = hm_scr.at[pl.ds(t, 1)]
        p0 = pos_ref[i * tm + t]
        p1 = pos_ref[N_TOK + i * tm + t]
        return (pltpu.make_async_copy(src, xs_ref.at[pl.ds(p0, 1)], sem),
                pltpu.make_async_copy(src, xs_ref.at[pl.ds(p1, 1)], sem))

    def start(t, c):
        for cp in copies(t):
            cp.start()
        return c

    lax.fori_loop(0, tm, start, 0, unroll=8)

    @pl.when(i == pl.num_programs(0) - 1)
    def _():
        zero_scr[...] = jnp.zeros_like(zero_scr)

        def pad_copies():
            out = []
            zero = lambda b: zero_scr.at[pl.ds(0, b)]
            for e in range(N_EXPERTS):
                off = pad_ref[e]
                n = pad_ref[N_EXPERTS + e]
                head = n & (SUBLANES - 1)
                for k in range(SUBLANES - 1):
                    out.append((k < head, pltpu.make_async_copy(zero(1), xs_ref.at[pl.ds(off + k, 1)], zsem)))
                off = off + head
                b = SUBLANES
                while b < TM_MOE:
                    rows = pl.ds(pl.multiple_of(off, SUBLANES), b)
                    out.append(((n & b) != 0, pltpu.make_async_copy(zero(b), xs_ref.at[rows], zsem)))
                    off = off + (n & b)
                    b *= 2
            return out

        def tile_copy(j):
            rows = pl.ds(pl.multiple_of(j * TM_MOE, TM_MOE), TM_MOE)
            return pltpu.make_async_copy(zero_scr, xs_ref.at[rows], zsem)

        for cond, cp in pad_copies():
            pl.when(cond)(cp.start)
        lax.fori_loop(na_ref[0], NT_MOE, lambda j, c: (tile_copy(j).start(), c)[1], 0)
        for cond, cp in pad_copies():
            pl.when(cond)(cp.wait)
        lax.fori_loop(na_ref[0], NT_MOE, lambda j, c: (tile_copy(j).wait(), c)[1], 0)

    rows = xs_ref.at[pl.ds(0, 2 * tm)]
    pltpu.make_async_copy(rows, rows, sem).wait()


def _moe_scatter(pos, pad, na, x_all, mod):
    tm = TM_ROUTE
    rf = _all_row_fn(tm)
    return pl.pallas_call(
        _moe_scatter_kernel,
        grid_spec=pltpu.PrefetchScalarGridSpec(
            num_scalar_prefetch=3,
            grid=(N_TOK // tm,),
            in_specs=[pl.BlockSpec((tm, D_MODEL), lambda i, *_: (i, 0)), _mod_spec(4, rf), _mod_spec(3, rf)],
            out_specs=pl.BlockSpec(memory_space=pl.ANY),
            scratch_shapes=[pltpu.VMEM((tm, D_MODEL), F32), pltpu.VMEM((TM_MOE, D_MODEL), F32),
                            pltpu.SemaphoreType.DMA(()), pltpu.SemaphoreType.DMA(())],
        ),
        out_shape=jax.ShapeDtypeStruct((ROWS_SORTED, D_MODEL), F32),
        compiler_params=_cparams(("arbitrary",)),
        name="moe_scatter",
    )(pos, pad, na, x_all, mod, mod)


def _tile_changed(te_ref, j):
    prev = te_ref[jnp.maximum(j - 1, 0)]
    return (j == 0) | (te_ref[j] != prev)


def _moe_up_kernel(te_ref, na_ref, x_ref, wg_ref, wu_ref, o_ref, w_scr):
    j = pl.program_id(1)
    tf = wg_ref.shape[1]

    @pl.when(j < na_ref[0])
    def _():
        @pl.when(_tile_changed(te_ref, j))
        def _():
            w_scr[:, :tf] = wg_ref[...].astype(BF16)
            w_scr[:, tf:] = wu_ref[...].astype(BF16)

        gu = _dot(x_ref[...].astype(BF16), w_scr[...])
        o_ref[...] = (_silu(gu[:, :tf]) * gu[:, tf:]).astype(BF16)

    @pl.when(j >= na_ref[0])
    def _():
        o_ref[...] = jnp.zeros_like(o_ref)


def _moe_up(te, na, xs, w_in):
    tm, tf = TM_MOE, TF_MOE
    nf = FFN_EXPERT // tf
    row = lambda j, na: jnp.minimum(j, na[0] - 1)
    return pl.pallas_call(
        _moe_up_kernel,
        grid_spec=pltpu.PrefetchScalarGridSpec(
            num_scalar_prefetch=2,
            grid=(nf, NT_MOE),
            in_specs=[pl.BlockSpec((tm, D_MODEL), lambda f, j, te, na: (row(j, na), 0)),
                      pl.BlockSpec((None, D_MODEL, tf), lambda f, j, te, na: (te[j], 0, f)),
                      pl.BlockSpec((None, D_MODEL, tf), lambda f, j, te, na: (te[j], 0, nf + f))],
            out_specs=pl.BlockSpec((tm, tf), lambda f, j, te, na: (j, f)),
            scratch_shapes=[pltpu.VMEM((D_MODEL, 2 * tf), BF16)],
        ),
        out_shape=jax.ShapeDtypeStruct((ROWS_SORTED, FFN_EXPERT), BF16),
        compiler_params=_cparams(("arbitrary", "arbitrary")),
        name="moe_up",
    )(te, na, xs, w_in, w_in)


def _moe_down_kernel(te_ref, na_ref, h_ref, w_ref, o_ref, w_scr):
    j = pl.program_id(1)

    @pl.when(j < na_ref[0])
    def _():
        @pl.when(_tile_changed(te_ref, j))
        def _():
            w_scr[...] = w_ref[...].astype(BF16)

        o_ref[...] = _dot(h_ref[...], w_scr[...])

    @pl.when(j >= na_ref[0])
    def _():
        o_ref[...] = jnp.zeros_like(o_ref)


def _moe_down(te, na, hmid, w_out):
    tm = TM_MOE
    tn = D_MODEL // 2
    row = lambda j, na: jnp.minimum(j, na[0] - 1)
    return pl.pallas_call(
        _moe_down_kernel,
        grid_spec=pltpu.PrefetchScalarGridSpec(
            num_scalar_prefetch=2,
            grid=(D_MODEL // tn, NT_MOE),
            in_specs=[pl.BlockSpec((tm, FFN_EXPERT), lambda n, j, te, na: (row(j, na), 0)),
                      pl.BlockSpec((None, FFN_EXPERT, tn), lambda n, j, te, na: (te[j], 0, n))],
            out_specs=pl.BlockSpec((tm, tn), lambda n, j, te, na: (j, n)),
            scratch_shapes=[pltpu.VMEM((FFN_EXPERT, tn), BF16)],
        ),
        out_shape=jax.ShapeDtypeStruct((ROWS_SORTED, D_MODEL), F32),
        compiler_params=_cparams(("arbitrary", "arbitrary")),
        name="moe_down",
    )(te, na, hmid, w_out)


def _moe_combine_kernel(pos_ref, y_ref, rt_ref, x_ref, g_ref, lg_ref, lb_ref, o_ref, ybuf, sem, *, tok_off):
    i = pl.program_id(0)
    tm = x_ref.shape[0]

    def copies(t):
        p0 = pos_ref[tok_off + i * tm + t]
        p1 = pos_ref[N_TOK + tok_off + i * tm + t]
        return (pltpu.make_async_copy(y_ref.at[pl.ds(p0, 1)], ybuf.at[0, pl.ds(t, 1)], sem),
                pltpu.make_async_copy(y_ref.at[pl.ds(p1, 1)], ybuf.at[1, pl.ds(t, 1)], sem))

    def start(t, c):
        for cp in copies(t):
            cp.start()
        return c

    lax.fori_loop(0, tm, start, 0, unroll=8)
    pltpu.make_async_copy(ybuf, ybuf, sem).wait()
    rt = rt_ref[...]
    moe = rt[:, 2:3] * ybuf[0] + rt[:, 3:4] * ybuf[1]
    o_ref[...] = _layer_norm(ALPHA * x_ref[...] + g_ref[...] * moe, lg_ref[...], lb_ref[...])


def _moe_combine(pos, y, rt_all, x_all, mod, lg, lb, latent):
    n = N_LAT if latent else N_CTX
    tm = TM_ROUTE
    rf = _row_fn(latent, tm)
    tok_off = N_CTX if latent else 0
    off = tok_off // tm
    return pl.pallas_call(
        functools.partial(_moe_combine_kernel, tok_off=tok_off),
        grid_spec=pltpu.PrefetchScalarGridSpec(
            num_scalar_prefetch=1,
            grid=(n // tm,),
            in_specs=[pl.BlockSpec(memory_space=pl.ANY),
                      pl.BlockSpec((tm, LANES), lambda i, pos: (i + off, 0)),
                      pl.BlockSpec((tm, D_MODEL), lambda i, pos: (i + off, 0)),
                      _mod_spec(5, rf),
                      pl.BlockSpec((1, D_MODEL), lambda i, pos: (0, 0)),
                      pl.BlockSpec((1, D_MODEL), lambda i, pos: (0, 0))],
            out_specs=pl.BlockSpec((tm, D_MODEL), lambda i, pos: (i, 0)),
            scratch_shapes=[pltpu.VMEM((2, tm, D_MODEL), F32), pltpu.SemaphoreType.DMA(())],
        ),
        out_shape=jax.ShapeDtypeStruct((n, D_MODEL), F32),
        compiler_params=_cparams(("arbitrary",)),
        name="moe_combine_lat" if latent else "moe_combine_ctx",
    )(pos, y, rt_all, x_all, mod, lg, lb)


def _routing_positions(rt_all, counts):
    tm = TM_MOE
    cnt = counts[0, :N_EXPERTS].astype(jnp.int32)
    padded = ((cnt + tm - 1) // tm) * tm
    gend = jnp.cumsum(padded)
    gstart = gend - padded
    route = rt_all[:, :8].astype(jnp.int32)
    onehot = lambda idx: idx[:, None] == jnp.arange(N_EXPERTS, dtype=jnp.int32)[None, :]
    start_of = lambda idx: jnp.sum(jnp.where(onehot(idx), gstart[None, :], 0), axis=1)
    pos = jnp.concatenate([start_of(route[:, 0]) + route[:, 4], start_of(route[:, 1]) + route[:, 5]])
    pad = jnp.concatenate([gstart + cnt, padded - cnt]).astype(jnp.int32)
    tile_start = jnp.arange(NT_MOE, dtype=jnp.int32) * tm
    te = jnp.sum((tile_start[:, None] >= gend[None, :]).astype(jnp.int32), axis=1)
    n_active = (gend[-1] // tm).astype(jnp.int32)
    last_e = jnp.take(te, jnp.maximum(n_active - 1, 0))
    te = jnp.where(tile_start < gend[-1], te, last_e).astype(jnp.int32)
    return pos, pad, te, n_active.reshape(1)


def kernel(x_prompt, x_sample, cache_ckv_l0, cache_krope_l0, cache_k_l1, cache_v_l1, c, c_ctx, w_ada_l0, b_ada_l0, mla_w_dq, mla_g_q, mla_w_uq, mla_w_dkv, mla_g_kv, mla_w_ukv, mla_w_o, ln1_g_l0, ln1_b_l0, ffn_w_in, ffn_w_out, ln2_g_l0, ln2_b_l0, w_ada_l1, b_ada_l1, na_w_qkv, na_rpb, na_w_o, ln1_g_l1, ln1_b_l1, moe_w_router, moe_w_in, moe_w_out, ln2_g_l1, ln2_b_l1):
    row = lambda v: v.reshape(1, -1)
    xp = x_prompt.reshape(N_CTX, D_MODEL)
    xs = x_sample.reshape(N_LAT, D_MODEL)
    groups = ((xp, False), (xs, True))

    cvecs = jnp.concatenate([c_ctx[None], c, jnp.zeros((8 - 1 - DEC_BATCH, D_MODEL), F32)], axis=0)
    mod0 = _ada_mod(cvecs, w_ada_l0, b_ada_l0)
    mod1 = _ada_mod(cvecs, w_ada_l1, b_ada_l1)

    wa = jnp.concatenate([mla_w_dq, mla_w_dkv], axis=1).astype(BF16)
    wuq = mla_w_uq.reshape(MLA_Q_LORA, MLA_HEADS, MLA_NOPE + MLA_ROPE)
    wuq = jnp.pad(wuq, ((0, 0), (0, 0), (0, Q_HEAD_PAD - MLA_NOPE - MLA_ROPE)))
    wuq = wuq.reshape(MLA_Q_LORA, MLA_HEADS * Q_HEAD_PAD).astype(BF16)
    wukv = mla_w_ukv.astype(BF16)
    wo0 = mla_w_o.astype(BF16)
    w_in0 = ffn_w_in.astype(BF16)
    w_out0 = ffn_w_out.astype(BF16)
    wqkv = na_w_qkv.astype(BF16)
    wo1 = na_w_o.astype(BF16)
    wr = jnp.pad(moe_w_router, ((0, 0), (0, LANES - N_EXPERTS))).astype(BF16)
    bias_tab = _na_bias_table(na_rpb)
    tab = _rope_tables()

    kvx_cache = _matmul(cache_ckv_l0.reshape(DEC_BATCH * PAST_LEN, MLA_KV_LORA), wukv, BF16, 512, "mla_expand_cache")
    kvx_cache = kvx_cache.reshape(DEC_BATCH, PAST_LEN, -1)
    krp_cache = jnp.pad(cache_krope_l0, ((0, 0), (0, 0), (0, LANES - MLA_ROPE))).astype(BF16)
    x1 = []
    new_ckv = new_kr = None
    for x, latent in groups:
        q, kvx, ckv, kr, krp = _mla_proj(x, mod0, wa, row(mla_g_q), row(mla_g_kv), wuq, wukv, tab, latent)
        if latent:
            o = _mla_attn_lat(q, kvx, krp, kvx_cache, krp_cache)
        else:
            new_ckv, new_kr = ckv, kr
            o = _mla_attn_ctx(q, kvx, krp)
        x1.append(_attn_out_ffn(o, x, wo0, mod0, row(ln1_g_l0), row(ln1_b_l0), w_in0, w_out0,
                                row(ln2_g_l0), row(ln2_b_l0), latent))

    kct = cache_k_l1.reshape(DEC_BATCH, PAST_LEN, D_MODEL).transpose(0, 2, 1).astype(BF16)
    vc = cache_v_l1.reshape(DEC_BATCH, PAST_LEN, D_MODEL).astype(BF16)
    q, new_k, new_v, kt = _na_qkv(x1[0], mod1, wqkv, F32, False)
    o_ctx = _na_attn_ctx(q, kt, new_v)
    q, _, v, kt = _na_qkv(x1[1], mod1, wqkv, BF16, True)
    o_lat = _na_attn_lat(q, kt, v, kct, vc, bias_tab)
    x2_all, rt_all, counts = _proj_ln_route(o_ctx, x1[0], o_lat, x1[1], wo1, mod1, row(ln1_g_l1), row(ln1_b_l1), wr)

    pos, pad, te, n_active = _routing_positions(rt_all, counts)
    x_sorted = _moe_scatter(pos, pad, n_active, x2_all, mod1)
    hmid = _moe_up(te, n_active, x_sorted, moe_w_in)
    y = _moe_down(te, n_active, hmid, moe_w_out)
    outs = [_moe_combine(pos, y, rt_all, x2_all, mod1, row(ln2_g_l1), row(ln2_b_l1), latent)
            for latent in (False, True)]

    return (outs[0].reshape(BATCH, SEQ, D_MODEL),
            outs[1].reshape(DEC_BATCH, DEC_SEQ, D_MODEL),
            new_ckv.reshape(BATCH, SEQ, MLA_KV_LORA),
            new_kr.reshape(BATCH, SEQ, MLA_ROPE),
            new_k.reshape(BATCH, SEQ, NA_HEADS, NA_HD),
            new_v.reshape(BATCH, SEQ, NA_HEADS, NA_HD))
```

```python
import functools
import math

import numpy as np
import jax
import jax.numpy as jnp
from jax import lax
from jax.experimental import pallas as pl
from jax.experimental.pallas import tpu as pltpu

F32 = jnp.float32
BF16 = jnp.bfloat16

D_MODEL = 1024
BATCH = 32
SEQ = 256
DEPTH = 2
DEC_BATCH = 2
DEC_SEQ = 2048
PAST_LEN = 512
GRID_W = 64
MLA_HEADS = 8
MLA_NOPE = 128
MLA_ROPE = 64
MLA_V = 128
MLA_Q_LORA = 512
MLA_KV_LORA = 256
MLA_SCALE = 1.0 / math.sqrt(MLA_NOPE + MLA_ROPE)
ROPE_THETA = 10000.0
NA_HEADS = 16
NA_HD = D_MODEL // NA_HEADS
NA_WIN_ROWS = 8
NA_WIN_COLS = 16
NA_SCALE = 1.0 / math.sqrt(NA_HD)
FFN_DENSE = 2816
N_EXPERTS = 8
TOP_K = 2
FFN_EXPERT = 3584
ALPHA = (2 * DEPTH) ** 0.25
LN_EPS = 1e-5
RMS_EPS = 1e-6

N_CTX = BATCH * SEQ
N_LAT = DEC_BATCH * DEC_SEQ
N_TOK = N_CTX + N_LAT
GRID_ROWS = DEC_SEQ // GRID_W
Q_HEAD_PAD = 256
LANES = 128
SUBLANES = 8
NEG_BIG = -1e30
MLA_KEY_CHUNK = 256
NA_PAIR = 2 * NA_HD
NA_BLK_ROWS = 4
NA_KT_W = 2 * GRID_W
NA_WIN_PAIRS = (NA_BLK_ROWS + NA_WIN_ROWS) // 2
NA_N_BLOCKS = DEC_SEQ // (NA_BLK_ROWS * GRID_W)
NA_BLOCK_VARIANTS = (0, 1, NA_N_BLOCKS - 1)

VMEM_LIMIT = 56 * 1024 * 1024

TM_PROJ = 512
TM_FFN = 512
MXU_TILE = 256
FFN_CHUNKS = ((0, 6 * MXU_TILE), (6 * MXU_TILE, FFN_DENSE))
TM_MOE = 512
TF_MOE = 1792
N_PAIRS = N_TOK * TOP_K
NT_MOE = (N_PAIRS + N_EXPERTS * TM_MOE) // TM_MOE
ROWS_SORTED = NT_MOE * TM_MOE
TM_ROUTE = 256


def _cparams(sem, vmem=VMEM_LIMIT):
    return pltpu.CompilerParams(dimension_semantics=sem, vmem_limit_bytes=vmem)


def _silu(x):
    return x * jax.nn.sigmoid(x)


def _layer_norm(y, g, b):
    mu = jnp.mean(y, axis=-1, keepdims=True)
    d = y - mu
    var = jnp.mean(d * d, axis=-1, keepdims=True)
    return d * lax.rsqrt(var + LN_EPS) * g + b


def _rms_norm(y, g):
    return y * lax.rsqrt(jnp.mean(y * y, axis=-1, keepdims=True) + RMS_EPS) * g


def _dot(a, b):
    return jnp.dot(a, b, preferred_element_type=F32)


def _dot_nt(a, b):
    return lax.dot_general(a, b, (((1,), (1,)), ((), ())), preferred_element_type=F32)


def _ada_kernel(c_ref, w_ref, b_ref, o_ref):
    s = _silu(c_ref[...]).astype(BF16)
    o_ref[...] = _dot(s, w_ref[...].astype(BF16)) + b_ref[...]


def _ada_mod(cvecs, w, b):
    tn = 1536
    m = pl.pallas_call(
        _ada_kernel,
        grid=(6 * D_MODEL // tn,),
        in_specs=[pl.BlockSpec((8, D_MODEL), lambda j: (0, 0)),
                  pl.BlockSpec((D_MODEL, tn), lambda j: (0, j)),
                  pl.BlockSpec((1, tn), lambda j: (0, j))],
        out_specs=pl.BlockSpec((8, tn), lambda j: (0, j)),
        out_shape=jax.ShapeDtypeStruct((8, 6 * D_MODEL), F32),
        compiler_params=_cparams(("arbitrary",)),
        name="ada_mod",
    )(cvecs, w, b.reshape(1, -1))
    return m[:3].reshape(3 * 6, 1, D_MODEL)


def _mod_spec(j, row_fn):
    return pl.BlockSpec((None, 1, D_MODEL), lambda i, *_: (row_fn(i) * 6 + j, 0, 0))


def _row_fn(latent, tm):
    if not latent:
        return lambda i: 0
    per = DEC_SEQ // tm
    return lambda i: 1 + i // per


def _const_spec(shape):
    nd = len(shape)
    return pl.BlockSpec(shape, lambda *_: (0,) * nd, pipeline_mode=pl.Buffered(1))


def _rope_kernel(invf_ref, o_ref):
    i = pl.program_id(0)
    tm = o_ref.shape[1]
    t = i * tm + lax.broadcasted_iota(jnp.int32, (tm, LANES), 0)
    lane = lax.broadcasted_iota(jnp.int32, (tm, LANES), 1)
    row = t >> int(math.log2(GRID_W))
    col = t & (GRID_W - 1)
    pos = jnp.where(lane < MLA_ROPE // 2, row, col).astype(F32)
    ang = pos * invf_ref[...]
    cos = jnp.cos(ang)
    sin = jnp.sin(ang)
    unit = lane >> int(math.log2(MLA_ROPE // 4))
    first = (unit == 0) | (unit == 2)
    second = (unit == 1) | (unit == 3)
    o_ref[0] = jnp.where(lane < MLA_ROPE, cos, 0.0)
    o_ref[1] = jnp.where(first, -sin, 0.0)
    o_ref[2] = jnp.where(second, sin, 0.0)


def _rope_tables():
    half = MLA_ROPE // 2
    inv_freq = (1.0 / (ROPE_THETA ** (np.arange(0, half, 2, dtype=np.float32) / half))).astype(np.float32)
    lane_f = np.zeros((1, LANES), np.float32)
    lane_f[0, :MLA_ROPE] = np.tile(inv_freq, 4)
    tm = 256
    return pl.pallas_call(
        _rope_kernel,
        grid=(DEC_SEQ // tm,),
        in_specs=[_const_spec((1, LANES))],
        out_specs=pl.BlockSpec((3, tm, LANES), lambda i: (0, i, 0)),
        out_shape=jax.ShapeDtypeStruct((3, DEC_SEQ, LANES), F32),
        compiler_params=_cparams(("arbitrary",)),
        name="rope_tables",
    )(jnp.asarray(lane_f))


def _rotate(v, tab_ref):
    return (v * tab_ref[0] + pltpu.roll(v, LANES - MLA_ROPE // 4, 1) * tab_ref[1]
            + pltpu.roll(v, MLA_ROPE // 4, 1) * tab_ref[2])


def _mla_proj_kernel(*refs, rope):
    if rope:
        (x_ref, sc_ref, sh_ref, wa_ref, gq_ref, gkv_ref, wuq_ref, wukv_ref, tab_ref,
         qt_ref, kvx_ref, krp_ref, vt_ref) = refs
    else:
        (x_ref, sc_ref, sh_ref, wa_ref, gq_ref, gkv_ref, wuq_ref, wukv_ref,
         q_ref, kvx_ref, ckv_ref, kr_ref, krp_ref) = refs
    h = (x_ref[...] * (1.0 + sc_ref[...]) + sh_ref[...]).astype(BF16)
    t = _dot(h, wa_ref[...])
    cq = _rms_norm(t[:, :MLA_Q_LORA], gq_ref[...])
    ckv = _rms_norm(t[:, MLA_Q_LORA:MLA_Q_LORA + MLA_KV_LORA], gkv_ref[...])
    kr = t[:, MLA_Q_LORA + MLA_KV_LORA:]
    kvx = _dot(ckv.astype(BF16), wukv_ref[...])
    kvx_ref[...] = kvx.astype(BF16)
    q = _dot(cq.astype(BF16), wuq_ref[...])
    krp = jnp.concatenate([kr, jnp.zeros_like(kr)], axis=-1)
    if rope:
        krp_ref[...] = _rotate(krp, tab_ref).astype(BF16)
        parts = []
        for hd in range(MLA_HEADS):
            lo = hd * Q_HEAD_PAD
            parts += [q[:, lo:lo + MLA_NOPE], _rotate(q[:, lo + MLA_NOPE:lo + Q_HEAD_PAD], tab_ref)]
        qt_ref[...] = jnp.concatenate(parts, axis=1).T.astype(BF16)
        vcols = [kvx[:, hd * Q_HEAD_PAD + MLA_NOPE:(hd + 1) * Q_HEAD_PAD] for hd in range(MLA_HEADS)]
        vt_ref[...] = jnp.concatenate(vcols, axis=1).T.astype(BF16)
    else:
        ckv_ref[...] = ckv
        kr_ref[...] = kr
        krp_ref[...] = krp.astype(BF16)
        q_ref[...] = q.astype(BF16)


def _mla_proj(x, mod, wa, gq, gkv, wuq, wukv, tab, latent):
    n = x.shape[0]
    tm = TM_PROJ
    rf = _row_fn(latent, tm)
    tok = lambda w: pl.BlockSpec((tm, w), lambda i: (i, 0))
    in_specs = [tok(D_MODEL), _mod_spec(1, rf), _mod_spec(0, rf),
                _const_spec(wa.shape), _const_spec(gq.shape), _const_spec(gkv.shape),
                _const_spec(wuq.shape), _const_spec(wukv.shape)]
    args = [x, mod, mod, wa, gq, gkv, wuq, wukv]
    wq, wkv = MLA_HEADS * Q_HEAD_PAD, MLA_HEADS * (MLA_NOPE + MLA_V)
    if latent:
        per = DEC_SEQ // tm
        in_specs.append(pl.BlockSpec((3, tm, LANES), lambda i: (0, i % per, 0)))
        args.append(tab)
        out_specs = [pl.BlockSpec((wq, tm), lambda i: (0, i)), tok(wkv), tok(LANES),
                     pl.BlockSpec((None, MLA_HEADS * MLA_V, tm), lambda i: (i // per, 0, i % per))]
        out_shape = [jax.ShapeDtypeStruct((wq, n), BF16), jax.ShapeDtypeStruct((n, wkv), BF16),
                     jax.ShapeDtypeStruct((n, LANES), BF16),
                     jax.ShapeDtypeStruct((DEC_BATCH, MLA_HEADS * MLA_V, DEC_SEQ), BF16)]
    else:
        out_specs = [tok(wq), tok(wkv), tok(MLA_KV_LORA), tok(MLA_ROPE), tok(LANES)]
        out_shape = [jax.ShapeDtypeStruct((n, wq), BF16), jax.ShapeDtypeStruct((n, wkv), BF16),
                     jax.ShapeDtypeStruct((n, MLA_KV_LORA), F32), jax.ShapeDtypeStruct((n, MLA_ROPE), F32),
                     jax.ShapeDtypeStruct((n, LANES), BF16)]
    return pl.pallas_call(
        functools.partial(_mla_proj_kernel, rope=latent),
        grid=(n // tm,),
        in_specs=in_specs,
        out_specs=out_specs,
        out_shape=out_shape,
        compiler_params=_cparams(("arbitrary",)),
        name="mla_proj_lat" if latent else "mla_proj_ctx",
    )(*args)


def _matmul_kernel(a_ref, b_ref, o_ref):
    o_ref[...] = _dot(a_ref[...].astype(BF16), b_ref[...]).astype(o_ref.dtype)


def _matmul(a, b, out_dtype, tm, name):
    m, k = a.shape
    n = b.shape[1]
    return pl.pallas_call(
        _matmul_kernel,
        grid=(m // tm,),
        in_specs=[pl.BlockSpec((tm, k), lambda i: (i, 0)), _const_spec(b.shape)],
        out_specs=pl.BlockSpec((tm, n), lambda i: (i, 0)),
        out_shape=jax.ShapeDtypeStruct((m, n), out_dtype),
        compiler_params=_cparams(("arbitrary",)),
        name=name,
    )(a, b)


def _mla_head(qh, kparts, vparts):
    ss = [_dot_nt(qh, k) for k in kparts]
    m = functools.reduce(jnp.maximum, [jnp.max(s, axis=-1, keepdims=True) for s in ss])
    ps = [jnp.exp2((s - m) * (MLA_SCALE * math.log2(math.e))) for s in ss]
    l = functools.reduce(jnp.add, [jnp.sum(p, axis=-1, keepdims=True) for p in ps])
    o = functools.reduce(jnp.add, [_dot(p.astype(BF16), v) for p, v in zip(ps, vparts)])
    return o / l


def _mla_attn_ctx_kernel(q_ref, kvx_ref, krp_ref, o_ref):
    krp = krp_ref[...]
    for hd in range(MLA_HEADS):
        lo = hd * (MLA_NOPE + MLA_V)
        qh = q_ref[:, hd * Q_HEAD_PAD:(hd + 1) * Q_HEAD_PAD]
        kh = jnp.concatenate([kvx_ref[:, lo:lo + MLA_NOPE], krp], axis=-1)
        vh = kvx_ref[:, lo + MLA_NOPE:lo + MLA_NOPE + MLA_V]
        o_ref[:, hd * MLA_V:(hd + 1) * MLA_V] = _mla_head(qh, [kh], [vh]).astype(BF16)


def _mla_attn_ctx(q, kvx, krp):
    tok = lambda w: pl.BlockSpec((SEQ, w), lambda b: (b, 0))
    return pl.pallas_call(
        _mla_attn_ctx_kernel,
        grid=(BATCH,),
        in_specs=[tok(q.shape[1]), tok(kvx.shape[1]), tok(LANES)],
        out_specs=tok(MLA_HEADS * MLA_V),
        out_shape=jax.ShapeDtypeStruct((N_CTX, MLA_HEADS * MLA_V), BF16),
        compiler_params=_cparams(("arbitrary",)),
        name="mla_attn_ctx",
    )(q, kvx, krp)


def _mla_attn_lat_kernel(qt_ref, kvl_ref, krl_ref, kvc_ref, krc_ref, vtl_ref, vtc_ref, o_ref):
    chunks = [(kvl_ref, krl_ref, vtl_ref, c) for c in range(DEC_SEQ // MLA_KEY_CHUNK)]
    chunks += [(kvc_ref, krc_ref, vtc_ref, c) for c in range(PAST_LEN // MLA_KEY_CHUNK)]
    for hd in range(MLA_HEADS):
        lo = hd * (MLA_NOPE + MLA_V)
        qt = qt_ref[hd * Q_HEAD_PAD:(hd + 1) * Q_HEAD_PAD, :]
        ss = []
        for kv_ref, kr_ref, _, c in chunks:
            rows = slice(c * MLA_KEY_CHUNK, (c + 1) * MLA_KEY_CHUNK)
            ss.append(_dot(jnp.concatenate([kv_ref[rows, lo:lo + MLA_NOPE], kr_ref[rows, :]], axis=-1), qt))
        m = functools.reduce(jnp.maximum, [jnp.max(s, axis=0, keepdims=True) for s in ss])
        ps = [jnp.exp2((s - m) * (MLA_SCALE * math.log2(math.e))) for s in ss]
        l = functools.reduce(jnp.add, [jnp.sum(p, axis=0, keepdims=True) for p in ps])
        ots = [_dot(vt_ref[hd * MLA_V:(hd + 1) * MLA_V, c * MLA_KEY_CHUNK:(c + 1) * MLA_KEY_CHUNK], p.astype(BF16))
               for (_, _, vt_ref, c), p in zip(chunks, ps)]
        ot = functools.reduce(jnp.add, ots) / l
        o_ref[:, hd * MLA_V:(hd + 1) * MLA_V] = ot.T.astype(BF16)


def _mla_attn_lat(qt, kvx_lat, krp_lat, kvx_ctx, krp_ctx, vt_lat, vt_ctx):
    tq = 256
    per = DEC_SEQ // tq
    wkv = kvx_lat.shape[-1]
    wv = MLA_HEADS * MLA_V
    batch = lambda rows, cols: pl.BlockSpec((None, rows, cols), lambda b, i: (b, 0, 0))
    return pl.pallas_call(
        _mla_attn_lat_kernel,
        grid=(DEC_BATCH, per),
        in_specs=[pl.BlockSpec((qt.shape[0], tq), lambda b, i: (0, b * per + i)),
                  batch(DEC_SEQ, wkv), batch(DEC_SEQ, LANES), batch(PAST_LEN, wkv), batch(PAST_LEN, LANES),
                  batch(wv, DEC_SEQ), batch(wv, PAST_LEN)],
        out_specs=pl.BlockSpec((tq, wv), lambda b, i: (b * per + i, 0)),
        out_shape=jax.ShapeDtypeStruct((N_LAT, wv), BF16),
        compiler_params=_cparams(("arbitrary", "arbitrary")),
        name="mla_attn_lat",
    )(qt, kvx_lat.reshape(DEC_BATCH, DEC_SEQ, wkv), krp_lat.reshape(DEC_BATCH, DEC_SEQ, LANES), kvx_ctx, krp_ctx,
      vt_lat, vt_ctx)


def _top2(logits):
    lane = lax.broadcasted_iota(jnp.int32, logits.shape, 1).astype(F32)
    m1 = jnp.max(logits, axis=-1, keepdims=True)
    i1 = jnp.min(jnp.where(logits == m1, lane, float(LANES)), axis=-1, keepdims=True)
    rest = jnp.where(lane == i1, -jnp.inf, logits)
    m2 = jnp.max(rest, axis=-1, keepdims=True)
    i2 = jnp.min(jnp.where(rest == m2, lane, float(LANES)), axis=-1, keepdims=True)
    e = jnp.exp(m2 - m1)
    w1 = 1.0 / (1.0 + e)
    w2 = e / (1.0 + e)
    return lane, i1, i2, w1, w2


def _proj_ln_route_kernel(oc_ref, xc_ref, ol_ref, xl_ref, wo_ref, g_ref, lg_ref, lb_ref, sc_ref, sh_ref, wr_ref,
                          x1_ref, rt_ref, cnt_ref, carry, *, n_ctx_tiles):
    i = pl.program_id(0)

    @pl.when(i == 0)
    def _():
        carry[...] = jnp.zeros_like(carry)

    def body(o_ref, x_ref):
        out = _dot(o_ref[...], wo_ref[...])
        x1 = _layer_norm(ALPHA * x_ref[...] + g_ref[...] * out, lg_ref[...], lb_ref[...])
        x1_ref[...] = x1
        hm = x1 * (1.0 + sc_ref[...]) + sh_ref[...]
        logits = _dot(hm.astype(BF16), wr_ref[...])
        lane = lax.broadcasted_iota(jnp.int32, logits.shape, 1)
        logits = jnp.where(lane < N_EXPERTS, logits, -jnp.inf)
        lane_f, i1, i2, w1, w2 = _top2(logits)
        tm = logits.shape[0]
        oh1 = (lane_f == i1).astype(F32)
        oh2 = (lane_f == i2).astype(F32)
        rr = lax.broadcasted_iota(jnp.int32, (tm, tm), 0)
        cc = lax.broadcasted_iota(jnp.int32, (tm, tm), 1)
        below = (cc < rr).astype(BF16)
        tot1 = jnp.sum(oh1, axis=0, keepdims=True)
        tot2 = jnp.sum(oh2, axis=0, keepdims=True)
        base = carry[...]
        cum1 = _dot(below, oh1.astype(BF16)) + base
        cum2 = _dot(below, oh2.astype(BF16)) + (base + tot1)
        rank1 = jnp.sum(oh1 * cum1, axis=-1, keepdims=True)
        rank2 = jnp.sum(oh2 * cum2, axis=-1, keepdims=True)
        vals = (i1, i2, w1, w2, rank1, rank2)
        rt = jnp.zeros_like(logits)
        for k, val in enumerate(vals):
            rt = jnp.where(lane == k, val, rt)
        rt_ref[...] = rt
        carry[...] = base + tot1 + tot2
        cnt_ref[...] = jnp.broadcast_to(carry[...], cnt_ref.shape)

    is_ctx = i < n_ctx_tiles
    pl.when(is_ctx)(lambda: body(oc_ref, xc_ref))
    pl.when(jnp.logical_not(is_ctx))(lambda: body(ol_ref, xl_ref))


def _all_row_fn(tm):
    nc = N_CTX // tm
    per = DEC_SEQ // tm
    return lambda i: jnp.where(i < nc, 0, 1 + (i - nc) // per)


def _proj_ln_route(o_ctx, x_ctx, o_lat, x_lat, wo, mod, lg, lb, wr):
    tm = TM_PROJ
    nc = N_CTX // tm
    rf = _all_row_fn(tm)
    ctx = lambda w: pl.BlockSpec((tm, w), lambda i: (jnp.minimum(i, nc - 1), 0))
    lat = lambda w: pl.BlockSpec((tm, w), lambda i: (jnp.maximum(i - nc, 0), 0))
    tok = lambda w: pl.BlockSpec((tm, w), lambda i: (i, 0))
    return pl.pallas_call(
        functools.partial(_proj_ln_route_kernel, n_ctx_tiles=nc),
        grid=(N_TOK // tm,),
        in_specs=[ctx(D_MODEL), ctx(D_MODEL), lat(D_MODEL), lat(D_MODEL), _const_spec(wo.shape), _mod_spec(2, rf),
                  _const_spec(lg.shape), _const_spec(lb.shape), _mod_spec(4, rf), _mod_spec(3, rf),
                  _const_spec(wr.shape)],
        out_specs=[tok(D_MODEL), tok(LANES), pl.BlockSpec((8, LANES), lambda i: (0, 0))],
        out_shape=[jax.ShapeDtypeStruct((N_TOK, D_MODEL), F32),
                   jax.ShapeDtypeStruct((N_TOK, LANES), F32),
                   jax.ShapeDtypeStruct((8, LANES), F32)],
        scratch_shapes=[pltpu.VMEM((1, LANES), F32)],
        compiler_params=_cparams(("arbitrary",)),
        name="proj_ln_route",
    )(o_ctx, x_ctx, o_lat, x_lat, wo, mod, lg, lb, mod, mod, wr)


def _attn_out_ffn_kernel(o_ref, x_ref, wo_ref, g1_ref, lg1_ref, lb1_ref, sc_ref, sh_ref, g2_ref,
                         wi_ref, wd_ref, lg2_ref, lb2_ref, y_ref):
    x1 = _layer_norm(ALPHA * x_ref[...] + g1_ref[...] * _dot(o_ref[...], wo_ref[...]), lg1_ref[...], lb1_ref[...])
    h = (x1 * (1.0 + sc_ref[...]) + sh_ref[...]).astype(BF16)
    acc = None
    for lo, hi in FFN_CHUNKS:
        gate = _dot(h, wi_ref[:, lo:hi])
        up = _dot(h, wi_ref[:, FFN_DENSE + lo:FFN_DENSE + hi])
        part = _dot((_silu(gate) * up).astype(BF16), wd_ref[lo:hi, :])
        acc = part if acc is None else acc + part
    y_ref[...] = _layer_norm(ALPHA * x1 + g2_ref[...] * acc, lg2_ref[...], lb2_ref[...])


def _attn_out_ffn(o, x, wo, mod, lg1, lb1, w_in, w_out, lg2, lb2, latent):
    n = x.shape[0]
    tm = TM_FFN
    rf = _row_fn(latent, tm)
    tok = lambda w: pl.BlockSpec((tm, w), lambda i: (i, 0))
    vec = _const_spec((1, D_MODEL))
    return pl.pallas_call(
        _attn_out_ffn_kernel,
        grid=(n // tm,),
        in_specs=[tok(o.shape[1]), tok(D_MODEL), _const_spec(wo.shape), _mod_spec(2, rf), vec, vec,
                  _mod_spec(4, rf), _mod_spec(3, rf), _mod_spec(5, rf),
                  _const_spec(w_in.shape), _const_spec(w_out.shape), vec, vec],
        out_specs=tok(D_MODEL),
        out_shape=jax.ShapeDtypeStruct((n, D_MODEL), F32),
        compiler_params=_cparams(("arbitrary",)),
        name="attn_out_ffn_lat" if latent else "attn_out_ffn_ctx",
    )(o, x, wo, mod, lg1, lb1, mod, mod, mod, w_in, w_out, lg2, lb2)


def _na_qkv_kernel(x_ref, sc_ref, sh_ref, w_ref, q_ref, k_ref, v_ref, kt_ref):
    h = (x_ref[...] * (1.0 + sc_ref[...]) + sh_ref[...]).astype(BF16)
    qkv = _dot(h, w_ref[...])
    q_ref[...] = (qkv[:, :D_MODEL] * NA_SCALE).astype(BF16)
    k = qkv[:, D_MODEL:2 * D_MODEL]
    k_ref[...] = k.astype(k_ref.dtype)
    v_ref[...] = qkv[:, 2 * D_MODEL:].astype(v_ref.dtype)
    kt = k.T
    nblk, _, w = kt_ref.shape
    for t in range(nblk):
        kt_ref[t] = kt[:, t * w:(t + 1) * w].astype(BF16)


def _na_qkv(x, mod, w, kv_dtype, latent):
    n = x.shape[0]
    tm = TM_PROJ
    kt_w = NA_KT_W if latent else SEQ
    rf = _row_fn(latent, tm)
    tok = pl.BlockSpec((tm, D_MODEL), lambda i: (i, 0))
    return pl.pallas_call(
        _na_qkv_kernel,
        grid=(n // tm,),
        in_specs=[tok, _mod_spec(1, rf), _mod_spec(0, rf), _const_spec(w.shape)],
        out_specs=[tok, tok, tok, pl.BlockSpec((tm // kt_w, D_MODEL, kt_w), lambda i: (i, 0, 0))],
        out_shape=[jax.ShapeDtypeStruct((n, D_MODEL), BF16),
                   jax.ShapeDtypeStruct((n, D_MODEL), kv_dtype),
                   jax.ShapeDtypeStruct((n, D_MODEL), kv_dtype),
                   jax.ShapeDtypeStruct((n // kt_w, D_MODEL, kt_w), BF16)],
        compiler_params=_cparams(("arbitrary",)),
        name="na_qkv_lat" if latent else "na_qkv_ctx",
    )(x, mod, mod, w)


def _softmax_pv(scores, values):
    m = functools.reduce(jnp.maximum, [jnp.max(s, axis=-1, keepdims=True) for s in scores])
    ps = [jnp.exp(s - m) for s in scores]
    l = functools.reduce(jnp.add, [jnp.sum(p, axis=-1, keepdims=True) for p in ps])
    o = functools.reduce(jnp.add, [_dot(p.astype(BF16), v) for p, v in zip(ps, values)])
    return o / l


def _head_of_pair(x, half):
    lane = lax.broadcasted_iota(jnp.int32, x.shape, 1)
    keep = (lane < NA_HD) if half == 0 else (lane >= NA_HD)
    return jnp.where(keep, x, jnp.zeros_like(x))


def _merge_pair(o0, o1):
    lane = lax.broadcasted_iota(jnp.int32, o0.shape, 1)
    return jnp.where(lane < NA_HD, o0, o1)


def _na_attn_ctx_kernel(q_ref, kt_ref, v_ref, o_ref):
    for p in range(NA_HEADS // 2):
        cols = slice(p * NA_PAIR, (p + 1) * NA_PAIR)
        qp = q_ref[:, cols]
        ktp = kt_ref[cols, :]
        vp = v_ref[:, cols].astype(BF16)
        outs = [_softmax_pv([_dot(_head_of_pair(qp, half), ktp)], [vp]) for half in range(2)]
        o_ref[:, cols] = _merge_pair(*outs).astype(BF16)


def _na_attn_ctx(q, kt, v):
    tok = pl.BlockSpec((SEQ, D_MODEL), lambda b: (b, 0))
    return pl.pallas_call(
        _na_attn_ctx_kernel,
        grid=(BATCH,),
        in_specs=[tok, pl.BlockSpec((None, D_MODEL, SEQ), lambda b: (b, 0, 0)), tok],
        out_specs=tok,
        out_shape=jax.ShapeDtypeStruct((N_CTX, D_MODEL), BF16),
        compiler_params=_cparams(("arbitrary",)),
        name="na_attn_ctx",
    )(q, kt, v)


def _na_win_start(m, clip=jnp.clip):
    return clip(m * NA_BLK_ROWS // 2 - NA_WIN_ROWS // 4, 0, GRID_ROWS // 2 - NA_WIN_PAIRS)


def _na_variant(m):
    return jnp.where(m == 0, 0, jnp.where(m == NA_N_BLOCKS - 1, 2, 1))


def _na_attn_lat_kernel(q_ref, kt_ref, v_ref, kct_ref, vc_ref, bias_ref, o_ref):
    rp0 = _na_win_start(pl.program_id(1))
    win = pl.ds(pl.multiple_of(rp0 * NA_KT_W, NA_KT_W), NA_WIN_PAIRS * NA_KT_W)
    for p in range(NA_HEADS // 2):
        cols = slice(p * NA_PAIR, (p + 1) * NA_PAIR)
        qp = q_ref[:, cols]
        kw = jnp.concatenate([kt_ref[rp0 + t, cols, :] for t in range(NA_WIN_PAIRS)], axis=1)
        kc = kct_ref[cols, :]
        vw = v_ref[win, cols]
        vc = vc_ref[:, cols]
        outs = []
        for half in range(2):
            qh = _head_of_pair(qp, half)
            s_nb = _dot(qh, kw) + bias_ref[2 * p + half]
            outs.append(_softmax_pv([s_nb, _dot(qh, kc)], [vw, vc]))
        o_ref[:, cols] = _merge_pair(*outs).astype(BF16)


def _na_bias_kernel(e_ref, e64_ref, o_ref):
    n_dr = 2 * NA_WIN_ROWS - 1
    shape = (GRID_W, LANES)
    lane = lax.broadcasted_iota(jnp.int32, shape, 1)
    c = lax.broadcasted_iota(jnp.int32, shape, 0)
    kc = lane & (GRID_W - 1)
    cs = jnp.clip(c - NA_WIN_COLS // 2, 0, GRID_W - NA_WIN_COLS)
    valid = (kc >= cs) & (kc < cs + NA_WIN_COLS)
    toeplitz = lambda ref, a: pltpu.roll(jnp.broadcast_to(ref[a:a + 1, :], shape), 0, 1, stride=1, stride_axis=0)
    rows = ([toeplitz(e_ref, a) for a in range(n_dr)], [toeplitz(e64_ref, a) for a in range(n_dr)])
    masked = jnp.full(shape, NEG_BIG, F32)
    for v, m in enumerate(NA_BLOCK_VARIANTS):
        rp0 = _na_win_start(m, clip=lambda x, lo, hi: min(max(x, lo), hi))
        for i in range(NA_BLK_ROWS):
            r = NA_BLK_ROWS * m + i
            rs = min(max(r - NA_WIN_ROWS // 2, 0), GRID_ROWS - NA_WIN_ROWS)
            for t in range(NA_WIN_PAIRS):
                halves = []
                for u in range(2):
                    kr = 2 * (rp0 + t) + u
                    halves.append(rows[u][kr - r + NA_WIN_ROWS - 1] if rs <= kr < rs + NA_WIN_ROWS else masked)
                o_ref[v, i * GRID_W:(i + 1) * GRID_W, t * LANES:(t + 1) * LANES] = jnp.where(
                    valid, jnp.where(lane < GRID_W, halves[0], halves[1]), NEG_BIG)


def _na_bias_table(rpb):
    n_dr = 2 * NA_WIN_ROWS - 1
    blk = NA_BLK_ROWS * GRID_W
    e = jnp.zeros((NA_HEADS, n_dr, LANES), F32)
    e = e.at[:, :, :NA_WIN_COLS].set(rpb[:, :, NA_WIN_COLS - 1:])
    e = e.at[:, :, LANES - (NA_WIN_COLS - 1):].set(rpb[:, :, :NA_WIN_COLS - 1])
    e64 = jnp.roll(e, GRID_W, axis=-1)
    spec = pl.BlockSpec((None, n_dr, LANES), lambda h: (h, 0, 0))
    return pl.pallas_call(
        _na_bias_kernel,
        grid=(NA_HEADS,),
        in_specs=[spec, spec],
        out_specs=pl.BlockSpec((None, len(NA_BLOCK_VARIANTS), blk, NA_WIN_PAIRS * NA_KT_W), lambda h: (h, 0, 0, 0)),
        out_shape=jax.ShapeDtypeStruct((NA_HEADS, len(NA_BLOCK_VARIANTS), blk, NA_WIN_PAIRS * NA_KT_W), F32),
        compiler_params=_cparams(("arbitrary",)),
        name="na_bias_table",
    )(e, e64)


def _na_attn_lat(q, kt, v, kct, vc, bias_tab):
    blk = NA_BLK_ROWS * GRID_W
    nblk = NA_N_BLOCKS
    npair = DEC_SEQ // NA_KT_W
    row = pl.BlockSpec((blk, D_MODEL), lambda b, m: (b * nblk + m, 0))
    bias_spec = pl.BlockSpec((NA_HEADS, None, blk, NA_WIN_PAIRS * NA_KT_W), lambda b, m: (0, _na_variant(m), 0, 0),
                             pipeline_mode=pl.Buffered(1))
    return pl.pallas_call(
        _na_attn_lat_kernel,
        grid=(DEC_BATCH, nblk),
        in_specs=[row,
                  pl.BlockSpec((None, npair, D_MODEL, NA_KT_W), lambda b, m: (b, 0, 0, 0)),
                  pl.BlockSpec((None, DEC_SEQ, D_MODEL), lambda b, m: (b, 0, 0)),
                  pl.BlockSpec((None, D_MODEL, PAST_LEN), lambda b, m: (b, 0, 0)),
                  pl.BlockSpec((None, PAST_LEN, D_MODEL), lambda b, m: (b, 0, 0)),
                  bias_spec],
        out_specs=row,
        out_shape=jax.ShapeDtypeStruct((N_LAT, D_MODEL), BF16),
        compiler_params=_cparams(("arbitrary", "arbitrary")),
        name="na_attn_lat",
    )(q, kt.reshape(DEC_BATCH, npair, D_MODEL, NA_KT_W), v.reshape(DEC_BATCH, DEC_SEQ, D_MODEL), kct, vc, bias_tab)


def _moe_scatter_kernel(pos_ref, pad_ref, na_ref, x_ref, sc_ref, sh_ref, xs_ref, hm_scr, zero_scr, sem, zsem):
    i = pl.program_id(0)
    tm = x_ref.shape[0]
    hm_scr[...] = x_ref[...] * (1.0 + sc_ref[...]) + sh_ref[...]

    def copies(t):
        src = hm_scr.at[pl.ds(t, 1)]
        p0 = pos_ref[i * tm + t]
        p1 = pos_ref[N_TOK + i * tm + t]
        return (pltpu.make_async_copy(src, xs_ref.at[pl.ds(p0, 1)], sem),
                pltpu.make_async_copy(src, xs_ref.at[pl.ds(p1, 1)], sem))

    def start(t, c):
        for cp in copies(t):
            cp.start()
        return c

    lax.fori_loop(0, tm, start, 0, unroll=8)

    @pl.when(i == pl.num_programs(0) - 1)
    def _():
        zero_scr[...] = jnp.zeros_like(zero_scr)

        def pad_copies():
            out = []
            zero = lambda b: zero_scr.at[pl.ds(0, b)]
            for e in range(N_EXPERTS):
                off = pad_ref[e]
                n = pad_ref[N_EXPERTS + e]
                head = n & (SUBLANES - 1)
                for k in range(SUBLANES - 1):
                    out.append((k < head, pltpu.make_async_copy(zero(1), xs_ref.at[pl.ds(off + k, 1)], zsem)))
                off = off + head
                b = SUBLANES
                while b < TM_MOE:
                    rows = pl.ds(pl.multiple_of(off, SUBLANES), b)
                    out.append(((n & b) != 0, pltpu.make_async_copy(zero(b), xs_ref.at[rows], zsem)))
                    off = off + (n & b)
                    b *= 2
            return out

        def tile_copy(j):
            rows = pl.ds(pl.multiple_of(j * TM_MOE, TM_MOE), TM_MOE)
            return pltpu.make_async_copy(zero_scr, xs_ref.at[rows], zsem)

        for cond, cp in pad_copies():
            pl.when(cond)(cp.start)
        lax.fori_loop(na_ref[0], NT_MOE, lambda j, c: (tile_copy(j).start(), c)[1], 0)
        for cond, cp in pad_copies():
            pl.when(cond)(cp.wait)
        lax.fori_loop(na_ref[0], NT_MOE, lambda j, c: (tile_copy(j).wait(), c)[1], 0)

    rows = xs_ref.at[pl.ds(0, 2 * tm)]
    pltpu.make_async_copy(rows, rows, sem).wait()


def _moe_scatter(pos, pad, na, x_all, mod):
    tm = TM_ROUTE
    rf = _all_row_fn(tm)
    return pl.pallas_call(
        _moe_scatter_kernel,
        grid_spec=pltpu.PrefetchScalarGridSpec(
            num_scalar_prefetch=3,
            grid=(N_TOK // tm,),
            in_specs=[pl.BlockSpec((tm, D_MODEL), lambda i, *_: (i, 0)), _mod_spec(4, rf), _mod_spec(3, rf)],
            out_specs=pl.BlockSpec(memory_space=pl.ANY),
            scratch_shapes=[pltpu.VMEM((tm, D_MODEL), F32), pltpu.VMEM((TM_MOE, D_MODEL), F32),
                            pltpu.SemaphoreType.DMA(()), pltpu.SemaphoreType.DMA(())],
        ),
        out_shape=jax.ShapeDtypeStruct((ROWS_SORTED, D_MODEL), F32),
        compiler_params=_cparams(("arbitrary",)),
        name="moe_scatter",
    )(pos, pad, na, x_all, mod, mod)


def _tile_changed(te_ref, j):
    prev = te_ref[jnp.maximum(j - 1, 0)]
    return (j == 0) | (te_ref[j] != prev)


def _moe_up_kernel(te_ref, na_ref, x_ref, wg_ref, wu_ref, o_ref, w_scr):
    j = pl.program_id(1)
    tf = wg_ref.shape[1]

    @pl.when(j < na_ref[0])
    def _():
        @pl.when(_tile_changed(te_ref, j))
        def _():
            w_scr[:, :tf] = wg_ref[...].astype(BF16)
            w_scr[:, tf:] = wu_ref[...].astype(BF16)

        gu = _dot(x_ref[...].astype(BF16), w_scr[...])
        o_ref[...] = (_silu(gu[:, :tf]) * gu[:, tf:]).astype(BF16)

    @pl.when(j >= na_ref[0])
    def _():
        o_ref[...] = jnp.zeros_like(o_ref)


def _moe_up(te, na, xs, w_in):
    tm, tf = TM_MOE, TF_MOE
    nf = FFN_EXPERT // tf
    row = lambda j, na: jnp.minimum(j, na[0] - 1)
    return pl.pallas_call(
        _moe_up_kernel,
        grid_spec=pltpu.PrefetchScalarGridSpec(
            num_scalar_prefetch=2,
            grid=(nf, NT_MOE),
            in_specs=[pl.BlockSpec((tm, D_MODEL), lambda f, j, te, na: (row(j, na), 0)),
                      pl.BlockSpec((None, D_MODEL, tf), lambda f, j, te, na: (te[j], 0, f)),
                      pl.BlockSpec((None, D_MODEL, tf), lambda f, j, te, na: (te[j], 0, nf + f))],
            out_specs=pl.BlockSpec((tm, tf), lambda f, j, te, na: (j, f)),
            scratch_shapes=[pltpu.VMEM((D_MODEL, 2 * tf), BF16)],
        ),
        out_shape=jax.ShapeDtypeStruct((ROWS_SORTED, FFN_EXPERT), BF16),
        compiler_params=_cparams(("arbitrary", "arbitrary")),
        name="moe_up",
    )(te, na, xs, w_in, w_in)


def _moe_down_kernel(te_ref, na_ref, h_ref, w_ref, o_ref, w_scr):
    j = pl.program_id(1)

    @pl.when(j < na_ref[0])
    def _():
        @pl.when(_tile_changed(te_ref, j))
        def _():
            w_scr[...] = w_ref[...].astype(BF16)

        o_ref[...] = _dot(h_ref[...], w_scr[...])

    @pl.when(j >= na_ref[0])
    def _():
        o_ref[...] = jnp.zeros_like(o_ref)


def _moe_down(te, na, hmid, w_out):
    tm = TM_MOE
    tn = D_MODEL
    row = lambda j, na: jnp.minimum(j, na[0] - 1)
    return pl.pallas_call(
        _moe_down_kernel,
        grid_spec=pltpu.PrefetchScalarGridSpec(
            num_scalar_prefetch=2,
            grid=(D_MODEL // tn, NT_MOE),
            in_specs=[pl.BlockSpec((tm, FFN_EXPERT), lambda n, j, te, na: (row(j, na), 0)),
                      pl.BlockSpec((None, FFN_EXPERT, tn), lambda n, j, te, na: (te[j], 0, n))],
            out_specs=pl.BlockSpec((tm, tn), lambda n, j, te, na: (j, n)),
            scratch_shapes=[pltpu.VMEM((FFN_EXPERT, tn), BF16)],
        ),
        out_shape=jax.ShapeDtypeStruct((ROWS_SORTED, D_MODEL), F32),
        compiler_params=_cparams(("arbitrary", "arbitrary")),
        name="moe_down",
    )(te, na, hmid, w_out)


def _moe_combine_kernel(pos_ref, y_ref, rt_ref, x_ref, g_ref, lg_ref, lb_ref, o_ref, ybuf, sem, *, tok_off):
    i = pl.program_id(0)
    tm = x_ref.shape[0]

    def copies(t):
        p0 = pos_ref[tok_off + i * tm + t]
        p1 = pos_ref[N_TOK + tok_off + i * tm + t]
        return (pltpu.make_async_copy(y_ref.at[pl.ds(p0, 1)], ybuf.at[0, pl.ds(t, 1)], sem),
                pltpu.make_async_copy(y_ref.at[pl.ds(p1, 1)], ybuf.at[1, pl.ds(t, 1)], sem))

    def start(t, c):
        for cp in copies(t):
            cp.start()
        return c

    lax.fori_loop(0, tm, start, 0, unroll=8)
    pltpu.make_async_copy(ybuf, ybuf, sem).wait()
    rt = rt_ref[...]
    moe = rt[:, 2:3] * ybuf[0] + rt[:, 3:4] * ybuf[1]
    o_ref[...] = _layer_norm(ALPHA * x_ref[...] + g_ref[...] * moe, lg_ref[...], lb_ref[...])


def _moe_combine(pos, y, rt_all, x_all, mod, lg, lb, latent):
    n = N_LAT if latent else N_CTX
    tm = TM_ROUTE
    rf = _row_fn(latent, tm)
    tok_off = N_CTX if latent else 0
    off = tok_off // tm
    return pl.pallas_call(
        functools.partial(_moe_combine_kernel, tok_off=tok_off),
        grid_spec=pltpu.PrefetchScalarGridSpec(
            num_scalar_prefetch=1,
            grid=(n // tm,),
            in_specs=[pl.BlockSpec(memory_space=pl.ANY),
                      pl.BlockSpec((tm, LANES), lambda i, pos: (i + off, 0)),
                      pl.BlockSpec((tm, D_MODEL), lambda i, pos: (i + off, 0)),
                      _mod_spec(5, rf),
                      pl.BlockSpec((1, D_MODEL), lambda i, pos: (0, 0)),
                      pl.BlockSpec((1, D_MODEL), lambda i, pos: (0, 0))],
            out_specs=pl.BlockSpec((tm, D_MODEL), lambda i, pos: (i, 0)),
            scratch_shapes=[pltpu.VMEM((2, tm, D_MODEL), F32), pltpu.SemaphoreType.DMA(())],
        ),
        out_shape=jax.ShapeDtypeStruct((n, D_MODEL), F32),
        compiler_params=_cparams(("arbitrary",)),
        name="moe_combine_lat" if latent else "moe_combine_ctx",
    )(pos, y, rt_all, x_all, mod, lg, lb)


def _routing_positions(rt_all, counts):
    tm = TM_MOE
    cnt = counts[0, :N_EXPERTS].astype(jnp.int32)
    padded = ((cnt + tm - 1) // tm) * tm
    gend = jnp.cumsum(padded)
    gstart = gend - padded
    route = rt_all[:, :8].astype(jnp.int32)
    onehot = lambda idx: idx[:, None] == jnp.arange(N_EXPERTS, dtype=jnp.int32)[None, :]
    start_of = lambda idx: jnp.sum(jnp.where(onehot(idx), gstart[None, :], 0), axis=1)
    pos = jnp.concatenate([start_of(route[:, 0]) + route[:, 4], start_of(route[:, 1]) + route[:, 5]])
    pad = jnp.concatenate([gstart + cnt, padded - cnt]).astype(jnp.int32)
    tile_start = jnp.arange(NT_MOE, dtype=jnp.int32) * tm
    te = jnp.sum((tile_start[:, None] >= gend[None, :]).astype(jnp.int32), axis=1)
    n_active = (gend[-1] // tm).astype(jnp.int32)
    last_e = jnp.take(te, jnp.maximum(n_active - 1, 0))
    te = jnp.where(tile_start < gend[-1], te, last_e).astype(jnp.int32)
    return pos, pad, te, n_active.reshape(1)


def kernel(x_prompt, x_sample, cache_ckv_l0, cache_krope_l0, cache_k_l1, cache_v_l1, c, c_ctx, w_ada_l0, b_ada_l0, mla_w_dq, mla_g_q, mla_w_uq, mla_w_dkv, mla_g_kv, mla_w_ukv, mla_w_o, ln1_g_l0, ln1_b_l0, ffn_w_in, ffn_w_out, ln2_g_l0, ln2_b_l0, w_ada_l1, b_ada_l1, na_w_qkv, na_rpb, na_w_o, ln1_g_l1, ln1_b_l1, moe_w_router, moe_w_in, moe_w_out, ln2_g_l1, ln2_b_l1):
    row = lambda v: v.reshape(1, -1)
    xp = x_prompt.reshape(N_CTX, D_MODEL)
    xs = x_sample.reshape(N_LAT, D_MODEL)
    groups = ((xp, False), (xs, True))

    cvecs = jnp.concatenate([c_ctx[None], c, jnp.zeros((8 - 1 - DEC_BATCH, D_MODEL), F32)], axis=0)
    mod0 = _ada_mod(cvecs, w_ada_l0, b_ada_l0)
    mod1 = _ada_mod(cvecs, w_ada_l1, b_ada_l1)

    wa = jnp.concatenate([mla_w_dq, mla_w_dkv], axis=1).astype(BF16)
    wuq = mla_w_uq.reshape(MLA_Q_LORA, MLA_HEADS, MLA_NOPE + MLA_ROPE)
    wuq = jnp.pad(wuq, ((0, 0), (0, 0), (0, Q_HEAD_PAD - MLA_NOPE - MLA_ROPE)))
    wuq = wuq.reshape(MLA_Q_LORA, MLA_HEADS * Q_HEAD_PAD).astype(BF16)
    wukv = mla_w_ukv.astype(BF16)
    wo0 = mla_w_o.astype(BF16)
    w_in0 = ffn_w_in.astype(BF16)
    w_out0 = ffn_w_out.astype(BF16)
    wqkv = na_w_qkv.astype(BF16)
    wo1 = na_w_o.astype(BF16)
    wr = jnp.pad(moe_w_router, ((0, 0), (0, LANES - N_EXPERTS))).astype(BF16)
    bias_tab = _na_bias_table(na_rpb)
    tab = _rope_tables()

    kvx_cache = _matmul(cache_ckv_l0.reshape(DEC_BATCH * PAST_LEN, MLA_KV_LORA), wukv, BF16, 512, "mla_expand_cache")
    kvx_cache = kvx_cache.reshape(DEC_BATCH, PAST_LEN, -1)
    krp_cache = jnp.pad(cache_krope_l0, ((0, 0), (0, 0), (0, LANES - MLA_ROPE))).astype(BF16)
    x1 = []
    new_ckv = new_kr = None
    vt_cache = kvx_cache.reshape(DEC_BATCH, PAST_LEN, MLA_HEADS, MLA_NOPE + MLA_V)[..., MLA_NOPE:]
    vt_cache = vt_cache.reshape(DEC_BATCH, PAST_LEN, MLA_HEADS * MLA_V).transpose(0, 2, 1)
    for x, latent in groups:
        proj = _mla_proj(x, mod0, wa, row(mla_g_q), row(mla_g_kv), wuq, wukv, tab, latent)
        if latent:
            qt, kvx, krp, vt = proj
            o = _mla_attn_lat(qt, kvx, krp, kvx_cache, krp_cache, vt, vt_cache)
        else:
            q, kvx, new_ckv, new_kr, krp = proj
            o = _mla_attn_ctx(q, kvx, krp)
        x1.append(_attn_out_ffn(o, x, wo0, mod0, row(ln1_g_l0), row(ln1_b_l0), w_in0, w_out0,
                                row(ln2_g_l0), row(ln2_b_l0), latent))

    kct = cache_k_l1.reshape(DEC_BATCH, PAST_LEN, D_MODEL).transpose(0, 2, 1).astype(BF16)
    vc = cache_v_l1.reshape(DEC_BATCH, PAST_LEN, D_MODEL).astype(BF16)
    q, new_k, new_v, kt = _na_qkv(x1[0], mod1, wqkv, F32, False)
    o_ctx = _na_attn_ctx(q, kt, new_v)
    q, _, v, kt = _na_qkv(x1[1], mod1, wqkv, BF16, True)
    o_lat = _na_attn_lat(q, kt, v, kct, vc, bias_tab)
    x2_all, rt_all, counts = _proj_ln_route(o_ctx, x1[0], o_lat, x1[1], wo1, mod1, row(ln1_g_l1), row(ln1_b_l1), wr)

    pos, pad, te, n_active = _routing_positions(rt_all, counts)
    x_sorted = _moe_scatter(pos, pad, n_active, x2_all, mod1)
    hmid = _moe_up(te, n_active, x_sorted, moe_w_in)
    y = _moe_down(te, n_active, hmid, moe_w_out)
    outs = [_moe_combine(pos, y, rt_all, x2_all, mod1, row(ln2_g_l1), row(ln2_b_l1), latent)
            for latent in (False, True)]

    return (outs[0].reshape(BATCH, SEQ, D_MODEL),
            outs[1].reshape(DEC_BATCH, DEC_SEQ, D_MODEL),
            new_ckv.reshape(BATCH, SEQ, MLA_KV_LORA),
            new_kr.reshape(BATCH, SEQ, MLA_ROPE),
            new_k.reshape(BATCH, SEQ, NA_HEADS, NA_HD),
            new_v.reshape(BATCH, SEQ, NA_HEADS, NA_HD))
```

```python
import functools
import math

import numpy as np
import jax
import jax.numpy as jnp
from jax import lax
from jax.experimental import pallas as pl
from jax.experimental.pallas import tpu as pltpu

F32 = jnp.float32
BF16 = jnp.bfloat16

D_MODEL = 1024
BATCH = 32
SEQ = 256
DEPTH = 2
DEC_BATCH = 2
DEC_SEQ = 2048
PAST_LEN = 512
GRID_W = 64
MLA_HEADS = 8
MLA_NOPE = 128
MLA_ROPE = 64
MLA_V = 128
MLA_Q_LORA = 512
MLA_KV_LORA = 256
MLA_SCALE = 1.0 / math.sqrt(MLA_NOPE + MLA_ROPE)
ROPE_THETA = 10000.0
NA_HEADS = 16
NA_HD = D_MODEL // NA_HEADS
NA_WIN_ROWS = 8
NA_WIN_COLS = 16
NA_SCALE = 1.0 / math.sqrt(NA_HD)
FFN_DENSE = 2816
N_EXPERTS = 8
TOP_K = 2
FFN_EXPERT = 3584
ALPHA = (2 * DEPTH) ** 0.25
LN_EPS = 1e-5
RMS_EPS = 1e-6

N_CTX = BATCH * SEQ
N_LAT = DEC_BATCH * DEC_SEQ
N_TOK = N_CTX + N_LAT
GRID_ROWS = DEC_SEQ // GRID_W
Q_HEAD_PAD = 256
LANES = 128
SUBLANES = 8
NEG_BIG = -1e30
MLA_KEY_CHUNK = 256
NA_PAIR = 2 * NA_HD
NA_BLK_ROWS = 4
NA_KT_W = 2 * GRID_W
NA_WIN_PAIRS = (NA_BLK_ROWS + NA_WIN_ROWS) // 2
NA_N_BLOCKS = DEC_SEQ // (NA_BLK_ROWS * GRID_W)
NA_BLOCK_VARIANTS = (0, 1, NA_N_BLOCKS - 1)

VMEM_LIMIT = 56 * 1024 * 1024

TM_PROJ = 512
TM_FFN = 512
MXU_TILE = 256
FFN_CHUNKS = ((0, 6 * MXU_TILE), (6 * MXU_TILE, FFN_DENSE))
TM_MOE = 512
MOE_ROW_STEP = 128
TF_MOE = 1792
N_PAIRS = N_TOK * TOP_K
NT_MOE = (N_PAIRS + N_EXPERTS * TM_MOE) // TM_MOE
ROWS_SORTED = NT_MOE * TM_MOE
TM_ROUTE = 256


def _cparams(sem, vmem=VMEM_LIMIT):
    return pltpu.CompilerParams(dimension_semantics=sem, vmem_limit_bytes=vmem)


def _silu(x):
    return x * jax.nn.sigmoid(x)


def _layer_norm(y, g, b):
    mu = jnp.mean(y, axis=-1, keepdims=True)
    d = y - mu
    var = jnp.mean(d * d, axis=-1, keepdims=True)
    return d * lax.rsqrt(var + LN_EPS) * g + b


def _rms_norm(y, g):
    return y * lax.rsqrt(jnp.mean(y * y, axis=-1, keepdims=True) + RMS_EPS) * g


def _dot(a, b):
    return jnp.dot(a, b, preferred_element_type=F32)


def _dot_nt(a, b):
    return lax.dot_general(a, b, (((1,), (1,)), ((), ())), preferred_element_type=F32)


def _ada_kernel(c_ref, w_ref, b_ref, o_ref):
    s = _silu(c_ref[...]).astype(BF16)
    o_ref[...] = _dot(s, w_ref[...].astype(BF16)) + b_ref[...]


def _ada_mod(cvecs, w, b):
    tn = 1536
    m = pl.pallas_call(
        _ada_kernel,
        grid=(6 * D_MODEL // tn,),
        in_specs=[pl.BlockSpec((8, D_MODEL), lambda j: (0, 0)),
                  pl.BlockSpec((D_MODEL, tn), lambda j: (0, j)),
                  pl.BlockSpec((1, tn), lambda j: (0, j))],
        out_specs=pl.BlockSpec((8, tn), lambda j: (0, j)),
        out_shape=jax.ShapeDtypeStruct((8, 6 * D_MODEL), F32),
        compiler_params=_cparams(("arbitrary",)),
        name="ada_mod",
    )(cvecs, w, b.reshape(1, -1))
    return m[:3].reshape(3 * 6, 1, D_MODEL)


def _mod_spec(j, row_fn):
    return pl.BlockSpec((None, 1, D_MODEL), lambda i, *_: (row_fn(i) * 6 + j, 0, 0))


def _row_fn(latent, tm):
    if not latent:
        return lambda i: 0
    per = DEC_SEQ // tm
    return lambda i: 1 + i // per


def _const_spec(shape):
    nd = len(shape)
    return pl.BlockSpec(shape, lambda *_: (0,) * nd, pipeline_mode=pl.Buffered(1))


def _rope_kernel(invf_ref, o_ref):
    i = pl.program_id(0)
    tm = o_ref.shape[1]
    t = i * tm + lax.broadcasted_iota(jnp.int32, (tm, LANES), 0)
    lane = lax.broadcasted_iota(jnp.int32, (tm, LANES), 1)
    row = t >> int(math.log2(GRID_W))
    col = t & (GRID_W - 1)
    pos = jnp.where(lane < MLA_ROPE // 2, row, col).astype(F32)
    ang = pos * invf_ref[...]
    cos = jnp.cos(ang)
    sin = jnp.sin(ang)
    unit = lane >> int(math.log2(MLA_ROPE // 4))
    first = (unit == 0) | (unit == 2)
    second = (unit == 1) | (unit == 3)
    o_ref[0] = jnp.where(lane < MLA_ROPE, cos, 0.0)
    o_ref[1] = jnp.where(first, -sin, 0.0)
    o_ref[2] = jnp.where(second, sin, 0.0)


def _rope_tables():
    half = MLA_ROPE // 2
    inv_freq = (1.0 / (ROPE_THETA ** (np.arange(0, half, 2, dtype=np.float32) / half))).astype(np.float32)
    lane_f = np.zeros((1, LANES), np.float32)
    lane_f[0, :MLA_ROPE] = np.tile(inv_freq, 4)
    tm = 256
    return pl.pallas_call(
        _rope_kernel,
        grid=(DEC_SEQ // tm,),
        in_specs=[_const_spec((1, LANES))],
        out_specs=pl.BlockSpec((3, tm, LANES), lambda i: (0, i, 0)),
        out_shape=jax.ShapeDtypeStruct((3, DEC_SEQ, LANES), F32),
        compiler_params=_cparams(("arbitrary",)),
        name="rope_tables",
    )(jnp.asarray(lane_f))


def _rotate(v, tab_ref):
    return (v * tab_ref[0] + pltpu.roll(v, LANES - MLA_ROPE // 4, 1) * tab_ref[1]
            + pltpu.roll(v, MLA_ROPE // 4, 1) * tab_ref[2])


def _mla_proj_kernel(*refs, rope):
    if rope:
        (x_ref, sc_ref, sh_ref, wa_ref, gq_ref, gkv_ref, wuq_ref, wukv_ref, tab_ref,
         qt_ref, kvx_ref, krp_ref, vt_ref) = refs
    else:
        (x_ref, sc_ref, sh_ref, wa_ref, gq_ref, gkv_ref, wuq_ref, wukv_ref,
         q_ref, kvx_ref, ckv_ref, kr_ref, krp_ref) = refs
    h = (x_ref[...] * (1.0 + sc_ref[...]) + sh_ref[...]).astype(BF16)
    t = _dot(h, wa_ref[...])
    cq = _rms_norm(t[:, :MLA_Q_LORA], gq_ref[...])
    ckv = _rms_norm(t[:, MLA_Q_LORA:MLA_Q_LORA + MLA_KV_LORA], gkv_ref[...])
    kr = t[:, MLA_Q_LORA + MLA_KV_LORA:]
    kvx = _dot(ckv.astype(BF16), wukv_ref[...])
    kvx_ref[...] = kvx.astype(BF16)
    q = _dot(cq.astype(BF16), wuq_ref[...])
    krp = jnp.concatenate([kr, jnp.zeros_like(kr)], axis=-1)
    if rope:
        krp_ref[...] = _rotate(krp, tab_ref).astype(BF16)
        parts = []
        for hd in range(MLA_HEADS):
            lo = hd * Q_HEAD_PAD
            parts += [q[:, lo:lo + MLA_NOPE], _rotate(q[:, lo + MLA_NOPE:lo + Q_HEAD_PAD], tab_ref)]
        qt_ref[...] = jnp.concatenate(parts, axis=1).T.astype(BF16)
        vcols = [kvx[:, hd * Q_HEAD_PAD + MLA_NOPE:(hd + 1) * Q_HEAD_PAD] for hd in range(MLA_HEADS)]
        vt_ref[...] = jnp.concatenate(vcols, axis=1).T.astype(BF16)
    else:
        ckv_ref[...] = ckv
        kr_ref[...] = kr
        krp_ref[...] = krp.astype(BF16)
        q_ref[...] = q.astype(BF16)


def _mla_proj(x, mod, wa, gq, gkv, wuq, wukv, tab, latent):
    n = x.shape[0]
    tm = TM_PROJ
    rf = _row_fn(latent, tm)
    tok = lambda w: pl.BlockSpec((tm, w), lambda i: (i, 0))
    in_specs = [tok(D_MODEL), _mod_spec(1, rf), _mod_spec(0, rf),
                _const_spec(wa.shape), _const_spec(gq.shape), _const_spec(gkv.shape),
                _const_spec(wuq.shape), _const_spec(wukv.shape)]
    args = [x, mod, mod, wa, gq, gkv, wuq, wukv]
    wq, wkv = MLA_HEADS * Q_HEAD_PAD, MLA_HEADS * (MLA_NOPE + MLA_V)
    if latent:
        per = DEC_SEQ // tm
        in_specs.append(pl.BlockSpec((3, tm, LANES), lambda i: (0, i % per, 0)))
        args.append(tab)
        out_specs = [pl.BlockSpec((wq, tm), lambda i: (0, i)), tok(wkv), tok(LANES),
                     pl.BlockSpec((None, MLA_HEADS * MLA_V, tm), lambda i: (i // per, 0, i % per))]
        out_shape = [jax.ShapeDtypeStruct((wq, n), BF16), jax.ShapeDtypeStruct((n, wkv), BF16),
                     jax.ShapeDtypeStruct((n, LANES), BF16),
                     jax.ShapeDtypeStruct((DEC_BATCH, MLA_HEADS * MLA_V, DEC_SEQ), BF16)]
    else:
        out_specs = [tok(wq), tok(wkv), tok(MLA_KV_LORA), tok(MLA_ROPE), tok(LANES)]
        out_shape = [jax.ShapeDtypeStruct((n, wq), BF16), jax.ShapeDtypeStruct((n, wkv), BF16),
                     jax.ShapeDtypeStruct((n, MLA_KV_LORA), F32), jax.ShapeDtypeStruct((n, MLA_ROPE), F32),
                     jax.ShapeDtypeStruct((n, LANES), BF16)]
    return pl.pallas_call(
        functools.partial(_mla_proj_kernel, rope=latent),
        grid=(n // tm,),
        in_specs=in_specs,
        out_specs=out_specs,
        out_shape=out_shape,
        compiler_params=_cparams(("arbitrary",)),
        name="mla_proj_lat" if latent else "mla_proj_ctx",
    )(*args)


def _matmul_kernel(a_ref, b_ref, o_ref):
    o_ref[...] = _dot(a_ref[...].astype(BF16), b_ref[...]).astype(o_ref.dtype)


def _matmul(a, b, out_dtype, tm, name):
    m, k = a.shape
    n = b.shape[1]
    return pl.pallas_call(
        _matmul_kernel,
        grid=(m // tm,),
        in_specs=[pl.BlockSpec((tm, k), lambda i: (i, 0)), _const_spec(b.shape)],
        out_specs=pl.BlockSpec((tm, n), lambda i: (i, 0)),
        out_shape=jax.ShapeDtypeStruct((m, n), out_dtype),
        compiler_params=_cparams(("arbitrary",)),
        name=name,
    )(a, b)


def _mla_head(qh, kparts, vparts):
    ss = [_dot_nt(qh, k) for k in kparts]
    m = functools.reduce(jnp.maximum, [jnp.max(s, axis=-1, keepdims=True) for s in ss])
    ps = [jnp.exp2((s - m) * (MLA_SCALE * math.log2(math.e))) for s in ss]
    l = functools.reduce(jnp.add, [jnp.sum(p, axis=-1, keepdims=True) for p in ps])
    o = functools.reduce(jnp.add, [_dot(p.astype(BF16), v) for p, v in zip(ps, vparts)])
    return o / l


def _mla_attn_ctx_kernel(q_ref, kvx_ref, krp_ref, o_ref):
    krp = krp_ref[...]
    for hd in range(MLA_HEADS):
        lo = hd * (MLA_NOPE + MLA_V)
        qh = q_ref[:, hd * Q_HEAD_PAD:(hd + 1) * Q_HEAD_PAD]
        kh = jnp.concatenate([kvx_ref[:, lo:lo + MLA_NOPE], krp], axis=-1)
        vh = kvx_ref[:, lo + MLA_NOPE:lo + MLA_NOPE + MLA_V]
        o_ref[:, hd * MLA_V:(hd + 1) * MLA_V] = _mla_head(qh, [kh], [vh]).astype(BF16)


def _mla_attn_ctx(q, kvx, krp):
    tok = lambda w: pl.BlockSpec((SEQ, w), lambda b: (b, 0))
    return pl.pallas_call(
        _mla_attn_ctx_kernel,
        grid=(BATCH,),
        in_specs=[tok(q.shape[1]), tok(kvx.shape[1]), tok(LANES)],
        out_specs=tok(MLA_HEADS * MLA_V),
        out_shape=jax.ShapeDtypeStruct((N_CTX, MLA_HEADS * MLA_V), BF16),
        compiler_params=_cparams(("arbitrary",)),
        name="mla_attn_ctx",
    )(q, kvx, krp)


def _mla_attn_lat_kernel(qt_ref, kvl_ref, krl_ref, kvc_ref, krc_ref, vtl_ref, vtc_ref, o_ref):
    chunks = [(kvl_ref, krl_ref, vtl_ref, c) for c in range(DEC_SEQ // MLA_KEY_CHUNK)]
    chunks += [(kvc_ref, krc_ref, vtc_ref, c) for c in range(PAST_LEN // MLA_KEY_CHUNK)]
    for hd in range(MLA_HEADS):
        lo = hd * (MLA_NOPE + MLA_V)
        qt = qt_ref[hd * Q_HEAD_PAD:(hd + 1) * Q_HEAD_PAD, :]
        ss = []
        for kv_ref, kr_ref, _, c in chunks:
            rows = slice(c * MLA_KEY_CHUNK, (c + 1) * MLA_KEY_CHUNK)
            ss.append(_dot(jnp.concatenate([kv_ref[rows, lo:lo + MLA_NOPE], kr_ref[rows, :]], axis=-1), qt))
        m = functools.reduce(jnp.maximum, [jnp.max(s, axis=0, keepdims=True) for s in ss])
        ps = [jnp.exp2((s - m) * (MLA_SCALE * math.log2(math.e))) for s in ss]
        l = functools.reduce(jnp.add, [jnp.sum(p, axis=0, keepdims=True) for p in ps])
        ots = [_dot(vt_ref[hd * MLA_V:(hd + 1) * MLA_V, c * MLA_KEY_CHUNK:(c + 1) * MLA_KEY_CHUNK], p.astype(BF16))
               for (_, _, vt_ref, c), p in zip(chunks, ps)]
        ot = functools.reduce(jnp.add, ots) / l
        o_ref[:, hd * MLA_V:(hd + 1) * MLA_V] = ot.T.astype(BF16)


def _mla_attn_lat(qt, kvx_lat, krp_lat, kvx_ctx, krp_ctx, vt_lat, vt_ctx):
    tq = 256
    per = DEC_SEQ // tq
    wkv = kvx_lat.shape[-1]
    wv = MLA_HEADS * MLA_V
    batch = lambda rows, cols: pl.BlockSpec((None, rows, cols), lambda b, i: (b, 0, 0))
    return pl.pallas_call(
        _mla_attn_lat_kernel,
        grid=(DEC_BATCH, per),
        in_specs=[pl.BlockSpec((qt.shape[0], tq), lambda b, i: (0, b * per + i)),
                  batch(DEC_SEQ, wkv), batch(DEC_SEQ, LANES), batch(PAST_LEN, wkv), batch(PAST_LEN, LANES),
                  batch(wv, DEC_SEQ), batch(wv, PAST_LEN)],
        out_specs=pl.BlockSpec((tq, wv), lambda b, i: (b * per + i, 0)),
        out_shape=jax.ShapeDtypeStruct((N_LAT, wv), BF16),
        compiler_params=_cparams(("arbitrary", "arbitrary")),
        name="mla_attn_lat",
    )(qt, kvx_lat.reshape(DEC_BATCH, DEC_SEQ, wkv), krp_lat.reshape(DEC_BATCH, DEC_SEQ, LANES), kvx_ctx, krp_ctx,
      vt_lat, vt_ctx)


def _top2(logits):
    lane = lax.broadcasted_iota(jnp.int32, logits.shape, 1).astype(F32)
    m1 = jnp.max(logits, axis=-1, keepdims=True)
    i1 = jnp.min(jnp.where(logits == m1, lane, float(LANES)), axis=-1, keepdims=True)
    rest = jnp.where(lane == i1, -jnp.inf, logits)
    m2 = jnp.max(rest, axis=-1, keepdims=True)
    i2 = jnp.min(jnp.where(rest == m2, lane, float(LANES)), axis=-1, keepdims=True)
    e = jnp.exp(m2 - m1)
    w1 = 1.0 / (1.0 + e)
    w2 = e / (1.0 + e)
    return lane, i1, i2, w1, w2


def _proj_ln_route_kernel(oc_ref, xc_ref, ol_ref, xl_ref, wo_ref, g_ref, lg_ref, lb_ref, sc_ref, sh_ref, wr_ref,
                          x1_ref, rt_ref, cnt_ref, carry, *, n_ctx_tiles):
    i = pl.program_id(0)

    @pl.when(i == 0)
    def _():
        carry[...] = jnp.zeros_like(carry)

    def body(o_ref, x_ref):
        out = _dot(o_ref[...], wo_ref[...])
        x1 = _layer_norm(ALPHA * x_ref[...] + g_ref[...] * out, lg_ref[...], lb_ref[...])
        x1_ref[...] = x1
        hm = x1 * (1.0 + sc_ref[...]) + sh_ref[...]
        logits = _dot(hm.astype(BF16), wr_ref[...])
        lane = lax.broadcasted_iota(jnp.int32, logits.shape, 1)
        logits = jnp.where(lane < N_EXPERTS, logits, -jnp.inf)
        lane_f, i1, i2, w1, w2 = _top2(logits)
        tm = logits.shape[0]
        oh1 = (lane_f == i1).astype(F32)
        oh2 = (lane_f == i2).astype(F32)
        rr = lax.broadcasted_iota(jnp.int32, (tm, tm), 0)
        cc = lax.broadcasted_iota(jnp.int32, (tm, tm), 1)
        below = (cc < rr).astype(BF16)
        tot1 = jnp.sum(oh1, axis=0, keepdims=True)
        tot2 = jnp.sum(oh2, axis=0, keepdims=True)
        base = carry[...]
        cum1 = _dot(below, oh1.astype(BF16)) + base
        cum2 = _dot(below, oh2.astype(BF16)) + (base + tot1)
        rank1 = jnp.sum(oh1 * cum1, axis=-1, keepdims=True)
        rank2 = jnp.sum(oh2 * cum2, axis=-1, keepdims=True)
        vals = (i1, i2, w1, w2, rank1, rank2)
        rt = jnp.zeros_like(logits)
        for k, val in enumerate(vals):
            rt = jnp.where(lane == k, val, rt)
        rt_ref[...] = rt
        carry[...] = base + tot1 + tot2
        cnt_ref[...] = jnp.broadcast_to(carry[...], cnt_ref.shape)

    is_ctx = i < n_ctx_tiles
    pl.when(is_ctx)(lambda: body(oc_ref, xc_ref))
    pl.when(jnp.logical_not(is_ctx))(lambda: body(ol_ref, xl_ref))


def _all_row_fn(tm):
    nc = N_CTX // tm
    per = DEC_SEQ // tm
    return lambda i: jnp.where(i < nc, 0, 1 + (i - nc) // per)


def _proj_ln_route(o_ctx, x_ctx, o_lat, x_lat, wo, mod, lg, lb, wr):
    tm = TM_PROJ
    nc = N_CTX // tm
    rf = _all_row_fn(tm)
    ctx = lambda w: pl.BlockSpec((tm, w), lambda i: (jnp.minimum(i, nc - 1), 0))
    lat = lambda w: pl.BlockSpec((tm, w), lambda i: (jnp.maximum(i - nc, 0), 0))
    tok = lambda w: pl.BlockSpec((tm, w), lambda i: (i, 0))
    return pl.pallas_call(
        functools.partial(_proj_ln_route_kernel, n_ctx_tiles=nc),
        grid=(N_TOK // tm,),
        in_specs=[ctx(D_MODEL), ctx(D_MODEL), lat(D_MODEL), lat(D_MODEL), _const_spec(wo.shape), _mod_spec(2, rf),
                  _const_spec(lg.shape), _const_spec(lb.shape), _mod_spec(4, rf), _mod_spec(3, rf),
                  _const_spec(wr.shape)],
        out_specs=[tok(D_MODEL), tok(LANES), pl.BlockSpec((8, LANES), lambda i: (0, 0))],
        out_shape=[jax.ShapeDtypeStruct((N_TOK, D_MODEL), F32),
                   jax.ShapeDtypeStruct((N_TOK, LANES), F32),
                   jax.ShapeDtypeStruct((8, LANES), F32)],
        scratch_shapes=[pltpu.VMEM((1, LANES), F32)],
        compiler_params=_cparams(("arbitrary",)),
        name="proj_ln_route",
    )(o_ctx, x_ctx, o_lat, x_lat, wo, mod, lg, lb, mod, mod, wr)


def _attn_out_ffn_kernel(o_ref, x_ref, wo_ref, g1_ref, lg1_ref, lb1_ref, sc_ref, sh_ref, g2_ref,
                         wi_ref, wd_ref, lg2_ref, lb2_ref, y_ref):
    x1 = _layer_norm(ALPHA * x_ref[...] + g1_ref[...] * _dot(o_ref[...], wo_ref[...]), lg1_ref[...], lb1_ref[...])
    h = (x1 * (1.0 + sc_ref[...]) + sh_ref[...]).astype(BF16)
    acc = None
    for lo, hi in FFN_CHUNKS:
        gate = _dot(h, wi_ref[:, lo:hi])
        up = _dot(h, wi_ref[:, FFN_DENSE + lo:FFN_DENSE + hi])
        part = _dot((_silu(gate) * up).astype(BF16), wd_ref[lo:hi, :])
        acc = part if acc is None else acc + part
    y_ref[...] = _layer_norm(ALPHA * x1 + g2_ref[...] * acc, lg2_ref[...], lb2_ref[...])


def _attn_out_ffn(o, x, wo, mod, lg1, lb1, w_in, w_out, lg2, lb2, latent):
    n = x.shape[0]
    tm = TM_FFN
    rf = _row_fn(latent, tm)
    tok = lambda w: pl.BlockSpec((tm, w), lambda i: (i, 0))
    vec = _const_spec((1, D_MODEL))
    return pl.pallas_call(
        _attn_out_ffn_kernel,
        grid=(n // tm,),
        in_specs=[tok(o.shape[1]), tok(D_MODEL), _const_spec(wo.shape), _mod_spec(2, rf), vec, vec,
                  _mod_spec(4, rf), _mod_spec(3, rf), _mod_spec(5, rf),
                  _const_spec(w_in.shape), _const_spec(w_out.shape), vec, vec],
        out_specs=tok(D_MODEL),
        out_shape=jax.ShapeDtypeStruct((n, D_MODEL), F32),
        compiler_params=_cparams(("arbitrary",)),
        name="attn_out_ffn_lat" if latent else "attn_out_ffn_ctx",
    )(o, x, wo, mod, lg1, lb1, mod, mod, mod, w_in, w_out, lg2, lb2)


def _na_qkv_kernel(x_ref, sc_ref, sh_ref, w_ref, q_ref, k_ref, v_ref, kt_ref):
    h = (x_ref[...] * (1.0 + sc_ref[...]) + sh_ref[...]).astype(BF16)
    qkv = _dot(h, w_ref[...])
    q_ref[...] = (qkv[:, :D_MODEL] * NA_SCALE).astype(BF16)
    k = qkv[:, D_MODEL:2 * D_MODEL]
    k_ref[...] = k.astype(k_ref.dtype)
    v_ref[...] = qkv[:, 2 * D_MODEL:].astype(v_ref.dtype)
    kt = k.T
    nblk, _, w = kt_ref.shape
    for t in range(nblk):
        kt_ref[t] = kt[:, t * w:(t + 1) * w].astype(BF16)


def _na_qkv(x, mod, w, kv_dtype, latent):
    n = x.shape[0]
    tm = TM_PROJ
    kt_w = NA_KT_W if latent else SEQ
    rf = _row_fn(latent, tm)
    tok = pl.BlockSpec((tm, D_MODEL), lambda i: (i, 0))
    return pl.pallas_call(
        _na_qkv_kernel,
        grid=(n // tm,),
        in_specs=[tok, _mod_spec(1, rf), _mod_spec(0, rf), _const_spec(w.shape)],
        out_specs=[tok, tok, tok, pl.BlockSpec((tm // kt_w, D_MODEL, kt_w), lambda i: (i, 0, 0))],
        out_shape=[jax.ShapeDtypeStruct((n, D_MODEL), BF16),
                   jax.ShapeDtypeStruct((n, D_MODEL), kv_dtype),
                   jax.ShapeDtypeStruct((n, D_MODEL), kv_dtype),
                   jax.ShapeDtypeStruct((n // kt_w, D_MODEL, kt_w), BF16)],
        compiler_params=_cparams(("arbitrary",)),
        name="na_qkv_lat" if latent else "na_qkv_ctx",
    )(x, mod, mod, w)


def _softmax_pv(scores, values):
    m = functools.reduce(jnp.maximum, [jnp.max(s, axis=-1, keepdims=True) for s in scores])
    ps = [jnp.exp(s - m) for s in scores]
    l = functools.reduce(jnp.add, [jnp.sum(p, axis=-1, keepdims=True) for p in ps])
    o = functools.reduce(jnp.add, [_dot(p.astype(BF16), v) for p, v in zip(ps, values)])
    return o / l


def _head_of_pair(x, half):
    lane = lax.broadcasted_iota(jnp.int32, x.shape, 1)
    keep = (lane < NA_HD) if half == 0 else (lane >= NA_HD)
    return jnp.where(keep, x, jnp.zeros_like(x))


def _merge_pair(o0, o1):
    lane = lax.broadcasted_iota(jnp.int32, o0.shape, 1)
    return jnp.where(lane < NA_HD, o0, o1)


def _na_attn_ctx_kernel(q_ref, kt_ref, v_ref, o_ref):
    for p in range(NA_HEADS // 2):
        cols = slice(p * NA_PAIR, (p + 1) * NA_PAIR)
        qp = q_ref[:, cols]
        ktp = kt_ref[cols, :]
        vp = v_ref[:, cols].astype(BF16)
        outs = [_softmax_pv([_dot(_head_of_pair(qp, half), ktp)], [vp]) for half in range(2)]
        o_ref[:, cols] = _merge_pair(*outs).astype(BF16)


def _na_attn_ctx(q, kt, v):
    tok = pl.BlockSpec((SEQ, D_MODEL), lambda b: (b, 0))
    return pl.pallas_call(
        _na_attn_ctx_kernel,
        grid=(BATCH,),
        in_specs=[tok, pl.BlockSpec((None, D_MODEL, SEQ), lambda b: (b, 0, 0)), tok],
        out_specs=tok,
        out_shape=jax.ShapeDtypeStruct((N_CTX, D_MODEL), BF16),
        compiler_params=_cparams(("arbitrary",)),
        name="na_attn_ctx",
    )(q, kt, v)


def _na_win_start(m, clip=jnp.clip):
    return clip(m * NA_BLK_ROWS // 2 - NA_WIN_ROWS // 4, 0, GRID_ROWS // 2 - NA_WIN_PAIRS)


def _na_variant(m):
    return jnp.where(m == 0, 0, jnp.where(m == NA_N_BLOCKS - 1, 2, 1))


def _na_attn_lat_kernel(q_ref, kt_ref, v_ref, kct_ref, vc_ref, bias_ref, o_ref):
    rp0 = _na_win_start(pl.program_id(1))
    win = pl.ds(pl.multiple_of(rp0 * NA_KT_W, NA_KT_W), NA_WIN_PAIRS * NA_KT_W)
    for p in range(NA_HEADS // 2):
        cols = slice(p * NA_PAIR, (p + 1) * NA_PAIR)
        qp = q_ref[:, cols]
        kw = jnp.concatenate([kt_ref[rp0 + t, cols, :] for t in range(NA_WIN_PAIRS)], axis=1)
        kc = kct_ref[cols, :]
        vw = v_ref[win, cols]
        vc = vc_ref[:, cols]
        outs = []
        for half in range(2):
            qh = _head_of_pair(qp, half)
            s_nb = _dot(qh, kw) + bias_ref[2 * p + half]
            outs.append(_softmax_pv([s_nb, _dot(qh, kc)], [vw, vc]))
        o_ref[:, cols] = _merge_pair(*outs).astype(BF16)


def _na_bias_kernel(e_ref, e64_ref, o_ref):
    n_dr = 2 * NA_WIN_ROWS - 1
    shape = (GRID_W, LANES)
    lane = lax.broadcasted_iota(jnp.int32, shape, 1)
    c = lax.broadcasted_iota(jnp.int32, shape, 0)
    kc = lane & (GRID_W - 1)
    cs = jnp.clip(c - NA_WIN_COLS // 2, 0, GRID_W - NA_WIN_COLS)
    valid = (kc >= cs) & (kc < cs + NA_WIN_COLS)
    toeplitz = lambda ref, a: pltpu.roll(jnp.broadcast_to(ref[a:a + 1, :], shape), 0, 1, stride=1, stride_axis=0)
    rows = ([toeplitz(e_ref, a) for a in range(n_dr)], [toeplitz(e64_ref, a) for a in range(n_dr)])
    masked = jnp.full(shape, NEG_BIG, F32)
    for v, m in enumerate(NA_BLOCK_VARIANTS):
        rp0 = _na_win_start(m, clip=lambda x, lo, hi: min(max(x, lo), hi))
        for i in range(NA_BLK_ROWS):
            r = NA_BLK_ROWS * m + i
            rs = min(max(r - NA_WIN_ROWS // 2, 0), GRID_ROWS - NA_WIN_ROWS)
            for t in range(NA_WIN_PAIRS):
                halves = []
                for u in range(2):
                    kr = 2 * (rp0 + t) + u
                    halves.append(rows[u][kr - r + NA_WIN_ROWS - 1] if rs <= kr < rs + NA_WIN_ROWS else masked)
                o_ref[v, i * GRID_W:(i + 1) * GRID_W, t * LANES:(t + 1) * LANES] = jnp.where(
                    valid, jnp.where(lane < GRID_W, halves[0], halves[1]), NEG_BIG)


def _na_bias_table(rpb):
    n_dr = 2 * NA_WIN_ROWS - 1
    blk = NA_BLK_ROWS * GRID_W
    e = jnp.zeros((NA_HEADS, n_dr, LANES), F32)
    e = e.at[:, :, :NA_WIN_COLS].set(rpb[:, :, NA_WIN_COLS - 1:])
    e = e.at[:, :, LANES - (NA_WIN_COLS - 1):].set(rpb[:, :, :NA_WIN_COLS - 1])
    e64 = jnp.roll(e, GRID_W, axis=-1)
    spec = pl.BlockSpec((None, n_dr, LANES), lambda h: (h, 0, 0))
    return pl.pallas_call(
        _na_bias_kernel,
        grid=(NA_HEADS,),
        in_specs=[spec, spec],
        out_specs=pl.BlockSpec((None, len(NA_BLOCK_VARIANTS), blk, NA_WIN_PAIRS * NA_KT_W), lambda h: (h, 0, 0, 0)),
        out_shape=jax.ShapeDtypeStruct((NA_HEADS, len(NA_BLOCK_VARIANTS), blk, NA_WIN_PAIRS * NA_KT_W), F32),
        compiler_params=_cparams(("arbitrary",)),
        name="na_bias_table",
    )(e, e64)


def _na_attn_lat(q, kt, v, kct, vc, bias_tab):
    blk = NA_BLK_ROWS * GRID_W
    nblk = NA_N_BLOCKS
    npair = DEC_SEQ // NA_KT_W
    row = pl.BlockSpec((blk, D_MODEL), lambda b, m: (b * nblk + m, 0))
    bias_spec = pl.BlockSpec((NA_HEADS, None, blk, NA_WIN_PAIRS * NA_KT_W), lambda b, m: (0, _na_variant(m), 0, 0),
                             pipeline_mode=pl.Buffered(1))
    return pl.pallas_call(
        _na_attn_lat_kernel,
        grid=(DEC_BATCH, nblk),
        in_specs=[row,
                  pl.BlockSpec((None, npair, D_MODEL, NA_KT_W), lambda b, m: (b, 0, 0, 0)),
                  pl.BlockSpec((None, DEC_SEQ, D_MODEL), lambda b, m: (b, 0, 0)),
                  pl.BlockSpec((None, D_MODEL, PAST_LEN), lambda b, m: (b, 0, 0)),
                  pl.BlockSpec((None, PAST_LEN, D_MODEL), lambda b, m: (b, 0, 0)),
                  bias_spec],
        out_specs=row,
        out_shape=jax.ShapeDtypeStruct((N_LAT, D_MODEL), BF16),
        compiler_params=_cparams(("arbitrary", "arbitrary")),
        name="na_attn_lat",
    )(q, kt.reshape(DEC_BATCH, npair, D_MODEL, NA_KT_W), v.reshape(DEC_BATCH, DEC_SEQ, D_MODEL), kct, vc, bias_tab)


def _moe_scatter_kernel(pos_ref, pad_ref, na_ref, x_ref, sc_ref, sh_ref, xs_ref, hm_scr, zero_scr, sems, zsem):
    i = pl.program_id(0)
    tm = x_ref.shape[0]
    slot = i % 2
    hm_scr[slot] = x_ref[...] * (1.0 + sc_ref[...]) + sh_ref[...]

    def start(t, c):
        src = hm_scr.at[slot, pl.ds(t, 1)]
        for k in range(TOP_K):
            p = pos_ref[k * N_TOK + i * tm + t]
            pltpu.make_async_copy(src, xs_ref.at[pl.ds(p, 1)], sems.at[slot]).start()
        return c

    lax.fori_loop(0, tm, start, 0, unroll=8)

    def wait_slot(s):
        rows = xs_ref.at[pl.ds(0, TOP_K * tm)]
        pltpu.make_async_copy(rows, rows, sems.at[s]).wait()

    @pl.when(i > 0)
    def _():
        wait_slot(1 - slot)

    @pl.when(i == pl.num_programs(0) - 1)
    def _():
        zero_scr[...] = jnp.zeros_like(zero_scr)

        def pad_copies():
            out = []
            zero = lambda b: zero_scr.at[pl.ds(0, b)]
            for e in range(N_EXPERTS):
                off = pad_ref[e]
                n = pad_ref[N_EXPERTS + e]
                head = n & (SUBLANES - 1)
                for k in range(SUBLANES - 1):
                    out.append((k < head, pltpu.make_async_copy(zero(1), xs_ref.at[pl.ds(off + k, 1)], zsem)))
                off = off + head
                b = SUBLANES
                while b < TM_MOE:
                    rows = pl.ds(pl.multiple_of(off, SUBLANES), b)
                    out.append(((n & b) != 0, pltpu.make_async_copy(zero(b), xs_ref.at[rows], zsem)))
                    off = off + (n & b)
                    b *= 2
            return out

        def tile_copy(j):
            rows = pl.ds(pl.multiple_of(j * TM_MOE, TM_MOE), TM_MOE)
            return pltpu.make_async_copy(zero_scr, xs_ref.at[rows], zsem)

        for cond, cp in pad_copies():
            pl.when(cond)(cp.start)
        lax.fori_loop(na_ref[0], NT_MOE, lambda j, c: (tile_copy(j).start(), c)[1], 0)
        for cond, cp in pad_copies():
            pl.when(cond)(cp.wait)
        lax.fori_loop(na_ref[0], NT_MOE, lambda j, c: (tile_copy(j).wait(), c)[1], 0)
        wait_slot(slot)


def _moe_scatter(pos, pad, na, x_all, mod):
    tm = TM_ROUTE
    rf = _all_row_fn(tm)
    return pl.pallas_call(
        _moe_scatter_kernel,
        grid_spec=pltpu.PrefetchScalarGridSpec(
            num_scalar_prefetch=3,
            grid=(N_TOK // tm,),
            in_specs=[pl.BlockSpec((tm, D_MODEL), lambda i, *_: (i, 0)), _mod_spec(4, rf), _mod_spec(3, rf)],
            out_specs=pl.BlockSpec(memory_space=pl.ANY),
            scratch_shapes=[pltpu.VMEM((2, tm, D_MODEL), F32), pltpu.VMEM((TM_MOE, D_MODEL), F32),
                            pltpu.SemaphoreType.DMA((2,)), pltpu.SemaphoreType.DMA(())],
        ),
        out_shape=jax.ShapeDtypeStruct((ROWS_SORTED, D_MODEL), F32),
        compiler_params=_cparams(("arbitrary",)),
        name="moe_scatter",
    )(pos, pad, na, x_all, mod, mod)


def _tile_changed(te_ref, j):
    prev = te_ref[jnp.maximum(j - 1, 0)]
    return (j == 0) | (te_ref[j] != prev)


def _for_valid_rows(tv, tm, fn):
    for rows in range(MOE_ROW_STEP, tm + 1, MOE_ROW_STEP):
        pl.when((tv > rows - MOE_ROW_STEP) & (tv <= rows))(functools.partial(fn, rows))


def _moe_up_kernel(te_ref, na_ref, tv_ref, x_ref, wg_ref, wu_ref, o_ref, w_scr):
    j = pl.program_id(1)
    tm = x_ref.shape[0]
    tf = wg_ref.shape[1]

    @pl.when(j < na_ref[0])
    def _():
        @pl.when(_tile_changed(te_ref, j))
        def _():
            w_scr[:, :tf] = wg_ref[...].astype(BF16)
            w_scr[:, tf:] = wu_ref[...].astype(BF16)

        def compute(rows):
            gu = _dot(x_ref[:rows].astype(BF16), w_scr[...])
            o_ref[:rows] = (_silu(gu[:, :tf]) * gu[:, tf:]).astype(BF16)
            if rows < tm:
                o_ref[rows:] = jnp.zeros((tm - rows, tf), BF16)

        _for_valid_rows(tv_ref[j], tm, compute)

    @pl.when(j >= na_ref[0])
    def _():
        o_ref[...] = jnp.zeros_like(o_ref)


def _moe_up(te, na, tv, xs, w_in):
    tm, tf = TM_MOE, TF_MOE
    nf = FFN_EXPERT // tf
    row = lambda j, na: jnp.minimum(j, na[0] - 1)
    return pl.pallas_call(
        _moe_up_kernel,
        grid_spec=pltpu.PrefetchScalarGridSpec(
            num_scalar_prefetch=3,
            grid=(nf, NT_MOE),
            in_specs=[pl.BlockSpec((tm, D_MODEL), lambda f, j, te, na, tv: (row(j, na), 0)),
                      pl.BlockSpec((None, D_MODEL, tf), lambda f, j, te, na, tv: (te[j], 0, f)),
                      pl.BlockSpec((None, D_MODEL, tf), lambda f, j, te, na, tv: (te[j], 0, nf + f))],
            out_specs=pl.BlockSpec((tm, tf), lambda f, j, te, na, tv: (j, f)),
            scratch_shapes=[pltpu.VMEM((D_MODEL, 2 * tf), BF16)],
        ),
        out_shape=jax.ShapeDtypeStruct((ROWS_SORTED, FFN_EXPERT), BF16),
        compiler_params=_cparams(("arbitrary", "arbitrary")),
        name="moe_up",
    )(te, na, tv, xs, w_in, w_in)


def _moe_down_kernel(te_ref, na_ref, tv_ref, h_ref, w_ref, o_ref, w_scr):
    j = pl.program_id(0)
    tm, tn = o_ref.shape

    @pl.when(j < na_ref[0])
    def _():
        @pl.when(_tile_changed(te_ref, j))
        def _():
            w_scr[...] = w_ref[...].astype(BF16)

        def compute(rows):
            o_ref[:rows] = _dot(h_ref[:rows], w_scr[...])
            if rows < tm:
                o_ref[rows:] = jnp.zeros((tm - rows, tn), F32)

        _for_valid_rows(tv_ref[j], tm, compute)

    @pl.when(j >= na_ref[0])
    def _():
        o_ref[...] = jnp.zeros_like(o_ref)


def _moe_down(te, na, tv, hmid, w_out):
    tm = TM_MOE
    row = lambda j, na: jnp.minimum(j, na[0] - 1)
    return pl.pallas_call(
        _moe_down_kernel,
        grid_spec=pltpu.PrefetchScalarGridSpec(
            num_scalar_prefetch=3,
            grid=(NT_MOE,),
            in_specs=[pl.BlockSpec((tm, FFN_EXPERT), lambda j, te, na, tv: (row(j, na), 0)),
                      pl.BlockSpec((None, FFN_EXPERT, D_MODEL), lambda j, te, na, tv: (te[j], 0, 0))],
            out_specs=pl.BlockSpec((tm, D_MODEL), lambda j, te, na, tv: (j, 0)),
            scratch_shapes=[pltpu.VMEM((FFN_EXPERT, D_MODEL), BF16)],
        ),
        out_shape=jax.ShapeDtypeStruct((ROWS_SORTED, D_MODEL), F32),
        compiler_params=_cparams(("arbitrary",)),
        name="moe_down",
    )(te, na, tv, hmid, w_out)


def _moe_combine_kernel(pos_ref, y_ref, rt_ref, x_ref, g_ref, lg_ref, lb_ref, o_ref, ybuf, sems, *, tok_off):
    i = pl.program_id(0)
    tm = x_ref.shape[0]

    def fetch(tile, slot):
        def start(t, c):
            for k in range(TOP_K):
                p = pos_ref[k * N_TOK + tok_off + tile * tm + t]
                pltpu.make_async_copy(y_ref.at[pl.ds(p, 1)], ybuf.at[slot, k, pl.ds(t, 1)], sems.at[slot]).start()
            return c

        lax.fori_loop(0, tm, start, 0, unroll=8)

    @pl.when(i == 0)
    def _():
        fetch(0, 0)

    @pl.when(i + 1 < pl.num_programs(0))
    def _():
        fetch(i + 1, (i + 1) % 2)

    slot = i % 2
    pltpu.make_async_copy(ybuf.at[slot], ybuf.at[slot], sems.at[slot]).wait()
    rt = rt_ref[...]
    moe = rt[:, 2:3] * ybuf[slot, 0] + rt[:, 3:4] * ybuf[slot, 1]
    o_ref[...] = _layer_norm(ALPHA * x_ref[...] + g_ref[...] * moe, lg_ref[...], lb_ref[...])


def _moe_combine(pos, y, rt_all, x_all, mod, lg, lb, latent):
    n = N_LAT if latent else N_CTX
    tm = TM_ROUTE
    rf = _row_fn(latent, tm)
    tok_off = N_CTX if latent else 0
    off = tok_off // tm
    return pl.pallas_call(
        functools.partial(_moe_combine_kernel, tok_off=tok_off),
        grid_spec=pltpu.PrefetchScalarGridSpec(
            num_scalar_prefetch=1,
            grid=(n // tm,),
            in_specs=[pl.BlockSpec(memory_space=pl.ANY),
                      pl.BlockSpec((tm, LANES), lambda i, pos: (i + off, 0)),
                      pl.BlockSpec((tm, D_MODEL), lambda i, pos: (i + off, 0)),
                      _mod_spec(5, rf),
                      pl.BlockSpec((1, D_MODEL), lambda i, pos: (0, 0)),
                      pl.BlockSpec((1, D_MODEL), lambda i, pos: (0, 0))],
            out_specs=pl.BlockSpec((tm, D_MODEL), lambda i, pos: (i, 0)),
            scratch_shapes=[pltpu.VMEM((2, TOP_K, tm, D_MODEL), F32), pltpu.SemaphoreType.DMA((2,))],
        ),
        out_shape=jax.ShapeDtypeStruct((n, D_MODEL), F32),
        compiler_params=_cparams(("arbitrary",)),
        name="moe_combine_lat" if latent else "moe_combine_ctx",
    )(pos, y, rt_all, x_all, mod, lg, lb)


def _routing_positions(rt_all, counts):
    tm = TM_MOE
    cnt = counts[0, :N_EXPERTS].astype(jnp.int32)
    padded = ((cnt + tm - 1) // tm) * tm
    gend = jnp.cumsum(padded)
    gstart = gend - padded
    route = rt_all[:, :8].astype(jnp.int32)
    onehot = lambda idx: idx[:, None] == jnp.arange(N_EXPERTS, dtype=jnp.int32)[None, :]
    start_of = lambda idx: jnp.sum(jnp.where(onehot(idx), gstart[None, :], 0), axis=1)
    pos = jnp.concatenate([start_of(route[:, 0]) + route[:, 4], start_of(route[:, 1]) + route[:, 5]])
    pad = jnp.concatenate([gstart + cnt, padded - cnt]).astype(jnp.int32)
    tile_start = jnp.arange(NT_MOE, dtype=jnp.int32) * tm
    te = jnp.sum((tile_start[:, None] >= gend[None, :]).astype(jnp.int32), axis=1)
    n_active = (gend[-1] // tm).astype(jnp.int32)
    last_e = jnp.take(te, jnp.maximum(n_active - 1, 0))
    te = jnp.where(tile_start < gend[-1], te, last_e).astype(jnp.int32)
    tile_valid = jnp.clip(jnp.take(gstart + cnt, te) - tile_start, 0, tm)
    tile_valid = jnp.where(tile_start < gend[-1], tile_valid, 0).astype(jnp.int32)
    return pos, pad, te, n_active.reshape(1), tile_valid


def kernel(x_prompt, x_sample, cache_ckv_l0, cache_krope_l0, cache_k_l1, cache_v_l1, c, c_ctx, w_ada_l0, b_ada_l0, mla_w_dq, mla_g_q, mla_w_uq, mla_w_dkv, mla_g_kv, mla_w_ukv, mla_w_o, ln1_g_l0, ln1_b_l0, ffn_w_in, ffn_w_out, ln2_g_l0, ln2_b_l0, w_ada_l1, b_ada_l1, na_w_qkv, na_rpb, na_w_o, ln1_g_l1, ln1_b_l1, moe_w_router, moe_w_in, moe_w_out, ln2_g_l1, ln2_b_l1):
    row = lambda v: v.reshape(1, -1)
    xp = x_prompt.reshape(N_CTX, D_MODEL)
    xs = x_sample.reshape(N_LAT, D_MODEL)
    groups = ((xp, False), (xs, True))

    cvecs = jnp.concatenate([c_ctx[None], c, jnp.zeros((8 - 1 - DEC_BATCH, D_MODEL), F32)], axis=0)
    mod0 = _ada_mod(cvecs, w_ada_l0, b_ada_l0)
    mod1 = _ada_mod(cvecs, w_ada_l1, b_ada_l1)

    wa = jnp.concatenate([mla_w_dq, mla_w_dkv], axis=1).astype(BF16)
    wuq = mla_w_uq.reshape(MLA_Q_LORA, MLA_HEADS, MLA_NOPE + MLA_ROPE)
    wuq = jnp.pad(wuq, ((0, 0), (0, 0), (0, Q_HEAD_PAD - MLA_NOPE - MLA_ROPE)))
    wuq = wuq.reshape(MLA_Q_LORA, MLA_HEADS * Q_HEAD_PAD).astype(BF16)
    wukv = mla_w_ukv.astype(BF16)
    wo0 = mla_w_o.astype(BF16)
    w_in0 = ffn_w_in.astype(BF16)
    w_out0 = ffn_w_out.astype(BF16)
    wqkv = na_w_qkv.astype(BF16)
    wo1 = na_w_o.astype(BF16)
    wr = jnp.pad(moe_w_router, ((0, 0), (0, LANES - N_EXPERTS))).astype(BF16)
    bias_tab = _na_bias_table(na_rpb)
    tab = _rope_tables()

    kvx_cache = _matmul(cache_ckv_l0.reshape(DEC_BATCH * PAST_LEN, MLA_KV_LORA), wukv, BF16, 512, "mla_expand_cache")
    kvx_cache = kvx_cache.reshape(DEC_BATCH, PAST_LEN, -1)
    krp_cache = jnp.pad(cache_krope_l0, ((0, 0), (0, 0), (0, LANES - MLA_ROPE))).astype(BF16)
    x1 = []
    new_ckv = new_kr = None
    vt_cache = kvx_cache.reshape(DEC_BATCH, PAST_LEN, MLA_HEADS, MLA_NOPE + MLA_V)[..., MLA_NOPE:]
    vt_cache = vt_cache.reshape(DEC_BATCH, PAST_LEN, MLA_HEADS * MLA_V).transpose(0, 2, 1)
    for x, latent in groups:
        proj = _mla_proj(x, mod0, wa, row(mla_g_q), row(mla_g_kv), wuq, wukv, tab, latent)
        if latent:
            qt, kvx, krp, vt = proj
            o = _mla_attn_lat(qt, kvx, krp, kvx_cache, krp_cache, vt, vt_cache)
        else:
            q, kvx, new_ckv, new_kr, krp = proj
            o = _mla_attn_ctx(q, kvx, krp)
        x1.append(_attn_out_ffn(o, x, wo0, mod0, row(ln1_g_l0), row(ln1_b_l0), w_in0, w_out0,
                                row(ln2_g_l0), row(ln2_b_l0), latent))

    kct = cache_k_l1.reshape(DEC_BATCH, PAST_LEN, D_MODEL).transpose(0, 2, 1).astype(BF16)
    vc = cache_v_l1.reshape(DEC_BATCH, PAST_LEN, D_MODEL).astype(BF16)
    q, new_k, new_v, kt = _na_qkv(x1[0], mod1, wqkv, F32, False)
    o_ctx = _na_attn_ctx(q, kt, new_v)
    q, _, v, kt = _na_qkv(x1[1], mod1, wqkv, BF16, True)
    o_lat = _na_attn_lat(q, kt, v, kct, vc, bias_tab)
    x2_all, rt_all, counts = _proj_ln_route(o_ctx, x1[0], o_lat, x1[1], wo1, mod1, row(ln1_g_l1), row(ln1_b_l1), wr)

    pos, pad, te, n_active, tile_valid = _routing_positions(rt_all, counts)
    x_sorted = _moe_scatter(pos, pad, n_active, x2_all, mod1)
    hmid = _moe_up(te, n_active, tile_valid, x_sorted, moe_w_in)
    y = _moe_down(te, n_active, tile_valid, hmid, moe_w_out)
    outs = [_moe_combine(pos, y, rt_all, x2_all, mod1, row(ln2_g_l1), row(ln2_b_l1), latent)
            for latent in (False, True)]

    return (outs[0].reshape(BATCH, SEQ, D_MODEL),
            outs[1].reshape(DEC_BATCH, DEC_SEQ, D_MODEL),
            new_ckv.reshape(BATCH, SEQ, MLA_KV_LORA),
            new_kr.reshape(BATCH, SEQ, MLA_ROPE),
            new_k.reshape(BATCH, SEQ, NA_HEADS, NA_HD),
            new_v.reshape(BATCH, SEQ, NA_HEADS, NA_HD))
```

```python
import functools
import math

import numpy as np
import jax
import jax.numpy as jnp
from jax import lax
from jax.experimental import pallas as pl
from jax.experimental.pallas import tpu as pltpu

F32 = jnp.float32
BF16 = jnp.bfloat16

D_MODEL = 1024
BATCH = 32
SEQ = 256
DEPTH = 2
DEC_BATCH = 2
DEC_SEQ = 2048
PAST_LEN = 512
GRID_W = 64
MLA_HEADS = 8
MLA_NOPE = 128
MLA_ROPE = 64
MLA_V = 128
MLA_Q_LORA = 512
MLA_KV_LORA = 256
MLA_SCALE = 1.0 / math.sqrt(MLA_NOPE + MLA_ROPE)
ROPE_THETA = 10000.0
NA_HEADS = 16
NA_HD = D_MODEL // NA_HEADS
NA_WIN_ROWS = 8
NA_WIN_COLS = 16
NA_SCALE = 1.0 / math.sqrt(NA_HD)
FFN_DENSE = 2816
N_EXPERTS = 8
TOP_K = 2
FFN_EXPERT = 3584
ALPHA = (2 * DEPTH) ** 0.25
LN_EPS = 1e-5
RMS_EPS = 1e-6

N_CTX = BATCH * SEQ
N_LAT = DEC_BATCH * DEC_SEQ
N_TOK = N_CTX + N_LAT
GRID_ROWS = DEC_SEQ // GRID_W
Q_HEAD_PAD = 256
LANES = 128
SUBLANES = 8
NEG_BIG = -1e30
MLA_KEY_CHUNK = 256
NA_PAIR = 2 * NA_HD
NA_BLK_ROWS = 4
NA_KT_W = 2 * GRID_W
NA_WIN_PAIRS = (NA_BLK_ROWS + NA_WIN_ROWS) // 2
NA_N_BLOCKS = DEC_SEQ // (NA_BLK_ROWS * GRID_W)
NA_BLOCK_VARIANTS = (0, 1, NA_N_BLOCKS - 1)

VMEM_LIMIT = 56 * 1024 * 1024

TM_PROJ = 512
TM_FFN = 512
MXU_TILE = 256
FFN_CHUNKS = ((0, 6 * MXU_TILE), (6 * MXU_TILE, FFN_DENSE))
TM_MOE = 512
TF_MOE = 1792
N_PAIRS = N_TOK * TOP_K
MOE_LOCAL_ROWS = TOP_K * TM_PROJ + N_EXPERTS * SUBLANES
N_SEG_PAD = N_EXPERTS * (N_TOK // TM_PROJ) * (SUBLANES - 1)
NT_MOE = -(-(N_PAIRS + N_SEG_PAD + N_EXPERTS * (TM_MOE - 1)) // TM_MOE)
ROWS_SORTED = NT_MOE * TM_MOE
TM_ROUTE = 256


def _cparams(sem, vmem=VMEM_LIMIT):
    return pltpu.CompilerParams(dimension_semantics=sem, vmem_limit_bytes=vmem)


def _silu(x):
    return x * jax.nn.sigmoid(x)


def _layer_norm(y, g, b):
    mu = jnp.mean(y, axis=-1, keepdims=True)
    d = y - mu
    var = jnp.mean(d * d, axis=-1, keepdims=True)
    return d * lax.rsqrt(var + LN_EPS) * g + b


def _rms_norm(y, g):
    return y * lax.rsqrt(jnp.mean(y * y, axis=-1, keepdims=True) + RMS_EPS) * g


def _dot(a, b):
    return jnp.dot(a, b, preferred_element_type=F32)


def _dot_nt(a, b):
    return lax.dot_general(a, b, (((1,), (1,)), ((), ())), preferred_element_type=F32)


def _ada_kernel(c_ref, w_ref, b_ref, o_ref):
    s = _silu(c_ref[...]).astype(BF16)
    o_ref[...] = _dot(s, w_ref[...].astype(BF16)) + b_ref[...]


def _ada_mod(cvecs, w, b):
    tn = 1536
    m = pl.pallas_call(
        _ada_kernel,
        grid=(6 * D_MODEL // tn,),
        in_specs=[pl.BlockSpec((8, D_MODEL), lambda j: (0, 0)),
                  pl.BlockSpec((D_MODEL, tn), lambda j: (0, j)),
                  pl.BlockSpec((1, tn), lambda j: (0, j))],
        out_specs=pl.BlockSpec((8, tn), lambda j: (0, j)),
        out_shape=jax.ShapeDtypeStruct((8, 6 * D_MODEL), F32),
        compiler_params=_cparams(("arbitrary",)),
        name="ada_mod",
    )(cvecs, w, b.reshape(1, -1))
    return m[:3].reshape(3 * 6, 1, D_MODEL)


def _mod_spec(j, row_fn):
    return pl.BlockSpec((None, 1, D_MODEL), lambda i, *_: (row_fn(i) * 6 + j, 0, 0))


def _row_fn(latent, tm):
    if not latent:
        return lambda i: 0
    per = DEC_SEQ // tm
    return lambda i: 1 + i // per


def _const_spec(shape):
    nd = len(shape)
    return pl.BlockSpec(shape, lambda *_: (0,) * nd, pipeline_mode=pl.Buffered(1))


def _rope_kernel(invf_ref, o_ref):
    i = pl.program_id(0)
    tm = o_ref.shape[1]
    t = i * tm + lax.broadcasted_iota(jnp.int32, (tm, LANES), 0)
    lane = lax.broadcasted_iota(jnp.int32, (tm, LANES), 1)
    row = t >> int(math.log2(GRID_W))
    col = t & (GRID_W - 1)
    pos = jnp.where(lane < MLA_ROPE // 2, row, col).astype(F32)
    ang = pos * invf_ref[...]
    cos = jnp.cos(ang)
    sin = jnp.sin(ang)
    unit = lane >> int(math.log2(MLA_ROPE // 4))
    first = (unit == 0) | (unit == 2)
    second = (unit == 1) | (unit == 3)
    o_ref[0] = jnp.where(lane < MLA_ROPE, cos, 0.0)
    o_ref[1] = jnp.where(first, -sin, 0.0)
    o_ref[2] = jnp.where(second, sin, 0.0)


def _rope_tables():
    half = MLA_ROPE // 2
    inv_freq = (1.0 / (ROPE_THETA ** (np.arange(0, half, 2, dtype=np.float32) / half))).astype(np.float32)
    lane_f = np.zeros((1, LANES), np.float32)
    lane_f[0, :MLA_ROPE] = np.tile(inv_freq, 4)
    tm = 256
    return pl.pallas_call(
        _rope_kernel,
        grid=(DEC_SEQ // tm,),
        in_specs=[_const_spec((1, LANES))],
        out_specs=pl.BlockSpec((3, tm, LANES), lambda i: (0, i, 0)),
        out_shape=jax.ShapeDtypeStruct((3, DEC_SEQ, LANES), F32),
        compiler_params=_cparams(("arbitrary",)),
        name="rope_tables",
    )(jnp.asarray(lane_f))


def _rotate(v, tab_ref):
    return (v * tab_ref[0] + pltpu.roll(v, LANES - MLA_ROPE // 4, 1) * tab_ref[1]
            + pltpu.roll(v, MLA_ROPE // 4, 1) * tab_ref[2])


def _mla_proj_kernel(*refs, rope):
    if rope:
        (x_ref, sc_ref, sh_ref, wa_ref, gq_ref, gkv_ref, wuq_ref, wukv_ref, tab_ref,
         qt_ref, kvx_ref, krp_ref, vt_ref) = refs
    else:
        (x_ref, sc_ref, sh_ref, wa_ref, gq_ref, gkv_ref, wuq_ref, wukv_ref,
         q_ref, kvx_ref, ckv_ref, kr_ref, krp_ref) = refs
    h = (x_ref[...] * (1.0 + sc_ref[...]) + sh_ref[...]).astype(BF16)
    t = _dot(h, wa_ref[...])
    cq = _rms_norm(t[:, :MLA_Q_LORA], gq_ref[...])
    ckv = _rms_norm(t[:, MLA_Q_LORA:MLA_Q_LORA + MLA_KV_LORA], gkv_ref[...])
    kr = t[:, MLA_Q_LORA + MLA_KV_LORA:]
    kvx = _dot(ckv.astype(BF16), wukv_ref[...])
    kvx_ref[...] = kvx.astype(BF16)
    q = _dot(cq.astype(BF16), wuq_ref[...])
    krp = jnp.concatenate([kr, jnp.zeros_like(kr)], axis=-1)
    if rope:
        krp_ref[...] = _rotate(krp, tab_ref).astype(BF16)
        parts = []
        for hd in range(MLA_HEADS):
            lo = hd * Q_HEAD_PAD
            parts += [q[:, lo:lo + MLA_NOPE], _rotate(q[:, lo + MLA_NOPE:lo + Q_HEAD_PAD], tab_ref)]
        qt_ref[...] = jnp.concatenate(parts, axis=1).T.astype(BF16)
        vcols = [kvx[:, hd * Q_HEAD_PAD + MLA_NOPE:(hd + 1) * Q_HEAD_PAD] for hd in range(MLA_HEADS)]
        vt_ref[...] = jnp.concatenate(vcols, axis=1).T.astype(BF16)
    else:
        ckv_ref[...] = ckv
        kr_ref[...] = kr
        krp_ref[...] = krp.astype(BF16)
        q_ref[...] = q.astype(BF16)


def _mla_proj(x, mod, wa, gq, gkv, wuq, wukv, tab, latent):
    n = x.shape[0]
    tm = TM_PROJ
    rf = _row_fn(latent, tm)
    tok = lambda w: pl.BlockSpec((tm, w), lambda i: (i, 0))
    in_specs = [tok(D_MODEL), _mod_spec(1, rf), _mod_spec(0, rf),
                _const_spec(wa.shape), _const_spec(gq.shape), _const_spec(gkv.shape),
                _const_spec(wuq.shape), _const_spec(wukv.shape)]
    args = [x, mod, mod, wa, gq, gkv, wuq, wukv]
    wq, wkv = MLA_HEADS * Q_HEAD_PAD, MLA_HEADS * (MLA_NOPE + MLA_V)
    if latent:
        per = DEC_SEQ // tm
        in_specs.append(pl.BlockSpec((3, tm, LANES), lambda i: (0, i % per, 0)))
        args.append(tab)
        out_specs = [pl.BlockSpec((wq, tm), lambda i: (0, i)), tok(wkv), tok(LANES),
                     pl.BlockSpec((None, MLA_HEADS * MLA_V, tm), lambda i: (i // per, 0, i % per))]
        out_shape = [jax.ShapeDtypeStruct((wq, n), BF16), jax.ShapeDtypeStruct((n, wkv), BF16),
                     jax.ShapeDtypeStruct((n, LANES), BF16),
                     jax.ShapeDtypeStruct((DEC_BATCH, MLA_HEADS * MLA_V, DEC_SEQ), BF16)]
    else:
        out_specs = [tok(wq), tok(wkv), tok(MLA_KV_LORA), tok(MLA_ROPE), tok(LANES)]
        out_shape = [jax.ShapeDtypeStruct((n, wq), BF16), jax.ShapeDtypeStruct((n, wkv), BF16),
                     jax.ShapeDtypeStruct((n, MLA_KV_LORA), F32), jax.ShapeDtypeStruct((n, MLA_ROPE), F32),
                     jax.ShapeDtypeStruct((n, LANES), BF16)]
    return pl.pallas_call(
        functools.partial(_mla_proj_kernel, rope=latent),
        grid=(n // tm,),
        in_specs=in_specs,
        out_specs=out_specs,
        out_shape=out_shape,
        compiler_params=_cparams(("arbitrary",)),
        name="mla_proj_lat" if latent else "mla_proj_ctx",
    )(*args)


def _matmul_kernel(a_ref, b_ref, o_ref):
    o_ref[...] = _dot(a_ref[...].astype(BF16), b_ref[...]).astype(o_ref.dtype)


def _matmul(a, b, out_dtype, tm, name):
    m, k = a.shape
    n = b.shape[1]
    return pl.pallas_call(
        _matmul_kernel,
        grid=(m // tm,),
        in_specs=[pl.BlockSpec((tm, k), lambda i: (i, 0)), _const_spec(b.shape)],
        out_specs=pl.BlockSpec((tm, n), lambda i: (i, 0)),
        out_shape=jax.ShapeDtypeStruct((m, n), out_dtype),
        compiler_params=_cparams(("arbitrary",)),
        name=name,
    )(a, b)


def _mla_head(qh, kparts, vparts):
    ss = [_dot_nt(qh, k) for k in kparts]
    m = functools.reduce(jnp.maximum, [jnp.max(s, axis=-1, keepdims=True) for s in ss])
    ps = [jnp.exp2((s - m) * (MLA_SCALE * math.log2(math.e))) for s in ss]
    l = functools.reduce(jnp.add, [jnp.sum(p, axis=-1, keepdims=True) for p in ps])
    o = functools.reduce(jnp.add, [_dot(p.astype(BF16), v) for p, v in zip(ps, vparts)])
    return o / l


def _mla_attn_ctx_kernel(q_ref, kvx_ref, krp_ref, o_ref):
    krp = krp_ref[...]
    for hd in range(MLA_HEADS):
        lo = hd * (MLA_NOPE + MLA_V)
        qh = q_ref[:, hd * Q_HEAD_PAD:(hd + 1) * Q_HEAD_PAD]
        kh = jnp.concatenate([kvx_ref[:, lo:lo + MLA_NOPE], krp], axis=-1)
        vh = kvx_ref[:, lo + MLA_NOPE:lo + MLA_NOPE + MLA_V]
        o_ref[:, hd * MLA_V:(hd + 1) * MLA_V] = _mla_head(qh, [kh], [vh]).astype(BF16)


def _mla_attn_ctx(q, kvx, krp):
    tok = lambda w: pl.BlockSpec((SEQ, w), lambda b: (b, 0))
    return pl.pallas_call(
        _mla_attn_ctx_kernel,
        grid=(BATCH,),
        in_specs=[tok(q.shape[1]), tok(kvx.shape[1]), tok(LANES)],
        out_specs=tok(MLA_HEADS * MLA_V),
        out_shape=jax.ShapeDtypeStruct((N_CTX, MLA_HEADS * MLA_V), BF16),
        compiler_params=_cparams(("arbitrary",)),
        name="mla_attn_ctx",
    )(q, kvx, krp)


def _mla_attn_lat_kernel(qt_ref, kvl_ref, krl_ref, kvc_ref, krc_ref, vtl_ref, vtc_ref, o_ref):
    chunks = [(kvl_ref, krl_ref, vtl_ref, c) for c in range(DEC_SEQ // MLA_KEY_CHUNK)]
    chunks += [(kvc_ref, krc_ref, vtc_ref, c) for c in range(PAST_LEN // MLA_KEY_CHUNK)]
    for hd in range(MLA_HEADS):
        lo = hd * (MLA_NOPE + MLA_V)
        qt = qt_ref[hd * Q_HEAD_PAD:(hd + 1) * Q_HEAD_PAD, :]
        ss = []
        for kv_ref, kr_ref, _, c in chunks:
            rows = slice(c * MLA_KEY_CHUNK, (c + 1) * MLA_KEY_CHUNK)
            ss.append(_dot(jnp.concatenate([kv_ref[rows, lo:lo + MLA_NOPE], kr_ref[rows, :]], axis=-1), qt))
        m = functools.reduce(jnp.maximum, [jnp.max(s, axis=0, keepdims=True) for s in ss])
        ps = [jnp.exp2((s - m) * (MLA_SCALE * math.log2(math.e))) for s in ss]
        l = functools.reduce(jnp.add, [jnp.sum(p, axis=0, keepdims=True) for p in ps])
        ots = [_dot(vt_ref[hd * MLA_V:(hd + 1) * MLA_V, c * MLA_KEY_CHUNK:(c + 1) * MLA_KEY_CHUNK], p.astype(BF16))
               for (_, _, vt_ref, c), p in zip(chunks, ps)]
        ot = functools.reduce(jnp.add, ots) / l
        o_ref[:, hd * MLA_V:(hd + 1) * MLA_V] = ot.T.astype(BF16)


def _mla_attn_lat(qt, kvx_lat, krp_lat, kvx_ctx, krp_ctx, vt_lat, vt_ctx):
    tq = 256
    per = DEC_SEQ // tq
    wkv = kvx_lat.shape[-1]
    wv = MLA_HEADS * MLA_V
    batch = lambda rows, cols: pl.BlockSpec((None, rows, cols), lambda b, i: (b, 0, 0))
    return pl.pallas_call(
        _mla_attn_lat_kernel,
        grid=(DEC_BATCH, per),
        in_specs=[pl.BlockSpec((qt.shape[0], tq), lambda b, i: (0, b * per + i)),
                  batch(DEC_SEQ, wkv), batch(DEC_SEQ, LANES), batch(PAST_LEN, wkv), batch(PAST_LEN, LANES),
                  batch(wv, DEC_SEQ), batch(wv, PAST_LEN)],
        out_specs=pl.BlockSpec((tq, wv), lambda b, i: (b * per + i, 0)),
        out_shape=jax.ShapeDtypeStruct((N_LAT, wv), BF16),
        compiler_params=_cparams(("arbitrary", "arbitrary")),
        name="mla_attn_lat",
    )(qt, kvx_lat.reshape(DEC_BATCH, DEC_SEQ, wkv), krp_lat.reshape(DEC_BATCH, DEC_SEQ, LANES), kvx_ctx, krp_ctx,
      vt_lat, vt_ctx)


def _top2(logits):
    lane = lax.broadcasted_iota(jnp.int32, logits.shape, 1).astype(F32)
    m1 = jnp.max(logits, axis=-1, keepdims=True)
    i1 = jnp.min(jnp.where(logits == m1, lane, float(LANES)), axis=-1, keepdims=True)
    rest = jnp.where(lane == i1, -jnp.inf, logits)
    m2 = jnp.max(rest, axis=-1, keepdims=True)
    i2 = jnp.min(jnp.where(rest == m2, lane, float(LANES)), axis=-1, keepdims=True)
    e = jnp.exp(m2 - m1)
    w1 = 1.0 / (1.0 + e)
    w2 = e / (1.0 + e)
    return lane, i1, i2, w1, w2


def _proj_ln_route_kernel(oc_ref, xc_ref, ol_ref, xl_ref, wo_ref, g_ref, lg_ref, lb_ref, sc_ref, sh_ref, wr_ref,
                          x1_ref, rt_ref, cnt_ref, meta_ref, carry, *, n_ctx_tiles):
    i = pl.program_id(0)

    @pl.when(i == 0)
    def _():
        carry[...] = jnp.zeros_like(carry)

    def body(o_ref, x_ref):
        out = _dot(o_ref[...], wo_ref[...])
        x1 = _layer_norm(ALPHA * x_ref[...] + g_ref[...] * out, lg_ref[...], lb_ref[...])
        x1_ref[...] = x1
        hm = x1 * (1.0 + sc_ref[...]) + sh_ref[...]
        logits = _dot(hm.astype(BF16), wr_ref[...])
        lane = lax.broadcasted_iota(jnp.int32, logits.shape, 1)
        logits = jnp.where(lane < N_EXPERTS, logits, -jnp.inf)
        lane_f, i1, i2, w1, w2 = _top2(logits)
        tm = logits.shape[0]
        oh1 = (lane_f == i1).astype(F32)
        oh2 = (lane_f == i2).astype(F32)
        rr = lax.broadcasted_iota(jnp.int32, (tm, tm), 0)
        cc = lax.broadcasted_iota(jnp.int32, (tm, tm), 1)
        below = (cc < rr).astype(BF16)
        tot1 = jnp.sum(oh1, axis=0, keepdims=True)
        tot2 = jnp.sum(oh2, axis=0, keepdims=True)
        seg_len = jnp.floor((tot1 + tot2 + (SUBLANES - 1)) * (1.0 / SUBLANES)) * SUBLANES
        e_row = lax.broadcasted_iota(jnp.int32, (LANES, LANES), 0)
        e_col = lax.broadcasted_iota(jnp.int32, (LANES, LANES), 1)
        before = (e_row < e_col).astype(BF16)
        seg_loc = _dot(jnp.broadcast_to(seg_len, (SUBLANES, LANES)).astype(BF16), before)[0:1]
        seg_glob = carry[...]
        cum1 = _dot(below, oh1.astype(BF16))
        cum2 = _dot(below, oh2.astype(BF16)) + tot1
        pick = lambda oh, v: jnp.sum(oh * v, axis=-1, keepdims=True)
        vals = (i1, i2, w1, w2, pick(oh1, cum1 + seg_glob), pick(oh2, cum2 + seg_glob),
                pick(oh1, cum1 + seg_loc), pick(oh2, cum2 + seg_loc))
        rt = jnp.zeros_like(logits)
        for k, val in enumerate(vals):
            rt = jnp.where(lane == k, val, rt)
        rt_ref[...] = rt
        sub = lax.broadcasted_iota(jnp.int32, (SUBLANES, LANES), 0)
        meta_ref[...] = jnp.where(sub == 0, seg_len, jnp.where(sub == 1, seg_loc, jnp.where(sub == 2, seg_glob, 0.0)))
        carry[...] = seg_glob + seg_len
        cnt_ref[...] = jnp.broadcast_to(carry[...], cnt_ref.shape)

    is_ctx = i < n_ctx_tiles
    pl.when(is_ctx)(lambda: body(oc_ref, xc_ref))
    pl.when(jnp.logical_not(is_ctx))(lambda: body(ol_ref, xl_ref))


def _all_row_fn(tm):
    nc = N_CTX // tm
    per = DEC_SEQ // tm
    return lambda i: jnp.where(i < nc, 0, 1 + (i - nc) // per)


def _proj_ln_route(o_ctx, x_ctx, o_lat, x_lat, wo, mod, lg, lb, wr):
    tm = TM_PROJ
    nc = N_CTX // tm
    rf = _all_row_fn(tm)
    ctx = lambda w: pl.BlockSpec((tm, w), lambda i: (jnp.minimum(i, nc - 1), 0))
    lat = lambda w: pl.BlockSpec((tm, w), lambda i: (jnp.maximum(i - nc, 0), 0))
    tok = lambda w: pl.BlockSpec((tm, w), lambda i: (i, 0))
    return pl.pallas_call(
        functools.partial(_proj_ln_route_kernel, n_ctx_tiles=nc),
        grid=(N_TOK // tm,),
        in_specs=[ctx(D_MODEL), ctx(D_MODEL), lat(D_MODEL), lat(D_MODEL), _const_spec(wo.shape), _mod_spec(2, rf),
                  _const_spec(lg.shape), _const_spec(lb.shape), _mod_spec(4, rf), _mod_spec(3, rf),
                  _const_spec(wr.shape)],
        out_specs=[tok(D_MODEL), tok(LANES), pl.BlockSpec((SUBLANES, LANES), lambda i: (0, 0)),
                   pl.BlockSpec((None, SUBLANES, LANES), lambda i: (i, 0, 0))],
        out_shape=[jax.ShapeDtypeStruct((N_TOK, D_MODEL), F32),
                   jax.ShapeDtypeStruct((N_TOK, LANES), F32),
                   jax.ShapeDtypeStruct((SUBLANES, LANES), F32),
                   jax.ShapeDtypeStruct((N_TOK // tm, SUBLANES, LANES), F32)],
        scratch_shapes=[pltpu.VMEM((1, LANES), F32)],
        compiler_params=_cparams(("arbitrary",)),
        name="proj_ln_route",
    )(o_ctx, x_ctx, o_lat, x_lat, wo, mod, lg, lb, mod, mod, wr)


def _attn_out_ffn_kernel(o_ref, x_ref, wo_ref, g1_ref, lg1_ref, lb1_ref, sc_ref, sh_ref, g2_ref,
                         wi_ref, wd_ref, lg2_ref, lb2_ref, y_ref):
    x1 = _layer_norm(ALPHA * x_ref[...] + g1_ref[...] * _dot(o_ref[...], wo_ref[...]), lg1_ref[...], lb1_ref[...])
    h = (x1 * (1.0 + sc_ref[...]) + sh_ref[...]).astype(BF16)
    acc = None
    for lo, hi in FFN_CHUNKS:
        gate = _dot(h, wi_ref[:, lo:hi])
        up = _dot(h, wi_ref[:, FFN_DENSE + lo:FFN_DENSE + hi])
        part = _dot((_silu(gate) * up).astype(BF16), wd_ref[lo:hi, :])
        acc = part if acc is None else acc + part
    y_ref[...] = _layer_norm(ALPHA * x1 + g2_ref[...] * acc, lg2_ref[...], lb2_ref[...])


def _attn_out_ffn(o, x, wo, mod, lg1, lb1, w_in, w_out, lg2, lb2, latent):
    n = x.shape[0]
    tm = TM_FFN
    rf = _row_fn(latent, tm)
    tok = lambda w: pl.BlockSpec((tm, w), lambda i: (i, 0))
    vec = _const_spec((1, D_MODEL))
    return pl.pallas_call(
        _attn_out_ffn_kernel,
        grid=(n // tm,),
        in_specs=[tok(o.shape[1]), tok(D_MODEL), _const_spec(wo.shape), _mod_spec(2, rf), vec, vec,
                  _mod_spec(4, rf), _mod_spec(3, rf), _mod_spec(5, rf),
                  _const_spec(w_in.shape), _const_spec(w_out.shape), vec, vec],
        out_specs=tok(D_MODEL),
        out_shape=jax.ShapeDtypeStruct((n, D_MODEL), F32),
        compiler_params=_cparams(("arbitrary",)),
        name="attn_out_ffn_lat" if latent else "attn_out_ffn_ctx",
    )(o, x, wo, mod, lg1, lb1, mod, mod, mod, w_in, w_out, lg2, lb2)


def _na_qkv_kernel(x_ref, sc_ref, sh_ref, w_ref, q_ref, k_ref, v_ref, kt_ref):
    h = (x_ref[...] * (1.0 + sc_ref[...]) + sh_ref[...]).astype(BF16)
    qkv = _dot(h, w_ref[...])
    q_ref[...] = (qkv[:, :D_MODEL] * NA_SCALE).astype(BF16)
    k = qkv[:, D_MODEL:2 * D_MODEL]
    k_ref[...] = k.astype(k_ref.dtype)
    v_ref[...] = qkv[:, 2 * D_MODEL:].astype(v_ref.dtype)
    kt = k.T
    nblk, _, w = kt_ref.shape
    for t in range(nblk):
        kt_ref[t] = kt[:, t * w:(t + 1) * w].astype(BF16)


def _na_qkv(x, mod, w, kv_dtype, latent):
    n = x.shape[0]
    tm = TM_PROJ
    kt_w = NA_KT_W if latent else SEQ
    rf = _row_fn(latent, tm)
    tok = pl.BlockSpec((tm, D_MODEL), lambda i: (i, 0))
    return pl.pallas_call(
        _na_qkv_kernel,
        grid=(n // tm,),
        in_specs=[tok, _mod_spec(1, rf), _mod_spec(0, rf), _const_spec(w.shape)],
        out_specs=[tok, tok, tok, pl.BlockSpec((tm // kt_w, D_MODEL, kt_w), lambda i: (i, 0, 0))],
        out_shape=[jax.ShapeDtypeStruct((n, D_MODEL), BF16),
                   jax.ShapeDtypeStruct((n, D_MODEL), kv_dtype),
                   jax.ShapeDtypeStruct((n, D_MODEL), kv_dtype),
                   jax.ShapeDtypeStruct((n // kt_w, D_MODEL, kt_w), BF16)],
        compiler_params=_cparams(("arbitrary",)),
        name="na_qkv_lat" if latent else "na_qkv_ctx",
    )(x, mod, mod, w)


def _softmax_pv(scores, values):
    m = functools.reduce(jnp.maximum, [jnp.max(s, axis=-1, keepdims=True) for s in scores])
    ps = [jnp.exp(s - m) for s in scores]
    l = functools.reduce(jnp.add, [jnp.sum(p, axis=-1, keepdims=True) for p in ps])
    o = functools.reduce(jnp.add, [_dot(p.astype(BF16), v) for p, v in zip(ps, values)])
    return o / l


def _head_of_pair(x, half):
    lane = lax.broadcasted_iota(jnp.int32, x.shape, 1)
    keep = (lane < NA_HD) if half == 0 else (lane >= NA_HD)
    return jnp.where(keep, x, jnp.zeros_like(x))


def _merge_pair(o0, o1):
    lane = lax.broadcasted_iota(jnp.int32, o0.shape, 1)
    return jnp.where(lane < NA_HD, o0, o1)


def _na_attn_ctx_kernel(q_ref, kt_ref, v_ref, o_ref):
    for p in range(NA_HEADS // 2):
        cols = slice(p * NA_PAIR, (p + 1) * NA_PAIR)
        qp = q_ref[:, cols]
        ktp = kt_ref[cols, :]
        vp = v_ref[:, cols].astype(BF16)
        outs = [_softmax_pv([_dot(_head_of_pair(qp, half), ktp)], [vp]) for half in range(2)]
        o_ref[:, cols] = _merge_pair(*outs).astype(BF16)


def _na_attn_ctx(q, kt, v):
    tok = pl.BlockSpec((SEQ, D_MODEL), lambda b: (b, 0))
    return pl.pallas_call(
        _na_attn_ctx_kernel,
        grid=(BATCH,),
        in_specs=[tok, pl.BlockSpec((None, D_MODEL, SEQ), lambda b: (b, 0, 0)), tok],
        out_specs=tok,
        out_shape=jax.ShapeDtypeStruct((N_CTX, D_MODEL), BF16),
        compiler_params=_cparams(("arbitrary",)),
        name="na_attn_ctx",
    )(q, kt, v)


def _na_win_start(m, clip=jnp.clip):
    return clip(m * NA_BLK_ROWS // 2 - NA_WIN_ROWS // 4, 0, GRID_ROWS // 2 - NA_WIN_PAIRS)


def _na_variant(m):
    return jnp.where(m == 0, 0, jnp.where(m == NA_N_BLOCKS - 1, 2, 1))


def _na_attn_lat_kernel(q_ref, kt_ref, v_ref, kct_ref, vc_ref, bias_ref, o_ref):
    rp0 = _na_win_start(pl.program_id(1))
    win = pl.ds(pl.multiple_of(rp0 * NA_KT_W, NA_KT_W), NA_WIN_PAIRS * NA_KT_W)
    for p in range(NA_HEADS // 2):
        cols = slice(p * NA_PAIR, (p + 1) * NA_PAIR)
        qp = q_ref[:, cols]
        kw = jnp.concatenate([kt_ref[rp0 + t, cols, :] for t in range(NA_WIN_PAIRS)], axis=1)
        kc = kct_ref[cols, :]
        vw = v_ref[win, cols]
        vc = vc_ref[:, cols]
        outs = []
        for half in range(2):
            qh = _head_of_pair(qp, half)
            s_nb = _dot(qh, kw) + bias_ref[2 * p + half]
            outs.append(_softmax_pv([s_nb, _dot(qh, kc)], [vw, vc]))
        o_ref[:, cols] = _merge_pair(*outs).astype(BF16)


def _na_bias_kernel(e_ref, e64_ref, o_ref):
    n_dr = 2 * NA_WIN_ROWS - 1
    shape = (GRID_W, LANES)
    lane = lax.broadcasted_iota(jnp.int32, shape, 1)
    c = lax.broadcasted_iota(jnp.int32, shape, 0)
    kc = lane & (GRID_W - 1)
    cs = jnp.clip(c - NA_WIN_COLS // 2, 0, GRID_W - NA_WIN_COLS)
    valid = (kc >= cs) & (kc < cs + NA_WIN_COLS)
    toeplitz = lambda ref, a: pltpu.roll(jnp.broadcast_to(ref[a:a + 1, :], shape), 0, 1, stride=1, stride_axis=0)
    rows = ([toeplitz(e_ref, a) for a in range(n_dr)], [toeplitz(e64_ref, a) for a in range(n_dr)])
    masked = jnp.full(shape, NEG_BIG, F32)
    for v, m in enumerate(NA_BLOCK_VARIANTS):
        rp0 = _na_win_start(m, clip=lambda x, lo, hi: min(max(x, lo), hi))
        for i in range(NA_BLK_ROWS):
            r = NA_BLK_ROWS * m + i
            rs = min(max(r - NA_WIN_ROWS // 2, 0), GRID_ROWS - NA_WIN_ROWS)
            for t in range(NA_WIN_PAIRS):
                halves = []
                for u in range(2):
                    kr = 2 * (rp0 + t) + u
                    halves.append(rows[u][kr - r + NA_WIN_ROWS - 1] if rs <= kr < rs + NA_WIN_ROWS else masked)
                o_ref[v, i * GRID_W:(i + 1) * GRID_W, t * LANES:(t + 1) * LANES] = jnp.where(
                    valid, jnp.where(lane < GRID_W, halves[0], halves[1]), NEG_BIG)


def _na_bias_table(rpb):
    n_dr = 2 * NA_WIN_ROWS - 1
    blk = NA_BLK_ROWS * GRID_W
    e = jnp.zeros((NA_HEADS, n_dr, LANES), F32)
    e = e.at[:, :, :NA_WIN_COLS].set(rpb[:, :, NA_WIN_COLS - 1:])
    e = e.at[:, :, LANES - (NA_WIN_COLS - 1):].set(rpb[:, :, :NA_WIN_COLS - 1])
    e64 = jnp.roll(e, GRID_W, axis=-1)
    spec = pl.BlockSpec((None, n_dr, LANES), lambda h: (h, 0, 0))
    return pl.pallas_call(
        _na_bias_kernel,
        grid=(NA_HEADS,),
        in_specs=[spec, spec],
        out_specs=pl.BlockSpec((None, len(NA_BLOCK_VARIANTS), blk, NA_WIN_PAIRS * NA_KT_W), lambda h: (h, 0, 0, 0)),
        out_shape=jax.ShapeDtypeStruct((NA_HEADS, len(NA_BLOCK_VARIANTS), blk, NA_WIN_PAIRS * NA_KT_W), F32),
        compiler_params=_cparams(("arbitrary",)),
        name="na_bias_table",
    )(e, e64)


def _na_attn_lat(q, kt, v, kct, vc, bias_tab):
    blk = NA_BLK_ROWS * GRID_W
    nblk = NA_N_BLOCKS
    npair = DEC_SEQ // NA_KT_W
    row = pl.BlockSpec((blk, D_MODEL), lambda b, m: (b * nblk + m, 0))
    bias_spec = pl.BlockSpec((NA_HEADS, None, blk, NA_WIN_PAIRS * NA_KT_W), lambda b, m: (0, _na_variant(m), 0, 0),
                             pipeline_mode=pl.Buffered(1))
    return pl.pallas_call(
        _na_attn_lat_kernel,
        grid=(DEC_BATCH, nblk),
        in_specs=[row,
                  pl.BlockSpec((None, npair, D_MODEL, NA_KT_W), lambda b, m: (b, 0, 0, 0)),
                  pl.BlockSpec((None, DEC_SEQ, D_MODEL), lambda b, m: (b, 0, 0)),
                  pl.BlockSpec((None, D_MODEL, PAST_LEN), lambda b, m: (b, 0, 0)),
                  pl.BlockSpec((None, PAST_LEN, D_MODEL), lambda b, m: (b, 0, 0)),
                  bias_spec],
        out_specs=row,
        out_shape=jax.ShapeDtypeStruct((N_LAT, D_MODEL), BF16),
        compiler_params=_cparams(("arbitrary", "arbitrary")),
        name="na_attn_lat",
    )(q, kt.reshape(DEC_BATCH, npair, D_MODEL, NA_KT_W), v.reshape(DEC_BATCH, DEC_SEQ, D_MODEL), kct, vc, bias_tab)


def _pow2_runs(n, src, dst, largest, make_copy):
    out = []
    b = largest
    while b >= SUBLANES:
        out.append(((n & b) != 0, make_copy(src, dst, b)))
        src = src + (n & b)
        dst = dst + (n & b)
        b //= 2
    return out


def _rows8(start, size):
    return pl.ds(pl.multiple_of(start, SUBLANES), size)


def _moe_scatter_kernel(seg_ref, pad_ref, na_ref, x_ref, sc_ref, sh_ref, rt_ref, xs_ref, srt_scr, zero_scr, sems, zsem):
    i = pl.program_id(0)
    tm = x_ref.shape[0]
    slot = i % 2
    hm = (x_ref[...] * (1.0 + sc_ref[...]) + sh_ref[...]).astype(BF16)
    rtt = rt_ref[...].T
    row = lax.broadcasted_iota(jnp.int32, (MOE_LOCAL_ROWS, tm), 0).astype(F32)
    chosen = (row == rtt[6:7, :]) | (row == rtt[7:8, :])
    srt_scr[slot] = _dot(jnp.where(chosen, 1.0, 0.0).astype(BF16), hm)

    def seg_copies(tile, s):
        out = []
        for e in range(N_EXPERTS):
            base = (tile * N_EXPERTS + e) * 3
            out += _pow2_runs(
                seg_ref[base], seg_ref[base + 1], seg_ref[base + 2], tm,
                lambda src, dst, b: pltpu.make_async_copy(srt_scr.at[s, _rows8(src, b)], xs_ref.at[_rows8(dst, b)],
                                                          sems.at[s]))
        return out

    for cond, cp in seg_copies(i, slot):
        pl.when(cond)(cp.start)

    @pl.when(i > 0)
    def _():
        for cond, cp in seg_copies(i - 1, 1 - slot):
            pl.when(cond)(cp.wait)

    @pl.when(i == pl.num_programs(0) - 1)
    def _():
        zero_scr[...] = jnp.zeros_like(zero_scr)

        def pad_copies():
            out = []
            for e in range(N_EXPERTS):
                out += _pow2_runs(
                    pad_ref[N_EXPERTS + e], 0, pad_ref[e], TM_MOE // 2,
                    lambda src, dst, b: pltpu.make_async_copy(zero_scr.at[pl.ds(0, b)], xs_ref.at[_rows8(dst, b)], zsem))
            return out

        def tile_copy(j):
            rows = pl.ds(pl.multiple_of(j * TM_MOE, TM_MOE), TM_MOE)
            return pltpu.make_async_copy(zero_scr, xs_ref.at[rows], zsem)

        for cond, cp in pad_copies():
            pl.when(cond)(cp.start)
        lax.fori_loop(na_ref[0], NT_MOE, lambda j, c: (tile_copy(j).start(), c)[1], 0)
        for cond, cp in pad_copies():
            pl.when(cond)(cp.wait)
        lax.fori_loop(na_ref[0], NT_MOE, lambda j, c: (tile_copy(j).wait(), c)[1], 0)
        for cond, cp in seg_copies(i, slot):
            pl.when(cond)(cp.wait)


def _moe_scatter(seg, pad, na, x_all, rt_all, mod):
    tm = TM_PROJ
    rf = _all_row_fn(tm)
    return pl.pallas_call(
        _moe_scatter_kernel,
        grid_spec=pltpu.PrefetchScalarGridSpec(
            num_scalar_prefetch=3,
            grid=(N_TOK // tm,),
            in_specs=[pl.BlockSpec((tm, D_MODEL), lambda i, *_: (i, 0)), _mod_spec(4, rf), _mod_spec(3, rf),
                      pl.BlockSpec((tm, LANES), lambda i, *_: (i, 0))],
            out_specs=pl.BlockSpec(memory_space=pl.ANY),
            scratch_shapes=[pltpu.VMEM((2, MOE_LOCAL_ROWS, D_MODEL), F32), pltpu.VMEM((TM_MOE, D_MODEL), F32),
                            pltpu.SemaphoreType.DMA((2,)), pltpu.SemaphoreType.DMA(())],
        ),
        out_shape=jax.ShapeDtypeStruct((ROWS_SORTED, D_MODEL), F32),
        compiler_params=_cparams(("arbitrary",)),
        name="moe_scatter",
    )(seg, pad, na, x_all, mod, mod, rt_all)


def _tile_changed(te_ref, j):
    prev = te_ref[jnp.maximum(j - 1, 0)]
    return (j == 0) | (te_ref[j] != prev)


def _moe_up_kernel(te_ref, na_ref, x_ref, wg_ref, wu_ref, o_ref, w_scr):
    j = pl.program_id(1)
    tf = wg_ref.shape[1]

    @pl.when(j < na_ref[0])
    def _():
        @pl.when(_tile_changed(te_ref, j))
        def _():
            w_scr[:, :tf] = wg_ref[...].astype(BF16)
            w_scr[:, tf:] = wu_ref[...].astype(BF16)

        gu = _dot(x_ref[...].astype(BF16), w_scr[...])
        o_ref[...] = (_silu(gu[:, :tf]) * gu[:, tf:]).astype(BF16)

    @pl.when(j >= na_ref[0])
    def _():
        o_ref[...] = jnp.zeros_like(o_ref)


def _moe_up(te, na, xs, w_in):
    tm, tf = TM_MOE, TF_MOE
    nf = FFN_EXPERT // tf
    row = lambda j, na: jnp.minimum(j, na[0] - 1)
    return pl.pallas_call(
        _moe_up_kernel,
        grid_spec=pltpu.PrefetchScalarGridSpec(
            num_scalar_prefetch=2,
            grid=(nf, NT_MOE),
            in_specs=[pl.BlockSpec((tm, D_MODEL), lambda f, j, te, na: (row(j, na), 0)),
                      pl.BlockSpec((None, D_MODEL, tf), lambda f, j, te, na: (te[j], 0, f)),
                      pl.BlockSpec((None, D_MODEL, tf), lambda f, j, te, na: (te[j], 0, nf + f))],
            out_specs=pl.BlockSpec((tm, tf), lambda f, j, te, na: (j, f)),
            scratch_shapes=[pltpu.VMEM((D_MODEL, 2 * tf), BF16)],
        ),
        out_shape=jax.ShapeDtypeStruct((ROWS_SORTED, FFN_EXPERT), BF16),
        compiler_params=_cparams(("arbitrary", "arbitrary")),
        name="moe_up",
    )(te, na, xs, w_in, w_in)


def _moe_down_kernel(te_ref, na_ref, h_ref, w_ref, o_ref, w_scr):
    j = pl.program_id(0)

    @pl.when(j < na_ref[0])
    def _():
        @pl.when(_tile_changed(te_ref, j))
        def _():
            w_scr[...] = w_ref[...].astype(BF16)

        o_ref[...] = _dot(h_ref[...], w_scr[...])

    @pl.when(j >= na_ref[0])
    def _():
        o_ref[...] = jnp.zeros_like(o_ref)


def _moe_down(te, na, hmid, w_out):
    tm = TM_MOE
    row = lambda j, na: jnp.minimum(j, na[0] - 1)
    return pl.pallas_call(
        _moe_down_kernel,
        grid_spec=pltpu.PrefetchScalarGridSpec(
            num_scalar_prefetch=2,
            grid=(NT_MOE,),
            in_specs=[pl.BlockSpec((tm, FFN_EXPERT), lambda j, te, na: (row(j, na), 0)),
                      pl.BlockSpec((None, FFN_EXPERT, D_MODEL), lambda j, te, na: (te[j], 0, 0))],
            out_specs=pl.BlockSpec((tm, D_MODEL), lambda j, te, na: (j, 0)),
            scratch_shapes=[pltpu.VMEM((FFN_EXPERT, D_MODEL), BF16)],
        ),
        out_shape=jax.ShapeDtypeStruct((ROWS_SORTED, D_MODEL), F32),
        compiler_params=_cparams(("arbitrary",)),
        name="moe_down",
    )(te, na, hmid, w_out)


def _moe_combine_kernel(pos_ref, y_ref, rt_ref, x_ref, g_ref, lg_ref, lb_ref, o_ref, ybuf, sems, *, tok_off):
    i = pl.program_id(0)
    tm = x_ref.shape[0]

    def fetch(tile, slot):
        def start(t, c):
            for k in range(TOP_K):
                p = pos_ref[k * N_TOK + tok_off + tile * tm + t]
                pltpu.make_async_copy(y_ref.at[pl.ds(p, 1)], ybuf.at[slot, k, pl.ds(t, 1)], sems.at[slot]).start()
            return c

        lax.fori_loop(0, tm, start, 0, unroll=8)

    @pl.when(i == 0)
    def _():
        fetch(0, 0)

    @pl.when(i + 1 < pl.num_programs(0))
    def _():
        fetch(i + 1, (i + 1) % 2)

    slot = i % 2
    pltpu.make_async_copy(ybuf.at[slot], ybuf.at[slot], sems.at[slot]).wait()
    rt = rt_ref[...]
    moe = rt[:, 2:3] * ybuf[slot, 0] + rt[:, 3:4] * ybuf[slot, 1]
    o_ref[...] = _layer_norm(ALPHA * x_ref[...] + g_ref[...] * moe, lg_ref[...], lb_ref[...])


def _moe_combine(pos, y, rt_all, x_all, mod, lg, lb, latent):
    n = N_LAT if latent else N_CTX
    tm = TM_ROUTE
    rf = _row_fn(latent, tm)
    tok_off = N_CTX if latent else 0
    off = tok_off // tm
    return pl.pallas_call(
        functools.partial(_moe_combine_kernel, tok_off=tok_off),
        grid_spec=pltpu.PrefetchScalarGridSpec(
            num_scalar_prefetch=1,
            grid=(n // tm,),
            in_specs=[pl.BlockSpec(memory_space=pl.ANY),
                      pl.BlockSpec((tm, LANES), lambda i, pos: (i + off, 0)),
                      pl.BlockSpec((tm, D_MODEL), lambda i, pos: (i + off, 0)),
                      _mod_spec(5, rf),
                      pl.BlockSpec((1, D_MODEL), lambda i, pos: (0, 0)),
                      pl.BlockSpec((1, D_MODEL), lambda i, pos: (0, 0))],
            out_specs=pl.BlockSpec((tm, D_MODEL), lambda i, pos: (i, 0)),
            scratch_shapes=[pltpu.VMEM((2, TOP_K, tm, D_MODEL), F32), pltpu.SemaphoreType.DMA((2,))],
        ),
        out_shape=jax.ShapeDtypeStruct((n, D_MODEL), F32),
        compiler_params=_cparams(("arbitrary",)),
        name="moe_combine_lat" if latent else "moe_combine_ctx",
    )(pos, y, rt_all, x_all, mod, lg, lb)


def _routing_positions(rt_all, counts, meta):
    tm = TM_MOE
    cnt = counts[0, :N_EXPERTS].astype(jnp.int32)
    padded = ((cnt + tm - 1) // tm) * tm
    gend = jnp.cumsum(padded)
    gstart = gend - padded
    route = rt_all[:, :8].astype(jnp.int32)
    onehot = lambda idx: idx[:, None] == jnp.arange(N_EXPERTS, dtype=jnp.int32)[None, :]
    start_of = lambda idx: jnp.sum(jnp.where(onehot(idx), gstart[None, :], 0), axis=1)
    pos = jnp.concatenate([start_of(route[:, 0]) + route[:, 4], start_of(route[:, 1]) + route[:, 5]])
    pad = jnp.concatenate([gstart + cnt, padded - cnt]).astype(jnp.int32)
    tile_start = jnp.arange(NT_MOE, dtype=jnp.int32) * tm
    te = jnp.sum((tile_start[:, None] >= gend[None, :]).astype(jnp.int32), axis=1)
    n_active = (gend[-1] // tm).astype(jnp.int32)
    last_e = jnp.take(te, jnp.maximum(n_active - 1, 0))
    te = jnp.where(tile_start < gend[-1], te, last_e).astype(jnp.int32)
    m = meta[:, :3, :N_EXPERTS].astype(jnp.int32)
    seg = jnp.stack([m[:, 0], m[:, 1], m[:, 2] + gstart[None, :]], axis=-1).reshape(-1)
    return pos, seg, pad, te, n_active.reshape(1)


def kernel(x_prompt, x_sample, cache_ckv_l0, cache_krope_l0, cache_k_l1, cache_v_l1, c, c_ctx, w_ada_l0, b_ada_l0, mla_w_dq, mla_g_q, mla_w_uq, mla_w_dkv, mla_g_kv, mla_w_ukv, mla_w_o, ln1_g_l0, ln1_b_l0, ffn_w_in, ffn_w_out, ln2_g_l0, ln2_b_l0, w_ada_l1, b_ada_l1, na_w_qkv, na_rpb, na_w_o, ln1_g_l1, ln1_b_l1, moe_w_router, moe_w_in, moe_w_out, ln2_g_l1, ln2_b_l1):
    row = lambda v: v.reshape(1, -1)
    xp = x_prompt.reshape(N_CTX, D_MODEL)
    xs = x_sample.reshape(N_LAT, D_MODEL)
    groups = ((xp, False), (xs, True))

    cvecs = jnp.concatenate([c_ctx[None], c, jnp.zeros((8 - 1 - DEC_BATCH, D_MODEL), F32)], axis=0)
    mod0 = _ada_mod(cvecs, w_ada_l0, b_ada_l0)
    mod1 = _ada_mod(cvecs, w_ada_l1, b_ada_l1)

    wa = jnp.concatenate([mla_w_dq, mla_w_dkv], axis=1).astype(BF16)
    wuq = mla_w_uq.reshape(MLA_Q_LORA, MLA_HEADS, MLA_NOPE + MLA_ROPE)
    wuq = jnp.pad(wuq, ((0, 0), (0, 0), (0, Q_HEAD_PAD - MLA_NOPE - MLA_ROPE)))
    wuq = wuq.reshape(MLA_Q_LORA, MLA_HEADS * Q_HEAD_PAD).astype(BF16)
    wukv = mla_w_ukv.astype(BF16)
    wo0 = mla_w_o.astype(BF16)
    w_in0 = ffn_w_in.astype(BF16)
    w_out0 = ffn_w_out.astype(BF16)
    wqkv = na_w_qkv.astype(BF16)
    wo1 = na_w_o.astype(BF16)
    wr = jnp.pad(moe_w_router, ((0, 0), (0, LANES - N_EXPERTS))).astype(BF16)
    bias_tab = _na_bias_table(na_rpb)
    tab = _rope_tables()

    kvx_cache = _matmul(cache_ckv_l0.reshape(DEC_BATCH * PAST_LEN, MLA_KV_LORA), wukv, BF16, 512, "mla_expand_cache")
    kvx_cache = kvx_cache.reshape(DEC_BATCH, PAST_LEN, -1)
    krp_cache = jnp.pad(cache_krope_l0, ((0, 0), (0, 0), (0, LANES - MLA_ROPE))).astype(BF16)
    x1 = []
    new_ckv = new_kr = None
    vt_cache = kvx_cache.reshape(DEC_BATCH, PAST_LEN, MLA_HEADS, MLA_NOPE + MLA_V)[..., MLA_NOPE:]
    vt_cache = vt_cache.reshape(DEC_BATCH, PAST_LEN, MLA_HEADS * MLA_V).transpose(0, 2, 1)
    for x, latent in groups:
        proj = _mla_proj(x, mod0, wa, row(mla_g_q), row(mla_g_kv), wuq, wukv, tab, latent)
        if latent:
            qt, kvx, krp, vt = proj
            o = _mla_attn_lat(qt, kvx, krp, kvx_cache, krp_cache, vt, vt_cache)
        else:
            q, kvx, new_ckv, new_kr, krp = proj
            o = _mla_attn_ctx(q, kvx, krp)
        x1.append(_attn_out_ffn(o, x, wo0, mod0, row(ln1_g_l0), row(ln1_b_l0), w_in0, w_out0,
                                row(ln2_g_l0), row(ln2_b_l0), latent))

    kct = cache_k_l1.reshape(DEC_BATCH, PAST_LEN, D_MODEL).transpose(0, 2, 1).astype(BF16)
    vc = cache_v_l1.reshape(DEC_BATCH, PAST_LEN, D_MODEL).astype(BF16)
    q, new_k, new_v, kt = _na_qkv(x1[0], mod1, wqkv, F32, False)
    o_ctx = _na_attn_ctx(q, kt, new_v)
    q, _, v, kt = _na_qkv(x1[1], mod1, wqkv, BF16, True)
    o_lat = _na_attn_lat(q, kt, v, kct, vc, bias_tab)
    x2_all, rt_all, counts, meta = _proj_ln_route(o_ctx, x1[0], o_lat, x1[1], wo1, mod1, row(ln1_g_l1),
                                                  row(ln1_b_l1), wr)

    pos, seg, pad, te, n_active = _routing_positions(rt_all, counts, meta)
    x_sorted = _moe_scatter(seg, pad, n_active, x2_all, rt_all, mod1)
    hmid = _moe_up(te, n_active, x_sorted, moe_w_in)
    y = _moe_down(te, n_active, hmid, moe_w_out)
    outs = [_moe_combine(pos, y, rt_all, x2_all, mod1, row(ln2_g_l1), row(ln2_b_l1), latent)
            for latent in (False, True)]

    return (outs[0].reshape(BATCH, SEQ, D_MODEL),
            outs[1].reshape(DEC_BATCH, DEC_SEQ, D_MODEL),
            new_ckv.reshape(BATCH, SEQ, MLA_KV_LORA),
            new_kr.reshape(BATCH, SEQ, MLA_ROPE),
            new_k.reshape(BATCH, SEQ, NA_HEADS, NA_HD),
            new_v.reshape(BATCH, SEQ, NA_HEADS, NA_HD))
```

```python
import functools
import math

import numpy as np
import jax
import jax.numpy as jnp
from jax import lax
from jax.experimental import pallas as pl
from jax.experimental.pallas import tpu as pltpu

F32 = jnp.float32
BF16 = jnp.bfloat16

D_MODEL = 1024
BATCH = 32
SEQ = 256
DEPTH = 2
DEC_BATCH = 2
DEC_SEQ = 2048
PAST_LEN = 512
GRID_W = 64
MLA_HEADS = 8
MLA_NOPE = 128
MLA_ROPE = 64
MLA_V = 128
MLA_Q_LORA = 512
MLA_KV_LORA = 256
MLA_SCALE = 1.0 / math.sqrt(MLA_NOPE + MLA_ROPE)
ROPE_THETA = 10000.0
NA_HEADS = 16
NA_HD = D_MODEL // NA_HEADS
NA_WIN_ROWS = 8
NA_WIN_COLS = 16
NA_SCALE = 1.0 / math.sqrt(NA_HD)
FFN_DENSE = 2816
N_EXPERTS = 8
TOP_K = 2
FFN_EXPERT = 3584
ALPHA = (2 * DEPTH) ** 0.25
LN_EPS = 1e-5
RMS_EPS = 1e-6

N_CTX = BATCH * SEQ
N_LAT = DEC_BATCH * DEC_SEQ
N_TOK = N_CTX + N_LAT
GRID_ROWS = DEC_SEQ // GRID_W
Q_HEAD_PAD = 256
LANES = 128
SUBLANES = 8
NEG_BIG = -1e30
MLA_KEY_CHUNK = 256
NA_PAIR = 2 * NA_HD
NA_BLK_ROWS = 4
NA_KT_W = 2 * GRID_W
NA_WIN_PAIRS = (NA_BLK_ROWS + NA_WIN_ROWS) // 2
NA_N_BLOCKS = DEC_SEQ // (NA_BLK_ROWS * GRID_W)
NA_BLOCK_VARIANTS = (0, 1, NA_N_BLOCKS - 1)

VMEM_LIMIT = 56 * 1024 * 1024

TM_PROJ = 512
TM_FFN = 512
MXU_TILE = 256
FFN_CHUNKS = ((0, 6 * MXU_TILE), (6 * MXU_TILE, FFN_DENSE))
TM_MOE = 512
TF_MOE = 1792
N_PAIRS = N_TOK * TOP_K
MOE_LOCAL_ROWS = TOP_K * TM_PROJ + N_EXPERTS * SUBLANES
N_SEG_PAD = N_EXPERTS * (N_TOK // TM_PROJ) * (SUBLANES - 1)
NT_MOE = -(-(N_PAIRS + N_SEG_PAD + N_EXPERTS * (TM_MOE - 1)) // TM_MOE)
ROWS_SORTED = NT_MOE * TM_MOE
TM_ROUTE = 256


def _cparams(sem, vmem=VMEM_LIMIT):
    return pltpu.CompilerParams(dimension_semantics=sem, vmem_limit_bytes=vmem)


def _silu(x):
    return x * jax.nn.sigmoid(x)


def _layer_norm(y, g, b):
    mu = jnp.mean(y, axis=-1, keepdims=True)
    d = y - mu
    var = jnp.mean(d * d, axis=-1, keepdims=True)
    return d * lax.rsqrt(var + LN_EPS) * g + b


def _rms_norm(y, g):
    return y * lax.rsqrt(jnp.mean(y * y, axis=-1, keepdims=True) + RMS_EPS) * g


def _dot(a, b):
    return jnp.dot(a, b, preferred_element_type=F32)


def _dot_nt(a, b):
    return lax.dot_general(a, b, (((1,), (1,)), ((), ())), preferred_element_type=F32)


def _ada_kernel(c_ref, w_ref, b_ref, o_ref):
    s = _silu(c_ref[...]).astype(BF16)
    o_ref[...] = _dot(s, w_ref[...].astype(BF16)) + b_ref[...]


def _ada_mod(cvecs, w, b):
    tn = 1536
    m = pl.pallas_call(
        _ada_kernel,
        grid=(6 * D_MODEL // tn,),
        in_specs=[pl.BlockSpec((8, D_MODEL), lambda j: (0, 0)),
                  pl.BlockSpec((D_MODEL, tn), lambda j: (0, j)),
                  pl.BlockSpec((1, tn), lambda j: (0, j))],
        out_specs=pl.BlockSpec((8, tn), lambda j: (0, j)),
        out_shape=jax.ShapeDtypeStruct((8, 6 * D_MODEL), F32),
        compiler_params=_cparams(("arbitrary",)),
        name="ada_mod",
    )(cvecs, w, b.reshape(1, -1))
    return m[:3].reshape(3 * 6, 1, D_MODEL)


def _mod_spec(j, row_fn):
    return pl.BlockSpec((None, 1, D_MODEL), lambda i, *_: (row_fn(i) * 6 + j, 0, 0))


def _row_fn(latent, tm):
    if not latent:
        return lambda i: 0
    per = DEC_SEQ // tm
    return lambda i: 1 + i // per


def _const_spec(shape):
    nd = len(shape)
    return pl.BlockSpec(shape, lambda *_: (0,) * nd, pipeline_mode=pl.Buffered(1))


def _rope_kernel(invf_ref, o_ref):
    i = pl.program_id(0)
    tm = o_ref.shape[1]
    t = i * tm + lax.broadcasted_iota(jnp.int32, (tm, LANES), 0)
    lane = lax.broadcasted_iota(jnp.int32, (tm, LANES), 1)
    row = t >> int(math.log2(GRID_W))
    col = t & (GRID_W - 1)
    pos = jnp.where(lane < MLA_ROPE // 2, row, col).astype(F32)
    ang = pos * invf_ref[...]
    cos = jnp.cos(ang)
    sin = jnp.sin(ang)
    unit = lane >> int(math.log2(MLA_ROPE // 4))
    first = (unit == 0) | (unit == 2)
    second = (unit == 1) | (unit == 3)
    o_ref[0] = jnp.where(lane < MLA_ROPE, cos, 0.0)
    o_ref[1] = jnp.where(first, -sin, 0.0)
    o_ref[2] = jnp.where(second, sin, 0.0)


def _rope_tables():
    half = MLA_ROPE // 2
    inv_freq = (1.0 / (ROPE_THETA ** (np.arange(0, half, 2, dtype=np.float32) / half))).astype(np.float32)
    lane_f = np.zeros((1, LANES), np.float32)
    lane_f[0, :MLA_ROPE] = np.tile(inv_freq, 4)
    tm = 256
    return pl.pallas_call(
        _rope_kernel,
        grid=(DEC_SEQ // tm,),
        in_specs=[_const_spec((1, LANES))],
        out_specs=pl.BlockSpec((3, tm, LANES), lambda i: (0, i, 0)),
        out_shape=jax.ShapeDtypeStruct((3, DEC_SEQ, LANES), F32),
        compiler_params=_cparams(("arbitrary",)),
        name="rope_tables",
    )(jnp.asarray(lane_f))


def _rotate(v, tab_ref):
    return (v * tab_ref[0] + pltpu.roll(v, LANES - MLA_ROPE // 4, 1) * tab_ref[1]
            + pltpu.roll(v, MLA_ROPE // 4, 1) * tab_ref[2])


def _mla_proj_kernel(*refs, rope):
    if rope:
        (x_ref, sc_ref, sh_ref, wa_ref, gq_ref, gkv_ref, wuq_ref, wukv_ref, tab_ref,
         qt_ref, kvx_ref, krp_ref, vt_ref) = refs
    else:
        (x_ref, sc_ref, sh_ref, wa_ref, gq_ref, gkv_ref, wuq_ref, wukv_ref,
         q_ref, kvx_ref, ckv_ref, kr_ref, krp_ref) = refs
    h = (x_ref[...] * (1.0 + sc_ref[...]) + sh_ref[...]).astype(BF16)
    t = _dot(h, wa_ref[...])
    cq = _rms_norm(t[:, :MLA_Q_LORA], gq_ref[...])
    ckv = _rms_norm(t[:, MLA_Q_LORA:MLA_Q_LORA + MLA_KV_LORA], gkv_ref[...])
    kr = t[:, MLA_Q_LORA + MLA_KV_LORA:]
    kvx = _dot(ckv.astype(BF16), wukv_ref[...])
    kvx_ref[...] = kvx.astype(BF16)
    q = _dot(cq.astype(BF16), wuq_ref[...])
    krp = jnp.concatenate([kr, jnp.zeros_like(kr)], axis=-1)
    if rope:
        krp_ref[...] = _rotate(krp, tab_ref).astype(BF16)
        parts = []
        for hd in range(MLA_HEADS):
            lo = hd * Q_HEAD_PAD
            parts += [q[:, lo:lo + MLA_NOPE], _rotate(q[:, lo + MLA_NOPE:lo + Q_HEAD_PAD], tab_ref)]
        qt_ref[...] = jnp.concatenate(parts, axis=1).T.astype(BF16)
        vcols = [kvx[:, hd * Q_HEAD_PAD + MLA_NOPE:(hd + 1) * Q_HEAD_PAD] for hd in range(MLA_HEADS)]
        vt_ref[...] = jnp.concatenate(vcols, axis=1).T.astype(BF16)
    else:
        ckv_ref[...] = ckv
        krt = kr.T
        for t in range(kr_ref.shape[0]):
            kr_ref[t] = krt[:, t * SEQ:(t + 1) * SEQ]
        krp_ref[...] = krp.astype(BF16)
        q_ref[...] = q.astype(BF16)


def _mla_proj(x, mod, wa, gq, gkv, wuq, wukv, tab, latent):
    n = x.shape[0]
    tm = TM_PROJ
    rf = _row_fn(latent, tm)
    tok = lambda w: pl.BlockSpec((tm, w), lambda i: (i, 0))
    in_specs = [tok(D_MODEL), _mod_spec(1, rf), _mod_spec(0, rf),
                _const_spec(wa.shape), _const_spec(gq.shape), _const_spec(gkv.shape),
                _const_spec(wuq.shape), _const_spec(wukv.shape)]
    args = [x, mod, mod, wa, gq, gkv, wuq, wukv]
    wq, wkv = MLA_HEADS * Q_HEAD_PAD, MLA_HEADS * (MLA_NOPE + MLA_V)
    if latent:
        per = DEC_SEQ // tm
        in_specs.append(pl.BlockSpec((3, tm, LANES), lambda i: (0, i % per, 0)))
        args.append(tab)
        out_specs = [pl.BlockSpec((wq, tm), lambda i: (0, i)), tok(wkv), tok(LANES),
                     pl.BlockSpec((None, MLA_HEADS * MLA_V, tm), lambda i: (i // per, 0, i % per))]
        out_shape = [jax.ShapeDtypeStruct((wq, n), BF16), jax.ShapeDtypeStruct((n, wkv), BF16),
                     jax.ShapeDtypeStruct((n, LANES), BF16),
                     jax.ShapeDtypeStruct((DEC_BATCH, MLA_HEADS * MLA_V, DEC_SEQ), BF16)]
    else:
        out_specs = [tok(wq), tok(wkv), tok(MLA_KV_LORA),
                     pl.BlockSpec((tm // SEQ, MLA_ROPE, SEQ), lambda i: (i, 0, 0)), tok(LANES)]
        out_shape = [jax.ShapeDtypeStruct((n, wq), BF16), jax.ShapeDtypeStruct((n, wkv), BF16),
                     jax.ShapeDtypeStruct((n, MLA_KV_LORA), F32), jax.ShapeDtypeStruct((n // SEQ, MLA_ROPE, SEQ), F32),
                     jax.ShapeDtypeStruct((n, LANES), BF16)]
    return pl.pallas_call(
        functools.partial(_mla_proj_kernel, rope=latent),
        grid=(n // tm,),
        in_specs=in_specs,
        out_specs=out_specs,
        out_shape=out_shape,
        compiler_params=_cparams(("arbitrary",)),
        name="mla_proj_lat" if latent else "mla_proj_ctx",
    )(*args)


def _matmul_kernel(a_ref, b_ref, o_ref):
    o_ref[...] = _dot(a_ref[...].astype(BF16), b_ref[...]).astype(o_ref.dtype)


def _matmul(a, b, out_dtype, tm, name):
    m, k = a.shape
    n = b.shape[1]
    return pl.pallas_call(
        _matmul_kernel,
        grid=(m // tm,),
        in_specs=[pl.BlockSpec((tm, k), lambda i: (i, 0)), _const_spec(b.shape)],
        out_specs=pl.BlockSpec((tm, n), lambda i: (i, 0)),
        out_shape=jax.ShapeDtypeStruct((m, n), out_dtype),
        compiler_params=_cparams(("arbitrary",)),
        name=name,
    )(a, b)


def _mla_head(qh, kparts, vparts):
    ss = [_dot_nt(qh, k) for k in kparts]
    m = functools.reduce(jnp.maximum, [jnp.max(s, axis=-1, keepdims=True) for s in ss])
    ps = [jnp.exp2((s - m) * (MLA_SCALE * math.log2(math.e))) for s in ss]
    l = functools.reduce(jnp.add, [jnp.sum(p, axis=-1, keepdims=True) for p in ps])
    o = functools.reduce(jnp.add, [_dot(p.astype(BF16), v) for p, v in zip(ps, vparts)])
    return o / l


def _mla_attn_ctx_kernel(q_ref, kvx_ref, krp_ref, o_ref):
    krp = krp_ref[...]
    for hd in range(MLA_HEADS):
        lo = hd * (MLA_NOPE + MLA_V)
        qh = q_ref[:, hd * Q_HEAD_PAD:(hd + 1) * Q_HEAD_PAD]
        kh = jnp.concatenate([kvx_ref[:, lo:lo + MLA_NOPE], krp], axis=-1)
        vh = kvx_ref[:, lo + MLA_NOPE:lo + MLA_NOPE + MLA_V]
        o_ref[:, hd * MLA_V:(hd + 1) * MLA_V] = _mla_head(qh, [kh], [vh]).astype(BF16)


def _mla_attn_ctx(q, kvx, krp):
    tok = lambda w: pl.BlockSpec((SEQ, w), lambda b: (b, 0))
    return pl.pallas_call(
        _mla_attn_ctx_kernel,
        grid=(BATCH,),
        in_specs=[tok(q.shape[1]), tok(kvx.shape[1]), tok(LANES)],
        out_specs=tok(MLA_HEADS * MLA_V),
        out_shape=jax.ShapeDtypeStruct((N_CTX, MLA_HEADS * MLA_V), BF16),
        compiler_params=_cparams(("arbitrary",)),
        name="mla_attn_ctx",
    )(q, kvx, krp)


def _mla_attn_lat_kernel(qt_ref, kvl_ref, krl_ref, kvc_ref, krc_ref, vtl_ref, vtc_ref, o_ref):
    chunks = [(kvl_ref, krl_ref, vtl_ref, c) for c in range(DEC_SEQ // MLA_KEY_CHUNK)]
    chunks += [(kvc_ref, krc_ref, vtc_ref, c) for c in range(PAST_LEN // MLA_KEY_CHUNK)]
    for hd in range(MLA_HEADS):
        lo = hd * (MLA_NOPE + MLA_V)
        qt = qt_ref[hd * Q_HEAD_PAD:(hd + 1) * Q_HEAD_PAD, :]
        ss = []
        for kv_ref, kr_ref, _, c in chunks:
            rows = slice(c * MLA_KEY_CHUNK, (c + 1) * MLA_KEY_CHUNK)
            ss.append(_dot(jnp.concatenate([kv_ref[rows, lo:lo + MLA_NOPE], kr_ref[rows, :]], axis=-1), qt))
        m = functools.reduce(jnp.maximum, [jnp.max(s, axis=0, keepdims=True) for s in ss])
        ps = [jnp.exp2((s - m) * (MLA_SCALE * math.log2(math.e))) for s in ss]
        l = functools.reduce(jnp.add, [jnp.sum(p, axis=0, keepdims=True) for p in ps])
        ots = [_dot(vt_ref[hd * MLA_V:(hd + 1) * MLA_V, c * MLA_KEY_CHUNK:(c + 1) * MLA_KEY_CHUNK], p.astype(BF16))
               for (_, _, vt_ref, c), p in zip(chunks, ps)]
        ot = functools.reduce(jnp.add, ots) / l
        o_ref[:, hd * MLA_V:(hd + 1) * MLA_V] = ot.T.astype(BF16)


def _mla_attn_lat(qt, kvx_lat, krp_lat, kvx_ctx, krp_ctx, vt_lat, vt_ctx):
    tq = 256
    per = DEC_SEQ // tq
    wkv = kvx_lat.shape[-1]
    wv = MLA_HEADS * MLA_V
    batch = lambda rows, cols: pl.BlockSpec((None, rows, cols), lambda b, i: (b, 0, 0))
    return pl.pallas_call(
        _mla_attn_lat_kernel,
        grid=(DEC_BATCH, per),
        in_specs=[pl.BlockSpec((qt.shape[0], tq), lambda b, i: (0, b * per + i)),
                  batch(DEC_SEQ, wkv), batch(DEC_SEQ, LANES), batch(PAST_LEN, wkv), batch(PAST_LEN, LANES),
                  batch(wv, DEC_SEQ), batch(wv, PAST_LEN)],
        out_specs=pl.BlockSpec((tq, wv), lambda b, i: (b * per + i, 0)),
        out_shape=jax.ShapeDtypeStruct((N_LAT, wv), BF16),
        compiler_params=_cparams(("arbitrary", "arbitrary")),
        name="mla_attn_lat",
    )(qt, kvx_lat.reshape(DEC_BATCH, DEC_SEQ, wkv), krp_lat.reshape(DEC_BATCH, DEC_SEQ, LANES), kvx_ctx, krp_ctx,
      vt_lat, vt_ctx)


def _top2(logits):
    lane = lax.broadcasted_iota(jnp.int32, logits.shape, 1).astype(F32)
    m1 = jnp.max(logits, axis=-1, keepdims=True)
    i1 = jnp.min(jnp.where(logits == m1, lane, float(LANES)), axis=-1, keepdims=True)
    rest = jnp.where(lane == i1, -jnp.inf, logits)
    m2 = jnp.max(rest, axis=-1, keepdims=True)
    i2 = jnp.min(jnp.where(rest == m2, lane, float(LANES)), axis=-1, keepdims=True)
    e = jnp.exp(m2 - m1)
    w1 = 1.0 / (1.0 + e)
    w2 = e / (1.0 + e)
    return lane, i1, i2, w1, w2


def _proj_ln_route_kernel(oc_ref, xc_ref, ol_ref, xl_ref, wo_ref, g_ref, lg_ref, lb_ref, sc_ref, sh_ref, wr_ref,
                          x1_ref, rt_ref, cnt_ref, meta_ref, carry, *, n_ctx_tiles):
    i = pl.program_id(0)

    @pl.when(i == 0)
    def _():
        carry[...] = jnp.zeros_like(carry)

    def body(o_ref, x_ref):
        out = _dot(o_ref[...], wo_ref[...])
        x1 = _layer_norm(ALPHA * x_ref[...] + g_ref[...] * out, lg_ref[...], lb_ref[...])
        x1_ref[...] = x1
        hm = x1 * (1.0 + sc_ref[...]) + sh_ref[...]
        logits = _dot(hm.astype(BF16), wr_ref[...])
        lane = lax.broadcasted_iota(jnp.int32, logits.shape, 1)
        logits = jnp.where(lane < N_EXPERTS, logits, -jnp.inf)
        lane_f, i1, i2, w1, w2 = _top2(logits)
        tm = logits.shape[0]
        oh1 = (lane_f == i1).astype(F32)
        oh2 = (lane_f == i2).astype(F32)
        rr = lax.broadcasted_iota(jnp.int32, (tm, tm), 0)
        cc = lax.broadcasted_iota(jnp.int32, (tm, tm), 1)
        below = (cc < rr).astype(BF16)
        tot1 = jnp.sum(oh1, axis=0, keepdims=True)
        tot2 = jnp.sum(oh2, axis=0, keepdims=True)
        seg_len = jnp.floor((tot1 + tot2 + (SUBLANES - 1)) * (1.0 / SUBLANES)) * SUBLANES
        e_row = lax.broadcasted_iota(jnp.int32, (LANES, LANES), 0)
        e_col = lax.broadcasted_iota(jnp.int32, (LANES, LANES), 1)
        before = (e_row < e_col).astype(BF16)
        seg_loc = _dot(jnp.broadcast_to(seg_len, (SUBLANES, LANES)).astype(BF16), before)[0:1]
        seg_glob = carry[...]
        cum1 = _dot(below, oh1.astype(BF16))
        cum2 = _dot(below, oh2.astype(BF16)) + tot1
        pick = lambda oh, v: jnp.sum(oh * v, axis=-1, keepdims=True)
        vals = (i1, i2, w1, w2, pick(oh1, cum1 + seg_glob), pick(oh2, cum2 + seg_glob),
                pick(oh1, cum1 + seg_loc), pick(oh2, cum2 + seg_loc))
        rt = jnp.zeros_like(logits)
        for k, val in enumerate(vals):
            rt = jnp.where(lane == k, val, rt)
        rt_ref[...] = rt
        sub = lax.broadcasted_iota(jnp.int32, (SUBLANES, LANES), 0)
        meta_ref[...] = jnp.where(sub == 0, seg_len, jnp.where(sub == 1, seg_loc, jnp.where(sub == 2, seg_glob, 0.0)))
        carry[...] = seg_glob + seg_len
        cnt_ref[...] = jnp.broadcast_to(carry[...], cnt_ref.shape)

    is_ctx = i < n_ctx_tiles
    pl.when(is_ctx)(lambda: body(oc_ref, xc_ref))
    pl.when(jnp.logical_not(is_ctx))(lambda: body(ol_ref, xl_ref))


def _all_row_fn(tm):
    nc = N_CTX // tm
    per = DEC_SEQ // tm
    return lambda i: jnp.where(i < nc, 0, 1 + (i - nc) // per)


def _proj_ln_route(o_ctx, x_ctx, o_lat, x_lat, wo, mod, lg, lb, wr):
    tm = TM_PROJ
    nc = N_CTX // tm
    rf = _all_row_fn(tm)
    ctx = lambda w: pl.BlockSpec((tm, w), lambda i: (jnp.minimum(i, nc - 1), 0))
    lat = lambda w: pl.BlockSpec((tm, w), lambda i: (jnp.maximum(i - nc, 0), 0))
    tok = lambda w: pl.BlockSpec((tm, w), lambda i: (i, 0))
    return pl.pallas_call(
        functools.partial(_proj_ln_route_kernel, n_ctx_tiles=nc),
        grid=(N_TOK // tm,),
        in_specs=[ctx(D_MODEL), ctx(D_MODEL), lat(D_MODEL), lat(D_MODEL), _const_spec(wo.shape), _mod_spec(2, rf),
                  _const_spec(lg.shape), _const_spec(lb.shape), _mod_spec(4, rf), _mod_spec(3, rf),
                  _const_spec(wr.shape)],
        out_specs=[tok(D_MODEL), tok(LANES), pl.BlockSpec((SUBLANES, LANES), lambda i: (0, 0)),
                   pl.BlockSpec((None, SUBLANES, LANES), lambda i: (i, 0, 0))],
        out_shape=[jax.ShapeDtypeStruct((N_TOK, D_MODEL), F32),
                   jax.ShapeDtypeStruct((N_TOK, LANES), F32),
                   jax.ShapeDtypeStruct((SUBLANES, LANES), F32),
                   jax.ShapeDtypeStruct((N_TOK // tm, SUBLANES, LANES), F32)],
        scratch_shapes=[pltpu.VMEM((1, LANES), F32)],
        compiler_params=_cparams(("arbitrary",)),
        name="proj_ln_route",
    )(o_ctx, x_ctx, o_lat, x_lat, wo, mod, lg, lb, mod, mod, wr)


def _attn_out_ffn_kernel(o_ref, x_ref, wo_ref, g1_ref, lg1_ref, lb1_ref, sc_ref, sh_ref, g2_ref,
                         wi_ref, wd_ref, lg2_ref, lb2_ref, y_ref):
    x1 = _layer_norm(ALPHA * x_ref[...] + g1_ref[...] * _dot(o_ref[...], wo_ref[...]), lg1_ref[...], lb1_ref[...])
    h = (x1 * (1.0 + sc_ref[...]) + sh_ref[...]).astype(BF16)
    acc = None
    for lo, hi in FFN_CHUNKS:
        gate = _dot(h, wi_ref[:, lo:hi])
        up = _dot(h, wi_ref[:, FFN_DENSE + lo:FFN_DENSE + hi])
        part = _dot((_silu(gate) * up).astype(BF16), wd_ref[lo:hi, :])
        acc = part if acc is None else acc + part
    y_ref[...] = _layer_norm(ALPHA * x1 + g2_ref[...] * acc, lg2_ref[...], lb2_ref[...])


def _attn_out_ffn(o, x, wo, mod, lg1, lb1, w_in, w_out, lg2, lb2, latent):
    n = x.shape[0]
    tm = TM_FFN
    rf = _row_fn(latent, tm)
    tok = lambda w: pl.BlockSpec((tm, w), lambda i: (i, 0))
    vec = _const_spec((1, D_MODEL))
    return pl.pallas_call(
        _attn_out_ffn_kernel,
        grid=(n // tm,),
        in_specs=[tok(o.shape[1]), tok(D_MODEL), _const_spec(wo.shape), _mod_spec(2, rf), vec, vec,
                  _mod_spec(4, rf), _mod_spec(3, rf), _mod_spec(5, rf),
                  _const_spec(w_in.shape), _const_spec(w_out.shape), vec, vec],
        out_specs=tok(D_MODEL),
        out_shape=jax.ShapeDtypeStruct((n, D_MODEL), F32),
        compiler_params=_cparams(("arbitrary",)),
        name="attn_out_ffn_lat" if latent else "attn_out_ffn_ctx",
    )(o, x, wo, mod, lg1, lb1, mod, mod, mod, w_in, w_out, lg2, lb2)


def _na_qkv_kernel(x_ref, sc_ref, sh_ref, w_ref, q_ref, v_ref, kt_ref, *vt_refs):
    h = (x_ref[...] * (1.0 + sc_ref[...]) + sh_ref[...]).astype(BF16)
    qkv = _dot(h, w_ref[...])
    q_ref[...] = (qkv[:, :D_MODEL] * NA_SCALE).astype(BF16)
    v = qkv[:, 2 * D_MODEL:]
    v_ref[...] = v.astype(BF16)
    pairs = [(kt_ref, qkv[:, D_MODEL:2 * D_MODEL])] + [(ref, v) for ref in vt_refs]
    for ref, val in pairs:
        valt = val.T
        nblk, _, w = ref.shape
        for t in range(nblk):
            ref[t] = valt[:, t * w:(t + 1) * w].astype(ref.dtype)


def _na_qkv(x, mod, w, latent):
    n = x.shape[0]
    tm = TM_PROJ
    kt_w = NA_KT_W if latent else SEQ
    rf = _row_fn(latent, tm)
    tok = pl.BlockSpec((tm, D_MODEL), lambda i: (i, 0))
    tspec = pl.BlockSpec((tm // kt_w, D_MODEL, kt_w), lambda i: (i, 0, 0))
    tshape = lambda dt: jax.ShapeDtypeStruct((n // kt_w, D_MODEL, kt_w), dt)
    out_specs = [tok, tok, tspec]
    out_shape = [jax.ShapeDtypeStruct((n, D_MODEL), BF16), jax.ShapeDtypeStruct((n, D_MODEL), BF16),
                 tshape(BF16 if latent else F32)]
    if not latent:
        out_specs.append(tspec)
        out_shape.append(tshape(F32))
    return pl.pallas_call(
        _na_qkv_kernel,
        grid=(n // tm,),
        in_specs=[tok, _mod_spec(1, rf), _mod_spec(0, rf), _const_spec(w.shape)],
        out_specs=out_specs,
        out_shape=out_shape,
        compiler_params=_cparams(("arbitrary",)),
        name="na_qkv_lat" if latent else "na_qkv_ctx",
    )(x, mod, mod, w)


def _softmax_pv(scores, values):
    m = functools.reduce(jnp.maximum, [jnp.max(s, axis=-1, keepdims=True) for s in scores])
    ps = [jnp.exp(s - m) for s in scores]
    l = functools.reduce(jnp.add, [jnp.sum(p, axis=-1, keepdims=True) for p in ps])
    o = functools.reduce(jnp.add, [_dot(p.astype(BF16), v) for p, v in zip(ps, values)])
    return o / l


def _head_of_pair(x, half):
    lane = lax.broadcasted_iota(jnp.int32, x.shape, 1)
    keep = (lane < NA_HD) if half == 0 else (lane >= NA_HD)
    return jnp.where(keep, x, jnp.zeros_like(x))


def _merge_pair(o0, o1):
    lane = lax.broadcasted_iota(jnp.int32, o0.shape, 1)
    return jnp.where(lane < NA_HD, o0, o1)


def _na_attn_ctx_kernel(q_ref, kt_ref, v_ref, o_ref):
    for p in range(NA_HEADS // 2):
        cols = slice(p * NA_PAIR, (p + 1) * NA_PAIR)
        qp = q_ref[:, cols]
        ktp = kt_ref[cols, :].astype(BF16)
        vp = v_ref[:, cols]
        outs = [_softmax_pv([_dot(_head_of_pair(qp, half), ktp)], [vp]) for half in range(2)]
        o_ref[:, cols] = _merge_pair(*outs).astype(BF16)


def _na_attn_ctx(q, kt, v):
    tok = pl.BlockSpec((SEQ, D_MODEL), lambda b: (b, 0))
    return pl.pallas_call(
        _na_attn_ctx_kernel,
        grid=(BATCH,),
        in_specs=[tok, pl.BlockSpec((None, D_MODEL, SEQ), lambda b: (b, 0, 0)), tok],
        out_specs=tok,
        out_shape=jax.ShapeDtypeStruct((N_CTX, D_MODEL), BF16),
        compiler_params=_cparams(("arbitrary",)),
        name="na_attn_ctx",
    )(q, kt, v)


def _na_win_start(m, clip=jnp.clip):
    return clip(m * NA_BLK_ROWS // 2 - NA_WIN_ROWS // 4, 0, GRID_ROWS // 2 - NA_WIN_PAIRS)


def _na_variant(m):
    return jnp.where(m == 0, 0, jnp.where(m == NA_N_BLOCKS - 1, 2, 1))


def _na_attn_lat_kernel(q_ref, kt_ref, v_ref, kct_ref, vc_ref, bias_ref, o_ref):
    rp0 = _na_win_start(pl.program_id(1))
    win = pl.ds(pl.multiple_of(rp0 * NA_KT_W, NA_KT_W), NA_WIN_PAIRS * NA_KT_W)
    for p in range(NA_HEADS // 2):
        cols = slice(p * NA_PAIR, (p + 1) * NA_PAIR)
        qp = q_ref[:, cols]
        kw = jnp.concatenate([kt_ref[rp0 + t, cols, :] for t in range(NA_WIN_PAIRS)], axis=1)
        kc = kct_ref[cols, :]
        vw = v_ref[win, cols]
        vc = vc_ref[:, cols]
        outs = []
        for half in range(2):
            qh = _head_of_pair(qp, half)
            s_nb = _dot(qh, kw) + bias_ref[2 * p + half]
            outs.append(_softmax_pv([s_nb, _dot(qh, kc)], [vw, vc]))
        o_ref[:, cols] = _merge_pair(*outs).astype(BF16)


def _na_bias_kernel(e_ref, e64_ref, o_ref):
    n_dr = 2 * NA_WIN_ROWS - 1
    shape = (GRID_W, LANES)
    lane = lax.broadcasted_iota(jnp.int32, shape, 1)
    c = lax.broadcasted_iota(jnp.int32, shape, 0)
    kc = lane & (GRID_W - 1)
    cs = jnp.clip(c - NA_WIN_COLS // 2, 0, GRID_W - NA_WIN_COLS)
    valid = (kc >= cs) & (kc < cs + NA_WIN_COLS)
    toeplitz = lambda ref, a: pltpu.roll(jnp.broadcast_to(ref[a:a + 1, :], shape), 0, 1, stride=1, stride_axis=0)
    rows = ([toeplitz(e_ref, a) for a in range(n_dr)], [toeplitz(e64_ref, a) for a in range(n_dr)])
    masked = jnp.full(shape, NEG_BIG, F32)
    for v, m in enumerate(NA_BLOCK_VARIANTS):
        rp0 = _na_win_start(m, clip=lambda x, lo, hi: min(max(x, lo), hi))
        for i in range(NA_BLK_ROWS):
            r = NA_BLK_ROWS * m + i
            rs = min(max(r - NA_WIN_ROWS // 2, 0), GRID_ROWS - NA_WIN_ROWS)
            for t in range(NA_WIN_PAIRS):
                halves = []
                for u in range(2):
                    kr = 2 * (rp0 + t) + u
                    halves.append(rows[u][kr - r + NA_WIN_ROWS - 1] if rs <= kr < rs + NA_WIN_ROWS else masked)
                o_ref[v, i * GRID_W:(i + 1) * GRID_W, t * LANES:(t + 1) * LANES] = jnp.where(
                    valid, jnp.where(lane < GRID_W, halves[0], halves[1]), NEG_BIG)


def _na_bias_table(rpb):
    n_dr = 2 * NA_WIN_ROWS - 1
    blk = NA_BLK_ROWS * GRID_W
    e = jnp.zeros((NA_HEADS, n_dr, LANES), F32)
    e = e.at[:, :, :NA_WIN_COLS].set(rpb[:, :, NA_WIN_COLS - 1:])
    e = e.at[:, :, LANES - (NA_WIN_COLS - 1):].set(rpb[:, :, :NA_WIN_COLS - 1])
    e64 = jnp.roll(e, GRID_W, axis=-1)
    spec = pl.BlockSpec((None, n_dr, LANES), lambda h: (h, 0, 0))
    return pl.pallas_call(
        _na_bias_kernel,
        grid=(NA_HEADS,),
        in_specs=[spec, spec],
        out_specs=pl.BlockSpec((None, len(NA_BLOCK_VARIANTS), blk, NA_WIN_PAIRS * NA_KT_W), lambda h: (h, 0, 0, 0)),
        out_shape=jax.ShapeDtypeStruct((NA_HEADS, len(NA_BLOCK_VARIANTS), blk, NA_WIN_PAIRS * NA_KT_W), F32),
        compiler_params=_cparams(("arbitrary",)),
        name="na_bias_table",
    )(e, e64)


def _na_attn_lat(q, kt, v, kct, vc, bias_tab):
    blk = NA_BLK_ROWS * GRID_W
    nblk = NA_N_BLOCKS
    npair = DEC_SEQ // NA_KT_W
    row = pl.BlockSpec((blk, D_MODEL), lambda b, m: (b * nblk + m, 0))
    bias_spec = pl.BlockSpec((NA_HEADS, None, blk, NA_WIN_PAIRS * NA_KT_W), lambda b, m: (0, _na_variant(m), 0, 0),
                             pipeline_mode=pl.Buffered(1))
    return pl.pallas_call(
        _na_attn_lat_kernel,
        grid=(DEC_BATCH, nblk),
        in_specs=[row,
                  pl.BlockSpec((None, npair, D_MODEL, NA_KT_W), lambda b, m: (b, 0, 0, 0)),
                  pl.BlockSpec((None, DEC_SEQ, D_MODEL), lambda b, m: (b, 0, 0)),
                  pl.BlockSpec((None, D_MODEL, PAST_LEN), lambda b, m: (b, 0, 0)),
                  pl.BlockSpec((None, PAST_LEN, D_MODEL), lambda b, m: (b, 0, 0)),
                  bias_spec],
        out_specs=row,
        out_shape=jax.ShapeDtypeStruct((N_LAT, D_MODEL), BF16),
        compiler_params=_cparams(("arbitrary", "arbitrary")),
        name="na_attn_lat",
    )(q, kt.reshape(DEC_BATCH, npair, D_MODEL, NA_KT_W), v.reshape(DEC_BATCH, DEC_SEQ, D_MODEL), kct, vc, bias_tab)


def _pow2_runs(n, src, dst, largest, make_copy):
    out = []
    b = largest
    while b >= SUBLANES:
        out.append(((n & b) != 0, make_copy(src, dst, b)))
        src = src + (n & b)
        dst = dst + (n & b)
        b //= 2
    return out


def _rows8(start, size):
    return pl.ds(pl.multiple_of(start, SUBLANES), size)


def _moe_scatter_kernel(seg_ref, pad_ref, na_ref, x_ref, sc_ref, sh_ref, rt_ref, xs_ref, srt_scr, zero_scr, sems, zsem):
    i = pl.program_id(0)
    tm = x_ref.shape[0]
    slot = i % 2
    hm = (x_ref[...] * (1.0 + sc_ref[...]) + sh_ref[...]).astype(BF16)
    rtt = rt_ref[...].T
    row = lax.broadcasted_iota(jnp.int32, (MOE_LOCAL_ROWS, tm), 0).astype(F32)
    chosen = (row == rtt[6:7, :]) | (row == rtt[7:8, :])
    srt_scr[slot] = _dot(jnp.where(chosen, 1.0, 0.0).astype(BF16), hm)

    def seg_copies(tile, s):
        out = []
        for e in range(N_EXPERTS):
            base = (tile * N_EXPERTS + e) * 3
            out += _pow2_runs(
                seg_ref[base], seg_ref[base + 1], seg_ref[base + 2], tm,
                lambda src, dst, b: pltpu.make_async_copy(srt_scr.at[s, _rows8(src, b)], xs_ref.at[_rows8(dst, b)],
                                                          sems.at[s]))
        return out

    for cond, cp in seg_copies(i, slot):
        pl.when(cond)(cp.start)

    @pl.when(i > 0)
    def _():
        for cond, cp in seg_copies(i - 1, 1 - slot):
            pl.when(cond)(cp.wait)

    @pl.when(i == pl.num_programs(0) - 1)
    def _():
        zero_scr[...] = jnp.zeros_like(zero_scr)

        def pad_copies():
            out = []
            for e in range(N_EXPERTS):
                out += _pow2_runs(
                    pad_ref[N_EXPERTS + e], 0, pad_ref[e], TM_MOE // 2,
                    lambda src, dst, b: pltpu.make_async_copy(zero_scr.at[pl.ds(0, b)], xs_ref.at[_rows8(dst, b)], zsem))
            return out

        def tile_copy(j):
            rows = pl.ds(pl.multiple_of(j * TM_MOE, TM_MOE), TM_MOE)
            return pltpu.make_async_copy(zero_scr, xs_ref.at[rows], zsem)

        for cond, cp in pad_copies():
            pl.when(cond)(cp.start)
        lax.fori_loop(na_ref[0], NT_MOE, lambda j, c: (tile_copy(j).start(), c)[1], 0)
        for cond, cp in pad_copies():
            pl.when(cond)(cp.wait)
        lax.fori_loop(na_ref[0], NT_MOE, lambda j, c: (tile_copy(j).wait(), c)[1], 0)
        for cond, cp in seg_copies(i, slot):
            pl.when(cond)(cp.wait)


def _moe_scatter(seg, pad, na, x_all, rt_all, mod):
    tm = TM_PROJ
    rf = _all_row_fn(tm)
    return pl.pallas_call(
        _moe_scatter_kernel,
        grid_spec=pltpu.PrefetchScalarGridSpec(
            num_scalar_prefetch=3,
            grid=(N_TOK // tm,),
            in_specs=[pl.BlockSpec((tm, D_MODEL), lambda i, *_: (i, 0)), _mod_spec(4, rf), _mod_spec(3, rf),
                      pl.BlockSpec((tm, LANES), lambda i, *_: (i, 0))],
            out_specs=pl.BlockSpec(memory_space=pl.ANY),
            scratch_shapes=[pltpu.VMEM((2, MOE_LOCAL_ROWS, D_MODEL), F32), pltpu.VMEM((TM_MOE, D_MODEL), F32),
                            pltpu.SemaphoreType.DMA((2,)), pltpu.SemaphoreType.DMA(())],
        ),
        out_shape=jax.ShapeDtypeStruct((ROWS_SORTED, D_MODEL), F32),
        compiler_params=_cparams(("arbitrary",)),
        name="moe_scatter",
    )(seg, pad, na, x_all, mod, mod, rt_all)


def _tile_changed(te_ref, j):
    prev = te_ref[jnp.maximum(j - 1, 0)]
    return (j == 0) | (te_ref[j] != prev)


def _moe_up_kernel(te_ref, na_ref, x_ref, wg_ref, wu_ref, o_ref, w_scr):
    j = pl.program_id(1)
    tf = wg_ref.shape[1]

    @pl.when(j < na_ref[0])
    def _():
        @pl.when(_tile_changed(te_ref, j))
        def _():
            w_scr[:, :tf] = wg_ref[...].astype(BF16)
            w_scr[:, tf:] = wu_ref[...].astype(BF16)

        gu = _dot(x_ref[...].astype(BF16), w_scr[...])
        o_ref[...] = (_silu(gu[:, :tf]) * gu[:, tf:]).astype(BF16)

    @pl.when(j >= na_ref[0])
    def _():
        o_ref[...] = jnp.zeros_like(o_ref)


def _moe_up(te, na, xs, w_in):
    tm, tf = TM_MOE, TF_MOE
    nf = FFN_EXPERT // tf
    row = lambda j, na: jnp.minimum(j, na[0] - 1)
    return pl.pallas_call(
        _moe_up_kernel,
        grid_spec=pltpu.PrefetchScalarGridSpec(
            num_scalar_prefetch=2,
            grid=(nf, NT_MOE),
            in_specs=[pl.BlockSpec((tm, D_MODEL), lambda f, j, te, na: (row(j, na), 0)),
                      pl.BlockSpec((None, D_MODEL, tf), lambda f, j, te, na: (te[j], 0, f)),
                      pl.BlockSpec((None, D_MODEL, tf), lambda f, j, te, na: (te[j], 0, nf + f))],
            out_specs=pl.BlockSpec((tm, tf), lambda f, j, te, na: (j, f)),
            scratch_shapes=[pltpu.VMEM((D_MODEL, 2 * tf), BF16)],
        ),
        out_shape=jax.ShapeDtypeStruct((ROWS_SORTED, FFN_EXPERT), BF16),
        compiler_params=_cparams(("arbitrary", "arbitrary")),
        name="moe_up",
    )(te, na, xs, w_in, w_in)


def _moe_down_kernel(te_ref, na_ref, h_ref, w_ref, o_ref, w_scr):
    j = pl.program_id(0)

    @pl.when(j < na_ref[0])
    def _():
        @pl.when(_tile_changed(te_ref, j))
        def _():
            w_scr[...] = w_ref[...].astype(BF16)

        o_ref[...] = _dot(h_ref[...], w_scr[...])

    @pl.when(j >= na_ref[0])
    def _():
        o_ref[...] = jnp.zeros_like(o_ref)


def _moe_down(te, na, hmid, w_out):
    tm = TM_MOE
    row = lambda j, na: jnp.minimum(j, na[0] - 1)
    return pl.pallas_call(
        _moe_down_kernel,
        grid_spec=pltpu.PrefetchScalarGridSpec(
            num_scalar_prefetch=2,
            grid=(NT_MOE,),
            in_specs=[pl.BlockSpec((tm, FFN_EXPERT), lambda j, te, na: (row(j, na), 0)),
                      pl.BlockSpec((None, FFN_EXPERT, D_MODEL), lambda j, te, na: (te[j], 0, 0))],
            out_specs=pl.BlockSpec((tm, D_MODEL), lambda j, te, na: (j, 0)),
            scratch_shapes=[pltpu.VMEM((FFN_EXPERT, D_MODEL), BF16)],
        ),
        out_shape=jax.ShapeDtypeStruct((ROWS_SORTED, D_MODEL), F32),
        compiler_params=_cparams(("arbitrary",)),
        name="moe_down",
    )(te, na, hmid, w_out)


def _moe_combine_kernel(pos_ref, y_ref, rt_ref, x_ref, g_ref, lg_ref, lb_ref, o_ref, ybuf, sems, *, tok_off):
    i = pl.program_id(0)
    tm = x_ref.shape[0]

    def fetch(tile, slot):
        def start(t, c):
            for k in range(TOP_K):
                p = pos_ref[k * N_TOK + tok_off + tile * tm + t]
                pltpu.make_async_copy(y_ref.at[pl.ds(p, 1)], ybuf.at[slot, k, pl.ds(t, 1)], sems.at[slot]).start()
            return c

        lax.fori_loop(0, tm, start, 0, unroll=8)

    @pl.when(i == 0)
    def _():
        fetch(0, 0)

    @pl.when(i + 1 < pl.num_programs(0))
    def _():
        fetch(i + 1, (i + 1) % 2)

    slot = i % 2
    pltpu.make_async_copy(ybuf.at[slot], ybuf.at[slot], sems.at[slot]).wait()
    rt = rt_ref[...]
    moe = rt[:, 2:3] * ybuf[slot, 0] + rt[:, 3:4] * ybuf[slot, 1]
    o_ref[...] = _layer_norm(ALPHA * x_ref[...] + g_ref[...] * moe, lg_ref[...], lb_ref[...])


def _moe_combine(pos, y, rt_all, x_all, mod, lg, lb, latent):
    n = N_LAT if latent else N_CTX
    tm = TM_ROUTE
    rf = _row_fn(latent, tm)
    tok_off = N_CTX if latent else 0
    off = tok_off // tm
    return pl.pallas_call(
        functools.partial(_moe_combine_kernel, tok_off=tok_off),
        grid_spec=pltpu.PrefetchScalarGridSpec(
            num_scalar_prefetch=1,
            grid=(n // tm,),
            in_specs=[pl.BlockSpec(memory_space=pl.ANY),
                      pl.BlockSpec((tm, LANES), lambda i, pos: (i + off, 0)),
                      pl.BlockSpec((tm, D_MODEL), lambda i, pos: (i + off, 0)),
                      _mod_spec(5, rf),
                      pl.BlockSpec((1, D_MODEL), lambda i, pos: (0, 0)),
                      pl.BlockSpec((1, D_MODEL), lambda i, pos: (0, 0))],
            out_specs=pl.BlockSpec((tm, D_MODEL), lambda i, pos: (i, 0)),
            scratch_shapes=[pltpu.VMEM((2, TOP_K, tm, D_MODEL), F32), pltpu.SemaphoreType.DMA((2,))],
        ),
        out_shape=jax.ShapeDtypeStruct((n, D_MODEL), F32),
        compiler_params=_cparams(("arbitrary",)),
        name="moe_combine_lat" if latent else "moe_combine_ctx",
    )(pos, y, rt_all, x_all, mod, lg, lb)


def _routing_positions(rt_all, counts, meta):
    tm = TM_MOE
    cnt = counts[0, :N_EXPERTS].astype(jnp.int32)
    padded = ((cnt + tm - 1) // tm) * tm
    gend = jnp.cumsum(padded)
    gstart = gend - padded
    route = rt_all[:, :8].astype(jnp.int32)
    onehot = lambda idx: idx[:, None] == jnp.arange(N_EXPERTS, dtype=jnp.int32)[None, :]
    start_of = lambda idx: jnp.sum(jnp.where(onehot(idx), gstart[None, :], 0), axis=1)
    pos = jnp.concatenate([start_of(route[:, 0]) + route[:, 4], start_of(route[:, 1]) + route[:, 5]])
    pad = jnp.concatenate([gstart + cnt, padded - cnt]).astype(jnp.int32)
    tile_start = jnp.arange(NT_MOE, dtype=jnp.int32) * tm
    te = jnp.sum((tile_start[:, None] >= gend[None, :]).astype(jnp.int32), axis=1)
    n_active = (gend[-1] // tm).astype(jnp.int32)
    last_e = jnp.take(te, jnp.maximum(n_active - 1, 0))
    te = jnp.where(tile_start < gend[-1], te, last_e).astype(jnp.int32)
    m = meta[:, :3, :N_EXPERTS].astype(jnp.int32)
    seg = jnp.stack([m[:, 0], m[:, 1], m[:, 2] + gstart[None, :]], axis=-1).reshape(-1)
    return pos, seg, pad, te, n_active.reshape(1)


def kernel(x_prompt, x_sample, cache_ckv_l0, cache_krope_l0, cache_k_l1, cache_v_l1, c, c_ctx, w_ada_l0, b_ada_l0, mla_w_dq, mla_g_q, mla_w_uq, mla_w_dkv, mla_g_kv, mla_w_ukv, mla_w_o, ln1_g_l0, ln1_b_l0, ffn_w_in, ffn_w_out, ln2_g_l0, ln2_b_l0, w_ada_l1, b_ada_l1, na_w_qkv, na_rpb, na_w_o, ln1_g_l1, ln1_b_l1, moe_w_router, moe_w_in, moe_w_out, ln2_g_l1, ln2_b_l1):
    row = lambda v: v.reshape(1, -1)
    xp = x_prompt.reshape(N_CTX, D_MODEL)
    xs = x_sample.reshape(N_LAT, D_MODEL)
    groups = ((xp, False), (xs, True))

    cvecs = jnp.concatenate([c_ctx[None], c, jnp.zeros((8 - 1 - DEC_BATCH, D_MODEL), F32)], axis=0)
    mod0 = _ada_mod(cvecs, w_ada_l0, b_ada_l0)
    mod1 = _ada_mod(cvecs, w_ada_l1, b_ada_l1)

    wa = jnp.concatenate([mla_w_dq, mla_w_dkv], axis=1).astype(BF16)
    wuq = mla_w_uq.reshape(MLA_Q_LORA, MLA_HEADS, MLA_NOPE + MLA_ROPE)
    wuq = jnp.pad(wuq, ((0, 0), (0, 0), (0, Q_HEAD_PAD - MLA_NOPE - MLA_ROPE)))
    wuq = wuq.reshape(MLA_Q_LORA, MLA_HEADS * Q_HEAD_PAD).astype(BF16)
    wukv = mla_w_ukv.astype(BF16)
    wo0 = mla_w_o.astype(BF16)
    w_in0 = ffn_w_in.astype(BF16)
    w_out0 = ffn_w_out.astype(BF16)
    wqkv = na_w_qkv.astype(BF16)
    wo1 = na_w_o.astype(BF16)
    wr = jnp.pad(moe_w_router, ((0, 0), (0, LANES - N_EXPERTS))).astype(BF16)
    bias_tab = _na_bias_table(na_rpb)
    tab = _rope_tables()

    kvx_cache = _matmul(cache_ckv_l0.reshape(DEC_BATCH * PAST_LEN, MLA_KV_LORA), wukv, BF16, 512, "mla_expand_cache")
    kvx_cache = kvx_cache.reshape(DEC_BATCH, PAST_LEN, -1)
    krp_cache = jnp.pad(cache_krope_l0, ((0, 0), (0, 0), (0, LANES - MLA_ROPE))).astype(BF16)
    x1 = []
    new_ckv = new_kr = None
    vt_cache = kvx_cache.reshape(DEC_BATCH, PAST_LEN, MLA_HEADS, MLA_NOPE + MLA_V)[..., MLA_NOPE:]
    vt_cache = vt_cache.reshape(DEC_BATCH, PAST_LEN, MLA_HEADS * MLA_V).transpose(0, 2, 1)
    for x, latent in groups:
        proj = _mla_proj(x, mod0, wa, row(mla_g_q), row(mla_g_kv), wuq, wukv, tab, latent)
        if latent:
            qt, kvx, krp, vt = proj
            o = _mla_attn_lat(qt, kvx, krp, kvx_cache, krp_cache, vt, vt_cache)
        else:
            q, kvx, new_ckv, new_kr, krp = proj
            o = _mla_attn_ctx(q, kvx, krp)
        x1.append(_attn_out_ffn(o, x, wo0, mod0, row(ln1_g_l0), row(ln1_b_l0), w_in0, w_out0,
                                row(ln2_g_l0), row(ln2_b_l0), latent))

    kct = cache_k_l1.reshape(DEC_BATCH, PAST_LEN, D_MODEL).transpose(0, 2, 1).astype(BF16)
    vc = cache_v_l1.reshape(DEC_BATCH, PAST_LEN, D_MODEL).astype(BF16)
    q, v, new_kt, new_vt = _na_qkv(x1[0], mod1, wqkv, False)
    o_ctx = _na_attn_ctx(q, new_kt, v)
    q, v, kt = _na_qkv(x1[1], mod1, wqkv, True)
    o_lat = _na_attn_lat(q, kt, v, kct, vc, bias_tab)
    x2_all, rt_all, counts, meta = _proj_ln_route(o_ctx, x1[0], o_lat, x1[1], wo1, mod1, row(ln1_g_l1),
                                                  row(ln1_b_l1), wr)

    pos, seg, pad, te, n_active = _routing_positions(rt_all, counts, meta)
    x_sorted = _moe_scatter(seg, pad, n_active, x2_all, rt_all, mod1)
    hmid = _moe_up(te, n_active, x_sorted, moe_w_in)
    y = _moe_down(te, n_active, hmid, moe_w_out)
    outs = [_moe_combine(pos, y, rt_all, x2_all, mod1, row(ln2_g_l1), row(ln2_b_l1), latent)
            for latent in (False, True)]

    return (outs[0].reshape(BATCH, SEQ, D_MODEL),
            outs[1].reshape(DEC_BATCH, DEC_SEQ, D_MODEL),
            new_ckv.reshape(BATCH, SEQ, MLA_KV_LORA),
            new_kr.transpose(0, 2, 1),
            new_kt.reshape(BATCH, NA_HEADS, NA_HD, SEQ).transpose(0, 3, 1, 2),
            new_vt.reshape(BATCH, NA_HEADS, NA_HD, SEQ).transpose(0, 3, 1, 2))
```

```python
import functools
import math

import numpy as np
import jax
import jax.numpy as jnp
from jax import lax
from jax.experimental import pallas as pl
from jax.experimental.pallas import tpu as pltpu

F32 = jnp.float32
BF16 = jnp.bfloat16

D_MODEL = 1024
BATCH = 32
SEQ = 256
DEPTH = 2
DEC_BATCH = 2
DEC_SEQ = 2048
PAST_LEN = 512
GRID_W = 64
MLA_HEADS = 8
MLA_NOPE = 128
MLA_ROPE = 64
MLA_V = 128
MLA_Q_LORA = 512
MLA_KV_LORA = 256
MLA_SCALE = 1.0 / math.sqrt(MLA_NOPE + MLA_ROPE)
ROPE_THETA = 10000.0
NA_HEADS = 16
NA_HD = D_MODEL // NA_HEADS
NA_WIN_ROWS = 8
NA_WIN_COLS = 16
NA_SCALE = 1.0 / math.sqrt(NA_HD)
FFN_DENSE = 2816
N_EXPERTS = 8
TOP_K = 2
FFN_EXPERT = 3584
ALPHA = (2 * DEPTH) ** 0.25
LN_EPS = 1e-5
RMS_EPS = 1e-6

N_CTX = BATCH * SEQ
N_LAT = DEC_BATCH * DEC_SEQ
N_TOK = N_CTX + N_LAT
GRID_ROWS = DEC_SEQ // GRID_W
Q_HEAD_PAD = 256
LANES = 128
SUBLANES = 8
NEG_BIG = -1e30
MLA_KEY_CHUNK = 256
NA_PAIR = 2 * NA_HD
NA_BLK_ROWS = 4
NA_KT_W = 2 * GRID_W
NA_WIN_PAIRS = (NA_BLK_ROWS + NA_WIN_ROWS) // 2
NA_N_BLOCKS = DEC_SEQ // (NA_BLK_ROWS * GRID_W)
NA_BLOCK_VARIANTS = (0, 1, NA_N_BLOCKS - 1)

VMEM_LIMIT = 56 * 1024 * 1024

TM_PROJ = 512
TM_FFN = 512
MXU_TILE = 256
FFN_CHUNKS = ((0, 6 * MXU_TILE), (6 * MXU_TILE, FFN_DENSE))
TM_MOE = 512
TF_MOE = 1792
N_PAIRS = N_TOK * TOP_K
MOE_LOCAL_ROWS = TOP_K * TM_PROJ + N_EXPERTS * SUBLANES
N_SEG_PAD = N_EXPERTS * (N_TOK // TM_PROJ) * (SUBLANES - 1)
NT_MOE = -(-(N_PAIRS + N_SEG_PAD + N_EXPERTS * (TM_MOE - 1)) // TM_MOE)
ROWS_SORTED = NT_MOE * TM_MOE
TM_ROUTE = 256
RT_EXPERT, RT_GATE, RT_ROW, RT_LOCAL = 0, 2, 4, 6
RT_WIDTH = 8


def _cparams(sem, vmem=VMEM_LIMIT):
    return pltpu.CompilerParams(dimension_semantics=sem, vmem_limit_bytes=vmem)


def _silu(x):
    return x * jax.nn.sigmoid(x)


def _layer_norm(y, g, b):
    mu = jnp.mean(y, axis=-1, keepdims=True)
    d = y - mu
    var = jnp.mean(d * d, axis=-1, keepdims=True)
    return d * lax.rsqrt(var + LN_EPS) * g + b


def _rms_norm(y, g):
    return y * lax.rsqrt(jnp.mean(y * y, axis=-1, keepdims=True) + RMS_EPS) * g


def _dot(a, b):
    return jnp.dot(a, b, preferred_element_type=F32)


def _dot_nt(a, b):
    return lax.dot_general(a, b, (((1,), (1,)), ((), ())), preferred_element_type=F32)


def _ada_kernel(c_ref, w_ref, b_ref, o_ref):
    s = _silu(c_ref[...]).astype(BF16)
    o_ref[...] = _dot(s, w_ref[...].astype(BF16)) + b_ref[...]


def _ada_mod(cvecs, w, b):
    tn = 1536
    n_vec = 1 + DEC_BATCH
    m = pl.pallas_call(
        _ada_kernel,
        grid=(6 * D_MODEL // tn,),
        in_specs=[pl.BlockSpec((SUBLANES, D_MODEL), lambda j: (0, 0)),
                  pl.BlockSpec((D_MODEL, tn), lambda j: (0, j)),
                  pl.BlockSpec((1, tn), lambda j: (0, j))],
        out_specs=pl.BlockSpec((SUBLANES, tn), lambda j: (0, j)),
        out_shape=jax.ShapeDtypeStruct((SUBLANES, 6 * D_MODEL), F32),
        compiler_params=_cparams(("arbitrary",)),
        name="ada_mod",
    )(cvecs, w, b.reshape(1, -1))
    return m[:n_vec].reshape(n_vec * 6, 1, D_MODEL)


def _mod_spec(j, row_fn):
    return pl.BlockSpec((None, 1, D_MODEL), lambda i, *_: (row_fn(i) * 6 + j, 0, 0))


def _row_fn(latent, tm):
    if not latent:
        return lambda i: 0
    per = DEC_SEQ // tm
    return lambda i: 1 + i // per


def _const_spec(shape):
    nd = len(shape)
    return pl.BlockSpec(shape, lambda *_: (0,) * nd, pipeline_mode=pl.Buffered(1))


def _rope_kernel(invf_ref, o_ref):
    i = pl.program_id(0)
    tm = o_ref.shape[1]
    t = i * tm + lax.broadcasted_iota(jnp.int32, (tm, LANES), 0)
    lane = lax.broadcasted_iota(jnp.int32, (tm, LANES), 1)
    row = t >> int(math.log2(GRID_W))
    col = t & (GRID_W - 1)
    pos = jnp.where(lane < MLA_ROPE // 2, row, col).astype(F32)
    ang = pos * invf_ref[...]
    cos = jnp.cos(ang)
    sin = jnp.sin(ang)
    unit = lane >> int(math.log2(MLA_ROPE // 4))
    first = (unit == 0) | (unit == 2)
    second = (unit == 1) | (unit == 3)
    o_ref[0] = jnp.where(lane < MLA_ROPE, cos, 0.0)
    o_ref[1] = jnp.where(first, -sin, 0.0)
    o_ref[2] = jnp.where(second, sin, 0.0)


def _rope_tables():
    half = MLA_ROPE // 2
    inv_freq = (1.0 / (ROPE_THETA ** (np.arange(0, half, 2, dtype=np.float32) / half))).astype(np.float32)
    lane_f = np.zeros((1, LANES), np.float32)
    lane_f[0, :MLA_ROPE] = np.tile(inv_freq, 4)
    tm = 256
    return pl.pallas_call(
        _rope_kernel,
        grid=(DEC_SEQ // tm,),
        in_specs=[_const_spec((1, LANES))],
        out_specs=pl.BlockSpec((3, tm, LANES), lambda i: (0, i, 0)),
        out_shape=jax.ShapeDtypeStruct((3, DEC_SEQ, LANES), F32),
        compiler_params=_cparams(("arbitrary",)),
        name="rope_tables",
    )(jnp.asarray(lane_f))


def _rotate(v, tab_ref):
    return (v * tab_ref[0] + pltpu.roll(v, LANES - MLA_ROPE // 4, 1) * tab_ref[1]
            + pltpu.roll(v, MLA_ROPE // 4, 1) * tab_ref[2])


def _mla_proj_kernel(*refs, rope):
    if rope:
        (x_ref, sc_ref, sh_ref, wa_ref, gq_ref, gkv_ref, wuq_ref, wukv_ref, tab_ref,
         qt_ref, kvx_ref, krp_ref, vt_ref) = refs
    else:
        (x_ref, sc_ref, sh_ref, wa_ref, gq_ref, gkv_ref, wuq_ref, wukv_ref,
         q_ref, kvx_ref, ckv_ref, kr_ref, krp_ref) = refs
    h = (x_ref[...] * (1.0 + sc_ref[...]) + sh_ref[...]).astype(BF16)
    t = _dot(h, wa_ref[...])
    cq = _rms_norm(t[:, :MLA_Q_LORA], gq_ref[...])
    ckv = _rms_norm(t[:, MLA_Q_LORA:MLA_Q_LORA + MLA_KV_LORA], gkv_ref[...])
    kr = t[:, MLA_Q_LORA + MLA_KV_LORA:]
    kvx = _dot(ckv.astype(BF16), wukv_ref[...])
    kvx_ref[...] = kvx.astype(BF16)
    q = _dot(cq.astype(BF16), wuq_ref[...])
    krp = jnp.concatenate([kr, jnp.zeros_like(kr)], axis=-1)
    if rope:
        krp_ref[...] = _rotate(krp, tab_ref).astype(BF16)
        parts = []
        for hd in range(MLA_HEADS):
            lo = hd * Q_HEAD_PAD
            parts += [q[:, lo:lo + MLA_NOPE], _rotate(q[:, lo + MLA_NOPE:lo + Q_HEAD_PAD], tab_ref)]
        qt_ref[...] = jnp.concatenate(parts, axis=1).T.astype(BF16)
        vcols = [kvx[:, hd * Q_HEAD_PAD + MLA_NOPE:(hd + 1) * Q_HEAD_PAD] for hd in range(MLA_HEADS)]
        vt_ref[...] = jnp.concatenate(vcols, axis=1).T.astype(BF16)
    else:
        ckv_ref[...] = ckv
        krt = kr.T
        for t in range(kr_ref.shape[0]):
            kr_ref[t] = krt[:, t * SEQ:(t + 1) * SEQ]
        krp_ref[...] = krp.astype(BF16)
        q_ref[...] = q.astype(BF16)


def _mla_proj(x, mod, wa, gq, gkv, wuq, wukv, tab, latent):
    n = x.shape[0]
    tm = TM_PROJ
    rf = _row_fn(latent, tm)
    tok = lambda w: pl.BlockSpec((tm, w), lambda i: (i, 0))
    in_specs = [tok(D_MODEL), _mod_spec(1, rf), _mod_spec(0, rf),
                _const_spec(wa.shape), _const_spec(gq.shape), _const_spec(gkv.shape),
                _const_spec(wuq.shape), _const_spec(wukv.shape)]
    args = [x, mod, mod, wa, gq, gkv, wuq, wukv]
    wq, wkv = MLA_HEADS * Q_HEAD_PAD, MLA_HEADS * (MLA_NOPE + MLA_V)
    if latent:
        per = DEC_SEQ // tm
        in_specs.append(pl.BlockSpec((3, tm, LANES), lambda i: (0, i % per, 0)))
        args.append(tab)
        out_specs = [pl.BlockSpec((wq, tm), lambda i: (0, i)), tok(wkv), tok(LANES),
                     pl.BlockSpec((None, MLA_HEADS * MLA_V, tm), lambda i: (i // per, 0, i % per))]
        out_shape = [jax.ShapeDtypeStruct((wq, n), BF16), jax.ShapeDtypeStruct((n, wkv), BF16),
                     jax.ShapeDtypeStruct((n, LANES), BF16),
                     jax.ShapeDtypeStruct((DEC_BATCH, MLA_HEADS * MLA_V, DEC_SEQ), BF16)]
    else:
        out_specs = [tok(wq), tok(wkv), tok(MLA_KV_LORA),
                     pl.BlockSpec((tm // SEQ, MLA_ROPE, SEQ), lambda i: (i, 0, 0)), tok(LANES)]
        out_shape = [jax.ShapeDtypeStruct((n, wq), BF16), jax.ShapeDtypeStruct((n, wkv), BF16),
                     jax.ShapeDtypeStruct((n, MLA_KV_LORA), F32), jax.ShapeDtypeStruct((n // SEQ, MLA_ROPE, SEQ), F32),
                     jax.ShapeDtypeStruct((n, LANES), BF16)]
    return pl.pallas_call(
        functools.partial(_mla_proj_kernel, rope=latent),
        grid=(n // tm,),
        in_specs=in_specs,
        out_specs=out_specs,
        out_shape=out_shape,
        compiler_params=_cparams(("arbitrary",)),
        name="mla_proj_lat" if latent else "mla_proj_ctx",
    )(*args)


def _matmul_kernel(a_ref, b_ref, o_ref):
    o_ref[...] = _dot(a_ref[...].astype(BF16), b_ref[...]).astype(o_ref.dtype)


def _matmul(a, b, out_dtype, tm, name):
    m, k = a.shape
    n = b.shape[1]
    return pl.pallas_call(
        _matmul_kernel,
        grid=(m // tm,),
        in_specs=[pl.BlockSpec((tm, k), lambda i: (i, 0)), _const_spec(b.shape)],
        out_specs=pl.BlockSpec((tm, n), lambda i: (i, 0)),
        out_shape=jax.ShapeDtypeStruct((m, n), out_dtype),
        compiler_params=_cparams(("arbitrary",)),
        name=name,
    )(a, b)


def _mla_head(qh, kparts, vparts):
    ss = [_dot_nt(qh, k) for k in kparts]
    m = functools.reduce(jnp.maximum, [jnp.max(s, axis=-1, keepdims=True) for s in ss])
    ps = [jnp.exp2((s - m) * (MLA_SCALE * math.log2(math.e))) for s in ss]
    l = functools.reduce(jnp.add, [jnp.sum(p, axis=-1, keepdims=True) for p in ps])
    o = functools.reduce(jnp.add, [_dot(p.astype(BF16), v) for p, v in zip(ps, vparts)])
    return o / l


def _mla_attn_ctx_kernel(q_ref, kvx_ref, krp_ref, o_ref):
    krp = krp_ref[...]
    for hd in range(MLA_HEADS):
        lo = hd * (MLA_NOPE + MLA_V)
        qh = q_ref[:, hd * Q_HEAD_PAD:(hd + 1) * Q_HEAD_PAD]
        kh = jnp.concatenate([kvx_ref[:, lo:lo + MLA_NOPE], krp], axis=-1)
        vh = kvx_ref[:, lo + MLA_NOPE:lo + MLA_NOPE + MLA_V]
        o_ref[:, hd * MLA_V:(hd + 1) * MLA_V] = _mla_head(qh, [kh], [vh]).astype(BF16)


def _mla_attn_ctx(q, kvx, krp):
    tok = lambda w: pl.BlockSpec((SEQ, w), lambda b: (b, 0))
    return pl.pallas_call(
        _mla_attn_ctx_kernel,
        grid=(BATCH,),
        in_specs=[tok(q.shape[1]), tok(kvx.shape[1]), tok(LANES)],
        out_specs=tok(MLA_HEADS * MLA_V),
        out_shape=jax.ShapeDtypeStruct((N_CTX, MLA_HEADS * MLA_V), BF16),
        compiler_params=_cparams(("arbitrary",)),
        name="mla_attn_ctx",
    )(q, kvx, krp)


def _mla_attn_lat_kernel(qt_ref, kvl_ref, krl_ref, kvc_ref, krc_ref, vtl_ref, vtc_ref, o_ref):
    chunks = [(kvl_ref, krl_ref, vtl_ref, c) for c in range(DEC_SEQ // MLA_KEY_CHUNK)]
    chunks += [(kvc_ref, krc_ref, vtc_ref, c) for c in range(PAST_LEN // MLA_KEY_CHUNK)]
    for hd in range(MLA_HEADS):
        lo = hd * (MLA_NOPE + MLA_V)
        qt = qt_ref[hd * Q_HEAD_PAD:(hd + 1) * Q_HEAD_PAD, :]
        ss = []
        for kv_ref, kr_ref, _, c in chunks:
            rows = slice(c * MLA_KEY_CHUNK, (c + 1) * MLA_KEY_CHUNK)
            ss.append(_dot(jnp.concatenate([kv_ref[rows, lo:lo + MLA_NOPE], kr_ref[rows, :]], axis=-1), qt))
        m = functools.reduce(jnp.maximum, [jnp.max(s, axis=0, keepdims=True) for s in ss])
        ps = [jnp.exp2((s - m) * (MLA_SCALE * math.log2(math.e))) for s in ss]
        l = functools.reduce(jnp.add, [jnp.sum(p, axis=0, keepdims=True) for p in ps])
        ots = [_dot(vt_ref[hd * MLA_V:(hd + 1) * MLA_V, c * MLA_KEY_CHUNK:(c + 1) * MLA_KEY_CHUNK], p.astype(BF16))
               for (_, _, vt_ref, c), p in zip(chunks, ps)]
        ot = functools.reduce(jnp.add, ots) / l
        o_ref[:, hd * MLA_V:(hd + 1) * MLA_V] = ot.T.astype(BF16)


def _mla_attn_lat(qt, kvx_lat, krp_lat, kvx_ctx, krp_ctx, vt_lat, vt_ctx):
    tq = 256
    per = DEC_SEQ // tq
    wkv = kvx_lat.shape[-1]
    wv = MLA_HEADS * MLA_V
    batch = lambda rows, cols: pl.BlockSpec((None, rows, cols), lambda b, i: (b, 0, 0))
    return pl.pallas_call(
        _mla_attn_lat_kernel,
        grid=(DEC_BATCH, per),
        in_specs=[pl.BlockSpec((qt.shape[0], tq), lambda b, i: (0, b * per + i)),
                  batch(DEC_SEQ, wkv), batch(DEC_SEQ, LANES), batch(PAST_LEN, wkv), batch(PAST_LEN, LANES),
                  batch(wv, DEC_SEQ), batch(wv, PAST_LEN)],
        out_specs=pl.BlockSpec((tq, wv), lambda b, i: (b * per + i, 0)),
        out_shape=jax.ShapeDtypeStruct((N_LAT, wv), BF16),
        compiler_params=_cparams(("arbitrary", "arbitrary")),
        name="mla_attn_lat",
    )(qt, kvx_lat.reshape(DEC_BATCH, DEC_SEQ, wkv), krp_lat.reshape(DEC_BATCH, DEC_SEQ, LANES), kvx_ctx, krp_ctx,
      vt_lat, vt_ctx)


def _top2(logits):
    lane = lax.broadcasted_iota(jnp.int32, logits.shape, 1).astype(F32)
    m1 = jnp.max(logits, axis=-1, keepdims=True)
    i1 = jnp.min(jnp.where(logits == m1, lane, float(LANES)), axis=-1, keepdims=True)
    rest = jnp.where(lane == i1, -jnp.inf, logits)
    m2 = jnp.max(rest, axis=-1, keepdims=True)
    i2 = jnp.min(jnp.where(rest == m2, lane, float(LANES)), axis=-1, keepdims=True)
    e = jnp.exp(m2 - m1)
    w1 = 1.0 / (1.0 + e)
    w2 = e / (1.0 + e)
    return lane, i1, i2, w1, w2


def _proj_ln_route_kernel(oc_ref, xc_ref, ol_ref, xl_ref, wo_ref, g_ref, lg_ref, lb_ref, sc_ref, sh_ref, wr_ref,
                          x1_ref, rt_ref, cnt_ref, meta_ref, carry, *, n_ctx_tiles):
    i = pl.program_id(0)

    @pl.when(i == 0)
    def _():
        carry[...] = jnp.zeros_like(carry)

    def body(o_ref, x_ref):
        out = _dot(o_ref[...], wo_ref[...])
        x1 = _layer_norm(ALPHA * x_ref[...] + g_ref[...] * out, lg_ref[...], lb_ref[...])
        x1_ref[...] = x1
        hm = x1 * (1.0 + sc_ref[...]) + sh_ref[...]
        logits = _dot(hm.astype(BF16), wr_ref[...])
        lane = lax.broadcasted_iota(jnp.int32, logits.shape, 1)
        logits = jnp.where(lane < N_EXPERTS, logits, -jnp.inf)
        lane_f, i1, i2, w1, w2 = _top2(logits)
        tm = logits.shape[0]
        oh1 = (lane_f == i1).astype(F32)
        oh2 = (lane_f == i2).astype(F32)
        rr = lax.broadcasted_iota(jnp.int32, (tm, tm), 0)
        cc = lax.broadcasted_iota(jnp.int32, (tm, tm), 1)
        below = (cc < rr).astype(BF16)
        tot1 = jnp.sum(oh1, axis=0, keepdims=True)
        tot2 = jnp.sum(oh2, axis=0, keepdims=True)
        seg_len = jnp.floor((tot1 + tot2 + (SUBLANES - 1)) * (1.0 / SUBLANES)) * SUBLANES
        e_row = lax.broadcasted_iota(jnp.int32, (LANES, LANES), 0)
        e_col = lax.broadcasted_iota(jnp.int32, (LANES, LANES), 1)
        before = (e_row < e_col).astype(BF16)
        seg_loc = _dot(jnp.broadcast_to(seg_len, (SUBLANES, LANES)).astype(BF16), before)[0:1]
        seg_glob = carry[...]
        cum1 = _dot(below, oh1.astype(BF16))
        cum2 = _dot(below, oh2.astype(BF16)) + tot1
        pick = lambda oh, v: jnp.sum(oh * v, axis=-1, keepdims=True)
        vals = {RT_EXPERT: (i1, i2), RT_GATE: (w1, w2),
                RT_ROW: (pick(oh1, cum1 + seg_glob), pick(oh2, cum2 + seg_glob)),
                RT_LOCAL: (pick(oh1, cum1 + seg_loc), pick(oh2, cum2 + seg_loc))}
        rt = jnp.zeros_like(logits)
        for first_lane, pair in vals.items():
            for k, val in enumerate(pair):
                rt = jnp.where(lane == first_lane + k, val, rt)
        rt_ref[...] = rt
        sub = lax.broadcasted_iota(jnp.int32, (SUBLANES, LANES), 0)
        meta_ref[...] = jnp.where(sub == 0, seg_len, jnp.where(sub == 1, seg_loc, jnp.where(sub == 2, seg_glob, 0.0)))
        carry[...] = seg_glob + seg_len
        cnt_ref[...] = jnp.broadcast_to(carry[...], cnt_ref.shape)

    is_ctx = i < n_ctx_tiles
    pl.when(is_ctx)(lambda: body(oc_ref, xc_ref))
    pl.when(jnp.logical_not(is_ctx))(lambda: body(ol_ref, xl_ref))


def _all_row_fn(tm):
    nc = N_CTX // tm
    per = DEC_SEQ // tm
    return lambda i: jnp.where(i < nc, 0, 1 + (i - nc) // per)


def _proj_ln_route(o_ctx, x_ctx, o_lat, x_lat, wo, mod, lg, lb, wr):
    tm = TM_PROJ
    nc = N_CTX // tm
    rf = _all_row_fn(tm)
    ctx = lambda w: pl.BlockSpec((tm, w), lambda i: (jnp.minimum(i, nc - 1), 0))
    lat = lambda w: pl.BlockSpec((tm, w), lambda i: (jnp.maximum(i - nc, 0), 0))
    tok = lambda w: pl.BlockSpec((tm, w), lambda i: (i, 0))
    return pl.pallas_call(
        functools.partial(_proj_ln_route_kernel, n_ctx_tiles=nc),
        grid=(N_TOK // tm,),
        in_specs=[ctx(D_MODEL), ctx(D_MODEL), lat(D_MODEL), lat(D_MODEL), _const_spec(wo.shape), _mod_spec(2, rf),
                  _const_spec(lg.shape), _const_spec(lb.shape), _mod_spec(4, rf), _mod_spec(3, rf),
                  _const_spec(wr.shape)],
        out_specs=[tok(D_MODEL), tok(LANES), pl.BlockSpec((SUBLANES, LANES), lambda i: (0, 0)),
                   pl.BlockSpec((None, SUBLANES, LANES), lambda i: (i, 0, 0))],
        out_shape=[jax.ShapeDtypeStruct((N_TOK, D_MODEL), F32),
                   jax.ShapeDtypeStruct((N_TOK, LANES), F32),
                   jax.ShapeDtypeStruct((SUBLANES, LANES), F32),
                   jax.ShapeDtypeStruct((N_TOK // tm, SUBLANES, LANES), F32)],
        scratch_shapes=[pltpu.VMEM((1, LANES), F32)],
        compiler_params=_cparams(("arbitrary",)),
        name="proj_ln_route",
    )(o_ctx, x_ctx, o_lat, x_lat, wo, mod, lg, lb, mod, mod, wr)


def _attn_out_ffn_kernel(o_ref, x_ref, wo_ref, g1_ref, lg1_ref, lb1_ref, sc_ref, sh_ref, g2_ref,
                         wi_ref, wd_ref, lg2_ref, lb2_ref, y_ref):
    x1 = _layer_norm(ALPHA * x_ref[...] + g1_ref[...] * _dot(o_ref[...], wo_ref[...]), lg1_ref[...], lb1_ref[...])
    h = (x1 * (1.0 + sc_ref[...]) + sh_ref[...]).astype(BF16)
    acc = None
    for lo, hi in FFN_CHUNKS:
        gate = _dot(h, wi_ref[:, lo:hi])
        up = _dot(h, wi_ref[:, FFN_DENSE + lo:FFN_DENSE + hi])
        part = _dot((_silu(gate) * up).astype(BF16), wd_ref[lo:hi, :])
        acc = part if acc is None else acc + part
    y_ref[...] = _layer_norm(ALPHA * x1 + g2_ref[...] * acc, lg2_ref[...], lb2_ref[...])


def _attn_out_ffn(o, x, wo, mod, lg1, lb1, w_in, w_out, lg2, lb2, latent):
    n = x.shape[0]
    tm = TM_FFN
    rf = _row_fn(latent, tm)
    tok = lambda w: pl.BlockSpec((tm, w), lambda i: (i, 0))
    vec = _const_spec((1, D_MODEL))
    return pl.pallas_call(
        _attn_out_ffn_kernel,
        grid=(n // tm,),
        in_specs=[tok(o.shape[1]), tok(D_MODEL), _const_spec(wo.shape), _mod_spec(2, rf), vec, vec,
                  _mod_spec(4, rf), _mod_spec(3, rf), _mod_spec(5, rf),
                  _const_spec(w_in.shape), _const_spec(w_out.shape), vec, vec],
        out_specs=tok(D_MODEL),
        out_shape=jax.ShapeDtypeStruct((n, D_MODEL), F32),
        compiler_params=_cparams(("arbitrary",)),
        name="attn_out_ffn_lat" if latent else "attn_out_ffn_ctx",
    )(o, x, wo, mod, lg1, lb1, mod, mod, mod, w_in, w_out, lg2, lb2)


def _na_qkv_kernel(x_ref, sc_ref, sh_ref, w_ref, q_ref, v_ref, kt_ref, *vt_refs):
    h = (x_ref[...] * (1.0 + sc_ref[...]) + sh_ref[...]).astype(BF16)
    qkv = _dot(h, w_ref[...])
    q_ref[...] = (qkv[:, :D_MODEL] * NA_SCALE).astype(BF16)
    v = qkv[:, 2 * D_MODEL:]
    v_ref[...] = v.astype(BF16)
    pairs = [(kt_ref, qkv[:, D_MODEL:2 * D_MODEL])] + [(ref, v) for ref in vt_refs]
    for ref, val in pairs:
        valt = val.T
        nblk, _, w = ref.shape
        for t in range(nblk):
            ref[t] = valt[:, t * w:(t + 1) * w].astype(ref.dtype)


def _na_qkv(x, mod, w, latent):
    n = x.shape[0]
    tm = TM_PROJ
    kt_w = NA_KT_W if latent else SEQ
    rf = _row_fn(latent, tm)
    tok = pl.BlockSpec((tm, D_MODEL), lambda i: (i, 0))
    tspec = pl.BlockSpec((tm // kt_w, D_MODEL, kt_w), lambda i: (i, 0, 0))
    tshape = lambda dt: jax.ShapeDtypeStruct((n // kt_w, D_MODEL, kt_w), dt)
    out_specs = [tok, tok, tspec]
    out_shape = [jax.ShapeDtypeStruct((n, D_MODEL), BF16), jax.ShapeDtypeStruct((n, D_MODEL), BF16),
                 tshape(BF16 if latent else F32)]
    if not latent:
        out_specs.append(tspec)
        out_shape.append(tshape(F32))
    return pl.pallas_call(
        _na_qkv_kernel,
        grid=(n // tm,),
        in_specs=[tok, _mod_spec(1, rf), _mod_spec(0, rf), _const_spec(w.shape)],
        out_specs=out_specs,
        out_shape=out_shape,
        compiler_params=_cparams(("arbitrary",)),
        name="na_qkv_lat" if latent else "na_qkv_ctx",
    )(x, mod, mod, w)


def _softmax_pv(scores, values):
    m = functools.reduce(jnp.maximum, [jnp.max(s, axis=-1, keepdims=True) for s in scores])
    ps = [jnp.exp(s - m) for s in scores]
    l = functools.reduce(jnp.add, [jnp.sum(p, axis=-1, keepdims=True) for p in ps])
    o = functools.reduce(jnp.add, [_dot(p.astype(BF16), v) for p, v in zip(ps, values)])
    return o / l


def _head_of_pair(x, half):
    lane = lax.broadcasted_iota(jnp.int32, x.shape, 1)
    keep = (lane < NA_HD) if half == 0 else (lane >= NA_HD)
    return jnp.where(keep, x, jnp.zeros_like(x))


def _merge_pair(o0, o1):
    lane = lax.broadcasted_iota(jnp.int32, o0.shape, 1)
    return jnp.where(lane < NA_HD, o0, o1)


def _na_attn_ctx_kernel(q_ref, kt_ref, v_ref, o_ref):
    for p in range(NA_HEADS // 2):
        cols = slice(p * NA_PAIR, (p + 1) * NA_PAIR)
        qp = q_ref[:, cols]
        ktp = kt_ref[cols, :].astype(BF16)
        vp = v_ref[:, cols]
        outs = [_softmax_pv([_dot(_head_of_pair(qp, half), ktp)], [vp]) for half in range(2)]
        o_ref[:, cols] = _merge_pair(*outs).astype(BF16)


def _na_attn_ctx(q, kt, v):
    tok = pl.BlockSpec((SEQ, D_MODEL), lambda b: (b, 0))
    return pl.pallas_call(
        _na_attn_ctx_kernel,
        grid=(BATCH,),
        in_specs=[tok, pl.BlockSpec((None, D_MODEL, SEQ), lambda b: (b, 0, 0)), tok],
        out_specs=tok,
        out_shape=jax.ShapeDtypeStruct((N_CTX, D_MODEL), BF16),
        compiler_params=_cparams(("arbitrary",)),
        name="na_attn_ctx",
    )(q, kt, v)


def _na_win_start(m, clip=jnp.clip):
    return clip(m * NA_BLK_ROWS // 2 - NA_WIN_ROWS // 4, 0, GRID_ROWS // 2 - NA_WIN_PAIRS)


def _na_variant(m):
    return jnp.where(m == 0, 0, jnp.where(m == NA_N_BLOCKS - 1, 2, 1))


def _na_attn_lat_kernel(q_ref, kt_ref, v_ref, kct_ref, vc_ref, bias_ref, o_ref):
    rp0 = _na_win_start(pl.program_id(1))
    win = pl.ds(pl.multiple_of(rp0 * NA_KT_W, NA_KT_W), NA_WIN_PAIRS * NA_KT_W)
    for p in range(NA_HEADS // 2):
        cols = slice(p * NA_PAIR, (p + 1) * NA_PAIR)
        qp = q_ref[:, cols]
        kw = jnp.concatenate([kt_ref[rp0 + t, cols, :] for t in range(NA_WIN_PAIRS)], axis=1)
        kc = kct_ref[cols, :]
        vw = v_ref[win, cols]
        vc = vc_ref[:, cols]
        outs = []
        for half in range(2):
            qh = _head_of_pair(qp, half)
            s_nb = _dot(qh, kw) + bias_ref[2 * p + half]
            outs.append(_softmax_pv([s_nb, _dot(qh, kc)], [vw, vc]))
        o_ref[:, cols] = _merge_pair(*outs).astype(BF16)


def _na_bias_kernel(e_ref, e64_ref, o_ref):
    n_dr = 2 * NA_WIN_ROWS - 1
    shape = (GRID_W, LANES)
    lane = lax.broadcasted_iota(jnp.int32, shape, 1)
    c = lax.broadcasted_iota(jnp.int32, shape, 0)
    kc = lane & (GRID_W - 1)
    cs = jnp.clip(c - NA_WIN_COLS // 2, 0, GRID_W - NA_WIN_COLS)
    valid = (kc >= cs) & (kc < cs + NA_WIN_COLS)
    toeplitz = lambda ref, a: pltpu.roll(jnp.broadcast_to(ref[a:a + 1, :], shape), 0, 1, stride=1, stride_axis=0)
    rows = ([toeplitz(e_ref, a) for a in range(n_dr)], [toeplitz(e64_ref, a) for a in range(n_dr)])
    masked = jnp.full(shape, NEG_BIG, F32)
    for v, m in enumerate(NA_BLOCK_VARIANTS):
        rp0 = _na_win_start(m, clip=lambda x, lo, hi: min(max(x, lo), hi))
        for i in range(NA_BLK_ROWS):
            r = NA_BLK_ROWS * m + i
            rs = min(max(r - NA_WIN_ROWS // 2, 0), GRID_ROWS - NA_WIN_ROWS)
            for t in range(NA_WIN_PAIRS):
                halves = []
                for u in range(2):
                    kr = 2 * (rp0 + t) + u
                    halves.append(rows[u][kr - r + NA_WIN_ROWS - 1] if rs <= kr < rs + NA_WIN_ROWS else masked)
                o_ref[v, i * GRID_W:(i + 1) * GRID_W, t * LANES:(t + 1) * LANES] = jnp.where(
                    valid, jnp.where(lane < GRID_W, halves[0], halves[1]), NEG_BIG)


def _na_bias_table(rpb):
    n_dr = 2 * NA_WIN_ROWS - 1
    blk = NA_BLK_ROWS * GRID_W
    e = jnp.zeros((NA_HEADS, n_dr, LANES), F32)
    e = e.at[:, :, :NA_WIN_COLS].set(rpb[:, :, NA_WIN_COLS - 1:])
    e = e.at[:, :, LANES - (NA_WIN_COLS - 1):].set(rpb[:, :, :NA_WIN_COLS - 1])
    e64 = jnp.roll(e, GRID_W, axis=-1)
    spec = pl.BlockSpec((None, n_dr, LANES), lambda h: (h, 0, 0))
    return pl.pallas_call(
        _na_bias_kernel,
        grid=(NA_HEADS,),
        in_specs=[spec, spec],
        out_specs=pl.BlockSpec((None, len(NA_BLOCK_VARIANTS), blk, NA_WIN_PAIRS * NA_KT_W), lambda h: (h, 0, 0, 0)),
        out_shape=jax.ShapeDtypeStruct((NA_HEADS, len(NA_BLOCK_VARIANTS), blk, NA_WIN_PAIRS * NA_KT_W), F32),
        compiler_params=_cparams(("arbitrary",)),
        name="na_bias_table",
    )(e, e64)


def _na_attn_lat(q, kt, v, kct, vc, bias_tab):
    blk = NA_BLK_ROWS * GRID_W
    nblk = NA_N_BLOCKS
    npair = DEC_SEQ // NA_KT_W
    row = pl.BlockSpec((blk, D_MODEL), lambda b, m: (b * nblk + m, 0))
    per_batch = lambda *shape: pl.BlockSpec((None,) + shape, lambda b, m: (b,) + (0,) * len(shape),
                                            pipeline_mode=pl.Buffered(1))
    bias_spec = pl.BlockSpec((NA_HEADS, None, blk, NA_WIN_PAIRS * NA_KT_W), lambda b, m: (0, _na_variant(m), 0, 0))
    return pl.pallas_call(
        _na_attn_lat_kernel,
        grid=(DEC_BATCH, nblk),
        in_specs=[row, per_batch(npair, D_MODEL, NA_KT_W), per_batch(DEC_SEQ, D_MODEL),
                  per_batch(D_MODEL, PAST_LEN), per_batch(PAST_LEN, D_MODEL), bias_spec],
        out_specs=row,
        out_shape=jax.ShapeDtypeStruct((N_LAT, D_MODEL), BF16),
        compiler_params=_cparams(("arbitrary", "arbitrary")),
        name="na_attn_lat",
    )(q, kt.reshape(DEC_BATCH, npair, D_MODEL, NA_KT_W), v.reshape(DEC_BATCH, DEC_SEQ, D_MODEL), kct, vc, bias_tab)


def _pow2_runs(n, src, dst, largest, make_copy):
    out = []
    b = largest
    while b >= SUBLANES:
        out.append(((n & b) != 0, make_copy(src, dst, b)))
        src = src + (n & b)
        dst = dst + (n & b)
        b //= 2
    return out


def _rows8(start, size):
    return pl.ds(pl.multiple_of(start, SUBLANES), size)


def _moe_scatter_kernel(seg_ref, pad_ref, na_ref, x_ref, sc_ref, sh_ref, rt_ref, xs_ref, srt_scr, zero_scr, sems, zsem):
    i = pl.program_id(0)
    tm = x_ref.shape[0]
    slot = i % 2
    hm = (x_ref[...] * (1.0 + sc_ref[...]) + sh_ref[...]).astype(BF16)
    rtt = rt_ref[...].T
    row = lax.broadcasted_iota(jnp.int32, (MOE_LOCAL_ROWS, tm), 0).astype(F32)
    chosen = (row == rtt[RT_LOCAL:RT_LOCAL + 1, :]) | (row == rtt[RT_LOCAL + 1:RT_LOCAL + 2, :])
    srt_scr[slot] = _dot(jnp.where(chosen, 1.0, 0.0).astype(BF16), hm)

    def seg_copies(tile, s):
        out = []
        for e in range(N_EXPERTS):
            base = (tile * N_EXPERTS + e) * 3
            out += _pow2_runs(
                seg_ref[base], seg_ref[base + 1], seg_ref[base + 2], tm,
                lambda src, dst, b: pltpu.make_async_copy(srt_scr.at[s, _rows8(src, b)], xs_ref.at[_rows8(dst, b)],
                                                          sems.at[s]))
        return out

    for cond, cp in seg_copies(i, slot):
        pl.when(cond)(cp.start)

    @pl.when(i > 0)
    def _():
        for cond, cp in seg_copies(i - 1, 1 - slot):
            pl.when(cond)(cp.wait)

    @pl.when(i == pl.num_programs(0) - 1)
    def _():
        zero_scr[...] = jnp.zeros_like(zero_scr)

        def pad_copies():
            out = []
            for e in range(N_EXPERTS):
                out += _pow2_runs(
                    pad_ref[N_EXPERTS + e], 0, pad_ref[e], TM_MOE // 2,
                    lambda src, dst, b: pltpu.make_async_copy(zero_scr.at[pl.ds(0, b)], xs_ref.at[_rows8(dst, b)], zsem))
            return out

        def tile_copy(j):
            rows = pl.ds(pl.multiple_of(j * TM_MOE, TM_MOE), TM_MOE)
            return pltpu.make_async_copy(zero_scr, xs_ref.at[rows], zsem)

        for cond, cp in pad_copies():
            pl.when(cond)(cp.start)
        lax.fori_loop(na_ref[0], NT_MOE, lambda j, c: (tile_copy(j).start(), c)[1], 0)
        for cond, cp in pad_copies():
            pl.when(cond)(cp.wait)
        lax.fori_loop(na_ref[0], NT_MOE, lambda j, c: (tile_copy(j).wait(), c)[1], 0)
        for cond, cp in seg_copies(i, slot):
            pl.when(cond)(cp.wait)


def _moe_scatter(seg, pad, na, x_all, rt_all, mod):
    tm = TM_PROJ
    rf = _all_row_fn(tm)
    return pl.pallas_call(
        _moe_scatter_kernel,
        grid_spec=pltpu.PrefetchScalarGridSpec(
            num_scalar_prefetch=3,
            grid=(N_TOK // tm,),
            in_specs=[pl.BlockSpec((tm, D_MODEL), lambda i, *_: (i, 0)), _mod_spec(4, rf), _mod_spec(3, rf),
                      pl.BlockSpec((tm, LANES), lambda i, *_: (i, 0))],
            out_specs=pl.BlockSpec(memory_space=pl.ANY),
            scratch_shapes=[pltpu.VMEM((2, MOE_LOCAL_ROWS, D_MODEL), F32), pltpu.VMEM((TM_MOE, D_MODEL), F32),
                            pltpu.SemaphoreType.DMA((2,)), pltpu.SemaphoreType.DMA(())],
        ),
        out_shape=jax.ShapeDtypeStruct((ROWS_SORTED, D_MODEL), F32),
        compiler_params=_cparams(("arbitrary",)),
        name="moe_scatter",
    )(seg, pad, na, x_all, mod, mod, rt_all)


def _tile_changed(te_ref, j):
    prev = te_ref[jnp.maximum(j - 1, 0)]
    return (j == 0) | (te_ref[j] != prev)


def _moe_up_kernel(te_ref, na_ref, x_ref, wg_ref, wu_ref, o_ref, w_scr):
    j = pl.program_id(1)
    tf = wg_ref.shape[1]

    @pl.when(j < na_ref[0])
    def _():
        @pl.when(_tile_changed(te_ref, j))
        def _():
            w_scr[:, :tf] = wg_ref[...].astype(BF16)
            w_scr[:, tf:] = wu_ref[...].astype(BF16)

        gu = _dot(x_ref[...].astype(BF16), w_scr[...])
        o_ref[...] = (_silu(gu[:, :tf]) * gu[:, tf:]).astype(BF16)

    @pl.when(j >= na_ref[0])
    def _():
        o_ref[...] = jnp.zeros_like(o_ref)


def _moe_up(te, na, xs, w_in):
    tm, tf = TM_MOE, TF_MOE
    nf = FFN_EXPERT // tf
    row = lambda j, na: jnp.minimum(j, na[0] - 1)
    return pl.pallas_call(
        _moe_up_kernel,
        grid_spec=pltpu.PrefetchScalarGridSpec(
            num_scalar_prefetch=2,
            grid=(nf, NT_MOE),
            in_specs=[pl.BlockSpec((tm, D_MODEL), lambda f, j, te, na: (row(j, na), 0)),
                      pl.BlockSpec((None, D_MODEL, tf), lambda f, j, te, na: (te[j], 0, f)),
                      pl.BlockSpec((None, D_MODEL, tf), lambda f, j, te, na: (te[j], 0, nf + f))],
            out_specs=pl.BlockSpec((tm, tf), lambda f, j, te, na: (j, f)),
            scratch_shapes=[pltpu.VMEM((D_MODEL, 2 * tf), BF16)],
        ),
        out_shape=jax.ShapeDtypeStruct((ROWS_SORTED, FFN_EXPERT), BF16),
        compiler_params=_cparams(("arbitrary", "arbitrary")),
        name="moe_up",
    )(te, na, xs, w_in, w_in)


def _moe_down_kernel(te_ref, na_ref, h_ref, w_ref, o_ref, w_scr):
    j = pl.program_id(0)

    @pl.when(j < na_ref[0])
    def _():
        @pl.when(_tile_changed(te_ref, j))
        def _():
            w_scr[...] = w_ref[...].astype(BF16)

        o_ref[...] = _dot(h_ref[...], w_scr[...])

    @pl.when(j >= na_ref[0])
    def _():
        o_ref[...] = jnp.zeros_like(o_ref)


def _moe_down(te, na, hmid, w_out):
    tm = TM_MOE
    row = lambda j, na: jnp.minimum(j, na[0] - 1)
    return pl.pallas_call(
        _moe_down_kernel,
        grid_spec=pltpu.PrefetchScalarGridSpec(
            num_scalar_prefetch=2,
            grid=(NT_MOE,),
            in_specs=[pl.BlockSpec((tm, FFN_EXPERT), lambda j, te, na: (row(j, na), 0)),
                      pl.BlockSpec((None, FFN_EXPERT, D_MODEL), lambda j, te, na: (te[j], 0, 0))],
            out_specs=pl.BlockSpec((tm, D_MODEL), lambda j, te, na: (j, 0)),
            scratch_shapes=[pltpu.VMEM((FFN_EXPERT, D_MODEL), BF16)],
        ),
        out_shape=jax.ShapeDtypeStruct((ROWS_SORTED, D_MODEL), F32),
        compiler_params=_cparams(("arbitrary",)),
        name="moe_down",
    )(te, na, hmid, w_out)


def _moe_combine_kernel(pos_ref, y_ref, rt_ref, x_ref, g_ref, lg_ref, lb_ref, o_ref, ybuf, sems, *, tok_off):
    i = pl.program_id(0)
    tm = x_ref.shape[0]

    def fetch(tile, slot):
        def start(t, c):
            for k in range(TOP_K):
                p = pos_ref[k * N_TOK + tok_off + tile * tm + t]
                pltpu.make_async_copy(y_ref.at[pl.ds(p, 1)], ybuf.at[slot, k, pl.ds(t, 1)], sems.at[slot]).start()
            return c

        lax.fori_loop(0, tm, start, 0, unroll=8)

    @pl.when(i == 0)
    def _():
        fetch(0, 0)

    @pl.when(i + 1 < pl.num_programs(0))
    def _():
        fetch(i + 1, (i + 1) % 2)

    slot = i % 2
    pltpu.make_async_copy(ybuf.at[slot], ybuf.at[slot], sems.at[slot]).wait()
    rt = rt_ref[...]
    moe = rt[:, RT_GATE:RT_GATE + 1] * ybuf[slot, 0] + rt[:, RT_GATE + 1:RT_GATE + 2] * ybuf[slot, 1]
    o_ref[...] = _layer_norm(ALPHA * x_ref[...] + g_ref[...] * moe, lg_ref[...], lb_ref[...])


def _moe_combine(pos, y, rt_all, x_all, mod, lg, lb, latent):
    n = N_LAT if latent else N_CTX
    tm = TM_ROUTE
    rf = _row_fn(latent, tm)
    tok_off = N_CTX if latent else 0
    off = tok_off // tm
    return pl.pallas_call(
        functools.partial(_moe_combine_kernel, tok_off=tok_off),
        grid_spec=pltpu.PrefetchScalarGridSpec(
            num_scalar_prefetch=1,
            grid=(n // tm,),
            in_specs=[pl.BlockSpec(memory_space=pl.ANY),
                      pl.BlockSpec((tm, LANES), lambda i, pos: (i + off, 0)),
                      pl.BlockSpec((tm, D_MODEL), lambda i, pos: (i + off, 0)),
                      _mod_spec(5, rf),
                      pl.BlockSpec((1, D_MODEL), lambda i, pos: (0, 0)),
                      pl.BlockSpec((1, D_MODEL), lambda i, pos: (0, 0))],
            out_specs=pl.BlockSpec((tm, D_MODEL), lambda i, pos: (i, 0)),
            scratch_shapes=[pltpu.VMEM((2, TOP_K, tm, D_MODEL), F32), pltpu.SemaphoreType.DMA((2,))],
        ),
        out_shape=jax.ShapeDtypeStruct((n, D_MODEL), F32),
        compiler_params=_cparams(("arbitrary",)),
        name="moe_combine_lat" if latent else "moe_combine_ctx",
    )(pos, y, rt_all, x_all, mod, lg, lb)


def _routing_positions(rt_all, counts, meta):
    tm = TM_MOE
    cnt = counts[0, :N_EXPERTS].astype(jnp.int32)
    padded = ((cnt + tm - 1) // tm) * tm
    gend = jnp.cumsum(padded)
    gstart = gend - padded
    route = rt_all[:, :RT_WIDTH].astype(jnp.int32)
    onehot = lambda idx: idx[:, None] == jnp.arange(N_EXPERTS, dtype=jnp.int32)[None, :]
    start_of = lambda idx: jnp.sum(jnp.where(onehot(idx), gstart[None, :], 0), axis=1)
    pos = jnp.concatenate([start_of(route[:, RT_EXPERT + k]) + route[:, RT_ROW + k] for k in range(TOP_K)])
    pad = jnp.concatenate([gstart + cnt, padded - cnt]).astype(jnp.int32)
    tile_start = jnp.arange(NT_MOE, dtype=jnp.int32) * tm
    te = jnp.sum((tile_start[:, None] >= gend[None, :]).astype(jnp.int32), axis=1)
    n_active = (gend[-1] // tm).astype(jnp.int32)
    last_e = jnp.take(te, jnp.maximum(n_active - 1, 0))
    te = jnp.where(tile_start < gend[-1], te, last_e).astype(jnp.int32)
    m = meta[:, :3, :N_EXPERTS].astype(jnp.int32)
    seg = jnp.stack([m[:, 0], m[:, 1], m[:, 2] + gstart[None, :]], axis=-1).reshape(-1)
    return pos, seg, pad, te, n_active.reshape(1)


def kernel(x_prompt, x_sample, cache_ckv_l0, cache_krope_l0, cache_k_l1, cache_v_l1, c, c_ctx, w_ada_l0, b_ada_l0, mla_w_dq, mla_g_q, mla_w_uq, mla_w_dkv, mla_g_kv, mla_w_ukv, mla_w_o, ln1_g_l0, ln1_b_l0, ffn_w_in, ffn_w_out, ln2_g_l0, ln2_b_l0, w_ada_l1, b_ada_l1, na_w_qkv, na_rpb, na_w_o, ln1_g_l1, ln1_b_l1, moe_w_router, moe_w_in, moe_w_out, ln2_g_l1, ln2_b_l1):
    row = lambda v: v.reshape(1, -1)
    xp = x_prompt.reshape(N_CTX, D_MODEL)
    xs = x_sample.reshape(N_LAT, D_MODEL)
    groups = ((xp, False), (xs, True))

    cvecs = jnp.concatenate([c_ctx[None], c, jnp.zeros((SUBLANES - 1 - DEC_BATCH, D_MODEL), F32)], axis=0)
    mod0 = _ada_mod(cvecs, w_ada_l0, b_ada_l0)
    mod1 = _ada_mod(cvecs, w_ada_l1, b_ada_l1)

    wa = jnp.concatenate([mla_w_dq, mla_w_dkv], axis=1).astype(BF16)
    wuq = mla_w_uq.reshape(MLA_Q_LORA, MLA_HEADS, MLA_NOPE + MLA_ROPE)
    wuq = jnp.pad(wuq, ((0, 0), (0, 0), (0, Q_HEAD_PAD - MLA_NOPE - MLA_ROPE)))
    wuq = wuq.reshape(MLA_Q_LORA, MLA_HEADS * Q_HEAD_PAD).astype(BF16)
    wukv = mla_w_ukv.astype(BF16)
    wo0 = mla_w_o.astype(BF16)
    w_in0 = ffn_w_in.astype(BF16)
    w_out0 = ffn_w_out.astype(BF16)
    wqkv = na_w_qkv.astype(BF16)
    wo1 = na_w_o.astype(BF16)
    wr = jnp.pad(moe_w_router, ((0, 0), (0, LANES - N_EXPERTS))).astype(BF16)
    bias_tab = _na_bias_table(na_rpb)
    tab = _rope_tables()

    kvx_cache = _matmul(cache_ckv_l0.reshape(DEC_BATCH * PAST_LEN, MLA_KV_LORA), wukv, BF16, 512, "mla_expand_cache")
    kvx_cache = kvx_cache.reshape(DEC_BATCH, PAST_LEN, -1)
    krp_cache = jnp.pad(cache_krope_l0, ((0, 0), (0, 0), (0, LANES - MLA_ROPE))).astype(BF16)
    x1 = []
    new_ckv = new_kr = None
    vt_cache = kvx_cache.reshape(DEC_BATCH, PAST_LEN, MLA_HEADS, MLA_NOPE + MLA_V)[..., MLA_NOPE:]
    vt_cache = vt_cache.reshape(DEC_BATCH, PAST_LEN, MLA_HEADS * MLA_V).transpose(0, 2, 1)
    for x, latent in groups:
        proj = _mla_proj(x, mod0, wa, row(mla_g_q), row(mla_g_kv), wuq, wukv, tab, latent)
        if latent:
            qt, kvx, krp, vt = proj
            o = _mla_attn_lat(qt, kvx, krp, kvx_cache, krp_cache, vt, vt_cache)
        else:
            q, kvx, new_ckv, new_kr, krp = proj
            o = _mla_attn_ctx(q, kvx, krp)
        x1.append(_attn_out_ffn(o, x, wo0, mod0, row(ln1_g_l0), row(ln1_b_l0), w_in0, w_out0,
                                row(ln2_g_l0), row(ln2_b_l0), latent))

    kct = cache_k_l1.reshape(DEC_BATCH, PAST_LEN, D_MODEL).transpose(0, 2, 1).astype(BF16)
    vc = cache_v_l1.reshape(DEC_BATCH, PAST_LEN, D_MODEL).astype(BF16)
    q, v, new_kt, new_vt = _na_qkv(x1[0], mod1, wqkv, False)
    o_ctx = _na_attn_ctx(q, new_kt, v)
    q, v, kt = _na_qkv(x1[1], mod1, wqkv, True)
    o_lat = _na_attn_lat(q, kt, v, kct, vc, bias_tab)
    x2_all, rt_all, counts, meta = _proj_ln_route(o_ctx, x1[0], o_lat, x1[1], wo1, mod1, row(ln1_g_l1),
                                                  row(ln1_b_l1), wr)

    pos, seg, pad, te, n_active = _routing_positions(rt_all, counts, meta)
    x_sorted = _moe_scatter(seg, pad, n_active, x2_all, rt_all, mod1)
    hmid = _moe_up(te, n_active, x_sorted, moe_w_in)
    y = _moe_down(te, n_active, hmid, moe_w_out)
    outs = [_moe_combine(pos, y, rt_all, x2_all, mod1, row(ln2_g_l1), row(ln2_b_l1), latent)
            for latent in (False, True)]

    return (outs[0].reshape(BATCH, SEQ, D_MODEL),
            outs[1].reshape(DEC_BATCH, DEC_SEQ, D_MODEL),
            new_ckv.reshape(BATCH, SEQ, MLA_KV_LORA),
            new_kr.transpose(0, 2, 1),
            new_kt.reshape(BATCH, NA_HEADS, NA_HD, SEQ).transpose(0, 3, 1, 2),
            new_vt.reshape(BATCH, NA_HEADS, NA_HD, SEQ).transpose(0, 3, 1, 2))
```

```python
import functools
import math

import numpy as np
import jax
import jax.numpy as jnp
from jax import lax
from jax.experimental import pallas as pl
from jax.experimental.pallas import tpu as pltpu

F32 = jnp.float32
BF16 = jnp.bfloat16

D_MODEL = 1024
BATCH = 32
SEQ = 256
DEPTH = 2
DEC_BATCH = 2
DEC_SEQ = 2048
PAST_LEN = 512
GRID_W = 64
MLA_HEADS = 8
MLA_NOPE = 128
MLA_ROPE = 64
MLA_V = 128
MLA_Q_LORA = 512
MLA_KV_LORA = 256
MLA_SCALE = 1.0 / math.sqrt(MLA_NOPE + MLA_ROPE)
ROPE_THETA = 10000.0
NA_HEADS = 16
NA_HD = D_MODEL // NA_HEADS
NA_WIN_ROWS = 8
NA_WIN_COLS = 16
NA_SCALE = 1.0 / math.sqrt(NA_HD)
FFN_DENSE = 2816
N_EXPERTS = 8
TOP_K = 2
FFN_EXPERT = 3584
ALPHA = (2 * DEPTH) ** 0.25
LN_EPS = 1e-5
RMS_EPS = 1e-6

N_CTX = BATCH * SEQ
N_LAT = DEC_BATCH * DEC_SEQ
N_TOK = N_CTX + N_LAT
GRID_ROWS = DEC_SEQ // GRID_W
Q_HEAD_PAD = 256
LANES = 128
SUBLANES = 8
NEG_BIG = -1e30
MLA_KEY_CHUNK = 256
NA_PAIR = 2 * NA_HD
NA_BLK_ROWS = 4
NA_KT_W = 2 * GRID_W
NA_WIN_PAIRS = (NA_BLK_ROWS + NA_WIN_ROWS) // 2
NA_N_BLOCKS = DEC_SEQ // (NA_BLK_ROWS * GRID_W)
NA_BLOCK_VARIANTS = (0, 1, NA_N_BLOCKS - 1)

VMEM_LIMIT = 56 * 1024 * 1024

TM_PROJ = 512
TM_FFN = 512
CTX_BATCH_PER_STEP = 2
MXU_TILE = 256
FFN_CHUNKS = ((0, 6 * MXU_TILE), (6 * MXU_TILE, FFN_DENSE))
TM_MOE = 512
TF_MOE = 1792
N_PAIRS = N_TOK * TOP_K
MOE_LOCAL_ROWS = TOP_K * TM_PROJ + N_EXPERTS * SUBLANES
N_SEG_PAD = N_EXPERTS * (N_TOK // TM_PROJ) * (SUBLANES - 1)
NT_MOE = -(-(N_PAIRS + N_SEG_PAD + N_EXPERTS * (TM_MOE - 1)) // TM_MOE)
ROWS_SORTED = NT_MOE * TM_MOE
TM_ROUTE = 256
RT_EXPERT, RT_GATE, RT_ROW, RT_LOCAL = 0, 2, 4, 6
RT_WIDTH = 8


def _cparams(sem, vmem=VMEM_LIMIT):
    return pltpu.CompilerParams(dimension_semantics=sem, vmem_limit_bytes=vmem)


def _silu(x):
    return x * jax.nn.sigmoid(x)


def _layer_norm(y, g, b):
    mu = jnp.mean(y, axis=-1, keepdims=True)
    d = y - mu
    var = jnp.mean(d * d, axis=-1, keepdims=True)
    return d * lax.rsqrt(var + LN_EPS) * g + b


def _rms_norm(y, g):
    return y * lax.rsqrt(jnp.mean(y * y, axis=-1, keepdims=True) + RMS_EPS) * g


def _dot(a, b):
    return jnp.dot(a, b, preferred_element_type=F32)


def _dot_nt(a, b):
    return lax.dot_general(a, b, (((1,), (1,)), ((), ())), preferred_element_type=F32)


def _ada_kernel(c_ref, w_ref, b_ref, o_ref):
    s = _silu(c_ref[...]).astype(BF16)
    o_ref[...] = _dot(s, w_ref[...].astype(BF16)) + b_ref[...]


def _ada_mod(cvecs, w, b):
    tn = 1536
    n_vec = 1 + DEC_BATCH
    m = pl.pallas_call(
        _ada_kernel,
        grid=(6 * D_MODEL // tn,),
        in_specs=[pl.BlockSpec((SUBLANES, D_MODEL), lambda j: (0, 0)),
                  pl.BlockSpec((D_MODEL, tn), lambda j: (0, j)),
                  pl.BlockSpec((1, tn), lambda j: (0, j))],
        out_specs=pl.BlockSpec((SUBLANES, tn), lambda j: (0, j)),
        out_shape=jax.ShapeDtypeStruct((SUBLANES, 6 * D_MODEL), F32),
        compiler_params=_cparams(("arbitrary",)),
        name="ada_mod",
    )(cvecs, w, b.reshape(1, -1))
    return m[:n_vec].reshape(n_vec * 6, 1, D_MODEL)


def _mod_spec(j, row_fn):
    return pl.BlockSpec((None, 1, D_MODEL), lambda i, *_: (row_fn(i) * 6 + j, 0, 0))


def _row_fn(latent, tm):
    if not latent:
        return lambda i: 0
    per = DEC_SEQ // tm
    return lambda i: 1 + i // per


def _const_spec(shape):
    nd = len(shape)
    return pl.BlockSpec(shape, lambda *_: (0,) * nd, pipeline_mode=pl.Buffered(1))


def _rope_kernel(invf_ref, o_ref):
    i = pl.program_id(0)
    tm = o_ref.shape[1]
    t = i * tm + lax.broadcasted_iota(jnp.int32, (tm, LANES), 0)
    lane = lax.broadcasted_iota(jnp.int32, (tm, LANES), 1)
    row = t >> int(math.log2(GRID_W))
    col = t & (GRID_W - 1)
    pos = jnp.where(lane < MLA_ROPE // 2, row, col).astype(F32)
    ang = pos * invf_ref[...]
    cos = jnp.cos(ang)
    sin = jnp.sin(ang)
    unit = lane >> int(math.log2(MLA_ROPE // 4))
    first = (unit == 0) | (unit == 2)
    second = (unit == 1) | (unit == 3)
    o_ref[0] = jnp.where(lane < MLA_ROPE, cos, 0.0)
    o_ref[1] = jnp.where(first, -sin, 0.0)
    o_ref[2] = jnp.where(second, sin, 0.0)


def _rope_tables():
    half = MLA_ROPE // 2
    inv_freq = (1.0 / (ROPE_THETA ** (np.arange(0, half, 2, dtype=np.float32) / half))).astype(np.float32)
    lane_f = np.zeros((1, LANES), np.float32)
    lane_f[0, :MLA_ROPE] = np.tile(inv_freq, 4)
    tm = 256
    return pl.pallas_call(
        _rope_kernel,
        grid=(DEC_SEQ // tm,),
        in_specs=[_const_spec((1, LANES))],
        out_specs=pl.BlockSpec((3, tm, LANES), lambda i: (0, i, 0)),
        out_shape=jax.ShapeDtypeStruct((3, DEC_SEQ, LANES), F32),
        compiler_params=_cparams(("arbitrary",)),
        name="rope_tables",
    )(jnp.asarray(lane_f))


def _rotate(v, tab_ref):
    return (v * tab_ref[0] + pltpu.roll(v, LANES - MLA_ROPE // 4, 1) * tab_ref[1]
            + pltpu.roll(v, MLA_ROPE // 4, 1) * tab_ref[2])


def _mla_proj_kernel(*refs, rope):
    if rope:
        (x_ref, sc_ref, sh_ref, wa_ref, gq_ref, gkv_ref, wuq_ref, wukv_ref, tab_ref,
         qt_ref, kvx_ref, krp_ref, vt_ref) = refs
    else:
        (x_ref, sc_ref, sh_ref, wa_ref, gq_ref, gkv_ref, wuq_ref, wukv_ref,
         q_ref, kvx_ref, ckv_ref, kr_ref, krp_ref) = refs
    h = (x_ref[...] * (1.0 + sc_ref[...]) + sh_ref[...]).astype(BF16)
    t = _dot(h, wa_ref[...])
    cq = _rms_norm(t[:, :MLA_Q_LORA], gq_ref[...])
    ckv = _rms_norm(t[:, MLA_Q_LORA:MLA_Q_LORA + MLA_KV_LORA], gkv_ref[...])
    kr = t[:, MLA_Q_LORA + MLA_KV_LORA:]
    kvx = _dot(ckv.astype(BF16), wukv_ref[...])
    kvx_ref[...] = kvx.astype(BF16)
    q = _dot(cq.astype(BF16), wuq_ref[...])
    krp = jnp.concatenate([kr, jnp.zeros_like(kr)], axis=-1)
    if rope:
        krp_ref[...] = _rotate(krp, tab_ref).astype(BF16)
        parts = []
        for hd in range(MLA_HEADS):
            lo = hd * Q_HEAD_PAD
            parts += [q[:, lo:lo + MLA_NOPE], _rotate(q[:, lo + MLA_NOPE:lo + Q_HEAD_PAD], tab_ref)]
        qt_ref[...] = jnp.concatenate(parts, axis=1).T.astype(BF16)
        vcols = [kvx[:, hd * Q_HEAD_PAD + MLA_NOPE:(hd + 1) * Q_HEAD_PAD] for hd in range(MLA_HEADS)]
        vt_ref[...] = jnp.concatenate(vcols, axis=1).T.astype(BF16)
    else:
        ckv_ref[...] = ckv
        krt = kr.T
        for t in range(kr_ref.shape[0]):
            kr_ref[t] = krt[:, t * SEQ:(t + 1) * SEQ]
        krp_ref[...] = krp.astype(BF16)
        q_ref[...] = q.astype(BF16)


def _mla_proj(x, mod, wa, gq, gkv, wuq, wukv, tab, latent):
    n = x.shape[0]
    tm = TM_PROJ
    rf = _row_fn(latent, tm)
    tok = lambda w: pl.BlockSpec((tm, w), lambda i: (i, 0))
    in_specs = [tok(D_MODEL), _mod_spec(1, rf), _mod_spec(0, rf),
                _const_spec(wa.shape), _const_spec(gq.shape), _const_spec(gkv.shape),
                _const_spec(wuq.shape), _const_spec(wukv.shape)]
    args = [x, mod, mod, wa, gq, gkv, wuq, wukv]
    wq, wkv = MLA_HEADS * Q_HEAD_PAD, MLA_HEADS * (MLA_NOPE + MLA_V)
    if latent:
        per = DEC_SEQ // tm
        in_specs.append(pl.BlockSpec((3, tm, LANES), lambda i: (0, i % per, 0)))
        args.append(tab)
        out_specs = [pl.BlockSpec((wq, tm), lambda i: (0, i)), tok(wkv), tok(LANES),
                     pl.BlockSpec((None, MLA_HEADS * MLA_V, tm), lambda i: (i // per, 0, i % per))]
        out_shape = [jax.ShapeDtypeStruct((wq, n), BF16), jax.ShapeDtypeStruct((n, wkv), BF16),
                     jax.ShapeDtypeStruct((n, LANES), BF16),
                     jax.ShapeDtypeStruct((DEC_BATCH, MLA_HEADS * MLA_V, DEC_SEQ), BF16)]
    else:
        out_specs = [tok(wq), tok(wkv), tok(MLA_KV_LORA),
                     pl.BlockSpec((tm // SEQ, MLA_ROPE, SEQ), lambda i: (i, 0, 0)), tok(LANES)]
        out_shape = [jax.ShapeDtypeStruct((n, wq), BF16), jax.ShapeDtypeStruct((n, wkv), BF16),
                     jax.ShapeDtypeStruct((n, MLA_KV_LORA), F32), jax.ShapeDtypeStruct((n // SEQ, MLA_ROPE, SEQ), F32),
                     jax.ShapeDtypeStruct((n, LANES), BF16)]
    return pl.pallas_call(
        functools.partial(_mla_proj_kernel, rope=latent),
        grid=(n // tm,),
        in_specs=in_specs,
        out_specs=out_specs,
        out_shape=out_shape,
        compiler_params=_cparams(("arbitrary",)),
        name="mla_proj_lat" if latent else "mla_proj_ctx",
    )(*args)


def _matmul_kernel(a_ref, b_ref, o_ref):
    o_ref[...] = _dot(a_ref[...].astype(BF16), b_ref[...]).astype(o_ref.dtype)


def _matmul(a, b, out_dtype, tm, name):
    m, k = a.shape
    n = b.shape[1]
    return pl.pallas_call(
        _matmul_kernel,
        grid=(m // tm,),
        in_specs=[pl.BlockSpec((tm, k), lambda i: (i, 0)), _const_spec(b.shape)],
        out_specs=pl.BlockSpec((tm, n), lambda i: (i, 0)),
        out_shape=jax.ShapeDtypeStruct((m, n), out_dtype),
        compiler_params=_cparams(("arbitrary",)),
        name=name,
    )(a, b)


def _mla_head(qh, kparts, vparts):
    ss = [_dot_nt(qh, k) for k in kparts]
    m = functools.reduce(jnp.maximum, [jnp.max(s, axis=-1, keepdims=True) for s in ss])
    ps = [jnp.exp2((s - m) * (MLA_SCALE * math.log2(math.e))) for s in ss]
    l = functools.reduce(jnp.add, [jnp.sum(p, axis=-1, keepdims=True) for p in ps])
    o = functools.reduce(jnp.add, [_dot(p.astype(BF16), v) for p, v in zip(ps, vparts)])
    return o / l


def _mla_attn_ctx_kernel(q_ref, kvx_ref, krp_ref, o_ref):
    for b in range(CTX_BATCH_PER_STEP):
        rows = slice(b * SEQ, (b + 1) * SEQ)
        krp = krp_ref[rows, :]
        for hd in range(MLA_HEADS):
            lo = hd * (MLA_NOPE + MLA_V)
            qh = q_ref[rows, hd * Q_HEAD_PAD:(hd + 1) * Q_HEAD_PAD]
            kh = jnp.concatenate([kvx_ref[rows, lo:lo + MLA_NOPE], krp], axis=-1)
            vh = kvx_ref[rows, lo + MLA_NOPE:lo + MLA_NOPE + MLA_V]
            o_ref[rows, hd * MLA_V:(hd + 1) * MLA_V] = _mla_head(qh, [kh], [vh]).astype(BF16)


def _mla_attn_ctx(q, kvx, krp):
    tok = lambda w: pl.BlockSpec((CTX_BATCH_PER_STEP * SEQ, w), lambda b: (b, 0))
    return pl.pallas_call(
        _mla_attn_ctx_kernel,
        grid=(BATCH // CTX_BATCH_PER_STEP,),
        in_specs=[tok(q.shape[1]), tok(kvx.shape[1]), tok(LANES)],
        out_specs=tok(MLA_HEADS * MLA_V),
        out_shape=jax.ShapeDtypeStruct((N_CTX, MLA_HEADS * MLA_V), BF16),
        compiler_params=_cparams(("arbitrary",)),
        name="mla_attn_ctx",
    )(q, kvx, krp)


def _mla_attn_lat_kernel(qt_ref, kvl_ref, krl_ref, kvc_ref, krc_ref, vtl_ref, vtc_ref, o_ref):
    chunks = [(kvl_ref, krl_ref, vtl_ref, c) for c in range(DEC_SEQ // MLA_KEY_CHUNK)]
    chunks += [(kvc_ref, krc_ref, vtc_ref, c) for c in range(PAST_LEN // MLA_KEY_CHUNK)]
    for hd in range(MLA_HEADS):
        lo = hd * (MLA_NOPE + MLA_V)
        qt = qt_ref[hd * Q_HEAD_PAD:(hd + 1) * Q_HEAD_PAD, :]
        ss = []
        for kv_ref, kr_ref, _, c in chunks:
            rows = slice(c * MLA_KEY_CHUNK, (c + 1) * MLA_KEY_CHUNK)
            ss.append(_dot(jnp.concatenate([kv_ref[rows, lo:lo + MLA_NOPE], kr_ref[rows, :]], axis=-1), qt))
        m = functools.reduce(jnp.maximum, [jnp.max(s, axis=0, keepdims=True) for s in ss])
        ps = [jnp.exp2((s - m) * (MLA_SCALE * math.log2(math.e))) for s in ss]
        l = functools.reduce(jnp.add, [jnp.sum(p, axis=0, keepdims=True) for p in ps])
        ots = [_dot(vt_ref[hd * MLA_V:(hd + 1) * MLA_V, c * MLA_KEY_CHUNK:(c + 1) * MLA_KEY_CHUNK], p.astype(BF16))
               for (_, _, vt_ref, c), p in zip(chunks, ps)]
        ot = functools.reduce(jnp.add, ots) / l
        o_ref[:, hd * MLA_V:(hd + 1) * MLA_V] = ot.T.astype(BF16)


def _mla_attn_lat(qt, kvx_lat, krp_lat, kvx_ctx, krp_ctx, vt_lat, vt_ctx):
    tq = 256
    per = DEC_SEQ // tq
    wkv = kvx_lat.shape[-1]
    wv = MLA_HEADS * MLA_V
    batch = lambda rows, cols: pl.BlockSpec((None, rows, cols), lambda b, i: (b, 0, 0))
    return pl.pallas_call(
        _mla_attn_lat_kernel,
        grid=(DEC_BATCH, per),
        in_specs=[pl.BlockSpec((qt.shape[0], tq), lambda b, i: (0, b * per + i)),
                  batch(DEC_SEQ, wkv), batch(DEC_SEQ, LANES), batch(PAST_LEN, wkv), batch(PAST_LEN, LANES),
                  batch(wv, DEC_SEQ), batch(wv, PAST_LEN)],
        out_specs=pl.BlockSpec((tq, wv), lambda b, i: (b * per + i, 0)),
        out_shape=jax.ShapeDtypeStruct((N_LAT, wv), BF16),
        compiler_params=_cparams(("arbitrary", "arbitrary")),
        name="mla_attn_lat",
    )(qt, kvx_lat.reshape(DEC_BATCH, DEC_SEQ, wkv), krp_lat.reshape(DEC_BATCH, DEC_SEQ, LANES), kvx_ctx, krp_ctx,
      vt_lat, vt_ctx)


def _top2(logits):
    lane = lax.broadcasted_iota(jnp.int32, logits.shape, 1).astype(F32)
    m1 = jnp.max(logits, axis=-1, keepdims=True)
    i1 = jnp.min(jnp.where(logits == m1, lane, float(LANES)), axis=-1, keepdims=True)
    rest = jnp.where(lane == i1, -jnp.inf, logits)
    m2 = jnp.max(rest, axis=-1, keepdims=True)
    i2 = jnp.min(jnp.where(rest == m2, lane, float(LANES)), axis=-1, keepdims=True)
    e = jnp.exp(m2 - m1)
    w1 = 1.0 / (1.0 + e)
    w2 = e / (1.0 + e)
    return lane, i1, i2, w1, w2


def _proj_ln_route_kernel(oc_ref, xc_ref, ol_ref, xl_ref, wo_ref, g_ref, lg_ref, lb_ref, sc_ref, sh_ref, wr_ref,
                          x1_ref, rt_ref, rtt_ref, cnt_ref, meta_ref, carry, *, n_ctx_tiles):
    i = pl.program_id(0)

    @pl.when(i == 0)
    def _():
        carry[...] = jnp.zeros_like(carry)

    def body(o_ref, x_ref):
        out = _dot(o_ref[...], wo_ref[...])
        x1 = _layer_norm(ALPHA * x_ref[...] + g_ref[...] * out, lg_ref[...], lb_ref[...])
        x1_ref[...] = x1
        hm = x1 * (1.0 + sc_ref[...]) + sh_ref[...]
        logits = _dot(hm.astype(BF16), wr_ref[...])
        lane = lax.broadcasted_iota(jnp.int32, logits.shape, 1)
        logits = jnp.where(lane < N_EXPERTS, logits, -jnp.inf)
        lane_f, i1, i2, w1, w2 = _top2(logits)
        tm = logits.shape[0]
        oh1 = (lane_f == i1).astype(F32)
        oh2 = (lane_f == i2).astype(F32)
        rr = lax.broadcasted_iota(jnp.int32, (tm, tm), 0)
        cc = lax.broadcasted_iota(jnp.int32, (tm, tm), 1)
        below = (cc < rr).astype(BF16)
        tot1 = jnp.sum(oh1, axis=0, keepdims=True)
        tot2 = jnp.sum(oh2, axis=0, keepdims=True)
        seg_len = jnp.floor((tot1 + tot2 + (SUBLANES - 1)) * (1.0 / SUBLANES)) * SUBLANES
        e_row = lax.broadcasted_iota(jnp.int32, (LANES, LANES), 0)
        e_col = lax.broadcasted_iota(jnp.int32, (LANES, LANES), 1)
        before = (e_row < e_col).astype(BF16)
        seg_loc = _dot(jnp.broadcast_to(seg_len, (SUBLANES, LANES)).astype(BF16), before)[0:1]
        seg_glob = carry[...]
        cum1 = _dot(below, oh1.astype(BF16))
        cum2 = _dot(below, oh2.astype(BF16)) + tot1
        pick = lambda oh, v: jnp.sum(oh * v, axis=-1, keepdims=True)
        vals = {RT_EXPERT: (i1, i2), RT_GATE: (w1, w2),
                RT_ROW: (pick(oh1, cum1 + seg_glob), pick(oh2, cum2 + seg_glob)),
                RT_LOCAL: (pick(oh1, cum1 + seg_loc), pick(oh2, cum2 + seg_loc))}
        rt = jnp.zeros_like(logits)
        for first_lane, pair in vals.items():
            for k, val in enumerate(pair):
                rt = jnp.where(lane == first_lane + k, val, rt)
        rt_ref[...] = rt
        rtt_ref[...] = rt.T[:RT_WIDTH]
        sub = lax.broadcasted_iota(jnp.int32, (SUBLANES, LANES), 0)
        meta_ref[...] = jnp.where(sub == 0, seg_len, jnp.where(sub == 1, seg_loc, jnp.where(sub == 2, seg_glob, 0.0)))
        carry[...] = seg_glob + seg_len
        cnt_ref[...] = jnp.broadcast_to(carry[...], cnt_ref.shape)

    is_ctx = i < n_ctx_tiles
    pl.when(is_ctx)(lambda: body(oc_ref, xc_ref))
    pl.when(jnp.logical_not(is_ctx))(lambda: body(ol_ref, xl_ref))


def _all_row_fn(tm):
    nc = N_CTX // tm
    per = DEC_SEQ // tm
    return lambda i: jnp.where(i < nc, 0, 1 + (i - nc) // per)


def _proj_ln_route(o_ctx, x_ctx, o_lat, x_lat, wo, mod, lg, lb, wr):
    tm = TM_PROJ
    nc = N_CTX // tm
    rf = _all_row_fn(tm)
    ctx = lambda w: pl.BlockSpec((tm, w), lambda i: (jnp.minimum(i, nc - 1), 0))
    lat = lambda w: pl.BlockSpec((tm, w), lambda i: (jnp.maximum(i - nc, 0), 0))
    tok = lambda w: pl.BlockSpec((tm, w), lambda i: (i, 0))
    return pl.pallas_call(
        functools.partial(_proj_ln_route_kernel, n_ctx_tiles=nc),
        grid=(N_TOK // tm,),
        in_specs=[ctx(D_MODEL), ctx(D_MODEL), lat(D_MODEL), lat(D_MODEL), _const_spec(wo.shape), _mod_spec(2, rf),
                  _const_spec(lg.shape), _const_spec(lb.shape), _mod_spec(4, rf), _mod_spec(3, rf),
                  _const_spec(wr.shape)],
        out_specs=[tok(D_MODEL), tok(LANES), pl.BlockSpec((RT_WIDTH, tm), lambda i: (0, i)),
                   pl.BlockSpec((SUBLANES, LANES), lambda i: (0, 0)),
                   pl.BlockSpec((None, SUBLANES, LANES), lambda i: (i, 0, 0))],
        out_shape=[jax.ShapeDtypeStruct((N_TOK, D_MODEL), F32),
                   jax.ShapeDtypeStruct((N_TOK, LANES), F32),
                   jax.ShapeDtypeStruct((RT_WIDTH, N_TOK), F32),
                   jax.ShapeDtypeStruct((SUBLANES, LANES), F32),
                   jax.ShapeDtypeStruct((N_TOK // tm, SUBLANES, LANES), F32)],
        scratch_shapes=[pltpu.VMEM((1, LANES), F32)],
        compiler_params=_cparams(("arbitrary",)),
        name="proj_ln_route",
    )(o_ctx, x_ctx, o_lat, x_lat, wo, mod, lg, lb, mod, mod, wr)


def _attn_out_ffn_kernel(o_ref, x_ref, wo_ref, g1_ref, lg1_ref, lb1_ref, sc_ref, sh_ref, g2_ref,
                         wi_ref, wd_ref, lg2_ref, lb2_ref, y_ref):
    x1 = _layer_norm(ALPHA * x_ref[...] + g1_ref[...] * _dot(o_ref[...], wo_ref[...]), lg1_ref[...], lb1_ref[...])
    h = (x1 * (1.0 + sc_ref[...]) + sh_ref[...]).astype(BF16)
    acc = None
    for lo, hi in FFN_CHUNKS:
        gate = _dot(h, wi_ref[:, lo:hi])
        up = _dot(h, wi_ref[:, FFN_DENSE + lo:FFN_DENSE + hi])
        part = _dot((_silu(gate) * up).astype(BF16), wd_ref[lo:hi, :])
        acc = part if acc is None else acc + part
    y_ref[...] = _layer_norm(ALPHA * x1 + g2_ref[...] * acc, lg2_ref[...], lb2_ref[...])


def _attn_out_ffn(o, x, wo, mod, lg1, lb1, w_in, w_out, lg2, lb2, latent):
    n = x.shape[0]
    tm = TM_FFN
    rf = _row_fn(latent, tm)
    tok = lambda w: pl.BlockSpec((tm, w), lambda i: (i, 0))
    vec = _const_spec((1, D_MODEL))
    return pl.pallas_call(
        _attn_out_ffn_kernel,
        grid=(n // tm,),
        in_specs=[tok(o.shape[1]), tok(D_MODEL), _const_spec(wo.shape), _mod_spec(2, rf), vec, vec,
                  _mod_spec(4, rf), _mod_spec(3, rf), _mod_spec(5, rf),
                  _const_spec(w_in.shape), _const_spec(w_out.shape), vec, vec],
        out_specs=tok(D_MODEL),
        out_shape=jax.ShapeDtypeStruct((n, D_MODEL), F32),
        compiler_params=_cparams(("arbitrary",)),
        name="attn_out_ffn_lat" if latent else "attn_out_ffn_ctx",
    )(o, x, wo, mod, lg1, lb1, mod, mod, mod, w_in, w_out, lg2, lb2)


def _na_qkv_kernel(x_ref, sc_ref, sh_ref, w_ref, q_ref, v_ref, kt_ref, *vt_refs):
    h = (x_ref[...] * (1.0 + sc_ref[...]) + sh_ref[...]).astype(BF16)
    qkv = _dot(h, w_ref[...])
    q_ref[...] = (qkv[:, :D_MODEL] * NA_SCALE).astype(BF16)
    v = qkv[:, 2 * D_MODEL:]
    v_ref[...] = v.astype(BF16)
    pairs = [(kt_ref, qkv[:, D_MODEL:2 * D_MODEL])] + [(ref, v) for ref in vt_refs]
    for ref, val in pairs:
        valt = val.T
        nblk, _, w = ref.shape
        for t in range(nblk):
            ref[t] = valt[:, t * w:(t + 1) * w].astype(ref.dtype)


def _na_qkv(x, mod, w, latent):
    n = x.shape[0]
    tm = TM_PROJ
    kt_w = NA_KT_W if latent else SEQ
    rf = _row_fn(latent, tm)
    tok = pl.BlockSpec((tm, D_MODEL), lambda i: (i, 0))
    tspec = pl.BlockSpec((tm // kt_w, D_MODEL, kt_w), lambda i: (i, 0, 0))
    tshape = lambda dt: jax.ShapeDtypeStruct((n // kt_w, D_MODEL, kt_w), dt)
    out_specs = [tok, tok, tspec]
    out_shape = [jax.ShapeDtypeStruct((n, D_MODEL), BF16), jax.ShapeDtypeStruct((n, D_MODEL), BF16),
                 tshape(BF16 if latent else F32)]
    if not latent:
        out_specs.append(tspec)
        out_shape.append(tshape(F32))
    return pl.pallas_call(
        _na_qkv_kernel,
        grid=(n // tm,),
        in_specs=[tok, _mod_spec(1, rf), _mod_spec(0, rf), _const_spec(w.shape)],
        out_specs=out_specs,
        out_shape=out_shape,
        compiler_params=_cparams(("arbitrary",)),
        name="na_qkv_lat" if latent else "na_qkv_ctx",
    )(x, mod, mod, w)


def _softmax_pv(scores, values):
    m = functools.reduce(jnp.maximum, [jnp.max(s, axis=-1, keepdims=True) for s in scores])
    ps = [jnp.exp(s - m) for s in scores]
    l = functools.reduce(jnp.add, [jnp.sum(p, axis=-1, keepdims=True) for p in ps])
    o = functools.reduce(jnp.add, [_dot(p.astype(BF16), v) for p, v in zip(ps, values)])
    return o / l


def _head_of_pair(x, half):
    lane = lax.broadcasted_iota(jnp.int32, x.shape, 1)
    keep = (lane < NA_HD) if half == 0 else (lane >= NA_HD)
    return jnp.where(keep, x, jnp.zeros_like(x))


def _merge_pair(o0, o1):
    lane = lax.broadcasted_iota(jnp.int32, o0.shape, 1)
    return jnp.where(lane < NA_HD, o0, o1)


def _na_attn_ctx_kernel(q_ref, kt_ref, v_ref, o_ref):
    for b in range(CTX_BATCH_PER_STEP):
        rows = slice(b * SEQ, (b + 1) * SEQ)
        for p in range(NA_HEADS // 2):
            cols = slice(p * NA_PAIR, (p + 1) * NA_PAIR)
            qp = q_ref[rows, cols]
            ktp = kt_ref[b, cols, :].astype(BF16)
            vp = v_ref[rows, cols]
            outs = [_softmax_pv([_dot(_head_of_pair(qp, half), ktp)], [vp]) for half in range(2)]
            o_ref[rows, cols] = _merge_pair(*outs).astype(BF16)


def _na_attn_ctx(q, kt, v):
    tok = pl.BlockSpec((CTX_BATCH_PER_STEP * SEQ, D_MODEL), lambda b: (b, 0))
    return pl.pallas_call(
        _na_attn_ctx_kernel,
        grid=(BATCH // CTX_BATCH_PER_STEP,),
        in_specs=[tok, pl.BlockSpec((CTX_BATCH_PER_STEP, D_MODEL, SEQ), lambda b: (b, 0, 0)), tok],
        out_specs=tok,
        out_shape=jax.ShapeDtypeStruct((N_CTX, D_MODEL), BF16),
        compiler_params=_cparams(("arbitrary",)),
        name="na_attn_ctx",
    )(q, kt, v)


def _na_win_start(m, clip=jnp.clip):
    return clip(m * NA_BLK_ROWS // 2 - NA_WIN_ROWS // 4, 0, GRID_ROWS // 2 - NA_WIN_PAIRS)


def _na_variant(m):
    return jnp.where(m == 0, 0, jnp.where(m == NA_N_BLOCKS - 1, 2, 1))


def _na_attn_lat_kernel(q_ref, kt_ref, v_ref, kct_ref, vc_ref, bias_ref, o_ref):
    rp0 = _na_win_start(pl.program_id(1))
    win = pl.ds(pl.multiple_of(rp0 * NA_KT_W, NA_KT_W), NA_WIN_PAIRS * NA_KT_W)
    for p in range(NA_HEADS // 2):
        cols = slice(p * NA_PAIR, (p + 1) * NA_PAIR)
        qp = q_ref[:, cols]
        kw = jnp.concatenate([kt_ref[rp0 + t, cols, :] for t in range(NA_WIN_PAIRS)], axis=1)
        kc = kct_ref[cols, :]
        vw = v_ref[win, cols]
        vc = vc_ref[:, cols]
        outs = []
        for half in range(2):
            qh = _head_of_pair(qp, half)
            s_nb = _dot(qh, kw) + bias_ref[2 * p + half]
            outs.append(_softmax_pv([s_nb, _dot(qh, kc)], [vw, vc]))
        o_ref[:, cols] = _merge_pair(*outs).astype(BF16)


def _na_bias_kernel(e_ref, e64_ref, o_ref):
    n_dr = 2 * NA_WIN_ROWS - 1
    shape = (GRID_W, LANES)
    lane = lax.broadcasted_iota(jnp.int32, shape, 1)
    c = lax.broadcasted_iota(jnp.int32, shape, 0)
    kc = lane & (GRID_W - 1)
    cs = jnp.clip(c - NA_WIN_COLS // 2, 0, GRID_W - NA_WIN_COLS)
    valid = (kc >= cs) & (kc < cs + NA_WIN_COLS)
    toeplitz = lambda ref, a: pltpu.roll(jnp.broadcast_to(ref[a:a + 1, :], shape), 0, 1, stride=1, stride_axis=0)
    rows = ([toeplitz(e_ref, a) for a in range(n_dr)], [toeplitz(e64_ref, a) for a in range(n_dr)])
    masked = jnp.full(shape, NEG_BIG, F32)
    for v, m in enumerate(NA_BLOCK_VARIANTS):
        rp0 = _na_win_start(m, clip=lambda x, lo, hi: min(max(x, lo), hi))
        for i in range(NA_BLK_ROWS):
            r = NA_BLK_ROWS * m + i
            rs = min(max(r - NA_WIN_ROWS // 2, 0), GRID_ROWS - NA_WIN_ROWS)
            for t in range(NA_WIN_PAIRS):
                halves = []
                for u in range(2):
                    kr = 2 * (rp0 + t) + u
                    halves.append(rows[u][kr - r + NA_WIN_ROWS - 1] if rs <= kr < rs + NA_WIN_ROWS else masked)
                o_ref[v, i * GRID_W:(i + 1) * GRID_W, t * LANES:(t + 1) * LANES] = jnp.where(
                    valid, jnp.where(lane < GRID_W, halves[0], halves[1]), NEG_BIG)


def _na_bias_table(rpb):
    n_dr = 2 * NA_WIN_ROWS - 1
    blk = NA_BLK_ROWS * GRID_W
    e = jnp.zeros((NA_HEADS, n_dr, LANES), F32)
    e = e.at[:, :, :NA_WIN_COLS].set(rpb[:, :, NA_WIN_COLS - 1:])
    e = e.at[:, :, LANES - (NA_WIN_COLS - 1):].set(rpb[:, :, :NA_WIN_COLS - 1])
    e64 = jnp.roll(e, GRID_W, axis=-1)
    spec = pl.BlockSpec((None, n_dr, LANES), lambda h: (h, 0, 0))
    return pl.pallas_call(
        _na_bias_kernel,
        grid=(NA_HEADS,),
        in_specs=[spec, spec],
        out_specs=pl.BlockSpec((None, len(NA_BLOCK_VARIANTS), blk, NA_WIN_PAIRS * NA_KT_W), lambda h: (h, 0, 0, 0)),
        out_shape=jax.ShapeDtypeStruct((NA_HEADS, len(NA_BLOCK_VARIANTS), blk, NA_WIN_PAIRS * NA_KT_W), F32),
        compiler_params=_cparams(("arbitrary",)),
        name="na_bias_table",
    )(e, e64)


def _na_attn_lat(q, kt, v, kct, vc, bias_tab):
    blk = NA_BLK_ROWS * GRID_W
    nblk = NA_N_BLOCKS
    npair = DEC_SEQ // NA_KT_W
    row = pl.BlockSpec((blk, D_MODEL), lambda b, m: (b * nblk + m, 0))
    per_batch = lambda *shape: pl.BlockSpec((None,) + shape, lambda b, m: (b,) + (0,) * len(shape),
                                            pipeline_mode=pl.Buffered(1))
    bias_spec = pl.BlockSpec((NA_HEADS, None, blk, NA_WIN_PAIRS * NA_KT_W), lambda b, m: (0, _na_variant(m), 0, 0))
    return pl.pallas_call(
        _na_attn_lat_kernel,
        grid=(DEC_BATCH, nblk),
        in_specs=[row, per_batch(npair, D_MODEL, NA_KT_W), per_batch(DEC_SEQ, D_MODEL),
                  per_batch(D_MODEL, PAST_LEN), per_batch(PAST_LEN, D_MODEL), bias_spec],
        out_specs=row,
        out_shape=jax.ShapeDtypeStruct((N_LAT, D_MODEL), BF16),
        compiler_params=_cparams(("arbitrary", "arbitrary")),
        name="na_attn_lat",
    )(q, kt.reshape(DEC_BATCH, npair, D_MODEL, NA_KT_W), v.reshape(DEC_BATCH, DEC_SEQ, D_MODEL), kct, vc, bias_tab)


def _pow2_runs(n, src, dst, largest, make_copy):
    out = []
    b = largest
    while b >= SUBLANES:
        out.append(((n & b) != 0, make_copy(src, dst, b)))
        src = src + (n & b)
        dst = dst + (n & b)
        b //= 2
    return out


def _rows8(start, size):
    return pl.ds(pl.multiple_of(start, SUBLANES), size)


def _moe_scatter_kernel(seg_ref, pad_ref, na_ref, x_ref, sc_ref, sh_ref, rt_ref, xs_ref, srt_scr, zero_scr, sems, zsem):
    i = pl.program_id(0)
    tm = x_ref.shape[0]
    slot = i % 2
    hm = (x_ref[...] * (1.0 + sc_ref[...]) + sh_ref[...]).astype(BF16)
    rtt = rt_ref[...]
    row = lax.broadcasted_iota(jnp.int32, (MOE_LOCAL_ROWS, tm), 0).astype(F32)
    chosen = (row == rtt[RT_LOCAL:RT_LOCAL + 1, :]) | (row == rtt[RT_LOCAL + 1:RT_LOCAL + 2, :])
    srt_scr[slot] = _dot(jnp.where(chosen, 1.0, 0.0).astype(BF16), hm)

    def seg_copies(tile, s):
        out = []
        for e in range(N_EXPERTS):
            base = (tile * N_EXPERTS + e) * 3
            out += _pow2_runs(
                seg_ref[base], seg_ref[base + 1], seg_ref[base + 2], tm,
                lambda src, dst, b: pltpu.make_async_copy(srt_scr.at[s, _rows8(src, b)], xs_ref.at[_rows8(dst, b)],
                                                          sems.at[s]))
        return out

    for cond, cp in seg_copies(i, slot):
        pl.when(cond)(cp.start)

    @pl.when(i > 0)
    def _():
        for cond, cp in seg_copies(i - 1, 1 - slot):
            pl.when(cond)(cp.wait)

    @pl.when(i == pl.num_programs(0) - 1)
    def _():
        zero_scr[...] = jnp.zeros_like(zero_scr)

        def pad_copies():
            out = []
            for e in range(N_EXPERTS):
                out += _pow2_runs(
                    pad_ref[N_EXPERTS + e], 0, pad_ref[e], TM_MOE // 2,
                    lambda src, dst, b: pltpu.make_async_copy(zero_scr.at[pl.ds(0, b)], xs_ref.at[_rows8(dst, b)], zsem))
            return out

        def tile_copy(j):
            rows = pl.ds(pl.multiple_of(j * TM_MOE, TM_MOE), TM_MOE)
            return pltpu.make_async_copy(zero_scr, xs_ref.at[rows], zsem)

        for cond, cp in pad_copies():
            pl.when(cond)(cp.start)
        lax.fori_loop(na_ref[0], NT_MOE, lambda j, c: (tile_copy(j).start(), c)[1], 0)
        for cond, cp in pad_copies():
            pl.when(cond)(cp.wait)
        lax.fori_loop(na_ref[0], NT_MOE, lambda j, c: (tile_copy(j).wait(), c)[1], 0)
        for cond, cp in seg_copies(i, slot):
            pl.when(cond)(cp.wait)


def _moe_scatter(seg, pad, na, x_all, rt_all, mod):
    tm = TM_PROJ
    rf = _all_row_fn(tm)
    return pl.pallas_call(
        _moe_scatter_kernel,
        grid_spec=pltpu.PrefetchScalarGridSpec(
            num_scalar_prefetch=3,
            grid=(N_TOK // tm,),
            in_specs=[pl.BlockSpec((tm, D_MODEL), lambda i, *_: (i, 0)), _mod_spec(4, rf), _mod_spec(3, rf),
                      pl.BlockSpec((RT_WIDTH, tm), lambda i, *_: (0, i))],
            out_specs=pl.BlockSpec(memory_space=pl.ANY),
            scratch_shapes=[pltpu.VMEM((2, MOE_LOCAL_ROWS, D_MODEL), F32), pltpu.VMEM((TM_MOE, D_MODEL), F32),
                            pltpu.SemaphoreType.DMA((2,)), pltpu.SemaphoreType.DMA(())],
        ),
        out_shape=jax.ShapeDtypeStruct((ROWS_SORTED, D_MODEL), F32),
        compiler_params=_cparams(("arbitrary",)),
        name="moe_scatter",
    )(seg, pad, na, x_all, mod, mod, rt_all)


def _tile_changed(te_ref, j):
    prev = te_ref[jnp.maximum(j - 1, 0)]
    return (j == 0) | (te_ref[j] != prev)


def _moe_up_kernel(te_ref, na_ref, x_ref, wg_ref, wu_ref, o_ref, w_scr):
    j = pl.program_id(1)
    tf = wg_ref.shape[1]

    @pl.when(j < na_ref[0])
    def _():
        @pl.when(_tile_changed(te_ref, j))
        def _():
            w_scr[:, :tf] = wg_ref[...].astype(BF16)
            w_scr[:, tf:] = wu_ref[...].astype(BF16)

        gu = _dot(x_ref[...].astype(BF16), w_scr[...])
        o_ref[...] = (_silu(gu[:, :tf]) * gu[:, tf:]).astype(BF16)

    @pl.when(j >= na_ref[0])
    def _():
        o_ref[...] = jnp.zeros_like(o_ref)


def _moe_up(te, na, xs, w_in):
    tm, tf = TM_MOE, TF_MOE
    nf = FFN_EXPERT // tf
    row = lambda j, na: jnp.minimum(j, na[0] - 1)
    return pl.pallas_call(
        _moe_up_kernel,
        grid_spec=pltpu.PrefetchScalarGridSpec(
            num_scalar_prefetch=2,
            grid=(nf, NT_MOE),
            in_specs=[pl.BlockSpec((tm, D_MODEL), lambda f, j, te, na: (row(j, na), 0)),
                      pl.BlockSpec((None, D_MODEL, tf), lambda f, j, te, na: (te[j], 0, f)),
                      pl.BlockSpec((None, D_MODEL, tf), lambda f, j, te, na: (te[j], 0, nf + f))],
            out_specs=pl.BlockSpec((tm, tf), lambda f, j, te, na: (j, f)),
            scratch_shapes=[pltpu.VMEM((D_MODEL, 2 * tf), BF16)],
        ),
        out_shape=jax.ShapeDtypeStruct((ROWS_SORTED, FFN_EXPERT), BF16),
        compiler_params=_cparams(("arbitrary", "arbitrary")),
        name="moe_up",
    )(te, na, xs, w_in, w_in)


def _moe_down_kernel(te_ref, na_ref, h_ref, w_ref, o_ref, w_scr):
    j = pl.program_id(0)

    @pl.when(j < na_ref[0])
    def _():
        @pl.when(_tile_changed(te_ref, j))
        def _():
            w_scr[...] = w_ref[...].astype(BF16)

        o_ref[...] = _dot(h_ref[...], w_scr[...])

    @pl.when(j >= na_ref[0])
    def _():
        o_ref[...] = jnp.zeros_like(o_ref)


def _moe_down(te, na, hmid, w_out):
    tm = TM_MOE
    row = lambda j, na: jnp.minimum(j, na[0] - 1)
    return pl.pallas_call(
        _moe_down_kernel,
        grid_spec=pltpu.PrefetchScalarGridSpec(
            num_scalar_prefetch=2,
            grid=(NT_MOE,),
            in_specs=[pl.BlockSpec((tm, FFN_EXPERT), lambda j, te, na: (row(j, na), 0)),
                      pl.BlockSpec((None, FFN_EXPERT, D_MODEL), lambda j, te, na: (te[j], 0, 0))],
            out_specs=pl.BlockSpec((tm, D_MODEL), lambda j, te, na: (j, 0)),
            scratch_shapes=[pltpu.VMEM((FFN_EXPERT, D_MODEL), BF16)],
        ),
        out_shape=jax.ShapeDtypeStruct((ROWS_SORTED, D_MODEL), F32),
        compiler_params=_cparams(("arbitrary",)),
        name="moe_down",
    )(te, na, hmid, w_out)


def _moe_combine_kernel(pos_ref, y_ref, rt_ref, x_ref, g_ref, lg_ref, lb_ref, o_ref, ybuf, sems, *, tok_off):
    i = pl.program_id(0)
    tm = x_ref.shape[0]

    def fetch(tile, slot):
        def start(t, c):
            for k in range(TOP_K):
                p = pos_ref[k * N_TOK + tok_off + tile * tm + t]
                pltpu.make_async_copy(y_ref.at[pl.ds(p, 1)], ybuf.at[slot, k, pl.ds(t, 1)], sems.at[slot]).start()
            return c

        lax.fori_loop(0, tm, start, 0, unroll=8)

    @pl.when(i == 0)
    def _():
        fetch(0, 0)

    @pl.when(i + 1 < pl.num_programs(0))
    def _():
        fetch(i + 1, (i + 1) % 2)

    slot = i % 2
    pltpu.make_async_copy(ybuf.at[slot], ybuf.at[slot], sems.at[slot]).wait()
    rt = rt_ref[...]
    moe = rt[:, RT_GATE:RT_GATE + 1] * ybuf[slot, 0] + rt[:, RT_GATE + 1:RT_GATE + 2] * ybuf[slot, 1]
    o_ref[...] = _layer_norm(ALPHA * x_ref[...] + g_ref[...] * moe, lg_ref[...], lb_ref[...])


def _moe_combine(pos, y, rt_all, x_all, mod, lg, lb, latent):
    n = N_LAT if latent else N_CTX
    tm = TM_ROUTE
    rf = _row_fn(latent, tm)
    tok_off = N_CTX if latent else 0
    off = tok_off // tm
    return pl.pallas_call(
        functools.partial(_moe_combine_kernel, tok_off=tok_off),
        grid_spec=pltpu.PrefetchScalarGridSpec(
            num_scalar_prefetch=1,
            grid=(n // tm,),
            in_specs=[pl.BlockSpec(memory_space=pl.ANY),
                      pl.BlockSpec((tm, LANES), lambda i, pos: (i + off, 0)),
                      pl.BlockSpec((tm, D_MODEL), lambda i, pos: (i + off, 0)),
                      _mod_spec(5, rf),
                      pl.BlockSpec((1, D_MODEL), lambda i, pos: (0, 0)),
                      pl.BlockSpec((1, D_MODEL), lambda i, pos: (0, 0))],
            out_specs=pl.BlockSpec((tm, D_MODEL), lambda i, pos: (i, 0)),
            scratch_shapes=[pltpu.VMEM((2, TOP_K, tm, D_MODEL), F32), pltpu.SemaphoreType.DMA((2,))],
        ),
        out_shape=jax.ShapeDtypeStruct((n, D_MODEL), F32),
        compiler_params=_cparams(("arbitrary",)),
        name="moe_combine_lat" if latent else "moe_combine_ctx",
    )(pos, y, rt_all, x_all, mod, lg, lb)


def _routing_positions(rtt_all, counts, meta):
    tm = TM_MOE
    cnt = counts[0, :N_EXPERTS].astype(jnp.int32)
    padded = ((cnt + tm - 1) // tm) * tm
    gend = jnp.cumsum(padded)
    gstart = gend - padded
    route = rtt_all.astype(jnp.int32)
    onehot = lambda idx: idx[None, :] == jnp.arange(N_EXPERTS, dtype=jnp.int32)[:, None]
    start_of = lambda idx: jnp.sum(jnp.where(onehot(idx), gstart[:, None], 0), axis=0)
    pos = jnp.concatenate([start_of(route[RT_EXPERT + k]) + route[RT_ROW + k] for k in range(TOP_K)])
    pad = jnp.concatenate([gstart + cnt, padded - cnt]).astype(jnp.int32)
    tile_start = jnp.arange(NT_MOE, dtype=jnp.int32) * tm
    te = jnp.sum((tile_start[:, None] >= gend[None, :]).astype(jnp.int32), axis=1)
    n_active = (gend[-1] // tm).astype(jnp.int32)
    last_e = jnp.take(te, jnp.maximum(n_active - 1, 0))
    te = jnp.where(tile_start < gend[-1], te, last_e).astype(jnp.int32)
    m = meta[:, :3, :N_EXPERTS].astype(jnp.int32)
    seg = jnp.stack([m[:, 0], m[:, 1], m[:, 2] + gstart[None, :]], axis=-1).reshape(-1)
    return pos, seg, pad, te, n_active.reshape(1)


def kernel(x_prompt, x_sample, cache_ckv_l0, cache_krope_l0, cache_k_l1, cache_v_l1, c, c_ctx, w_ada_l0, b_ada_l0, mla_w_dq, mla_g_q, mla_w_uq, mla_w_dkv, mla_g_kv, mla_w_ukv, mla_w_o, ln1_g_l0, ln1_b_l0, ffn_w_in, ffn_w_out, ln2_g_l0, ln2_b_l0, w_ada_l1, b_ada_l1, na_w_qkv, na_rpb, na_w_o, ln1_g_l1, ln1_b_l1, moe_w_router, moe_w_in, moe_w_out, ln2_g_l1, ln2_b_l1):
    row = lambda v: v.reshape(1, -1)
    xp = x_prompt.reshape(N_CTX, D_MODEL)
    xs = x_sample.reshape(N_LAT, D_MODEL)
    groups = ((xp, False), (xs, True))

    cvecs = jnp.concatenate([c_ctx[None], c, jnp.zeros((SUBLANES - 1 - DEC_BATCH, D_MODEL), F32)], axis=0)
    mod0 = _ada_mod(cvecs, w_ada_l0, b_ada_l0)
    mod1 = _ada_mod(cvecs, w_ada_l1, b_ada_l1)

    wa = jnp.concatenate([mla_w_dq, mla_w_dkv], axis=1).astype(BF16)
    wuq = mla_w_uq.reshape(MLA_Q_LORA, MLA_HEADS, MLA_NOPE + MLA_ROPE)
    wuq = jnp.pad(wuq, ((0, 0), (0, 0), (0, Q_HEAD_PAD - MLA_NOPE - MLA_ROPE)))
    wuq = wuq.reshape(MLA_Q_LORA, MLA_HEADS * Q_HEAD_PAD).astype(BF16)
    wukv = mla_w_ukv.astype(BF16)
    wo0 = mla_w_o.astype(BF16)
    w_in0 = ffn_w_in.astype(BF16)
    w_out0 = ffn_w_out.astype(BF16)
    wqkv = na_w_qkv.astype(BF16)
    wo1 = na_w_o.astype(BF16)
    wr = jnp.pad(moe_w_router, ((0, 0), (0, LANES - N_EXPERTS))).astype(BF16)
    bias_tab = _na_bias_table(na_rpb)
    tab = _rope_tables()

    kvx_cache = _matmul(cache_ckv_l0.reshape(DEC_BATCH * PAST_LEN, MLA_KV_LORA), wukv, BF16, 512, "mla_expand_cache")
    kvx_cache = kvx_cache.reshape(DEC_BATCH, PAST_LEN, -1)
    krp_cache = jnp.pad(cache_krope_l0, ((0, 0), (0, 0), (0, LANES - MLA_ROPE))).astype(BF16)
    x1 = []
    new_ckv = new_kr = None
    vt_cache = kvx_cache.reshape(DEC_BATCH, PAST_LEN, MLA_HEADS, MLA_NOPE + MLA_V)[..., MLA_NOPE:]
    vt_cache = vt_cache.reshape(DEC_BATCH, PAST_LEN, MLA_HEADS * MLA_V).transpose(0, 2, 1)
    for x, latent in groups:
        proj = _mla_proj(x, mod0, wa, row(mla_g_q), row(mla_g_kv), wuq, wukv, tab, latent)
        if latent:
            qt, kvx, krp, vt = proj
            o = _mla_attn_lat(qt, kvx, krp, kvx_cache, krp_cache, vt, vt_cache)
        else:
            q, kvx, new_ckv, new_kr, krp = proj
            o = _mla_attn_ctx(q, kvx, krp)
        x1.append(_attn_out_ffn(o, x, wo0, mod0, row(ln1_g_l0), row(ln1_b_l0), w_in0, w_out0,
                                row(ln2_g_l0), row(ln2_b_l0), latent))

    kct = cache_k_l1.reshape(DEC_BATCH, PAST_LEN, D_MODEL).transpose(0, 2, 1).astype(BF16)
    vc = cache_v_l1.reshape(DEC_BATCH, PAST_LEN, D_MODEL).astype(BF16)
    q, v, new_kt, new_vt = _na_qkv(x1[0], mod1, wqkv, False)
    o_ctx = _na_attn_ctx(q, new_kt, v)
    q, v, kt = _na_qkv(x1[1], mod1, wqkv, True)
    o_lat = _na_attn_lat(q, kt, v, kct, vc, bias_tab)
    x2_all, rt_all, rtt_all, counts, meta = _proj_ln_route(o_ctx, x1[0], o_lat, x1[1], wo1, mod1, row(ln1_g_l1),
                                                           row(ln1_b_l1), wr)

    pos, seg, pad, te, n_active = _routing_positions(rtt_all, counts, meta)
    x_sorted = _moe_scatter(seg, pad, n_active, x2_all, rtt_all, mod1)
    hmid = _moe_up(te, n_active, x_sorted, moe_w_in)
    y = _moe_down(te, n_active, hmid, moe_w_out)
    outs = [_moe_combine(pos, y, rt_all, x2_all, mod1, row(ln2_g_l1), row(ln2_b_l1), latent)
            for latent in (False, True)]

    return (outs[0].reshape(BATCH, SEQ, D_MODEL),
            outs[1].reshape(DEC_BATCH, DEC_SEQ, D_MODEL),
            new_ckv.reshape(BATCH, SEQ, MLA_KV_LORA),
            new_kr.transpose(0, 2, 1),
            new_kt.reshape(BATCH, NA_HEADS, NA_HD, SEQ).transpose(0, 3, 1, 2),
            new_vt.reshape(BATCH, NA_HEADS, NA_HD, SEQ).transpose(0, 3, 1, 2))
```

```python
import functools
import math

import numpy as np
import jax
import jax.numpy as jnp
from jax import lax
from jax.experimental import pallas as pl
from jax.experimental.pallas import tpu as pltpu

F32 = jnp.float32
BF16 = jnp.bfloat16

D_MODEL = 1024
BATCH = 32
SEQ = 256
DEPTH = 2
DEC_BATCH = 2
DEC_SEQ = 2048
PAST_LEN = 512
GRID_W = 64
MLA_HEADS = 8
MLA_NOPE = 128
MLA_ROPE = 64
MLA_V = 128
MLA_Q_LORA = 512
MLA_KV_LORA = 256
MLA_SCALE = 1.0 / math.sqrt(MLA_NOPE + MLA_ROPE)
ROPE_THETA = 10000.0
NA_HEADS = 16
NA_HD = D_MODEL // NA_HEADS
NA_WIN_ROWS = 8
NA_WIN_COLS = 16
NA_SCALE = 1.0 / math.sqrt(NA_HD)
FFN_DENSE = 2816
N_EXPERTS = 8
TOP_K = 2
FFN_EXPERT = 3584
ALPHA = (2 * DEPTH) ** 0.25
LN_EPS = 1e-5
RMS_EPS = 1e-6

N_CTX = BATCH * SEQ
N_LAT = DEC_BATCH * DEC_SEQ
N_TOK = N_CTX + N_LAT
GRID_ROWS = DEC_SEQ // GRID_W
Q_HEAD_PAD = 256
LANES = 128
SUBLANES = 8
NEG_BIG = -1e30
MLA_KEY_CHUNK = 256
NA_PAIR = 2 * NA_HD
NA_BLK_ROWS = 4
NA_KT_W = 2 * GRID_W
NA_WIN_PAIRS = (NA_BLK_ROWS + NA_WIN_ROWS) // 2
NA_N_BLOCKS = DEC_SEQ // (NA_BLK_ROWS * GRID_W)
NA_BLOCK_VARIANTS = (0, 1, NA_N_BLOCKS - 1)

VMEM_LIMIT = 56 * 1024 * 1024

TM_PROJ = 512
TM_FFN = 512
CTX_BATCH_PER_STEP = 4
MXU_TILE = 256
FFN_CHUNKS = ((0, 6 * MXU_TILE), (6 * MXU_TILE, FFN_DENSE))
TM_MOE = 512
TF_MOE = 1792
N_PAIRS = N_TOK * TOP_K
MOE_LOCAL_ROWS = TOP_K * TM_PROJ + N_EXPERTS * SUBLANES
N_SEG_PAD = N_EXPERTS * (N_TOK // TM_PROJ) * (SUBLANES - 1)
NT_MOE = -(-(N_PAIRS + N_SEG_PAD + N_EXPERTS * (TM_MOE - 1)) // TM_MOE)
ROWS_SORTED = NT_MOE * TM_MOE
TM_ROUTE = 512
RT_EXPERT, RT_GATE, RT_ROW, RT_LOCAL = 0, 2, 4, 6
RT_WIDTH = 8


def _cparams(sem, vmem=VMEM_LIMIT):
    return pltpu.CompilerParams(dimension_semantics=sem, vmem_limit_bytes=vmem)


def _silu(x):
    return x * jax.nn.sigmoid(x)


def _layer_norm(y, g, b):
    mu = jnp.mean(y, axis=-1, keepdims=True)
    d = y - mu
    var = jnp.mean(d * d, axis=-1, keepdims=True)
    return d * lax.rsqrt(var + LN_EPS) * g + b


def _rms_norm(y, g):
    return y * lax.rsqrt(jnp.mean(y * y, axis=-1, keepdims=True) + RMS_EPS) * g


def _dot(a, b):
    return jnp.dot(a, b, preferred_element_type=F32)


def _dot_nt(a, b):
    return lax.dot_general(a, b, (((1,), (1,)), ((), ())), preferred_element_type=F32)


def _ada_kernel(c_ref, w_ref, b_ref, o_ref):
    s = _silu(c_ref[...]).astype(BF16)
    o_ref[...] = _dot(s, w_ref[...].astype(BF16)) + b_ref[...]


def _ada_mod(cvecs, w, b):
    tn = 1536
    n_vec = 1 + DEC_BATCH
    m = pl.pallas_call(
        _ada_kernel,
        grid=(6 * D_MODEL // tn,),
        in_specs=[pl.BlockSpec((SUBLANES, D_MODEL), lambda j: (0, 0)),
                  pl.BlockSpec((D_MODEL, tn), lambda j: (0, j)),
                  pl.BlockSpec((1, tn), lambda j: (0, j))],
        out_specs=pl.BlockSpec((SUBLANES, tn), lambda j: (0, j)),
        out_shape=jax.ShapeDtypeStruct((SUBLANES, 6 * D_MODEL), F32),
        compiler_params=_cparams(("arbitrary",)),
        name="ada_mod",
    )(cvecs, w, b.reshape(1, -1))
    return m[:n_vec].reshape(n_vec * 6, 1, D_MODEL)


def _mod_spec(j, row_fn):
    return pl.BlockSpec((None, 1, D_MODEL), lambda i, *_: (row_fn(i) * 6 + j, 0, 0))


def _row_fn(latent, tm):
    if not latent:
        return lambda i: 0
    per = DEC_SEQ // tm
    return lambda i: 1 + i // per


def _const_spec(shape):
    nd = len(shape)
    return pl.BlockSpec(shape, lambda *_: (0,) * nd, pipeline_mode=pl.Buffered(1))


def _rope_kernel(invf_ref, o_ref):
    i = pl.program_id(0)
    tm = o_ref.shape[1]
    t = i * tm + lax.broadcasted_iota(jnp.int32, (tm, LANES), 0)
    lane = lax.broadcasted_iota(jnp.int32, (tm, LANES), 1)
    row = t >> int(math.log2(GRID_W))
    col = t & (GRID_W - 1)
    pos = jnp.where(lane < MLA_ROPE // 2, row, col).astype(F32)
    ang = pos * invf_ref[...]
    cos = jnp.cos(ang)
    sin = jnp.sin(ang)
    unit = lane >> int(math.log2(MLA_ROPE // 4))
    first = (unit == 0) | (unit == 2)
    second = (unit == 1) | (unit == 3)
    o_ref[0] = jnp.where(lane < MLA_ROPE, cos, 0.0)
    o_ref[1] = jnp.where(first, -sin, 0.0)
    o_ref[2] = jnp.where(second, sin, 0.0)


def _rope_tables():
    half = MLA_ROPE // 2
    inv_freq = (1.0 / (ROPE_THETA ** (np.arange(0, half, 2, dtype=np.float32) / half))).astype(np.float32)
    lane_f = np.zeros((1, LANES), np.float32)
    lane_f[0, :MLA_ROPE] = np.tile(inv_freq, 4)
    tm = 256
    return pl.pallas_call(
        _rope_kernel,
        grid=(DEC_SEQ // tm,),
        in_specs=[_const_spec((1, LANES))],
        out_specs=pl.BlockSpec((3, tm, LANES), lambda i: (0, i, 0)),
        out_shape=jax.ShapeDtypeStruct((3, DEC_SEQ, LANES), F32),
        compiler_params=_cparams(("arbitrary",)),
        name="rope_tables",
    )(jnp.asarray(lane_f))


def _rotate(v, tab_ref):
    return (v * tab_ref[0] + pltpu.roll(v, LANES - MLA_ROPE // 4, 1) * tab_ref[1]
            + pltpu.roll(v, MLA_ROPE // 4, 1) * tab_ref[2])


def _mla_proj_kernel(*refs, rope):
    if rope:
        (x_ref, sc_ref, sh_ref, wa_ref, gq_ref, gkv_ref, wuq_ref, wukv_ref, tab_ref,
         qt_ref, kvx_ref, krp_ref, vt_ref) = refs
    else:
        (x_ref, sc_ref, sh_ref, wa_ref, gq_ref, gkv_ref, wuq_ref, wukv_ref,
         q_ref, kvx_ref, ckv_ref, kr_ref, krp_ref) = refs
    h = (x_ref[...] * (1.0 + sc_ref[...]) + sh_ref[...]).astype(BF16)
    t = _dot(h, wa_ref[...])
    cq = _rms_norm(t[:, :MLA_Q_LORA], gq_ref[...])
    ckv = _rms_norm(t[:, MLA_Q_LORA:MLA_Q_LORA + MLA_KV_LORA], gkv_ref[...])
    kr = t[:, MLA_Q_LORA + MLA_KV_LORA:]
    kvx = _dot(ckv.astype(BF16), wukv_ref[...])
    kvx_ref[...] = kvx.astype(BF16)
    q = _dot(cq.astype(BF16), wuq_ref[...])
    krp = jnp.concatenate([kr, jnp.zeros_like(kr)], axis=-1)
    if rope:
        krp_ref[...] = _rotate(krp, tab_ref).astype(BF16)
        parts = []
        for hd in range(MLA_HEADS):
            lo = hd * Q_HEAD_PAD
            parts += [q[:, lo:lo + MLA_NOPE], _rotate(q[:, lo + MLA_NOPE:lo + Q_HEAD_PAD], tab_ref)]
        qt_ref[...] = jnp.concatenate(parts, axis=1).T.astype(BF16)
        vcols = [kvx[:, hd * Q_HEAD_PAD + MLA_NOPE:(hd + 1) * Q_HEAD_PAD] for hd in range(MLA_HEADS)]
        vt_ref[...] = jnp.concatenate(vcols, axis=1).T.astype(BF16)
    else:
        ckv_ref[...] = ckv
        krt = kr.T
        for t in range(kr_ref.shape[0]):
            kr_ref[t] = krt[:, t * SEQ:(t + 1) * SEQ]
        krp_ref[...] = krp.astype(BF16)
        q_ref[...] = q.astype(BF16)


def _mla_proj(x, mod, wa, gq, gkv, wuq, wukv, tab, latent):
    n = x.shape[0]
    tm = TM_PROJ
    rf = _row_fn(latent, tm)
    tok = lambda w: pl.BlockSpec((tm, w), lambda i: (i, 0))
    in_specs = [tok(D_MODEL), _mod_spec(1, rf), _mod_spec(0, rf),
                _const_spec(wa.shape), _const_spec(gq.shape), _const_spec(gkv.shape),
                _const_spec(wuq.shape), _const_spec(wukv.shape)]
    args = [x, mod, mod, wa, gq, gkv, wuq, wukv]
    wq, wkv = MLA_HEADS * Q_HEAD_PAD, MLA_HEADS * (MLA_NOPE + MLA_V)
    if latent:
        per = DEC_SEQ // tm
        in_specs.append(pl.BlockSpec((3, tm, LANES), lambda i: (0, i % per, 0)))
        args.append(tab)
        out_specs = [pl.BlockSpec((wq, tm), lambda i: (0, i)), tok(wkv), tok(LANES),
                     pl.BlockSpec((None, MLA_HEADS * MLA_V, tm), lambda i: (i // per, 0, i % per))]
        out_shape = [jax.ShapeDtypeStruct((wq, n), BF16), jax.ShapeDtypeStruct((n, wkv), BF16),
                     jax.ShapeDtypeStruct((n, LANES), BF16),
                     jax.ShapeDtypeStruct((DEC_BATCH, MLA_HEADS * MLA_V, DEC_SEQ), BF16)]
    else:
        out_specs = [tok(wq), tok(wkv), tok(MLA_KV_LORA),
                     pl.BlockSpec((tm // SEQ, MLA_ROPE, SEQ), lambda i: (i, 0, 0)), tok(LANES)]
        out_shape = [jax.ShapeDtypeStruct((n, wq), BF16), jax.ShapeDtypeStruct((n, wkv), BF16),
                     jax.ShapeDtypeStruct((n, MLA_KV_LORA), F32), jax.ShapeDtypeStruct((n // SEQ, MLA_ROPE, SEQ), F32),
                     jax.ShapeDtypeStruct((n, LANES), BF16)]
    return pl.pallas_call(
        functools.partial(_mla_proj_kernel, rope=latent),
        grid=(n // tm,),
        in_specs=in_specs,
        out_specs=out_specs,
        out_shape=out_shape,
        compiler_params=_cparams(("arbitrary",)),
        name="mla_proj_lat" if latent else "mla_proj_ctx",
    )(*args)


def _matmul_kernel(a_ref, b_ref, o_ref):
    o_ref[...] = _dot(a_ref[...].astype(BF16), b_ref[...]).astype(o_ref.dtype)


def _matmul(a, b, out_dtype, tm, name):
    m, k = a.shape
    n = b.shape[1]
    return pl.pallas_call(
        _matmul_kernel,
        grid=(m // tm,),
        in_specs=[pl.BlockSpec((tm, k), lambda i: (i, 0)), _const_spec(b.shape)],
        out_specs=pl.BlockSpec((tm, n), lambda i: (i, 0)),
        out_shape=jax.ShapeDtypeStruct((m, n), out_dtype),
        compiler_params=_cparams(("arbitrary",)),
        name=name,
    )(a, b)


def _mla_head(qh, kparts, vparts):
    ss = [_dot_nt(qh, k) for k in kparts]
    m = functools.reduce(jnp.maximum, [jnp.max(s, axis=-1, keepdims=True) for s in ss])
    ps = [jnp.exp2((s - m) * (MLA_SCALE * math.log2(math.e))) for s in ss]
    l = functools.reduce(jnp.add, [jnp.sum(p, axis=-1, keepdims=True) for p in ps])
    o = functools.reduce(jnp.add, [_dot(p.astype(BF16), v) for p, v in zip(ps, vparts)])
    return o / l


def _mla_attn_ctx_kernel(q_ref, kvx_ref, krp_ref, o_ref):
    for b in range(CTX_BATCH_PER_STEP):
        rows = slice(b * SEQ, (b + 1) * SEQ)
        krp = krp_ref[rows, :]
        for hd in range(MLA_HEADS):
            lo = hd * (MLA_NOPE + MLA_V)
            qh = q_ref[rows, hd * Q_HEAD_PAD:(hd + 1) * Q_HEAD_PAD]
            kh = jnp.concatenate([kvx_ref[rows, lo:lo + MLA_NOPE], krp], axis=-1)
            vh = kvx_ref[rows, lo + MLA_NOPE:lo + MLA_NOPE + MLA_V]
            o_ref[rows, hd * MLA_V:(hd + 1) * MLA_V] = _mla_head(qh, [kh], [vh]).astype(BF16)


def _mla_attn_ctx(q, kvx, krp):
    tok = lambda w: pl.BlockSpec((CTX_BATCH_PER_STEP * SEQ, w), lambda b: (b, 0))
    return pl.pallas_call(
        _mla_attn_ctx_kernel,
        grid=(BATCH // CTX_BATCH_PER_STEP,),
        in_specs=[tok(q.shape[1]), tok(kvx.shape[1]), tok(LANES)],
        out_specs=tok(MLA_HEADS * MLA_V),
        out_shape=jax.ShapeDtypeStruct((N_CTX, MLA_HEADS * MLA_V), BF16),
        compiler_params=_cparams(("arbitrary",)),
        name="mla_attn_ctx",
    )(q, kvx, krp)


def _mla_attn_lat_kernel(qt_ref, kvl_ref, krl_ref, kvc_ref, krc_ref, vtl_ref, vtc_ref, o_ref):
    chunks = [(kvl_ref, krl_ref, vtl_ref, c) for c in range(DEC_SEQ // MLA_KEY_CHUNK)]
    chunks += [(kvc_ref, krc_ref, vtc_ref, c) for c in range(PAST_LEN // MLA_KEY_CHUNK)]
    for hd in range(MLA_HEADS):
        lo = hd * (MLA_NOPE + MLA_V)
        qt = qt_ref[hd * Q_HEAD_PAD:(hd + 1) * Q_HEAD_PAD, :]
        ss = []
        for kv_ref, kr_ref, _, c in chunks:
            rows = slice(c * MLA_KEY_CHUNK, (c + 1) * MLA_KEY_CHUNK)
            ss.append(_dot(jnp.concatenate([kv_ref[rows, lo:lo + MLA_NOPE], kr_ref[rows, :]], axis=-1), qt))
        m = functools.reduce(jnp.maximum, [jnp.max(s, axis=0, keepdims=True) for s in ss])
        ps = [jnp.exp2((s - m) * (MLA_SCALE * math.log2(math.e))) for s in ss]
        l = functools.reduce(jnp.add, [jnp.sum(p, axis=0, keepdims=True) for p in ps])
        ots = [_dot(vt_ref[hd * MLA_V:(hd + 1) * MLA_V, c * MLA_KEY_CHUNK:(c + 1) * MLA_KEY_CHUNK], p.astype(BF16))
               for (_, _, vt_ref, c), p in zip(chunks, ps)]
        ot = functools.reduce(jnp.add, ots) / l
        o_ref[:, hd * MLA_V:(hd + 1) * MLA_V] = ot.T.astype(BF16)


def _mla_attn_lat(qt, kvx_lat, krp_lat, kvx_ctx, krp_ctx, vt_lat, vt_ctx):
    tq = 256
    per = DEC_SEQ // tq
    wkv = kvx_lat.shape[-1]
    wv = MLA_HEADS * MLA_V
    batch = lambda rows, cols: pl.BlockSpec((None, rows, cols), lambda b, i: (b, 0, 0))
    return pl.pallas_call(
        _mla_attn_lat_kernel,
        grid=(DEC_BATCH, per),
        in_specs=[pl.BlockSpec((qt.shape[0], tq), lambda b, i: (0, b * per + i)),
                  batch(DEC_SEQ, wkv), batch(DEC_SEQ, LANES), batch(PAST_LEN, wkv), batch(PAST_LEN, LANES),
                  batch(wv, DEC_SEQ), batch(wv, PAST_LEN)],
        out_specs=pl.BlockSpec((tq, wv), lambda b, i: (b * per + i, 0)),
        out_shape=jax.ShapeDtypeStruct((N_LAT, wv), BF16),
        compiler_params=_cparams(("arbitrary", "arbitrary")),
        name="mla_attn_lat",
    )(qt, kvx_lat.reshape(DEC_BATCH, DEC_SEQ, wkv), krp_lat.reshape(DEC_BATCH, DEC_SEQ, LANES), kvx_ctx, krp_ctx,
      vt_lat, vt_ctx)


def _top2(logits):
    lane = lax.broadcasted_iota(jnp.int32, logits.shape, 1).astype(F32)
    m1 = jnp.max(logits, axis=-1, keepdims=True)
    i1 = jnp.min(jnp.where(logits == m1, lane, float(LANES)), axis=-1, keepdims=True)
    rest = jnp.where(lane == i1, -jnp.inf, logits)
    m2 = jnp.max(rest, axis=-1, keepdims=True)
    i2 = jnp.min(jnp.where(rest == m2, lane, float(LANES)), axis=-1, keepdims=True)
    e = jnp.exp(m2 - m1)
    w1 = 1.0 / (1.0 + e)
    w2 = e / (1.0 + e)
    return lane, i1, i2, w1, w2


def _proj_ln_route_kernel(oc_ref, xc_ref, ol_ref, xl_ref, wo_ref, g_ref, lg_ref, lb_ref, sc_ref, sh_ref, wr_ref,
                          x1_ref, rt_ref, rtt_ref, cnt_ref, meta_ref, carry, *, n_ctx_tiles):
    i = pl.program_id(0)

    @pl.when(i == 0)
    def _():
        carry[...] = jnp.zeros_like(carry)

    def body(o_ref, x_ref):
        out = _dot(o_ref[...], wo_ref[...])
        x1 = _layer_norm(ALPHA * x_ref[...] + g_ref[...] * out, lg_ref[...], lb_ref[...])
        x1_ref[...] = x1
        hm = x1 * (1.0 + sc_ref[...]) + sh_ref[...]
        logits = _dot(hm.astype(BF16), wr_ref[...])
        lane = lax.broadcasted_iota(jnp.int32, logits.shape, 1)
        logits = jnp.where(lane < N_EXPERTS, logits, -jnp.inf)
        lane_f, i1, i2, w1, w2 = _top2(logits)
        tm = logits.shape[0]
        oh1 = (lane_f == i1).astype(F32)
        oh2 = (lane_f == i2).astype(F32)
        rr = lax.broadcasted_iota(jnp.int32, (tm, tm), 0)
        cc = lax.broadcasted_iota(jnp.int32, (tm, tm), 1)
        below = (cc < rr).astype(BF16)
        tot1 = jnp.sum(oh1, axis=0, keepdims=True)
        tot2 = jnp.sum(oh2, axis=0, keepdims=True)
        seg_len = jnp.floor((tot1 + tot2 + (SUBLANES - 1)) * (1.0 / SUBLANES)) * SUBLANES
        e_row = lax.broadcasted_iota(jnp.int32, (LANES, LANES), 0)
        e_col = lax.broadcasted_iota(jnp.int32, (LANES, LANES), 1)
        before = (e_row < e_col).astype(BF16)
        seg_loc = _dot(jnp.broadcast_to(seg_len, (SUBLANES, LANES)).astype(BF16), before)[0:1]
        seg_glob = carry[...]
        cum1 = _dot(below, oh1.astype(BF16))
        cum2 = _dot(below, oh2.astype(BF16)) + tot1
        pick = lambda oh, v: jnp.sum(oh * v, axis=-1, keepdims=True)
        vals = {RT_EXPERT: (i1, i2), RT_GATE: (w1, w2),
                RT_ROW: (pick(oh1, cum1 + seg_glob), pick(oh2, cum2 + seg_glob)),
                RT_LOCAL: (pick(oh1, cum1 + seg_loc), pick(oh2, cum2 + seg_loc))}
        rt = jnp.zeros_like(logits)
        for first_lane, pair in vals.items():
            for k, val in enumerate(pair):
                rt = jnp.where(lane == first_lane + k, val, rt)
        rt_ref[...] = rt
        rtt_ref[...] = rt.T[:RT_WIDTH]
        sub = lax.broadcasted_iota(jnp.int32, (SUBLANES, LANES), 0)
        meta_ref[...] = jnp.where(sub == 0, seg_len, jnp.where(sub == 1, seg_loc, jnp.where(sub == 2, seg_glob, 0.0)))
        carry[...] = seg_glob + seg_len
        cnt_ref[...] = jnp.broadcast_to(carry[...], cnt_ref.shape)

    is_ctx = i < n_ctx_tiles
    pl.when(is_ctx)(lambda: body(oc_ref, xc_ref))
    pl.when(jnp.logical_not(is_ctx))(lambda: body(ol_ref, xl_ref))


def _all_row_fn(tm):
    nc = N_CTX // tm
    per = DEC_SEQ // tm
    return lambda i: jnp.where(i < nc, 0, 1 + (i - nc) // per)


def _proj_ln_route(o_ctx, x_ctx, o_lat, x_lat, wo, mod, lg, lb, wr):
    tm = TM_PROJ
    nc = N_CTX // tm
    rf = _all_row_fn(tm)
    ctx = lambda w: pl.BlockSpec((tm, w), lambda i: (jnp.minimum(i, nc - 1), 0))
    lat = lambda w: pl.BlockSpec((tm, w), lambda i: (jnp.maximum(i - nc, 0), 0))
    tok = lambda w: pl.BlockSpec((tm, w), lambda i: (i, 0))
    return pl.pallas_call(
        functools.partial(_proj_ln_route_kernel, n_ctx_tiles=nc),
        grid=(N_TOK // tm,),
        in_specs=[ctx(D_MODEL), ctx(D_MODEL), lat(D_MODEL), lat(D_MODEL), _const_spec(wo.shape), _mod_spec(2, rf),
                  _const_spec(lg.shape), _const_spec(lb.shape), _mod_spec(4, rf), _mod_spec(3, rf),
                  _const_spec(wr.shape)],
        out_specs=[tok(D_MODEL), tok(LANES), pl.BlockSpec((RT_WIDTH, tm), lambda i: (0, i)),
                   pl.BlockSpec((SUBLANES, LANES), lambda i: (0, 0)),
                   pl.BlockSpec((None, SUBLANES, LANES), lambda i: (i, 0, 0))],
        out_shape=[jax.ShapeDtypeStruct((N_TOK, D_MODEL), F32),
                   jax.ShapeDtypeStruct((N_TOK, LANES), F32),
                   jax.ShapeDtypeStruct((RT_WIDTH, N_TOK), F32),
                   jax.ShapeDtypeStruct((SUBLANES, LANES), F32),
                   jax.ShapeDtypeStruct((N_TOK // tm, SUBLANES, LANES), F32)],
        scratch_shapes=[pltpu.VMEM((1, LANES), F32)],
        compiler_params=_cparams(("arbitrary",)),
        name="proj_ln_route",
    )(o_ctx, x_ctx, o_lat, x_lat, wo, mod, lg, lb, mod, mod, wr)


def _attn_out_ffn_kernel(o_ref, x_ref, wo_ref, g1_ref, lg1_ref, lb1_ref, sc_ref, sh_ref, g2_ref,
                         wi_ref, wd_ref, lg2_ref, lb2_ref, y_ref):
    x1 = _layer_norm(ALPHA * x_ref[...] + g1_ref[...] * _dot(o_ref[...], wo_ref[...]), lg1_ref[...], lb1_ref[...])
    h = (x1 * (1.0 + sc_ref[...]) + sh_ref[...]).astype(BF16)
    acc = None
    for lo, hi in FFN_CHUNKS:
        gate = _dot(h, wi_ref[:, lo:hi])
        up = _dot(h, wi_ref[:, FFN_DENSE + lo:FFN_DENSE + hi])
        part = _dot((_silu(gate) * up).astype(BF16), wd_ref[lo:hi, :])
        acc = part if acc is None else acc + part
    y_ref[...] = _layer_norm(ALPHA * x1 + g2_ref[...] * acc, lg2_ref[...], lb2_ref[...])


def _attn_out_ffn(o, x, wo, mod, lg1, lb1, w_in, w_out, lg2, lb2, latent):
    n = x.shape[0]
    tm = TM_FFN
    rf = _row_fn(latent, tm)
    tok = lambda w: pl.BlockSpec((tm, w), lambda i: (i, 0))
    vec = _const_spec((1, D_MODEL))
    return pl.pallas_call(
        _attn_out_ffn_kernel,
        grid=(n // tm,),
        in_specs=[tok(o.shape[1]), tok(D_MODEL), _const_spec(wo.shape), _mod_spec(2, rf), vec, vec,
                  _mod_spec(4, rf), _mod_spec(3, rf), _mod_spec(5, rf),
                  _const_spec(w_in.shape), _const_spec(w_out.shape), vec, vec],
        out_specs=tok(D_MODEL),
        out_shape=jax.ShapeDtypeStruct((n, D_MODEL), F32),
        compiler_params=_cparams(("arbitrary",)),
        name="attn_out_ffn_lat" if latent else "attn_out_ffn_ctx",
    )(o, x, wo, mod, lg1, lb1, mod, mod, mod, w_in, w_out, lg2, lb2)


def _na_qkv_kernel(x_ref, sc_ref, sh_ref, w_ref, q_ref, v_ref, kt_ref, *vt_refs):
    h = (x_ref[...] * (1.0 + sc_ref[...]) + sh_ref[...]).astype(BF16)
    qkv = _dot(h, w_ref[...])
    q_ref[...] = (qkv[:, :D_MODEL] * NA_SCALE).astype(BF16)
    v = qkv[:, 2 * D_MODEL:]
    v_ref[...] = v.astype(BF16)
    pairs = [(kt_ref, qkv[:, D_MODEL:2 * D_MODEL])] + [(ref, v) for ref in vt_refs]
    for ref, val in pairs:
        valt = val.T
        nblk, _, w = ref.shape
        for t in range(nblk):
            ref[t] = valt[:, t * w:(t + 1) * w].astype(ref.dtype)


def _na_qkv(x, mod, w, latent):
    n = x.shape[0]
    tm = TM_PROJ
    kt_w = NA_KT_W if latent else SEQ
    rf = _row_fn(latent, tm)
    tok = pl.BlockSpec((tm, D_MODEL), lambda i: (i, 0))
    tspec = pl.BlockSpec((tm // kt_w, D_MODEL, kt_w), lambda i: (i, 0, 0))
    tshape = lambda dt: jax.ShapeDtypeStruct((n // kt_w, D_MODEL, kt_w), dt)
    out_specs = [tok, tok, tspec]
    out_shape = [jax.ShapeDtypeStruct((n, D_MODEL), BF16), jax.ShapeDtypeStruct((n, D_MODEL), BF16),
                 tshape(BF16 if latent else F32)]
    if not latent:
        out_specs.append(tspec)
        out_shape.append(tshape(F32))
    return pl.pallas_call(
        _na_qkv_kernel,
        grid=(n // tm,),
        in_specs=[tok, _mod_spec(1, rf), _mod_spec(0, rf), _const_spec(w.shape)],
        out_specs=out_specs,
        out_shape=out_shape,
        compiler_params=_cparams(("arbitrary",)),
        name="na_qkv_lat" if latent else "na_qkv_ctx",
    )(x, mod, mod, w)


def _softmax_pv(scores, values):
    m = functools.reduce(jnp.maximum, [jnp.max(s, axis=-1, keepdims=True) for s in scores])
    ps = [jnp.exp(s - m) for s in scores]
    l = functools.reduce(jnp.add, [jnp.sum(p, axis=-1, keepdims=True) for p in ps])
    o = functools.reduce(jnp.add, [_dot(p.astype(BF16), v) for p, v in zip(ps, values)])
    return o / l


def _head_of_pair(x, half):
    lane = lax.broadcasted_iota(jnp.int32, x.shape, 1)
    keep = (lane < NA_HD) if half == 0 else (lane >= NA_HD)
    return jnp.where(keep, x, jnp.zeros_like(x))


def _merge_pair(o0, o1):
    lane = lax.broadcasted_iota(jnp.int32, o0.shape, 1)
    return jnp.where(lane < NA_HD, o0, o1)


def _na_attn_ctx_kernel(q_ref, kt_ref, v_ref, o_ref):
    for b in range(CTX_BATCH_PER_STEP):
        rows = slice(b * SEQ, (b + 1) * SEQ)
        for p in range(NA_HEADS // 2):
            cols = slice(p * NA_PAIR, (p + 1) * NA_PAIR)
            qp = q_ref[rows, cols]
            ktp = kt_ref[b, cols, :].astype(BF16)
            vp = v_ref[rows, cols]
            outs = [_softmax_pv([_dot(_head_of_pair(qp, half), ktp)], [vp]) for half in range(2)]
            o_ref[rows, cols] = _merge_pair(*outs).astype(BF16)


def _na_attn_ctx(q, kt, v):
    tok = pl.BlockSpec((CTX_BATCH_PER_STEP * SEQ, D_MODEL), lambda b: (b, 0))
    return pl.pallas_call(
        _na_attn_ctx_kernel,
        grid=(BATCH // CTX_BATCH_PER_STEP,),
        in_specs=[tok, pl.BlockSpec((CTX_BATCH_PER_STEP, D_MODEL, SEQ), lambda b: (b, 0, 0)), tok],
        out_specs=tok,
        out_shape=jax.ShapeDtypeStruct((N_CTX, D_MODEL), BF16),
        compiler_params=_cparams(("arbitrary",)),
        name="na_attn_ctx",
    )(q, kt, v)


def _na_win_start(m, clip=jnp.clip):
    return clip(m * NA_BLK_ROWS // 2 - NA_WIN_ROWS // 4, 0, GRID_ROWS // 2 - NA_WIN_PAIRS)


def _na_variant(m):
    return jnp.where(m == 0, 0, jnp.where(m == NA_N_BLOCKS - 1, 2, 1))


def _na_attn_lat_kernel(q_ref, kt_ref, v_ref, kct_ref, vc_ref, bias_ref, o_ref):
    rp0 = _na_win_start(pl.program_id(1))
    win = pl.ds(pl.multiple_of(rp0 * NA_KT_W, NA_KT_W), NA_WIN_PAIRS * NA_KT_W)
    for p in range(NA_HEADS // 2):
        cols = slice(p * NA_PAIR, (p + 1) * NA_PAIR)
        qp = q_ref[:, cols]
        kw = jnp.concatenate([kt_ref[rp0 + t, cols, :] for t in range(NA_WIN_PAIRS)], axis=1)
        kc = kct_ref[cols, :]
        vw = v_ref[win, cols]
        vc = vc_ref[:, cols]
        outs = []
        for half in range(2):
            qh = _head_of_pair(qp, half)
            s_nb = _dot(qh, kw) + bias_ref[2 * p + half]
            outs.append(_softmax_pv([s_nb, _dot(qh, kc)], [vw, vc]))
        o_ref[:, cols] = _merge_pair(*outs).astype(BF16)


def _na_bias_kernel(e_ref, e64_ref, o_ref):
    n_dr = 2 * NA_WIN_ROWS - 1
    shape = (GRID_W, LANES)
    lane = lax.broadcasted_iota(jnp.int32, shape, 1)
    c = lax.broadcasted_iota(jnp.int32, shape, 0)
    kc = lane & (GRID_W - 1)
    cs = jnp.clip(c - NA_WIN_COLS // 2, 0, GRID_W - NA_WIN_COLS)
    valid = (kc >= cs) & (kc < cs + NA_WIN_COLS)
    toeplitz = lambda ref, a: pltpu.roll(jnp.broadcast_to(ref[a:a + 1, :], shape), 0, 1, stride=1, stride_axis=0)
    rows = ([toeplitz(e_ref, a) for a in range(n_dr)], [toeplitz(e64_ref, a) for a in range(n_dr)])
    masked = jnp.full(shape, NEG_BIG, F32)
    for v, m in enumerate(NA_BLOCK_VARIANTS):
        rp0 = _na_win_start(m, clip=lambda x, lo, hi: min(max(x, lo), hi))
        for i in range(NA_BLK_ROWS):
            r = NA_BLK_ROWS * m + i
            rs = min(max(r - NA_WIN_ROWS // 2, 0), GRID_ROWS - NA_WIN_ROWS)
            for t in range(NA_WIN_PAIRS):
                halves = []
                for u in range(2):
                    kr = 2 * (rp0 + t) + u
                    halves.append(rows[u][kr - r + NA_WIN_ROWS - 1] if rs <= kr < rs + NA_WIN_ROWS else masked)
                o_ref[v, i * GRID_W:(i + 1) * GRID_W, t * LANES:(t + 1) * LANES] = jnp.where(
                    valid, jnp.where(lane < GRID_W, halves[0], halves[1]), NEG_BIG)


def _na_bias_table(rpb):
    n_dr = 2 * NA_WIN_ROWS - 1
    blk = NA_BLK_ROWS * GRID_W
    e = jnp.zeros((NA_HEADS, n_dr, LANES), F32)
    e = e.at[:, :, :NA_WIN_COLS].set(rpb[:, :, NA_WIN_COLS - 1:])
    e = e.at[:, :, LANES - (NA_WIN_COLS - 1):].set(rpb[:, :, :NA_WIN_COLS - 1])
    e64 = jnp.roll(e, GRID_W, axis=-1)
    spec = pl.BlockSpec((None, n_dr, LANES), lambda h: (h, 0, 0))
    return pl.pallas_call(
        _na_bias_kernel,
        grid=(NA_HEADS,),
        in_specs=[spec, spec],
        out_specs=pl.BlockSpec((None, len(NA_BLOCK_VARIANTS), blk, NA_WIN_PAIRS * NA_KT_W), lambda h: (h, 0, 0, 0)),
        out_shape=jax.ShapeDtypeStruct((NA_HEADS, len(NA_BLOCK_VARIANTS), blk, NA_WIN_PAIRS * NA_KT_W), F32),
        compiler_params=_cparams(("arbitrary",)),
        name="na_bias_table",
    )(e, e64)


def _na_attn_lat(q, kt, v, kct, vc, bias_tab):
    blk = NA_BLK_ROWS * GRID_W
    nblk = NA_N_BLOCKS
    npair = DEC_SEQ // NA_KT_W
    row = pl.BlockSpec((blk, D_MODEL), lambda b, m: (b * nblk + m, 0))
    per_batch = lambda *shape: pl.BlockSpec((None,) + shape, lambda b, m: (b,) + (0,) * len(shape),
                                            pipeline_mode=pl.Buffered(1))
    bias_spec = pl.BlockSpec((NA_HEADS, None, blk, NA_WIN_PAIRS * NA_KT_W), lambda b, m: (0, _na_variant(m), 0, 0))
    return pl.pallas_call(
        _na_attn_lat_kernel,
        grid=(DEC_BATCH, nblk),
        in_specs=[row, per_batch(npair, D_MODEL, NA_KT_W), per_batch(DEC_SEQ, D_MODEL),
                  per_batch(D_MODEL, PAST_LEN), per_batch(PAST_LEN, D_MODEL), bias_spec],
        out_specs=row,
        out_shape=jax.ShapeDtypeStruct((N_LAT, D_MODEL), BF16),
        compiler_params=_cparams(("arbitrary", "arbitrary")),
        name="na_attn_lat",
    )(q, kt.reshape(DEC_BATCH, npair, D_MODEL, NA_KT_W), v.reshape(DEC_BATCH, DEC_SEQ, D_MODEL), kct, vc, bias_tab)


def _pow2_runs(n, src, dst, largest, make_copy):
    out = []
    b = largest
    while b >= SUBLANES:
        out.append(((n & b) != 0, make_copy(src, dst, b)))
        src = src + (n & b)
        dst = dst + (n & b)
        b //= 2
    return out


def _rows8(start, size):
    return pl.ds(pl.multiple_of(start, SUBLANES), size)


def _moe_scatter_kernel(seg_ref, pad_ref, na_ref, x_ref, sc_ref, sh_ref, rt_ref, xs_ref, srt_scr, zero_scr, sems, zsem):
    i = pl.program_id(0)
    tm = x_ref.shape[0]
    slot = i % 2
    hm = (x_ref[...] * (1.0 + sc_ref[...]) + sh_ref[...]).astype(BF16)
    rtt = rt_ref[...]
    row = lax.broadcasted_iota(jnp.int32, (MOE_LOCAL_ROWS, tm), 0).astype(F32)
    chosen = (row == rtt[RT_LOCAL:RT_LOCAL + 1, :]) | (row == rtt[RT_LOCAL + 1:RT_LOCAL + 2, :])
    srt_scr[slot] = _dot(jnp.where(chosen, 1.0, 0.0).astype(BF16), hm)

    def seg_copies(tile, s):
        out = []
        for e in range(N_EXPERTS):
            base = (tile * N_EXPERTS + e) * 3
            out += _pow2_runs(
                seg_ref[base], seg_ref[base + 1], seg_ref[base + 2], tm,
                lambda src, dst, b: pltpu.make_async_copy(srt_scr.at[s, _rows8(src, b)], xs_ref.at[_rows8(dst, b)],
                                                          sems.at[s]))
        return out

    for cond, cp in seg_copies(i, slot):
        pl.when(cond)(cp.start)

    @pl.when(i > 0)
    def _():
        for cond, cp in seg_copies(i - 1, 1 - slot):
            pl.when(cond)(cp.wait)

    @pl.when(i == pl.num_programs(0) - 1)
    def _():
        zero_scr[...] = jnp.zeros_like(zero_scr)

        def pad_copies():
            out = []
            for e in range(N_EXPERTS):
                out += _pow2_runs(
                    pad_ref[N_EXPERTS + e], 0, pad_ref[e], TM_MOE // 2,
                    lambda src, dst, b: pltpu.make_async_copy(zero_scr.at[pl.ds(0, b)], xs_ref.at[_rows8(dst, b)], zsem))
            return out

        def tile_copy(j):
            rows = pl.ds(pl.multiple_of(j * TM_MOE, TM_MOE), TM_MOE)
            return pltpu.make_async_copy(zero_scr, xs_ref.at[rows], zsem)

        for cond, cp in pad_copies():
            pl.when(cond)(cp.start)
        lax.fori_loop(na_ref[0], NT_MOE, lambda j, c: (tile_copy(j).start(), c)[1], 0)
        for cond, cp in pad_copies():
            pl.when(cond)(cp.wait)
        lax.fori_loop(na_ref[0], NT_MOE, lambda j, c: (tile_copy(j).wait(), c)[1], 0)
        for cond, cp in seg_copies(i, slot):
            pl.when(cond)(cp.wait)


def _moe_scatter(seg, pad, na, x_all, rt_all, mod):
    tm = TM_PROJ
    rf = _all_row_fn(tm)
    return pl.pallas_call(
        _moe_scatter_kernel,
        grid_spec=pltpu.PrefetchScalarGridSpec(
            num_scalar_prefetch=3,
            grid=(N_TOK // tm,),
            in_specs=[pl.BlockSpec((tm, D_MODEL), lambda i, *_: (i, 0)), _mod_spec(4, rf), _mod_spec(3, rf),
                      pl.BlockSpec((RT_WIDTH, tm), lambda i, *_: (0, i))],
            out_specs=pl.BlockSpec(memory_space=pl.ANY),
            scratch_shapes=[pltpu.VMEM((2, MOE_LOCAL_ROWS, D_MODEL), F32), pltpu.VMEM((TM_MOE, D_MODEL), F32),
                            pltpu.SemaphoreType.DMA((2,)), pltpu.SemaphoreType.DMA(())],
        ),
        out_shape=jax.ShapeDtypeStruct((ROWS_SORTED, D_MODEL), F32),
        compiler_params=_cparams(("arbitrary",)),
        name="moe_scatter",
    )(seg, pad, na, x_all, mod, mod, rt_all)


def _tile_changed(te_ref, j):
    prev = te_ref[jnp.maximum(j - 1, 0)]
    return (j == 0) | (te_ref[j] != prev)


def _moe_up_kernel(te_ref, na_ref, x_ref, wg_ref, wu_ref, o_ref, w_scr):
    j = pl.program_id(1)
    tf = wg_ref.shape[1]

    @pl.when(j < na_ref[0])
    def _():
        @pl.when(_tile_changed(te_ref, j))
        def _():
            w_scr[:, :tf] = wg_ref[...].astype(BF16)
            w_scr[:, tf:] = wu_ref[...].astype(BF16)

        gu = _dot(x_ref[...].astype(BF16), w_scr[...])
        o_ref[...] = (_silu(gu[:, :tf]) * gu[:, tf:]).astype(BF16)

    @pl.when(j >= na_ref[0])
    def _():
        o_ref[...] = jnp.zeros_like(o_ref)


def _moe_up(te, na, xs, w_in):
    tm, tf = TM_MOE, TF_MOE
    nf = FFN_EXPERT // tf
    row = lambda j, na: jnp.minimum(j, na[0] - 1)
    return pl.pallas_call(
        _moe_up_kernel,
        grid_spec=pltpu.PrefetchScalarGridSpec(
            num_scalar_prefetch=2,
            grid=(nf, NT_MOE),
            in_specs=[pl.BlockSpec((tm, D_MODEL), lambda f, j, te, na: (row(j, na), 0)),
                      pl.BlockSpec((None, D_MODEL, tf), lambda f, j, te, na: (te[j], 0, f)),
                      pl.BlockSpec((None, D_MODEL, tf), lambda f, j, te, na: (te[j], 0, nf + f))],
            out_specs=pl.BlockSpec((tm, tf), lambda f, j, te, na: (j, f)),
            scratch_shapes=[pltpu.VMEM((D_MODEL, 2 * tf), BF16)],
        ),
        out_shape=jax.ShapeDtypeStruct((ROWS_SORTED, FFN_EXPERT), BF16),
        compiler_params=_cparams(("arbitrary", "arbitrary")),
        name="moe_up",
    )(te, na, xs, w_in, w_in)


def _moe_down_kernel(te_ref, na_ref, h_ref, w_ref, o_ref, w_scr):
    j = pl.program_id(0)

    @pl.when(j < na_ref[0])
    def _():
        @pl.when(_tile_changed(te_ref, j))
        def _():
            w_scr[...] = w_ref[...].astype(BF16)

        o_ref[...] = _dot(h_ref[...], w_scr[...])

    @pl.when(j >= na_ref[0])
    def _():
        o_ref[...] = jnp.zeros_like(o_ref)


def _moe_down(te, na, hmid, w_out):
    tm = TM_MOE
    row = lambda j, na: jnp.minimum(j, na[0] - 1)
    return pl.pallas_call(
        _moe_down_kernel,
        grid_spec=pltpu.PrefetchScalarGridSpec(
            num_scalar_prefetch=2,
            grid=(NT_MOE,),
            in_specs=[pl.BlockSpec((tm, FFN_EXPERT), lambda j, te, na: (row(j, na), 0)),
                      pl.BlockSpec((None, FFN_EXPERT, D_MODEL), lambda j, te, na: (te[j], 0, 0))],
            out_specs=pl.BlockSpec((tm, D_MODEL), lambda j, te, na: (j, 0)),
            scratch_shapes=[pltpu.VMEM((FFN_EXPERT, D_MODEL), BF16)],
        ),
        out_shape=jax.ShapeDtypeStruct((ROWS_SORTED, D_MODEL), F32),
        compiler_params=_cparams(("arbitrary",)),
        name="moe_down",
    )(te, na, hmid, w_out)


def _moe_combine_kernel(pos_ref, y_ref, rt_ref, x_ref, g_ref, lg_ref, lb_ref, o_ref, ybuf, sems, *, tok_off):
    i = pl.program_id(0)
    tm = x_ref.shape[0]

    def fetch(tile, slot):
        def start(t, c):
            for k in range(TOP_K):
                p = pos_ref[k * N_TOK + tok_off + tile * tm + t]
                pltpu.make_async_copy(y_ref.at[pl.ds(p, 1)], ybuf.at[slot, k, pl.ds(t, 1)], sems.at[slot]).start()
            return c

        lax.fori_loop(0, tm, start, 0, unroll=8)

    @pl.when(i == 0)
    def _():
        fetch(0, 0)

    @pl.when(i + 1 < pl.num_programs(0))
    def _():
        fetch(i + 1, (i + 1) % 2)

    slot = i % 2
    pltpu.make_async_copy(ybuf.at[slot], ybuf.at[slot], sems.at[slot]).wait()
    rt = rt_ref[...]
    moe = rt[:, RT_GATE:RT_GATE + 1] * ybuf[slot, 0] + rt[:, RT_GATE + 1:RT_GATE + 2] * ybuf[slot, 1]
    o_ref[...] = _layer_norm(ALPHA * x_ref[...] + g_ref[...] * moe, lg_ref[...], lb_ref[...])


def _moe_combine(pos, y, rt_all, x_all, mod, lg, lb, latent):
    n = N_LAT if latent else N_CTX
    tm = TM_ROUTE
    rf = _row_fn(latent, tm)
    tok_off = N_CTX if latent else 0
    off = tok_off // tm
    return pl.pallas_call(
        functools.partial(_moe_combine_kernel, tok_off=tok_off),
        grid_spec=pltpu.PrefetchScalarGridSpec(
            num_scalar_prefetch=1,
            grid=(n // tm,),
            in_specs=[pl.BlockSpec(memory_space=pl.ANY),
                      pl.BlockSpec((tm, LANES), lambda i, pos: (i + off, 0)),
                      pl.BlockSpec((tm, D_MODEL), lambda i, pos: (i + off, 0)),
                      _mod_spec(5, rf),
                      pl.BlockSpec((1, D_MODEL), lambda i, pos: (0, 0)),
                      pl.BlockSpec((1, D_MODEL), lambda i, pos: (0, 0))],
            out_specs=pl.BlockSpec((tm, D_MODEL), lambda i, pos: (i, 0)),
            scratch_shapes=[pltpu.VMEM((2, TOP_K, tm, D_MODEL), F32), pltpu.SemaphoreType.DMA((2,))],
        ),
        out_shape=jax.ShapeDtypeStruct((n, D_MODEL), F32),
        compiler_params=_cparams(("arbitrary",)),
        name="moe_combine_lat" if latent else "moe_combine_ctx",
    )(pos, y, rt_all, x_all, mod, lg, lb)


def _routing_positions(rtt_all, counts, meta):
    tm = TM_MOE
    cnt = counts[0, :N_EXPERTS].astype(jnp.int32)
    padded = ((cnt + tm - 1) // tm) * tm
    gend = jnp.cumsum(padded)
    gstart = gend - padded
    route = rtt_all.astype(jnp.int32)
    onehot = lambda idx: idx[None, :] == jnp.arange(N_EXPERTS, dtype=jnp.int32)[:, None]
    start_of = lambda idx: jnp.sum(jnp.where(onehot(idx), gstart[:, None], 0), axis=0)
    pos = jnp.concatenate([start_of(route[RT_EXPERT + k]) + route[RT_ROW + k] for k in range(TOP_K)])
    pad = jnp.concatenate([gstart + cnt, padded - cnt]).astype(jnp.int32)
    tile_start = jnp.arange(NT_MOE, dtype=jnp.int32) * tm
    te = jnp.sum((tile_start[:, None] >= gend[None, :]).astype(jnp.int32), axis=1)
    n_active = (gend[-1] // tm).astype(jnp.int32)
    last_e = jnp.take(te, jnp.maximum(n_active - 1, 0))
    te = jnp.where(tile_start < gend[-1], te, last_e).astype(jnp.int32)
    m = meta[:, :3, :N_EXPERTS].astype(jnp.int32)
    seg = jnp.stack([m[:, 0], m[:, 1], m[:, 2] + gstart[None, :]], axis=-1).reshape(-1)
    return pos, seg, pad, te, n_active.reshape(1)


def kernel(x_prompt, x_sample, cache_ckv_l0, cache_krope_l0, cache_k_l1, cache_v_l1, c, c_ctx, w_ada_l0, b_ada_l0, mla_w_dq, mla_g_q, mla_w_uq, mla_w_dkv, mla_g_kv, mla_w_ukv, mla_w_o, ln1_g_l0, ln1_b_l0, ffn_w_in, ffn_w_out, ln2_g_l0, ln2_b_l0, w_ada_l1, b_ada_l1, na_w_qkv, na_rpb, na_w_o, ln1_g_l1, ln1_b_l1, moe_w_router, moe_w_in, moe_w_out, ln2_g_l1, ln2_b_l1):
    row = lambda v: v.reshape(1, -1)
    xp = x_prompt.reshape(N_CTX, D_MODEL)
    xs = x_sample.reshape(N_LAT, D_MODEL)
    groups = ((xp, False), (xs, True))

    cvecs = jnp.concatenate([c_ctx[None], c, jnp.zeros((SUBLANES - 1 - DEC_BATCH, D_MODEL), F32)], axis=0)
    mod0 = _ada_mod(cvecs, w_ada_l0, b_ada_l0)
    mod1 = _ada_mod(cvecs, w_ada_l1, b_ada_l1)

    wa = jnp.concatenate([mla_w_dq, mla_w_dkv], axis=1).astype(BF16)
    wuq = mla_w_uq.reshape(MLA_Q_LORA, MLA_HEADS, MLA_NOPE + MLA_ROPE)
    wuq = jnp.pad(wuq, ((0, 0), (0, 0), (0, Q_HEAD_PAD - MLA_NOPE - MLA_ROPE)))
    wuq = wuq.reshape(MLA_Q_LORA, MLA_HEADS * Q_HEAD_PAD).astype(BF16)
    wukv = mla_w_ukv.astype(BF16)
    wo0 = mla_w_o.astype(BF16)
    w_in0 = ffn_w_in.astype(BF16)
    w_out0 = ffn_w_out.astype(BF16)
    wqkv = na_w_qkv.astype(BF16)
    wo1 = na_w_o.astype(BF16)
    wr = jnp.pad(moe_w_router, ((0, 0), (0, LANES - N_EXPERTS))).astype(BF16)
    bias_tab = _na_bias_table(na_rpb)
    tab = _rope_tables()

    kvx_cache = _matmul(cache_ckv_l0.reshape(DEC_BATCH * PAST_LEN, MLA_KV_LORA), wukv, BF16, 512, "mla_expand_cache")
    kvx_cache = kvx_cache.reshape(DEC_BATCH, PAST_LEN, -1)
    krp_cache = jnp.pad(cache_krope_l0, ((0, 0), (0, 0), (0, LANES - MLA_ROPE))).astype(BF16)
    x1 = []
    new_ckv = new_kr = None
    vt_cache = kvx_cache.reshape(DEC_BATCH, PAST_LEN, MLA_HEADS, MLA_NOPE + MLA_V)[..., MLA_NOPE:]
    vt_cache = vt_cache.reshape(DEC_BATCH, PAST_LEN, MLA_HEADS * MLA_V).transpose(0, 2, 1)
    for x, latent in groups:
        proj = _mla_proj(x, mod0, wa, row(mla_g_q), row(mla_g_kv), wuq, wukv, tab, latent)
        if latent:
            qt, kvx, krp, vt = proj
            o = _mla_attn_lat(qt, kvx, krp, kvx_cache, krp_cache, vt, vt_cache)
        else:
            q, kvx, new_ckv, new_kr, krp = proj
            o = _mla_attn_ctx(q, kvx, krp)
        x1.append(_attn_out_ffn(o, x, wo0, mod0, row(ln1_g_l0), row(ln1_b_l0), w_in0, w_out0,
                                row(ln2_g_l0), row(ln2_b_l0), latent))

    kct = cache_k_l1.reshape(DEC_BATCH, PAST_LEN, D_MODEL).transpose(0, 2, 1).astype(BF16)
    vc = cache_v_l1.reshape(DEC_BATCH, PAST_LEN, D_MODEL).astype(BF16)
    q, v, new_kt, new_vt = _na_qkv(x1[0], mod1, wqkv, False)
    o_ctx = _na_attn_ctx(q, new_kt, v)
    q, v, kt = _na_qkv(x1[1], mod1, wqkv, True)
    o_lat = _na_attn_lat(q, kt, v, kct, vc, bias_tab)
    x2_all, rt_all, rtt_all, counts, meta = _proj_ln_route(o_ctx, x1[0], o_lat, x1[1], wo1, mod1, row(ln1_g_l1),
                                                           row(ln1_b_l1), wr)

    pos, seg, pad, te, n_active = _routing_positions(rtt_all, counts, meta)
    x_sorted = _moe_scatter(seg, pad, n_active, x2_all, rtt_all, mod1)
    hmid = _moe_up(te, n_active, x_sorted, moe_w_in)
    y = _moe_down(te, n_active, hmid, moe_w_out)
    outs = [_moe_combine(pos, y, rt_all, x2_all, mod1, row(ln2_g_l1), row(ln2_b_l1), latent)
            for latent in (False, True)]

    return (outs[0].reshape(BATCH, SEQ, D_MODEL),
            outs[1].reshape(DEC_BATCH, DEC_SEQ, D_MODEL),
            new_ckv.reshape(BATCH, SEQ, MLA_KV_LORA),
            new_kr.transpose(0, 2, 1),
            new_kt.reshape(BATCH, NA_HEADS, NA_HD, SEQ).transpose(0, 3, 1, 2),
            new_vt.reshape(BATCH, NA_HEADS, NA_HD, SEQ).transpose(0, 3, 1, 2))
```

```python
import functools
import math

import numpy as np
import jax
import jax.numpy as jnp
from jax import lax
from jax.experimental import pallas as pl
from jax.experimental.pallas import tpu as pltpu

F32 = jnp.float32
BF16 = jnp.bfloat16

D_MODEL = 1024
BATCH = 32
SEQ = 256
DEPTH = 2
DEC_BATCH = 2
DEC_SEQ = 2048
PAST_LEN = 512
GRID_W = 64
MLA_HEADS = 8
MLA_NOPE = 128
MLA_ROPE = 64
MLA_V = 128
MLA_Q_LORA = 512
MLA_KV_LORA = 256
MLA_SCALE = 1.0 / math.sqrt(MLA_NOPE + MLA_ROPE)
ROPE_THETA = 10000.0
NA_HEADS = 16
NA_HD = D_MODEL // NA_HEADS
NA_WIN_ROWS = 8
NA_WIN_COLS = 16
NA_SCALE = 1.0 / math.sqrt(NA_HD)
FFN_DENSE = 2816
N_EXPERTS = 8
TOP_K = 2
FFN_EXPERT = 3584
ALPHA = (2 * DEPTH) ** 0.25
LN_EPS = 1e-5
RMS_EPS = 1e-6

N_CTX = BATCH * SEQ
N_LAT = DEC_BATCH * DEC_SEQ
N_TOK = N_CTX + N_LAT
GRID_ROWS = DEC_SEQ // GRID_W
Q_HEAD_PAD = 256
LANES = 128
SUBLANES = 8
NEG_BIG = -1e30
MLA_KEY_CHUNK = 256
NA_PAIR = 2 * NA_HD
NA_BLK_ROWS = 4
NA_KT_W = 2 * GRID_W
NA_WIN_PAIRS = (NA_BLK_ROWS + NA_WIN_ROWS) // 2
NA_N_BLOCKS = DEC_SEQ // (NA_BLK_ROWS * GRID_W)
NA_BLOCK_VARIANTS = (0, 1, NA_N_BLOCKS - 1)

VMEM_LIMIT = 56 * 1024 * 1024

TM_PROJ = 512
TM_FFN = 512
CTX_BATCH_PER_STEP = 4
MXU_TILE = 256
FFN_CHUNKS = ((0, 6 * MXU_TILE), (6 * MXU_TILE, FFN_DENSE))
TM_MOE = 512
TF_MOE = 1792
N_PAIRS = N_TOK * TOP_K
MOE_LOCAL_ROWS = TOP_K * TM_PROJ + N_EXPERTS * SUBLANES
N_SEG_PAD = N_EXPERTS * (N_TOK // TM_PROJ) * (SUBLANES - 1)
NT_MOE = -(-(N_PAIRS + N_SEG_PAD + N_EXPERTS * (TM_MOE - 1)) // TM_MOE)
ROWS_SORTED = NT_MOE * TM_MOE
TM_ROUTE = 512
RT_EXPERT, RT_GATE, RT_ROW, RT_LOCAL = 0, 2, 4, 6
RT_WIDTH = 8


def _cparams(sem, vmem=VMEM_LIMIT):
    return pltpu.CompilerParams(dimension_semantics=sem, vmem_limit_bytes=vmem)


def _silu(x):
    return x * jax.nn.sigmoid(x)


def _layer_norm(y, g, b):
    mu = jnp.mean(y, axis=-1, keepdims=True)
    d = y - mu
    var = jnp.mean(d * d, axis=-1, keepdims=True)
    return d * lax.rsqrt(var + LN_EPS) * g + b


def _rms_norm(y, g):
    return y * lax.rsqrt(jnp.mean(y * y, axis=-1, keepdims=True) + RMS_EPS) * g


def _dot(a, b):
    return jnp.dot(a, b, preferred_element_type=F32)


def _dot_nt(a, b):
    return lax.dot_general(a, b, (((1,), (1,)), ((), ())), preferred_element_type=F32)


def _ada_kernel(c_ref, w_ref, b_ref, o_ref):
    s = _silu(c_ref[...]).astype(BF16)
    o_ref[...] = _dot(s, w_ref[...].astype(BF16)) + b_ref[...]


def _ada_mod(cvecs, w, b):
    tn = 1536
    n_vec = 1 + DEC_BATCH
    m = pl.pallas_call(
        _ada_kernel,
        grid=(6 * D_MODEL // tn,),
        in_specs=[pl.BlockSpec((SUBLANES, D_MODEL), lambda j: (0, 0)),
                  pl.BlockSpec((D_MODEL, tn), lambda j: (0, j)),
                  pl.BlockSpec((1, tn), lambda j: (0, j))],
        out_specs=pl.BlockSpec((SUBLANES, tn), lambda j: (0, j)),
        out_shape=jax.ShapeDtypeStruct((SUBLANES, 6 * D_MODEL), F32),
        compiler_params=_cparams(("arbitrary",)),
        name="ada_mod",
    )(cvecs, w, b.reshape(1, -1))
    return m[:n_vec].reshape(n_vec * 6, 1, D_MODEL)


def _mod_spec(j, row_fn):
    return pl.BlockSpec((None, 1, D_MODEL), lambda i, *_: (row_fn(i) * 6 + j, 0, 0))


def _row_fn(latent, tm):
    if not latent:
        return lambda i: 0
    per = DEC_SEQ // tm
    return lambda i: 1 + i // per


def _const_spec(shape):
    nd = len(shape)
    return pl.BlockSpec(shape, lambda *_: (0,) * nd, pipeline_mode=pl.Buffered(1))


def _rope_kernel(invf_ref, o_ref):
    i = pl.program_id(0)
    tm = o_ref.shape[1]
    t = i * tm + lax.broadcasted_iota(jnp.int32, (tm, LANES), 0)
    lane = lax.broadcasted_iota(jnp.int32, (tm, LANES), 1)
    row = t >> int(math.log2(GRID_W))
    col = t & (GRID_W - 1)
    pos = jnp.where(lane < MLA_ROPE // 2, row, col).astype(F32)
    ang = pos * invf_ref[...]
    cos = jnp.cos(ang)
    sin = jnp.sin(ang)
    unit = lane >> int(math.log2(MLA_ROPE // 4))
    first = (unit == 0) | (unit == 2)
    second = (unit == 1) | (unit == 3)
    o_ref[0] = jnp.where(lane < MLA_ROPE, cos, 0.0)
    o_ref[1] = jnp.where(first, -sin, 0.0)
    o_ref[2] = jnp.where(second, sin, 0.0)


def _rope_tables():
    half = MLA_ROPE // 2
    inv_freq = (1.0 / (ROPE_THETA ** (np.arange(0, half, 2, dtype=np.float32) / half))).astype(np.float32)
    lane_f = np.zeros((1, LANES), np.float32)
    lane_f[0, :MLA_ROPE] = np.tile(inv_freq, 4)
    tm = 256
    return pl.pallas_call(
        _rope_kernel,
        grid=(DEC_SEQ // tm,),
        in_specs=[_const_spec((1, LANES))],
        out_specs=pl.BlockSpec((3, tm, LANES), lambda i: (0, i, 0)),
        out_shape=jax.ShapeDtypeStruct((3, DEC_SEQ, LANES), F32),
        compiler_params=_cparams(("arbitrary",)),
        name="rope_tables",
    )(jnp.asarray(lane_f))


def _rotate(v, tab_ref):
    return (v * tab_ref[0] + pltpu.roll(v, LANES - MLA_ROPE // 4, 1) * tab_ref[1]
            + pltpu.roll(v, MLA_ROPE // 4, 1) * tab_ref[2])


def _mla_proj_kernel(*refs, rope):
    if rope:
        (x_ref, sc_ref, sh_ref, wa_ref, gq_ref, gkv_ref, wuq_ref, wukv_ref, tab_ref,
         qt_ref, kvx_ref, krp_ref, vt_ref) = refs
    else:
        (x_ref, sc_ref, sh_ref, wa_ref, gq_ref, gkv_ref, wuq_ref, wukv_ref,
         q_ref, kvx_ref, ckv_ref, kr_ref, krp_ref) = refs
    h = (x_ref[...] * (1.0 + sc_ref[...]) + sh_ref[...]).astype(BF16)
    t = _dot(h, wa_ref[...])
    cq = _rms_norm(t[:, :MLA_Q_LORA], gq_ref[...])
    ckv = _rms_norm(t[:, MLA_Q_LORA:MLA_Q_LORA + MLA_KV_LORA], gkv_ref[...])
    kr = t[:, MLA_Q_LORA + MLA_KV_LORA:]
    kvx = _dot(ckv.astype(BF16), wukv_ref[...])
    kvx_ref[...] = kvx.astype(BF16)
    q = _dot(cq.astype(BF16), wuq_ref[...])
    krp = jnp.concatenate([kr, jnp.zeros_like(kr)], axis=-1)
    if rope:
        krp_ref[...] = _rotate(krp, tab_ref).astype(BF16)
        parts = []
        for hd in range(MLA_HEADS):
            lo = hd * Q_HEAD_PAD
            parts += [q[:, lo:lo + MLA_NOPE], _rotate(q[:, lo + MLA_NOPE:lo + Q_HEAD_PAD], tab_ref)]
        qt_ref[...] = jnp.concatenate(parts, axis=1).T.astype(BF16)
        vcols = [kvx[:, hd * Q_HEAD_PAD + MLA_NOPE:(hd + 1) * Q_HEAD_PAD] for hd in range(MLA_HEADS)]
        vt_ref[...] = jnp.concatenate(vcols, axis=1).T.astype(BF16)
    else:
        ckv_ref[...] = ckv
        krt = kr.T
        for t in range(kr_ref.shape[0]):
            kr_ref[t] = krt[:, t * SEQ:(t + 1) * SEQ]
        krp_ref[...] = krp.astype(BF16)
        q_ref[...] = q.astype(BF16)


def _mla_proj(x, mod, wa, gq, gkv, wuq, wukv, tab, latent):
    n = x.shape[0]
    tm = TM_PROJ
    rf = _row_fn(latent, tm)
    tok = lambda w: pl.BlockSpec((tm, w), lambda i: (i, 0))
    in_specs = [tok(D_MODEL), _mod_spec(1, rf), _mod_spec(0, rf),
                _const_spec(wa.shape), _const_spec(gq.shape), _const_spec(gkv.shape),
                _const_spec(wuq.shape), _const_spec(wukv.shape)]
    args = [x, mod, mod, wa, gq, gkv, wuq, wukv]
    wq, wkv = MLA_HEADS * Q_HEAD_PAD, MLA_HEADS * (MLA_NOPE + MLA_V)
    if latent:
        per = DEC_SEQ // tm
        in_specs.append(pl.BlockSpec((3, tm, LANES), lambda i: (0, i % per, 0)))
        args.append(tab)
        out_specs = [pl.BlockSpec((wq, tm), lambda i: (0, i)), tok(wkv), tok(LANES),
                     pl.BlockSpec((None, MLA_HEADS * MLA_V, tm), lambda i: (i // per, 0, i % per))]
        out_shape = [jax.ShapeDtypeStruct((wq, n), BF16), jax.ShapeDtypeStruct((n, wkv), BF16),
                     jax.ShapeDtypeStruct((n, LANES), BF16),
                     jax.ShapeDtypeStruct((DEC_BATCH, MLA_HEADS * MLA_V, DEC_SEQ), BF16)]
    else:
        out_specs = [tok(wq), tok(wkv), tok(MLA_KV_LORA),
                     pl.BlockSpec((tm // SEQ, MLA_ROPE, SEQ), lambda i: (i, 0, 0)), tok(LANES)]
        out_shape = [jax.ShapeDtypeStruct((n, wq), BF16), jax.ShapeDtypeStruct((n, wkv), BF16),
                     jax.ShapeDtypeStruct((n, MLA_KV_LORA), F32), jax.ShapeDtypeStruct((n // SEQ, MLA_ROPE, SEQ), F32),
                     jax.ShapeDtypeStruct((n, LANES), BF16)]
    return pl.pallas_call(
        functools.partial(_mla_proj_kernel, rope=latent),
        grid=(n // tm,),
        in_specs=in_specs,
        out_specs=out_specs,
        out_shape=out_shape,
        compiler_params=_cparams(("arbitrary",)),
        name="mla_proj_lat" if latent else "mla_proj_ctx",
    )(*args)


def _matmul_kernel(a_ref, b_ref, o_ref):
    o_ref[...] = _dot(a_ref[...].astype(BF16), b_ref[...]).astype(o_ref.dtype)


def _matmul(a, b, out_dtype, tm, name):
    m, k = a.shape
    n = b.shape[1]
    return pl.pallas_call(
        _matmul_kernel,
        grid=(m // tm,),
        in_specs=[pl.BlockSpec((tm, k), lambda i: (i, 0)), _const_spec(b.shape)],
        out_specs=pl.BlockSpec((tm, n), lambda i: (i, 0)),
        out_shape=jax.ShapeDtypeStruct((m, n), out_dtype),
        compiler_params=_cparams(("arbitrary",)),
        name=name,
    )(a, b)


def _mla_head(qh, kparts, vparts):
    ss = [_dot_nt(qh, k) for k in kparts]
    m = functools.reduce(jnp.maximum, [jnp.max(s, axis=-1, keepdims=True) for s in ss])
    ps = [jnp.exp2((s - m) * (MLA_SCALE * math.log2(math.e))) for s in ss]
    l = functools.reduce(jnp.add, [jnp.sum(p, axis=-1, keepdims=True) for p in ps])
    o = functools.reduce(jnp.add, [_dot(p.astype(BF16), v) for p, v in zip(ps, vparts)])
    return o / l


def _mla_attn_ctx_kernel(q_ref, kvx_ref, krp_ref, o_ref):
    for b in range(CTX_BATCH_PER_STEP):
        rows = slice(b * SEQ, (b + 1) * SEQ)
        krp = krp_ref[rows, :]
        for hd in range(MLA_HEADS):
            lo = hd * (MLA_NOPE + MLA_V)
            qh = q_ref[rows, hd * Q_HEAD_PAD:(hd + 1) * Q_HEAD_PAD]
            kh = jnp.concatenate([kvx_ref[rows, lo:lo + MLA_NOPE], krp], axis=-1)
            vh = kvx_ref[rows, lo + MLA_NOPE:lo + MLA_NOPE + MLA_V]
            o_ref[rows, hd * MLA_V:(hd + 1) * MLA_V] = _mla_head(qh, [kh], [vh]).astype(BF16)


def _mla_attn_ctx(q, kvx, krp):
    tok = lambda w: pl.BlockSpec((CTX_BATCH_PER_STEP * SEQ, w), lambda b: (b, 0))
    return pl.pallas_call(
        _mla_attn_ctx_kernel,
        grid=(BATCH // CTX_BATCH_PER_STEP,),
        in_specs=[tok(q.shape[1]), tok(kvx.shape[1]), tok(LANES)],
        out_specs=tok(MLA_HEADS * MLA_V),
        out_shape=jax.ShapeDtypeStruct((N_CTX, MLA_HEADS * MLA_V), BF16),
        compiler_params=_cparams(("arbitrary",)),
        name="mla_attn_ctx",
    )(q, kvx, krp)


def _mla_attn_lat_kernel(qt_ref, kvl_ref, krl_ref, kvc_ref, krc_ref, vtl_ref, vtc_ref, o_ref):
    chunks = [(kvl_ref, krl_ref, vtl_ref, c) for c in range(DEC_SEQ // MLA_KEY_CHUNK)]
    chunks += [(kvc_ref, krc_ref, vtc_ref, c) for c in range(PAST_LEN // MLA_KEY_CHUNK)]
    for hd in range(MLA_HEADS):
        lo = hd * (MLA_NOPE + MLA_V)
        qt = qt_ref[hd * Q_HEAD_PAD:(hd + 1) * Q_HEAD_PAD, :]
        ss = []
        for kv_ref, kr_ref, _, c in chunks:
            rows = slice(c * MLA_KEY_CHUNK, (c + 1) * MLA_KEY_CHUNK)
            ss.append(_dot(jnp.concatenate([kv_ref[rows, lo:lo + MLA_NOPE], kr_ref[rows, :]], axis=-1), qt))
        m = functools.reduce(jnp.maximum, [jnp.max(s, axis=0, keepdims=True) for s in ss])
        ps = [jnp.exp2((s - m) * (MLA_SCALE * math.log2(math.e))) for s in ss]
        l = functools.reduce(jnp.add, [jnp.sum(p, axis=0, keepdims=True) for p in ps])
        ots = [_dot(vt_ref[hd * MLA_V:(hd + 1) * MLA_V, c * MLA_KEY_CHUNK:(c + 1) * MLA_KEY_CHUNK], p.astype(BF16))
               for (_, _, vt_ref, c), p in zip(chunks, ps)]
        ot = functools.reduce(jnp.add, ots) / l
        o_ref[:, hd * MLA_V:(hd + 1) * MLA_V] = ot.T.astype(BF16)


def _mla_attn_lat(qt, kvx_lat, krp_lat, kvx_ctx, krp_ctx, vt_lat, vt_ctx):
    tq = 256
    per = DEC_SEQ // tq
    wkv = kvx_lat.shape[-1]
    wv = MLA_HEADS * MLA_V
    batch = lambda rows, cols: pl.BlockSpec((None, rows, cols), lambda b, i: (b, 0, 0))
    return pl.pallas_call(
        _mla_attn_lat_kernel,
        grid=(DEC_BATCH, per),
        in_specs=[pl.BlockSpec((qt.shape[0], tq), lambda b, i: (0, b * per + i)),
                  batch(DEC_SEQ, wkv), batch(DEC_SEQ, LANES), batch(PAST_LEN, wkv), batch(PAST_LEN, LANES),
                  batch(wv, DEC_SEQ), batch(wv, PAST_LEN)],
        out_specs=pl.BlockSpec((tq, wv), lambda b, i: (b * per + i, 0)),
        out_shape=jax.ShapeDtypeStruct((N_LAT, wv), BF16),
        compiler_params=_cparams(("arbitrary", "arbitrary")),
        name="mla_attn_lat",
    )(qt, kvx_lat.reshape(DEC_BATCH, DEC_SEQ, wkv), krp_lat.reshape(DEC_BATCH, DEC_SEQ, LANES), kvx_ctx, krp_ctx,
      vt_lat, vt_ctx)


def _top2(logits):
    lane = lax.broadcasted_iota(jnp.int32, logits.shape, 1).astype(F32)
    m1 = jnp.max(logits, axis=-1, keepdims=True)
    i1 = jnp.min(jnp.where(logits == m1, lane, float(LANES)), axis=-1, keepdims=True)
    rest = jnp.where(lane == i1, -jnp.inf, logits)
    m2 = jnp.max(rest, axis=-1, keepdims=True)
    i2 = jnp.min(jnp.where(rest == m2, lane, float(LANES)), axis=-1, keepdims=True)
    e = jnp.exp(m2 - m1)
    w1 = 1.0 / (1.0 + e)
    w2 = e / (1.0 + e)
    return lane, i1, i2, w1, w2


def _proj_ln_route_kernel(oc_ref, xc_ref, ol_ref, xl_ref, wo_ref, g_ref, lg_ref, lb_ref, sc_ref, sh_ref, wr_ref,
                          x1_ref, rt_ref, rtt_ref, cnt_ref, meta_ref, carry, *, n_ctx_tiles):
    i = pl.program_id(0)

    @pl.when(i == 0)
    def _():
        carry[...] = jnp.zeros_like(carry)

    def body(o_ref, x_ref):
        out = _dot(o_ref[...], wo_ref[...])
        x1 = _layer_norm(ALPHA * x_ref[...] + g_ref[...] * out, lg_ref[...], lb_ref[...])
        x1_ref[...] = x1
        hm = x1 * (1.0 + sc_ref[...]) + sh_ref[...]
        logits = _dot(hm.astype(BF16), wr_ref[...])
        lane = lax.broadcasted_iota(jnp.int32, logits.shape, 1)
        logits = jnp.where(lane < N_EXPERTS, logits, -jnp.inf)
        lane_f, i1, i2, w1, w2 = _top2(logits)
        tm = logits.shape[0]
        oh1 = (lane_f == i1).astype(F32)
        oh2 = (lane_f == i2).astype(F32)
        rr = lax.broadcasted_iota(jnp.int32, (tm, tm), 0)
        cc = lax.broadcasted_iota(jnp.int32, (tm, tm), 1)
        below = (cc < rr).astype(BF16)
        tot1 = jnp.sum(oh1, axis=0, keepdims=True)
        tot2 = jnp.sum(oh2, axis=0, keepdims=True)
        seg_len = jnp.floor((tot1 + tot2 + (SUBLANES - 1)) * (1.0 / SUBLANES)) * SUBLANES
        e_row = lax.broadcasted_iota(jnp.int32, (LANES, LANES), 0)
        e_col = lax.broadcasted_iota(jnp.int32, (LANES, LANES), 1)
        before = (e_row < e_col).astype(BF16)
        seg_loc = _dot(jnp.broadcast_to(seg_len, (SUBLANES, LANES)).astype(BF16), before)[0:1]
        seg_glob = carry[...]
        cum1 = _dot(below, oh1.astype(BF16))
        cum2 = _dot(below, oh2.astype(BF16)) + tot1
        pick = lambda oh, v: jnp.sum(oh * v, axis=-1, keepdims=True)
        vals = {RT_EXPERT: (i1, i2), RT_GATE: (w1, w2),
                RT_ROW: (pick(oh1, cum1 + seg_glob), pick(oh2, cum2 + seg_glob)),
                RT_LOCAL: (pick(oh1, cum1 + seg_loc), pick(oh2, cum2 + seg_loc))}
        rt = jnp.zeros_like(logits)
        for first_lane, pair in vals.items():
            for k, val in enumerate(pair):
                rt = jnp.where(lane == first_lane + k, val, rt)
        rt_ref[...] = rt
        rtt_ref[...] = rt.T[:RT_WIDTH]
        sub = lax.broadcasted_iota(jnp.int32, (SUBLANES, LANES), 0)
        meta_ref[...] = jnp.where(sub == 0, seg_len, jnp.where(sub == 1, seg_loc, jnp.where(sub == 2, seg_glob, 0.0)))
        carry[...] = seg_glob + seg_len
        cnt_ref[...] = jnp.broadcast_to(carry[...], cnt_ref.shape)

    is_ctx = i < n_ctx_tiles
    pl.when(is_ctx)(lambda: body(oc_ref, xc_ref))
    pl.when(jnp.logical_not(is_ctx))(lambda: body(ol_ref, xl_ref))


def _all_row_fn(tm):
    nc = N_CTX // tm
    per = DEC_SEQ // tm
    return lambda i: jnp.where(i < nc, 0, 1 + (i - nc) // per)


def _proj_ln_route(o_ctx, x_ctx, o_lat, x_lat, wo, mod, lg, lb, wr):
    tm = TM_PROJ
    nc = N_CTX // tm
    rf = _all_row_fn(tm)
    ctx = lambda w: pl.BlockSpec((tm, w), lambda i: (jnp.minimum(i, nc - 1), 0))
    lat = lambda w: pl.BlockSpec((tm, w), lambda i: (jnp.maximum(i - nc, 0), 0))
    tok = lambda w: pl.BlockSpec((tm, w), lambda i: (i, 0))
    return pl.pallas_call(
        functools.partial(_proj_ln_route_kernel, n_ctx_tiles=nc),
        grid=(N_TOK // tm,),
        in_specs=[ctx(D_MODEL), ctx(D_MODEL), lat(D_MODEL), lat(D_MODEL), _const_spec(wo.shape), _mod_spec(2, rf),
                  _const_spec(lg.shape), _const_spec(lb.shape), _mod_spec(4, rf), _mod_spec(3, rf),
                  _const_spec(wr.shape)],
        out_specs=[tok(D_MODEL), tok(LANES), pl.BlockSpec((RT_WIDTH, tm), lambda i: (0, i)),
                   pl.BlockSpec((SUBLANES, LANES), lambda i: (0, 0)),
                   pl.BlockSpec((None, SUBLANES, LANES), lambda i: (i, 0, 0))],
        out_shape=[jax.ShapeDtypeStruct((N_TOK, D_MODEL), F32),
                   jax.ShapeDtypeStruct((N_TOK, LANES), F32),
                   jax.ShapeDtypeStruct((RT_WIDTH, N_TOK), F32),
                   jax.ShapeDtypeStruct((SUBLANES, LANES), F32),
                   jax.ShapeDtypeStruct((N_TOK // tm, SUBLANES, LANES), F32)],
        scratch_shapes=[pltpu.VMEM((1, LANES), F32)],
        compiler_params=_cparams(("arbitrary",)),
        name="proj_ln_route",
    )(o_ctx, x_ctx, o_lat, x_lat, wo, mod, lg, lb, mod, mod, wr)


def _attn_out_ffn_kernel(o_ref, x_ref, wo_ref, g1_ref, lg1_ref, lb1_ref, sc_ref, sh_ref, g2_ref,
                         wi_ref, wd_ref, lg2_ref, lb2_ref, y_ref):
    x1 = _layer_norm(ALPHA * x_ref[...] + g1_ref[...] * _dot(o_ref[...], wo_ref[...]), lg1_ref[...], lb1_ref[...])
    h = (x1 * (1.0 + sc_ref[...]) + sh_ref[...]).astype(BF16)
    acc = None
    for lo, hi in FFN_CHUNKS:
        gate = _dot(h, wi_ref[:, lo:hi])
        up = _dot(h, wi_ref[:, FFN_DENSE + lo:FFN_DENSE + hi])
        part = _dot((_silu(gate) * up).astype(BF16), wd_ref[lo:hi, :])
        acc = part if acc is None else acc + part
    y_ref[...] = _layer_norm(ALPHA * x1 + g2_ref[...] * acc, lg2_ref[...], lb2_ref[...])


def _attn_out_ffn(o, x, wo, mod, lg1, lb1, w_in, w_out, lg2, lb2, latent):
    n = x.shape[0]
    tm = TM_FFN
    rf = _row_fn(latent, tm)
    tok = lambda w: pl.BlockSpec((tm, w), lambda i: (i, 0))
    vec = _const_spec((1, D_MODEL))
    return pl.pallas_call(
        _attn_out_ffn_kernel,
        grid=(n // tm,),
        in_specs=[tok(o.shape[1]), tok(D_MODEL), _const_spec(wo.shape), _mod_spec(2, rf), vec, vec,
                  _mod_spec(4, rf), _mod_spec(3, rf), _mod_spec(5, rf),
                  _const_spec(w_in.shape), _const_spec(w_out.shape), vec, vec],
        out_specs=tok(D_MODEL),
        out_shape=jax.ShapeDtypeStruct((n, D_MODEL), F32),
        compiler_params=_cparams(("arbitrary",)),
        name="attn_out_ffn_lat" if latent else "attn_out_ffn_ctx",
    )(o, x, wo, mod, lg1, lb1, mod, mod, mod, w_in, w_out, lg2, lb2)


def _na_qkv_kernel(x_ref, sc_ref, sh_ref, w_ref, q_ref, v_ref, kt_ref, *vt_refs):
    h = (x_ref[...] * (1.0 + sc_ref[...]) + sh_ref[...]).astype(BF16)
    qkv = _dot(h, w_ref[...])
    q_ref[...] = (qkv[:, :D_MODEL] * NA_SCALE).astype(BF16)
    v = qkv[:, 2 * D_MODEL:]
    v_ref[...] = v.astype(BF16)
    pairs = [(kt_ref, qkv[:, D_MODEL:2 * D_MODEL])] + [(ref, v) for ref in vt_refs]
    for ref, val in pairs:
        valt = val.T
        nblk, _, w = ref.shape
        for t in range(nblk):
            ref[t] = valt[:, t * w:(t + 1) * w].astype(ref.dtype)


def _na_qkv(x, mod, w, latent):
    n = x.shape[0]
    tm = TM_PROJ
    kt_w = NA_KT_W if latent else SEQ
    rf = _row_fn(latent, tm)
    tok = pl.BlockSpec((tm, D_MODEL), lambda i: (i, 0))
    tspec = pl.BlockSpec((tm // kt_w, D_MODEL, kt_w), lambda i: (i, 0, 0))
    tshape = lambda dt: jax.ShapeDtypeStruct((n // kt_w, D_MODEL, kt_w), dt)
    out_specs = [tok, tok, tspec]
    out_shape = [jax.ShapeDtypeStruct((n, D_MODEL), BF16), jax.ShapeDtypeStruct((n, D_MODEL), BF16),
                 tshape(BF16 if latent else F32)]
    if not latent:
        out_specs.append(tspec)
        out_shape.append(tshape(F32))
    return pl.pallas_call(
        _na_qkv_kernel,
        grid=(n // tm,),
        in_specs=[tok, _mod_spec(1, rf), _mod_spec(0, rf), _const_spec(w.shape)],
        out_specs=out_specs,
        out_shape=out_shape,
        compiler_params=_cparams(("arbitrary",)),
        name="na_qkv_lat" if latent else "na_qkv_ctx",
    )(x, mod, mod, w)


def _softmax_pv(scores, values):
    m = functools.reduce(jnp.maximum, [jnp.max(s, axis=-1, keepdims=True) for s in scores])
    ps = [jnp.exp(s - m) for s in scores]
    l = functools.reduce(jnp.add, [jnp.sum(p, axis=-1, keepdims=True) for p in ps])
    o = functools.reduce(jnp.add, [_dot(p.astype(BF16), v) for p, v in zip(ps, values)])
    return o / l


def _head_of_pair(x, half):
    lane = lax.broadcasted_iota(jnp.int32, x.shape, 1)
    keep = (lane < NA_HD) if half == 0 else (lane >= NA_HD)
    return jnp.where(keep, x, jnp.zeros_like(x))


def _merge_pair(o0, o1):
    lane = lax.broadcasted_iota(jnp.int32, o0.shape, 1)
    return jnp.where(lane < NA_HD, o0, o1)


def _na_attn_ctx_kernel(q_ref, kt_ref, v_ref, o_ref):
    for b in range(CTX_BATCH_PER_STEP):
        rows = slice(b * SEQ, (b + 1) * SEQ)
        for p in range(NA_HEADS // 2):
            cols = slice(p * NA_PAIR, (p + 1) * NA_PAIR)
            qp = q_ref[rows, cols]
            ktp = kt_ref[b, cols, :].astype(BF16)
            vp = v_ref[rows, cols]
            outs = [_softmax_pv([_dot(_head_of_pair(qp, half), ktp)], [vp]) for half in range(2)]
            o_ref[rows, cols] = _merge_pair(*outs).astype(BF16)


def _na_attn_ctx(q, kt, v):
    tok = pl.BlockSpec((CTX_BATCH_PER_STEP * SEQ, D_MODEL), lambda b: (b, 0))
    return pl.pallas_call(
        _na_attn_ctx_kernel,
        grid=(BATCH // CTX_BATCH_PER_STEP,),
        in_specs=[tok, pl.BlockSpec((CTX_BATCH_PER_STEP, D_MODEL, SEQ), lambda b: (b, 0, 0)), tok],
        out_specs=tok,
        out_shape=jax.ShapeDtypeStruct((N_CTX, D_MODEL), BF16),
        compiler_params=_cparams(("arbitrary",)),
        name="na_attn_ctx",
    )(q, kt, v)


def _na_win_start(m, clip=jnp.clip):
    return clip(m * NA_BLK_ROWS // 2 - NA_WIN_ROWS // 4, 0, GRID_ROWS // 2 - NA_WIN_PAIRS)


def _na_variant(m):
    return jnp.where(m == 0, 0, jnp.where(m == NA_N_BLOCKS - 1, 2, 1))


def _na_attn_lat_kernel(q_ref, kt_ref, v_ref, kct_ref, vc_ref, bias_ref, o_ref):
    rp0 = _na_win_start(pl.program_id(1))
    win = pl.ds(pl.multiple_of(rp0 * NA_KT_W, NA_KT_W), NA_WIN_PAIRS * NA_KT_W)
    for p in range(NA_HEADS // 2):
        cols = slice(p * NA_PAIR, (p + 1) * NA_PAIR)
        qp = q_ref[:, cols]
        kw = jnp.concatenate([kt_ref[rp0 + t, cols, :] for t in range(NA_WIN_PAIRS)], axis=1)
        kc = kct_ref[cols, :]
        vw = v_ref[win, cols]
        vc = vc_ref[:, cols]
        outs = []
        for half in range(2):
            qh = _head_of_pair(qp, half)
            s_nb = _dot(qh, kw) + bias_ref[2 * p + half]
            outs.append(_softmax_pv([s_nb, _dot(qh, kc)], [vw, vc]))
        o_ref[:, cols] = _merge_pair(*outs).astype(BF16)


def _na_bias_kernel(e_ref, e64_ref, o_ref):
    n_dr = 2 * NA_WIN_ROWS - 1
    shape = (GRID_W, LANES)
    lane = lax.broadcasted_iota(jnp.int32, shape, 1)
    c = lax.broadcasted_iota(jnp.int32, shape, 0)
    kc = lane & (GRID_W - 1)
    cs = jnp.clip(c - NA_WIN_COLS // 2, 0, GRID_W - NA_WIN_COLS)
    valid = (kc >= cs) & (kc < cs + NA_WIN_COLS)
    toeplitz = lambda ref, a: pltpu.roll(jnp.broadcast_to(ref[a:a + 1, :], shape), 0, 1, stride=1, stride_axis=0)
    rows = ([toeplitz(e_ref, a) for a in range(n_dr)], [toeplitz(e64_ref, a) for a in range(n_dr)])
    masked = jnp.full(shape, NEG_BIG, F32)
    for v, m in enumerate(NA_BLOCK_VARIANTS):
        rp0 = _na_win_start(m, clip=lambda x, lo, hi: min(max(x, lo), hi))
        for i in range(NA_BLK_ROWS):
            r = NA_BLK_ROWS * m + i
            rs = min(max(r - NA_WIN_ROWS // 2, 0), GRID_ROWS - NA_WIN_ROWS)
            for t in range(NA_WIN_PAIRS):
                halves = []
                for u in range(2):
                    kr = 2 * (rp0 + t) + u
                    halves.append(rows[u][kr - r + NA_WIN_ROWS - 1] if rs <= kr < rs + NA_WIN_ROWS else masked)
                o_ref[v, i * GRID_W:(i + 1) * GRID_W, t * LANES:(t + 1) * LANES] = jnp.where(
                    valid, jnp.where(lane < GRID_W, halves[0], halves[1]), NEG_BIG)


def _na_bias_table(rpb):
    n_dr = 2 * NA_WIN_ROWS - 1
    blk = NA_BLK_ROWS * GRID_W
    e = jnp.zeros((NA_HEADS, n_dr, LANES), F32)
    e = e.at[:, :, :NA_WIN_COLS].set(rpb[:, :, NA_WIN_COLS - 1:])
    e = e.at[:, :, LANES - (NA_WIN_COLS - 1):].set(rpb[:, :, :NA_WIN_COLS - 1])
    e64 = jnp.roll(e, GRID_W, axis=-1)
    spec = pl.BlockSpec((None, n_dr, LANES), lambda h: (h, 0, 0))
    return pl.pallas_call(
        _na_bias_kernel,
        grid=(NA_HEADS,),
        in_specs=[spec, spec],
        out_specs=pl.BlockSpec((None, len(NA_BLOCK_VARIANTS), blk, NA_WIN_PAIRS * NA_KT_W), lambda h: (h, 0, 0, 0)),
        out_shape=jax.ShapeDtypeStruct((NA_HEADS, len(NA_BLOCK_VARIANTS), blk, NA_WIN_PAIRS * NA_KT_W), F32),
        compiler_params=_cparams(("arbitrary",)),
        name="na_bias_table",
    )(e, e64)


def _na_attn_lat(q, kt, v, kct, vc, bias_tab):
    blk = NA_BLK_ROWS * GRID_W
    nblk = NA_N_BLOCKS
    npair = DEC_SEQ // NA_KT_W
    row = pl.BlockSpec((blk, D_MODEL), lambda b, m: (b * nblk + m, 0))
    per_batch = lambda *shape: pl.BlockSpec((None,) + shape, lambda b, m: (b,) + (0,) * len(shape),
                                            pipeline_mode=pl.Buffered(1))
    bias_spec = pl.BlockSpec((NA_HEADS, None, blk, NA_WIN_PAIRS * NA_KT_W), lambda b, m: (0, _na_variant(m), 0, 0))
    return pl.pallas_call(
        _na_attn_lat_kernel,
        grid=(DEC_BATCH, nblk),
        in_specs=[row, per_batch(npair, D_MODEL, NA_KT_W), per_batch(DEC_SEQ, D_MODEL),
                  per_batch(D_MODEL, PAST_LEN), per_batch(PAST_LEN, D_MODEL), bias_spec],
        out_specs=row,
        out_shape=jax.ShapeDtypeStruct((N_LAT, D_MODEL), BF16),
        compiler_params=_cparams(("arbitrary", "arbitrary")),
        name="na_attn_lat",
    )(q, kt.reshape(DEC_BATCH, npair, D_MODEL, NA_KT_W), v.reshape(DEC_BATCH, DEC_SEQ, D_MODEL), kct, vc, bias_tab)


def _pow2_runs(n, src, dst, largest, make_copy):
    out = []
    b = largest
    while b >= SUBLANES:
        out.append(((n & b) != 0, make_copy(src, dst, b)))
        src = src + (n & b)
        dst = dst + (n & b)
        b //= 2
    return out


def _rows8(start, size):
    return pl.ds(pl.multiple_of(start, SUBLANES), size)


def _moe_scatter_kernel(seg_ref, pad_ref, na_ref, x_ref, sc_ref, sh_ref, rt_ref, xs_ref, srt_scr, zero_scr, sems, zsem):
    i = pl.program_id(0)
    tm = x_ref.shape[0]
    slot = i % 2
    hm = (x_ref[...] * (1.0 + sc_ref[...]) + sh_ref[...]).astype(BF16)
    rtt = rt_ref[...]
    row = lax.broadcasted_iota(jnp.int32, (MOE_LOCAL_ROWS, tm), 0).astype(F32)
    chosen = (row == rtt[RT_LOCAL:RT_LOCAL + 1, :]) | (row == rtt[RT_LOCAL + 1:RT_LOCAL + 2, :])
    srt_scr[slot] = _dot(jnp.where(chosen, 1.0, 0.0).astype(BF16), hm)

    def seg_copies(tile, s):
        out = []
        for e in range(N_EXPERTS):
            base = (tile * N_EXPERTS + e) * 3
            out += _pow2_runs(
                seg_ref[base], seg_ref[base + 1], seg_ref[base + 2], tm,
                lambda src, dst, b: pltpu.make_async_copy(srt_scr.at[s, _rows8(src, b)], xs_ref.at[_rows8(dst, b)],
                                                          sems.at[s]))
        return out

    for cond, cp in seg_copies(i, slot):
        pl.when(cond)(cp.start)

    @pl.when(i > 0)
    def _():
        for cond, cp in seg_copies(i - 1, 1 - slot):
            pl.when(cond)(cp.wait)

    @pl.when(i == pl.num_programs(0) - 1)
    def _():
        zero_scr[...] = jnp.zeros_like(zero_scr)

        def pad_copies():
            out = []
            for e in range(N_EXPERTS):
                out += _pow2_runs(
                    pad_ref[N_EXPERTS + e], 0, pad_ref[e], TM_MOE // 2,
                    lambda src, dst, b: pltpu.make_async_copy(zero_scr.at[pl.ds(0, b)], xs_ref.at[_rows8(dst, b)], zsem))
            return out

        def tile_copy(j):
            rows = pl.ds(pl.multiple_of(j * TM_MOE, TM_MOE), TM_MOE)
            return pltpu.make_async_copy(zero_scr, xs_ref.at[rows], zsem)

        for cond, cp in pad_copies():
            pl.when(cond)(cp.start)
        lax.fori_loop(na_ref[0], NT_MOE, lambda j, c: (tile_copy(j).start(), c)[1], 0)
        for cond, cp in pad_copies():
            pl.when(cond)(cp.wait)
        lax.fori_loop(na_ref[0], NT_MOE, lambda j, c: (tile_copy(j).wait(), c)[1], 0)
        for cond, cp in seg_copies(i, slot):
            pl.when(cond)(cp.wait)


def _moe_scatter(seg, pad, na, x_all, rt_all, mod):
    tm = TM_PROJ
    rf = _all_row_fn(tm)
    return pl.pallas_call(
        _moe_scatter_kernel,
        grid_spec=pltpu.PrefetchScalarGridSpec(
            num_scalar_prefetch=3,
            grid=(N_TOK // tm,),
            in_specs=[pl.BlockSpec((tm, D_MODEL), lambda i, *_: (i, 0)), _mod_spec(4, rf), _mod_spec(3, rf),
                      pl.BlockSpec((RT_WIDTH, tm), lambda i, *_: (0, i))],
            out_specs=pl.BlockSpec(memory_space=pl.ANY),
            scratch_shapes=[pltpu.VMEM((2, MOE_LOCAL_ROWS, D_MODEL), F32), pltpu.VMEM((TM_MOE, D_MODEL), F32),
                            pltpu.SemaphoreType.DMA((2,)), pltpu.SemaphoreType.DMA(())],
        ),
        out_shape=jax.ShapeDtypeStruct((ROWS_SORTED, D_MODEL), F32),
        compiler_params=_cparams(("arbitrary",)),
        name="moe_scatter",
    )(seg, pad, na, x_all, mod, mod, rt_all)


def _tile_changed(te_ref, j):
    prev = te_ref[jnp.maximum(j - 1, 0)]
    return (j == 0) | (te_ref[j] != prev)


def _moe_up_kernel(te_ref, na_ref, x_ref, wg_ref, wu_ref, o_ref, w_scr):
    j = pl.program_id(1)
    tf = wg_ref.shape[1]

    active = j < na_ref[0]
    changed = _tile_changed(te_ref, j)

    @pl.when(active & changed)
    def _():
        x = x_ref[...].astype(BF16)
        for c in range(tf // MXU_TILE):
            cols = slice(c * MXU_TILE, (c + 1) * MXU_TILE)
            up_cols = slice(tf + c * MXU_TILE, tf + (c + 1) * MXU_TILE)
            w_scr[:, cols] = wg_ref[:, cols].astype(BF16)
            w_scr[:, up_cols] = wu_ref[:, cols].astype(BF16)
            o_ref[:, cols] = (_silu(_dot(x, w_scr[:, cols])) * _dot(x, w_scr[:, up_cols])).astype(BF16)

    @pl.when(active & jnp.logical_not(changed))
    def _():
        gu = _dot(x_ref[...].astype(BF16), w_scr[...])
        o_ref[...] = (_silu(gu[:, :tf]) * gu[:, tf:]).astype(BF16)

    @pl.when(j >= na_ref[0])
    def _():
        o_ref[...] = jnp.zeros_like(o_ref)


def _moe_up(te, na, xs, w_in):
    tm, tf = TM_MOE, TF_MOE
    nf = FFN_EXPERT // tf
    row = lambda j, na: jnp.minimum(j, na[0] - 1)
    return pl.pallas_call(
        _moe_up_kernel,
        grid_spec=pltpu.PrefetchScalarGridSpec(
            num_scalar_prefetch=2,
            grid=(nf, NT_MOE),
            in_specs=[pl.BlockSpec((tm, D_MODEL), lambda f, j, te, na: (row(j, na), 0)),
                      pl.BlockSpec((None, D_MODEL, tf), lambda f, j, te, na: (te[j], 0, f)),
                      pl.BlockSpec((None, D_MODEL, tf), lambda f, j, te, na: (te[j], 0, nf + f))],
            out_specs=pl.BlockSpec((tm, tf), lambda f, j, te, na: (j, f)),
            scratch_shapes=[pltpu.VMEM((D_MODEL, 2 * tf), BF16)],
        ),
        out_shape=jax.ShapeDtypeStruct((ROWS_SORTED, FFN_EXPERT), BF16),
        compiler_params=_cparams(("arbitrary", "arbitrary")),
        name="moe_up",
    )(te, na, xs, w_in, w_in)


def _moe_down_kernel(te_ref, na_ref, h_ref, w_ref, o_ref, w_scr):
    j = pl.program_id(0)

    active = j < na_ref[0]
    changed = _tile_changed(te_ref, j)

    @pl.when(active & changed)
    def _():
        h = h_ref[...]
        for c in range(o_ref.shape[1] // MXU_TILE):
            cols = slice(c * MXU_TILE, (c + 1) * MXU_TILE)
            w_scr[:, cols] = w_ref[:, cols].astype(BF16)
            o_ref[:, cols] = _dot(h, w_scr[:, cols])

    @pl.when(active & jnp.logical_not(changed))
    def _():
        o_ref[...] = _dot(h_ref[...], w_scr[...])

    @pl.when(j >= na_ref[0])
    def _():
        o_ref[...] = jnp.zeros_like(o_ref)


def _moe_down(te, na, hmid, w_out):
    tm = TM_MOE
    row = lambda j, na: jnp.minimum(j, na[0] - 1)
    return pl.pallas_call(
        _moe_down_kernel,
        grid_spec=pltpu.PrefetchScalarGridSpec(
            num_scalar_prefetch=2,
            grid=(NT_MOE,),
            in_specs=[pl.BlockSpec((tm, FFN_EXPERT), lambda j, te, na: (row(j, na), 0)),
                      pl.BlockSpec((None, FFN_EXPERT, D_MODEL), lambda j, te, na: (te[j], 0, 0))],
            out_specs=pl.BlockSpec((tm, D_MODEL), lambda j, te, na: (j, 0)),
            scratch_shapes=[pltpu.VMEM((FFN_EXPERT, D_MODEL), BF16)],
        ),
        out_shape=jax.ShapeDtypeStruct((ROWS_SORTED, D_MODEL), F32),
        compiler_params=_cparams(("arbitrary",)),
        name="moe_down",
    )(te, na, hmid, w_out)


def _moe_combine_kernel(pos_ref, y_ref, rt_ref, x_ref, g_ref, lg_ref, lb_ref, o_ref, ybuf, sems, *, tok_off):
    i = pl.program_id(0)
    tm = x_ref.shape[0]

    def fetch(tile, slot):
        def start(t, c):
            for k in range(TOP_K):
                p = pos_ref[k * N_TOK + tok_off + tile * tm + t]
                pltpu.make_async_copy(y_ref.at[pl.ds(p, 1)], ybuf.at[slot, k, pl.ds(t, 1)], sems.at[slot]).start()
            return c

        lax.fori_loop(0, tm, start, 0, unroll=8)

    @pl.when(i == 0)
    def _():
        fetch(0, 0)

    @pl.when(i + 1 < pl.num_programs(0))
    def _():
        fetch(i + 1, (i + 1) % 2)

    slot = i % 2
    pltpu.make_async_copy(ybuf.at[slot], ybuf.at[slot], sems.at[slot]).wait()
    rt = rt_ref[...]
    moe = rt[:, RT_GATE:RT_GATE + 1] * ybuf[slot, 0] + rt[:, RT_GATE + 1:RT_GATE + 2] * ybuf[slot, 1]
    o_ref[...] = _layer_norm(ALPHA * x_ref[...] + g_ref[...] * moe, lg_ref[...], lb_ref[...])


def _moe_combine(pos, y, rt_all, x_all, mod, lg, lb, latent):
    n = N_LAT if latent else N_CTX
    tm = TM_ROUTE
    rf = _row_fn(latent, tm)
    tok_off = N_CTX if latent else 0
    off = tok_off // tm
    return pl.pallas_call(
        functools.partial(_moe_combine_kernel, tok_off=tok_off),
        grid_spec=pltpu.PrefetchScalarGridSpec(
            num_scalar_prefetch=1,
            grid=(n // tm,),
            in_specs=[pl.BlockSpec(memory_space=pl.ANY),
                      pl.BlockSpec((tm, LANES), lambda i, pos: (i + off, 0)),
                      pl.BlockSpec((tm, D_MODEL), lambda i, pos: (i + off, 0)),
                      _mod_spec(5, rf),
                      pl.BlockSpec((1, D_MODEL), lambda i, pos: (0, 0)),
                      pl.BlockSpec((1, D_MODEL), lambda i, pos: (0, 0))],
            out_specs=pl.BlockSpec((tm, D_MODEL), lambda i, pos: (i, 0)),
            scratch_shapes=[pltpu.VMEM((2, TOP_K, tm, D_MODEL), F32), pltpu.SemaphoreType.DMA((2,))],
        ),
        out_shape=jax.ShapeDtypeStruct((n, D_MODEL), F32),
        compiler_params=_cparams(("arbitrary",)),
        name="moe_combine_lat" if latent else "moe_combine_ctx",
    )(pos, y, rt_all, x_all, mod, lg, lb)


def _routing_positions(rtt_all, counts, meta):
    tm = TM_MOE
    cnt = counts[0, :N_EXPERTS].astype(jnp.int32)
    padded = ((cnt + tm - 1) // tm) * tm
    gend = jnp.cumsum(padded)
    gstart = gend - padded
    route = rtt_all.astype(jnp.int32)
    onehot = lambda idx: idx[None, :] == jnp.arange(N_EXPERTS, dtype=jnp.int32)[:, None]
    start_of = lambda idx: jnp.sum(jnp.where(onehot(idx), gstart[:, None], 0), axis=0)
    pos = jnp.concatenate([start_of(route[RT_EXPERT + k]) + route[RT_ROW + k] for k in range(TOP_K)])
    pad = jnp.concatenate([gstart + cnt, padded - cnt]).astype(jnp.int32)
    tile_start = jnp.arange(NT_MOE, dtype=jnp.int32) * tm
    te = jnp.sum((tile_start[:, None] >= gend[None, :]).astype(jnp.int32), axis=1)
    n_active = (gend[-1] // tm).astype(jnp.int32)
    last_e = jnp.take(te, jnp.maximum(n_active - 1, 0))
    te = jnp.where(tile_start < gend[-1], te, last_e).astype(jnp.int32)
    m = meta[:, :3, :N_EXPERTS].astype(jnp.int32)
    seg = jnp.stack([m[:, 0], m[:, 1], m[:, 2] + gstart[None, :]], axis=-1).reshape(-1)
    return pos, seg, pad, te, n_active.reshape(1)


def kernel(x_prompt, x_sample, cache_ckv_l0, cache_krope_l0, cache_k_l1, cache_v_l1, c, c_ctx, w_ada_l0, b_ada_l0, mla_w_dq, mla_g_q, mla_w_uq, mla_w_dkv, mla_g_kv, mla_w_ukv, mla_w_o, ln1_g_l0, ln1_b_l0, ffn_w_in, ffn_w_out, ln2_g_l0, ln2_b_l0, w_ada_l1, b_ada_l1, na_w_qkv, na_rpb, na_w_o, ln1_g_l1, ln1_b_l1, moe_w_router, moe_w_in, moe_w_out, ln2_g_l1, ln2_b_l1):
    row = lambda v: v.reshape(1, -1)
    xp = x_prompt.reshape(N_CTX, D_MODEL)
    xs = x_sample.reshape(N_LAT, D_MODEL)
    groups = ((xp, False), (xs, True))

    cvecs = jnp.concatenate([c_ctx[None], c, jnp.zeros((SUBLANES - 1 - DEC_BATCH, D_MODEL), F32)], axis=0)
    mod0 = _ada_mod(cvecs, w_ada_l0, b_ada_l0)
    mod1 = _ada_mod(cvecs, w_ada_l1, b_ada_l1)

    wa = jnp.concatenate([mla_w_dq, mla_w_dkv], axis=1).astype(BF16)
    wuq = mla_w_uq.reshape(MLA_Q_LORA, MLA_HEADS, MLA_NOPE + MLA_ROPE)
    wuq = jnp.pad(wuq, ((0, 0), (0, 0), (0, Q_HEAD_PAD - MLA_NOPE - MLA_ROPE)))
    wuq = wuq.reshape(MLA_Q_LORA, MLA_HEADS * Q_HEAD_PAD).astype(BF16)
    wukv = mla_w_ukv.astype(BF16)
    wo0 = mla_w_o.astype(BF16)
    w_in0 = ffn_w_in.astype(BF16)
    w_out0 = ffn_w_out.astype(BF16)
    wqkv = na_w_qkv.astype(BF16)
    wo1 = na_w_o.astype(BF16)
    wr = jnp.pad(moe_w_router, ((0, 0), (0, LANES - N_EXPERTS))).astype(BF16)
    bias_tab = _na_bias_table(na_rpb)
    tab = _rope_tables()

    kvx_cache = _matmul(cache_ckv_l0.reshape(DEC_BATCH * PAST_LEN, MLA_KV_LORA), wukv, BF16, 512, "mla_expand_cache")
    kvx_cache = kvx_cache.reshape(DEC_BATCH, PAST_LEN, -1)
    krp_cache = jnp.pad(cache_krope_l0, ((0, 0), (0, 0), (0, LANES - MLA_ROPE))).astype(BF16)
    x1 = []
    new_ckv = new_kr = None
    vt_cache = kvx_cache.reshape(DEC_BATCH, PAST_LEN, MLA_HEADS, MLA_NOPE + MLA_V)[..., MLA_NOPE:]
    vt_cache = vt_cache.reshape(DEC_BATCH, PAST_LEN, MLA_HEADS * MLA_V).transpose(0, 2, 1)
    for x, latent in groups:
        proj = _mla_proj(x, mod0, wa, row(mla_g_q), row(mla_g_kv), wuq, wukv, tab, latent)
        if latent:
            qt, kvx, krp, vt = proj
            o = _mla_attn_lat(qt, kvx, krp, kvx_cache, krp_cache, vt, vt_cache)
        else:
            q, kvx, new_ckv, new_kr, krp = proj
            o = _mla_attn_ctx(q, kvx, krp)
        x1.append(_attn_out_ffn(o, x, wo0, mod0, row(ln1_g_l0), row(ln1_b_l0), w_in0, w_out0,
                                row(ln2_g_l0), row(ln2_b_l0), latent))

    kct = cache_k_l1.reshape(DEC_BATCH, PAST_LEN, D_MODEL).transpose(0, 2, 1).astype(BF16)
    vc = cache_v_l1.reshape(DEC_BATCH, PAST_LEN, D_MODEL).astype(BF16)
    q, v, new_kt, new_vt = _na_qkv(x1[0], mod1, wqkv, False)
    o_ctx = _na_attn_ctx(q, new_kt, v)
    q, v, kt = _na_qkv(x1[1], mod1, wqkv, True)
    o_lat = _na_attn_lat(q, kt, v, kct, vc, bias_tab)
    x2_all, rt_all, rtt_all, counts, meta = _proj_ln_route(o_ctx, x1[0], o_lat, x1[1], wo1, mod1, row(ln1_g_l1),
                                                           row(ln1_b_l1), wr)

    pos, seg, pad, te, n_active = _routing_positions(rtt_all, counts, meta)
    x_sorted = _moe_scatter(seg, pad, n_active, x2_all, rtt_all, mod1)
    hmid = _moe_up(te, n_active, x_sorted, moe_w_in)
    y = _moe_down(te, n_active, hmid, moe_w_out)
    outs = [_moe_combine(pos, y, rt_all, x2_all, mod1, row(ln2_g_l1), row(ln2_b_l1), latent)
            for latent in (False, True)]

    return (outs[0].reshape(BATCH, SEQ, D_MODEL),
            outs[1].reshape(DEC_BATCH, DEC_SEQ, D_MODEL),
            new_ckv.reshape(BATCH, SEQ, MLA_KV_LORA),
            new_kr.transpose(0, 2, 1),
            new_kt.reshape(BATCH, NA_HEADS, NA_HD, SEQ).transpose(0, 3, 1, 2),
            new_vt.reshape(BATCH, NA_HEADS, NA_HD, SEQ).transpose(0, 3, 1, 2))
```
